```python
import math
import jax, jax.numpy as jnp
from jax import lax
import numpy as np

D_MODEL = 1024
BATCH = 8
SEQ = 2048
DEPTH = 1

GDN_HEADS = 4
GDN_DK = 128
GDN_DV = 128
GDN_CONV = 4
GDN_CHUNK = 64
NSA_HEADS = 8
NSA_KV_HEADS = 2
NSA_DH = 64
CMP_BLOCK = 32
CMP_STRIDE = 16
CMP_HIDDEN = 256
SEL_BLOCK = 64
SEL_TOPK = 16
SEL_Q_BLOCK = 64
WINDOW = 512
WIN_Q_BLOCK = 128
FORCE_SCORE = 1e9
D_FF = 2816
FFN_CONV = 3
DN_ALPHA = (2 * DEPTH) ** 0.25
DN_BETA = (8 * DEPTH) ** -0.25
LN_EPS = 1e-5
RMS_EPS = 1e-6

GDN_QK = GDN_HEADS * GDN_DK
GDN_VW = GDN_HEADS * GDN_DV
NSA_QW = NSA_HEADS * NSA_DH
NSA_KVW = NSA_KV_HEADS * NSA_DH
IN_SIZES = (GDN_QK, GDN_QK, GDN_VW, GDN_VW, GDN_HEADS, GDN_HEADS,
            NSA_QW, NSA_KVW, NSA_KVW, NSA_KVW, NSA_KVW, NSA_KVW, NSA_KVW, 3 * NSA_HEADS,
            2 * D_MODEL)
D_IN = sum(IN_SIZES)
IN_SPLIT = [int(v) for v in np.cumsum(IN_SIZES)[:-1]]

kernel_name = "hybrid_gdn_nsa_convffn_deepnorm"


def layer_norm(x, g, b):
    xf = x.astype(jnp.float32)
    mu = jnp.mean(xf, -1, keepdims=True)
    var = jnp.mean(jnp.square(xf - mu), -1, keepdims=True)
    return ((xf - mu) * lax.rsqrt(var + LN_EPS)).astype(x.dtype) * g + b


def causal_dwconv(x, w):
    k = w.shape[0]
    return lax.conv_general_dilated(x, w[:, None, :], window_strides=(1,), padding=[(k - 1, 0)],
                                    dimension_numbers=('NWC', 'WIO', 'NWC'),
                                    feature_group_count=x.shape[-1])


def masked_softmax(s, mask):
    s = jnp.where(mask, s.astype(jnp.float32), -jnp.inf)
    m = jnp.max(s, axis=-1, keepdims=True)
    p = jnp.exp(s - jnp.where(jnp.isfinite(m), m, 0.0))
    d = jnp.sum(p, axis=-1, keepdims=True)
    return p / jnp.where(d > 0, d, 1.0)


def l2norm(t):
    return t * lax.rsqrt(jnp.sum(t * t, -1, keepdims=True) + RMS_EPS)


def chunk_gated_delta(q, k, v, g, beta):
    b, s, h, dk = q.shape
    dv = v.shape[-1]
    c = GDN_CHUNK
    n = s // c

    def to_chunks(t):
        return t.reshape(b, n, c, h, -1).transpose(0, 3, 1, 2, 4)

    q, k, v = to_chunks(q), to_chunks(k), to_chunks(v)
    g = g.reshape(b, n, c, h).transpose(0, 3, 1, 2)
    beta = beta.reshape(b, n, c, h).transpose(0, 3, 1, 2)
    gc = jnp.cumsum(g, axis=-1)
    tril = jnp.tril(jnp.ones((c, c), bool))
    strict = jnp.tril(jnp.ones((c, c), bool), -1)
    diff = gc[..., :, None] - gc[..., None, :]
    decay = jnp.where(tril, jnp.exp(jnp.where(tril, diff, 0.0)), 0.0)
    kk = jnp.einsum('bhnid,bhnjd->bhnij', k, k)
    a_mat = jnp.where(strict, kk * decay * beta[..., :, None], 0.0)
    lhs = jnp.eye(c, dtype=jnp.float32) + a_mat
    rhs = jnp.concatenate([v * beta[..., None], k * (beta * jnp.exp(gc))[..., None]], -1)
    sol = lax.linalg.triangular_solve(lhs, rhs, left_side=True, lower=True)
    u0, kcd = sol[..., :dv], sol[..., dv:]
    qk = jnp.einsum('bhnid,bhnjd->bhnij', q, k) * decay
    q_dec = q * jnp.exp(gc)[..., None]
    k_dec = k * jnp.exp(gc[..., -1:] - gc)[..., None]
    g_last = jnp.exp(gc[..., -1])

    def lead(t):
        return jnp.moveaxis(t, 2, 0)

    xs = (lead(u0), lead(kcd), lead(qk), lead(q_dec), lead(k_dec), jnp.moveaxis(g_last, 2, 0))

    def step(state, xc):
        u0_c, kcd_c, qk_c, qd_c, kd_c, gl_c = xc
        u = u0_c - jnp.einsum('bhck,bhkv->bhcv', kcd_c, state)
        o = jnp.einsum('bhck,bhkv->bhcv', qd_c, state) + jnp.einsum('bhij,bhjv->bhiv', qk_c, u)
        state = state * gl_c[..., None, None] + jnp.einsum('bhck,bhcv->bhkv', kd_c, u)
        return state, o

    s0 = jnp.zeros((b, h, dk, dv), jnp.float32)
    _, o = lax.scan(step, s0, xs)
    return o.transpose(1, 0, 3, 2, 4).reshape(b, s, h, dv)


def gated_deltanet(q, k, v, z, bb, aa, conv_w, a_log, dt_bias, norm_w):
    bsz, s, _ = q.shape
    dt = q.dtype
    qkv = jax.nn.silu(causal_dwconv(jnp.concatenate([q, k, v], -1), conv_w))
    q, k, v = jnp.split(qkv.astype(jnp.float32), [GDN_QK, 2 * GDN_QK], -1)
    q = l2norm(q.reshape(bsz, s, GDN_HEADS, GDN_DK)) * (GDN_DK ** -0.5)
    k = l2norm(k.reshape(bsz, s, GDN_HEADS, GDN_DK))
    v = v.reshape(bsz, s, GDN_HEADS, GDN_DV)
    beta = jax.nn.sigmoid(bb.astype(jnp.float32))
    g = -jnp.exp(a_log.astype(jnp.float32)) * jax.nn.softplus(aa.astype(jnp.float32) + dt_bias.astype(jnp.float32))
    o = chunk_gated_delta(q, k, v, g, beta)
    o = o * lax.rsqrt(jnp.mean(o * o, -1, keepdims=True) + RMS_EPS) * norm_w.astype(jnp.float32)
    o = o * jax.nn.silu(z.astype(jnp.float32).reshape(bsz, s, GDN_HEADS, GDN_DV))
    return o.reshape(bsz, s, GDN_VW).astype(dt)


def native_sparse_attention(q, k_c, v_c, k_s, v_s, k_w, v_w, gate,
                            pos_k, w1_k, w2_k, pos_v, w1_v, w2_v):
    bsz, s, _ = q.shape
    gkv, dh = NSA_KV_HEADS, NSA_DH
    rep = NSA_HEADS // NSA_KV_HEADS
    dt = q.dtype
    q = q.reshape(bsz, s, gkv, rep, dh) * (dh ** -0.5)
    k_c, v_c, k_s, v_s, k_w, v_w = [t.reshape(bsz, s, gkv, dh) for t in (k_c, v_c, k_s, v_s, k_w, v_w)]
    pos = jnp.arange(s)

    n_cmp = (s - CMP_BLOCK) // CMP_STRIDE + 1
    cidx = np.arange(n_cmp)[:, None] * CMP_STRIDE + np.arange(CMP_BLOCK)[None]

    def compress(t, pe, w1, w2):
        blk = t[:, cidx] + pe[:, None, :]
        blk = blk.transpose(0, 1, 3, 2, 4).reshape(bsz, n_cmp, gkv, CMP_BLOCK * dh)
        return jax.nn.gelu(blk @ w1) @ w2

    kc = compress(k_c, pos_k, w1_k, w2_k)
    vc = compress(v_c, pos_v, w1_v, w2_v)
    cmp_end = jnp.asarray(cidx[:, -1])
    s_cmp = jnp.einsum('bsgrd,bngd->bgrsn', q, kc)
    p_cmp = masked_softmax(s_cmp, cmp_end[None, :] <= pos[:, None])
    o_cmp = jnp.einsum('bgrsn,bngd->bsgrd', p_cmp.astype(dt), vc)

    n_sel_blk = s // SEL_BLOCK
    starts = np.arange(n_cmp) * CMP_STRIDE
    jb = np.arange(n_sel_blk) * SEL_BLOCK
    overlap = ((starts[:, None] < jb[None] + SEL_BLOCK) & (starts[:, None] + CMP_BLOCK > jb[None])).astype(np.float32)
    imp = jnp.einsum('bgrsn,nj->bgsj', p_cmp, jnp.asarray(overlap))
    jj = jnp.arange(n_sel_blk)
    cur = pos // SEL_BLOCK
    forced = (jj[None] == 0) | (jj[None] == cur[:, None]) | (jj[None] == cur[:, None] - 1)
    causal_blk = (jj[None] * SEL_BLOCK) <= pos[:, None]
    imp = jnp.where(forced, FORCE_SCORE, jnp.where(causal_blk, imp, -jnp.inf))
    n_top = min(SEL_TOPK, n_sel_blk)
    _, sel = lax.top_k(imp, n_top)

    ks_blk = k_s.reshape(bsz, n_sel_blk, SEL_BLOCK, gkv, dh).transpose(0, 3, 1, 2, 4)
    vs_blk = v_s.reshape(bsz, n_sel_blk, SEL_BLOCK, gkv, dh).transpose(0, 3, 1, 2, 4)
    nq = s // SEL_Q_BLOCK
    q_ch = q.reshape(bsz, nq, SEL_Q_BLOCK, gkv, rep, dh).transpose(1, 0, 2, 3, 4, 5)
    sel_ch = sel.reshape(bsz, gkv, nq, SEL_Q_BLOCK, n_top).transpose(2, 0, 1, 3, 4)
    pos_ch = pos.reshape(nq, SEL_Q_BLOCK)
    bi = jnp.arange(bsz)[:, None, None, None]
    gi = jnp.arange(gkv)[None, :, None, None]
    offs = jnp.arange(SEL_BLOCK)

    def sel_block(args):
        qc, sc, pc = args
        kg = ks_blk[bi, gi, sc]
        vg = vs_blk[bi, gi, sc]
        sc_ = jnp.einsum('bqgrd,bgqnld->bgrqnl', qc, kg)
        kpos = sc[..., None] * SEL_BLOCK + offs
        mask = (kpos <= pc[None, None, :, None, None]).reshape(bsz, gkv, 1, SEL_Q_BLOCK, n_top * SEL_BLOCK)
        p = masked_softmax(sc_.reshape(bsz, gkv, rep, SEL_Q_BLOCK, n_top * SEL_BLOCK), mask)
        p = p.reshape(bsz, gkv, rep, SEL_Q_BLOCK, n_top, SEL_BLOCK).astype(dt)
        return jnp.einsum('bgrqnl,bgqnld->bqgrd', p, vg)

    o_slc = lax.map(sel_block, (q_ch, sel_ch, pos_ch))
    o_slc = o_slc.transpose(1, 0, 2, 3, 4, 5).reshape(bsz, s, gkv, rep, dh)

    nb = s // WIN_Q_BLOCK
    span = WINDOW + WIN_Q_BLOCK
    widx = np.arange(nb)[:, None] * WIN_Q_BLOCK + np.arange(span)[None]
    kwb = jnp.pad(k_w, ((0, 0), (WINDOW, 0), (0, 0), (0, 0)))[:, widx]
    vwb = jnp.pad(v_w, ((0, 0), (WINDOW, 0), (0, 0), (0, 0)))[:, widx]
    qb = q.reshape(bsz, nb, WIN_Q_BLOCK, gkv, rep, dh)
    s_win = jnp.einsum('bcqgrd,bckgd->bcgrqk', qb, kwb)
    qpos = pos.reshape(nb, WIN_Q_BLOCK)[:, :, None]
    kpos = jnp.asarray(widx - WINDOW)[:, None, :]
    wmask = (kpos <= qpos) & (kpos > qpos - WINDOW) & (kpos >= 0)
    p_win = masked_softmax(s_win, wmask[None, :, None, None]).astype(dt)
    o_win = jnp.einsum('bcgrqk,bckgd->bcqgrd', p_win, vwb).reshape(bsz, s, gkv, rep, dh)

    gt = jax.nn.sigmoid(gate).reshape(bsz, s, gkv, rep, 3)
    o = gt[..., 0:1] * o_cmp + gt[..., 1:2] * o_slc + gt[..., 2:3] * o_win
    return o.reshape(bsz, s, NSA_QW)


def setup_inputs(seed: int = 0) -> dict:
    key = jax.random.key(seed)
    ks = jax.random.split(key, 24)
    f32 = jnp.float32
    L = DEPTH

    def nrm(k, shape, scale):
        return jax.random.normal(k, shape, f32) * scale

    x = jax.random.normal(ks[0], (BATCH, SEQ, D_MODEL), f32)
    w_in = nrm(ks[1], (L, D_MODEL, D_IN), D_MODEL ** -0.5)
    gdn_conv_w = nrm(ks[2], (L, GDN_CONV, 2 * GDN_QK + GDN_VW), GDN_CONV ** -0.5)
    gdn_a_log = jnp.log(jax.random.uniform(ks[3], (L, GDN_HEADS), f32, 1.0, 16.0))
    dtv = jnp.exp(jax.random.uniform(ks[4], (L, GDN_HEADS), f32, math.log(1e-3), math.log(1e-1)))
    gdn_dt_bias = dtv + jnp.log(-jnp.expm1(-dtv))
    gdn_norm_w = 1.0 + nrm(ks[5], (L, GDN_DV), 0.02)
    cmp_pos_k = nrm(ks[6], (L, CMP_BLOCK, NSA_DH), 0.02)
    cmp_w1_k = nrm(ks[7], (L, CMP_BLOCK * NSA_DH, CMP_HIDDEN), (CMP_BLOCK * NSA_DH) ** -0.5)
    cmp_w2_k = nrm(ks[8], (L, CMP_HIDDEN, NSA_DH), CMP_HIDDEN ** -0.5)
    cmp_pos_v = nrm(ks[9], (L, CMP_BLOCK, NSA_DH), 0.02)
    cmp_w1_v = nrm(ks[10], (L, CMP_BLOCK * NSA_DH, CMP_HIDDEN), (CMP_BLOCK * NSA_DH) ** -0.5)
    cmp_w2_v = nrm(ks[11], (L, CMP_HIDDEN, NSA_DH), CMP_HIDDEN ** -0.5)
    w_branch_gdn = nrm(ks[12], (L, GDN_VW, D_MODEL), GDN_VW ** -0.5)
    w_branch_nsa = nrm(ks[13], (L, NSA_QW, D_MODEL), NSA_QW ** -0.5)
    w_out = nrm(ks[14], (L, D_MODEL, D_MODEL), D_MODEL ** -0.5 * DN_BETA)
    ln1_g = 1.0 + nrm(ks[15], (L, D_MODEL), 0.02)
    ln1_b = nrm(ks[16], (L, D_MODEL), 0.02)
    w_up = nrm(ks[17], (L, D_MODEL, 2 * D_FF), D_MODEL ** -0.5)
    ffn_conv_w = nrm(ks[18], (L, FFN_CONV, 2 * D_FF), FFN_CONV ** -0.5)
    w_down = nrm(ks[19], (L, D_FF, D_MODEL), D_FF ** -0.5 * DN_BETA)
    ln2_g = 1.0 + nrm(ks[20], (L, D_MODEL), 0.02)
    ln2_b = nrm(ks[21], (L, D_MODEL), 0.02)
    return {"x": x, "w_in": w_in, "gdn_conv_w": gdn_conv_w, "gdn_a_log": gdn_a_log,
            "gdn_dt_bias": gdn_dt_bias, "gdn_norm_w": gdn_norm_w,
            "cmp_pos_k": cmp_pos_k, "cmp_w1_k": cmp_w1_k, "cmp_w2_k": cmp_w2_k,
            "cmp_pos_v": cmp_pos_v, "cmp_w1_v": cmp_w1_v, "cmp_w2_v": cmp_w2_v,
            "w_branch_gdn": w_branch_gdn, "w_branch_nsa": w_branch_nsa, "w_out": w_out,
            "ln1_g": ln1_g, "ln1_b": ln1_b, "w_up": w_up, "ffn_conv_w": ffn_conv_w,
            "w_down": w_down, "ln2_g": ln2_g, "ln2_b": ln2_b}


def reference(x, w_in, gdn_conv_w, gdn_a_log, gdn_dt_bias, gdn_norm_w,
              cmp_pos_k, cmp_w1_k, cmp_w2_k, cmp_pos_v, cmp_w1_v, cmp_w2_v,
              w_branch_gdn, w_branch_nsa, w_out, ln1_g, ln1_b,
              w_up, ffn_conv_w, w_down, ln2_g, ln2_b):
    for i in range(DEPTH):
        proj = x @ w_in[i]
        (g_q, g_k, g_v, g_z, g_b, g_a, n_q, n_kc, n_vc, n_ks, n_vs, n_kw, n_vw, n_gate,
         mix_gate) = jnp.split(proj, IN_SPLIT, axis=-1)
        o_a = gated_deltanet(g_q, g_k, g_v, g_z, g_b, g_a, gdn_conv_w[i], gdn_a_log[i],
                             gdn_dt_bias[i], gdn_norm_w[i])
        o_b = native_sparse_attention(n_q, n_kc, n_vc, n_ks, n_vs, n_kw, n_vw, n_gate,
                                      cmp_pos_k[i], cmp_w1_k[i], cmp_w2_k[i],
                                      cmp_pos_v[i], cmp_w1_v[i], cmp_w2_v[i])
        gate_a, gate_b = jnp.split(mix_gate, 2, axis=-1)
        mixed = (jax.nn.sigmoid(gate_a) * (o_a @ w_branch_gdn[i])
                 + jax.nn.sigmoid(gate_b) * (o_b @ w_branch_nsa[i]))
        h = layer_norm(DN_ALPHA * x + mixed @ w_out[i], ln1_g[i], ln1_b[i])
        u = causal_dwconv(h @ w_up[i], ffn_conv_w[i])
        u_g, u_v = jnp.split(u, 2, axis=-1)
        f = (jax.nn.silu(u_g) * u_v) @ w_down[i]
        x = layer_norm(DN_ALPHA * h + f, ln2_g[i], ln2_b[i])
    return x
```

```python
import functools
import math

import numpy as np
import jax
import jax.numpy as jnp
from jax import lax
from jax.experimental import pallas as pl
from jax.experimental.pallas import tpu as pltpu

F32 = jnp.float32
BF16 = jnp.bfloat16

D_MODEL = 1024
GDN_HEADS = 4
GDN_DK = 128
GDN_DV = 128
GDN_CONV = 4
GDN_CHUNK = 64
NSA_HEADS = 8
NSA_KV_HEADS = 2
NSA_REP = NSA_HEADS // NSA_KV_HEADS
NSA_DH = 64
CMP_BLOCK = 32
CMP_STRIDE = 16
CMP_HIDDEN = 256
SEL_BLOCK = 64
SEL_SHIFT = 6
SEL_TOPK = 16
WINDOW = 512
FORCE_SCORE = 1e9
D_FF = 2816
FFN_CONV = 3
DEPTH = 1
DN_ALPHA = (2 * DEPTH) ** 0.25
LN_EPS = 1e-5
RMS_EPS = 1e-6

GDN_QK = GDN_HEADS * GDN_DK
GDN_VW = GDN_HEADS * GDN_DV
NSA_QW = NSA_HEADS * NSA_DH
NSA_KVW = NSA_KV_HEADS * NSA_DH

LANE = 128
HEAD_PAD = LANE
NEG_BIG = -1e30
VMEM_LIMIT = 56 * 1024 * 1024

Q_BLOCK = 128
SEL_KEY_BLOCK = 256
FFN_CHUNK = 256
ROW_TILE = 512


def _cparams(sem):
    return pltpu.CompilerParams(dimension_semantics=sem, vmem_limit_bytes=VMEM_LIMIT)


def _dot(a, b):
    return jnp.dot(a.astype(BF16), b.astype(BF16), preferred_element_type=F32)


def _dot_nt(a, b):
    return lax.dot_general(a.astype(BF16), b.astype(BF16), (((1,), (1,)), ((), ())),
                           preferred_element_type=F32)


def _dot_tn(a, b):
    return lax.dot_general(a.astype(BF16), b.astype(BF16), (((0,), (0,)), ((), ())),
                           preferred_element_type=F32)


def _sigmoid(x):
    return jax.nn.sigmoid(x)


def _softplus(x):
    return jnp.maximum(x, 0.0) + jnp.log1p(jnp.exp(-jnp.abs(x)))


def _layer_norm(x, g, b):
    mu = jnp.mean(x, axis=-1, keepdims=True)
    xc = x - mu
    var = jnp.mean(xc * xc, axis=-1, keepdims=True)
    return xc * lax.rsqrt(var + LN_EPS) * g + b


def _full_spec(shape):
    nd = len(shape)
    return pl.BlockSpec(shape, lambda *_: (0,) * nd)


def _inproj_kernel(x_ref, *refs):
    n = len(refs) // 2
    xb = x_ref[...].astype(BF16)
    for w_ref, o_ref in zip(refs[:n], refs[n:]):
        o_ref[...] = jnp.dot(xb, w_ref[...], preferred_element_type=F32).astype(o_ref.dtype)


def _inproj(x2d, weights, out_dtypes):
    m = x2d.shape[0]
    tm = ROW_TILE
    in_specs = [pl.BlockSpec((tm, D_MODEL), lambda i: (i, 0))]
    in_specs += [_full_spec(w.shape) for w in weights]
    out_specs = [pl.BlockSpec((tm, w.shape[1]), lambda i: (i, 0)) for w in weights]
    out_shape = [jax.ShapeDtypeStruct((m, w.shape[1]), dt) for w, dt in zip(weights, out_dtypes)]
    return pl.pallas_call(
        _inproj_kernel, grid=(m // tm,), in_specs=in_specs, out_specs=out_specs, out_shape=out_shape,
        compiler_params=_cparams(("parallel",)), name="inproj")(x2d, *weights)


def _gdn_kernel(q_ref, k_ref, v_ref, z_ref, cwq_ref, cwk_ref, cwv_ref, gcol_ref, grow_ref,
                alog_ref, dtb_ref, nw_ref, o_ref, xpad, qh, kh, vh, *, seq):
    c = GDN_CHUNK
    pad = 8
    conv_rows = 256
    xpad[0:pad, :] = jnp.zeros((pad, 3 * LANE), F32)
    xpad[pad:, 0:LANE] = q_ref[...].astype(F32)
    xpad[pad:, LANE:2 * LANE] = k_ref[...].astype(F32)
    xpad[pad:, 2 * LANE:3 * LANE] = v_ref[...].astype(F32)
    cw = jnp.concatenate([cwq_ref[...], cwk_ref[...], cwv_ref[...]], axis=1)

    for t0 in range(0, seq, conv_rows):
        acc = jnp.zeros((conv_rows, 3 * LANE), F32)
        for j in range(GDN_CONV):
            xs = xpad[t0 + pad - (GDN_CONV - 1) + j:t0 + pad - (GDN_CONV - 1) + j + conv_rows, :]
            acc = acc + xs * cw[j:j + 1, :]
        y = acc * _sigmoid(acc)
        yq, yk, yv = y[:, 0:LANE], y[:, LANE:2 * LANE], y[:, 2 * LANE:3 * LANE]
        yq = yq * lax.rsqrt(jnp.sum(yq * yq, axis=-1, keepdims=True) + RMS_EPS) * (GDN_DK ** -0.5)
        yk = yk * lax.rsqrt(jnp.sum(yk * yk, axis=-1, keepdims=True) + RMS_EPS)
        qh[t0:t0 + conv_rows, :] = yq
        kh[t0:t0 + conv_rows, :] = yk
        vh[t0:t0 + conv_rows, :] = yv

    neg_a = -jnp.exp(alog_ref[0, :, 0:1])
    dtb = dtb_ref[0, :, 0:1]
    nw = nw_ref[...]
    ii = lax.broadcasted_iota(jnp.int32, (c, c), 0)
    jj = lax.broadcasted_iota(jnp.int32, (c, c), 1)
    incl = jj <= ii
    strict = jj < ii
    eye = (ii == jj).astype(F32)

    def chunk(ci, state):
        r0 = pl.multiple_of(ci * c, c)
        q = qh[pl.ds(r0, c), :]
        k = kh[pl.ds(r0, c), :]
        v = vh[pl.ds(r0, c), :]
        gcol = gcol_ref[0, 0, pl.ds(r0, c), :]
        grow = grow_ref[0, 0, ci]
        beta = _sigmoid(gcol[:, 0:1])
        g_c = neg_a * _softplus(gcol[:, 1:2] + dtb)
        g_r = neg_a * _softplus(grow[1:2, :] + dtb)
        gc_col = jnp.sum(jnp.where(incl, g_r, 0.0), axis=1, keepdims=True)
        gc_row = jnp.sum(jnp.where(ii <= jj, g_c, 0.0), axis=0, keepdims=True)
        g_tot = jnp.sum(g_c, axis=0, keepdims=True)
        decay = jnp.where(incl, jnp.exp(jnp.where(incl, gc_col - gc_row, 0.0)), 0.0)
        a_mat = jnp.where(strict, _dot_nt(k, k) * decay * beta, 0.0)
        x_inv = eye - a_mat
        p_pow = _dot(a_mat, a_mat)
        x_inv = x_inv + _dot(x_inv, p_pow)
        for _ in range(int(math.log2(c)) - 2):
            p_pow = _dot(p_pow, p_pow)
            x_inv = x_inv + _dot(x_inv, p_pow)
        e_gc = jnp.exp(gc_col)
        u0 = _dot(x_inv, v * beta)
        kcd = _dot(x_inv, k * (beta * e_gc))
        qk = _dot_nt(q, k) * decay
        q_dec = q * e_gc
        k_dec = k * jnp.exp(g_tot - gc_col)
        u = u0 - _dot(kcd, state)
        o = _dot(q_dec, state) + _dot(qk, u)
        state = state * jnp.exp(g_tot) + _dot_tn(k_dec, u)
        o = o * lax.rsqrt(jnp.mean(o * o, axis=-1, keepdims=True) + RMS_EPS) * nw
        z = z_ref[pl.ds(r0, c), :].astype(F32)
        o_ref[pl.ds(r0, c), :] = (o * (z * _sigmoid(z))).astype(o_ref.dtype)
        return state

    lax.fori_loop(0, seq // c, chunk, jnp.zeros((GDN_DK, GDN_DV), F32))


def _gdn(gqkvz, conv_w, gcol, grow, alog, dtb, nw, bsz, seq):
    h = GDN_HEADS
    blk = lambda off: pl.BlockSpec((seq, LANE), lambda b, hh: (b, off + hh))
    cwb = lambda off: pl.BlockSpec((GDN_CONV, LANE), lambda b, hh: (0, off + hh))
    in_specs = [blk(0), blk(h), blk(2 * h), blk(3 * h), cwb(0), cwb(h), cwb(2 * h),
                pl.BlockSpec((1, 1, seq, 2), lambda b, hh: (b, hh, 0, 0)),
                pl.BlockSpec((1, 1, seq // GDN_CHUNK, 2, GDN_CHUNK), lambda b, hh: (b, hh, 0, 0, 0)),
                pl.BlockSpec((1, 1, LANE), lambda b, hh: (hh, 0, 0)),
                pl.BlockSpec((1, 1, LANE), lambda b, hh: (hh, 0, 0)),
                pl.BlockSpec((1, LANE), lambda b, hh: (0, 0))]
    return pl.pallas_call(
        functools.partial(_gdn_kernel, seq=seq), grid=(bsz, h), in_specs=in_specs,
        out_specs=pl.BlockSpec((seq, LANE), lambda b, hh: (b, hh)),
        out_shape=jax.ShapeDtypeStruct((bsz * seq, GDN_VW), BF16),
        scratch_shapes=[pltpu.VMEM((seq + 8, 3 * LANE), F32), pltpu.VMEM((seq, LANE), F32),
                        pltpu.VMEM((seq, LANE), F32), pltpu.VMEM((seq, LANE), F32)],
        compiler_params=_cparams(("parallel", "parallel")), name="gdn",
    )(gqkvz, gqkvz, gqkvz, gqkvz, conv_w, conv_w, conv_w, gcol, grow, alog, dtb, nw)


def _gelu_tanh(x):
    return 0.5 * x * (1.0 + jnp.tanh(math.sqrt(2.0 / math.pi) * (x + 0.044715 * x * x * x)))


def _compress_kernel(k16_ref, v16_ref, wkc_ref, wvc_ref, w1k_ref, w1v_ref, pk_ref, pv_ref,
                     w2k_ref, w2v_ref, kc_ref, vc_ref):
    nblk = k16_ref.shape[1]
    for x_ref, wc_ref, w1_ref, p_ref, w2_ref, o_ref in (
            (k16_ref, wkc_ref, w1k_ref, pk_ref, w2k_ref, kc_ref),
            (v16_ref, wvc_ref, w1v_ref, pv_ref, w2v_ref, vc_ref)):
        y = jnp.dot(x_ref[0], wc_ref[...], preferred_element_type=F32)
        pos_term = jnp.dot(p_ref[...], w1_ref[...], preferred_element_type=F32)[0:1, :]
        for g in range(NSA_KV_HEADS):
            top = y[:, g * CMP_HIDDEN:(g + 1) * CMP_HIDDEN]
            bot = y[:, (NSA_KV_HEADS + g) * CMP_HIDDEN:(NSA_KV_HEADS + g + 1) * CMP_HIDDEN]
            hid = top + pltpu.roll(bot, nblk - 1, 0) + pos_term
            o_ref[0, g] = _dot(_gelu_tanh(hid), w2_ref[...]).astype(o_ref.dtype)


def _compress(k16, v16, wkc, wvc, w1k, w1v, pk, pv, w2k, w2v):
    bsz, nblk, width = k16.shape
    in_specs = [pl.BlockSpec((1, nblk, width), lambda b: (b, 0, 0))] * 2
    in_specs += [_full_spec(a.shape) for a in (wkc, wvc, w1k, w1v, pk, pv, w2k, w2v)]
    out_spec = pl.BlockSpec((1, NSA_KV_HEADS, nblk, HEAD_PAD), lambda b: (b, 0, 0, 0))
    out_shape = jax.ShapeDtypeStruct((bsz, NSA_KV_HEADS, nblk, HEAD_PAD), BF16)
    return pl.pallas_call(
        _compress_kernel, grid=(bsz,), in_specs=in_specs, out_specs=[out_spec, out_spec],
        out_shape=[out_shape, out_shape], compiler_params=_cparams(("parallel",)), name="nsa_compress",
    )(k16, v16, wkc, wvc, w1k, w1v, pk, pv, w2k, w2v)


def _nsa_kernel(q_ref, kv_ref, kc_ref, vc_ref, sm_ref, ov_ref, o_ref, *, seq):
    qb = pl.program_id(1)
    q0 = qb * Q_BLOCK
    rows = NSA_REP * Q_BLOCK
    scale = NSA_DH ** -0.5
    n_cmp_pad = kc_ref.shape[2]
    n_sel_blk = seq // SEL_BLOCK
    t_tok = q0 + lax.broadcasted_iota(jnp.int32, (Q_BLOCK, 1), 0)
    t_row = jnp.concatenate([t_tok] * NSA_REP, axis=0)
    small = sm_ref[...]

    for g in range(NSA_KV_HEADS):
        qs = jnp.concatenate(
            [q_ref[:, (g * NSA_REP + r) * HEAD_PAD:(g * NSA_REP + r + 1) * HEAD_PAD] for r in range(NSA_REP)],
            axis=0)

        s_c = _dot_nt(qs, kc_ref[0, g]) * scale
        n_idx = lax.broadcasted_iota(jnp.int32, (1, n_cmp_pad), 1)
        valid = (n_idx * CMP_STRIDE + (CMP_BLOCK - 1)) <= t_row
        m_c = jnp.max(jnp.where(valid, s_c, NEG_BIG), axis=-1, keepdims=True)
        p_c = jnp.where(valid, jnp.exp(s_c - m_c), 0.0)
        d_c = jnp.sum(p_c, axis=-1, keepdims=True)
        p_c = p_c * (1.0 / jnp.where(d_c > 0, d_c, 1.0))
        o_cmp = _dot(p_c, vc_ref[0, g])

        p_sum = p_c[0:Q_BLOCK]
        for r in range(1, NSA_REP):
            p_sum = p_sum + p_c[r * Q_BLOCK:(r + 1) * Q_BLOCK]
        imp = _dot(p_sum, ov_ref[...])
        j_idx = lax.broadcasted_iota(jnp.int32, (1, LANE), 1)
        cur = jnp.right_shift(t_tok, SEL_SHIFT)
        forced = (j_idx == 0) | (j_idx == cur) | (j_idx == cur - 1)
        causal_blk = (j_idx * SEL_BLOCK) <= t_tok
        imp = jnp.where(forced, FORCE_SCORE, jnp.where(causal_blk, imp, -jnp.inf))
        rank = jnp.zeros((Q_BLOCK, LANE), F32)
        for jp in range(n_sel_blk):
            col = imp[:, jp:jp + 1]
            ahead = (col > imp) | ((col == imp) & (j_idx > jp))
            rank = rank + ahead.astype(F32)
        sel = ((rank < SEL_TOPK) & (j_idx < n_sel_blk)).astype(BF16)

        def sel_step(kb, carry):
            m_i, l_i, acc = carry
            k0 = pl.multiple_of(kb * SEL_KEY_BLOCK, SEL_KEY_BLOCK)
            k_blk = kv_ref[pl.ds(k0, SEL_KEY_BLOCK), g * HEAD_PAD:(g + 1) * HEAD_PAD]
            v_blk = kv_ref[pl.ds(k0, SEL_KEY_BLOCK), (2 + g) * HEAD_PAD:(3 + g) * HEAD_PAD]
            s = _dot_nt(qs, k_blk) * scale
            eb = lax.broadcasted_iota(jnp.int32, (LANE, SEL_KEY_BLOCK), 0)
            ec = lax.broadcasted_iota(jnp.int32, (LANE, SEL_KEY_BLOCK), 1)
            expand = (eb == kb * (SEL_KEY_BLOCK // SEL_BLOCK) + jnp.right_shift(ec, SEL_SHIFT)).astype(BF16)
            sel_k = jnp.dot(sel, expand, preferred_element_type=F32)
            kpos = k0 + lax.broadcasted_iota(jnp.int32, (1, SEL_KEY_BLOCK), 1)
            bias = jnp.where((sel_k > 0.5) & (kpos <= t_tok), 0.0, NEG_BIG)
            s = s + jnp.concatenate([bias] * NSA_REP, axis=0)
            m_new = jnp.maximum(m_i, jnp.max(s, axis=-1, keepdims=True))
            alpha = jnp.exp(m_i - m_new)
            p = jnp.exp(s - m_new)
            l_new = alpha * l_i + jnp.sum(p, axis=-1, keepdims=True)
            acc_new = alpha * acc + _dot(p, v_blk)
            return m_new, l_new, acc_new

        n_kb = lax.div(q0 + (Q_BLOCK + SEL_KEY_BLOCK - 1), SEL_KEY_BLOCK)
        init = (jnp.full((rows, 1), NEG_BIG, F32), jnp.zeros((rows, 1), F32), jnp.zeros((rows, HEAD_PAD), F32))
        _, l_s, acc_s = lax.fori_loop(0, n_kb, sel_step, init)
        o_slc = acc_s * (1.0 / l_s)

        span = WINDOW + Q_BLOCK
        w0 = pl.multiple_of(jnp.maximum(q0 - WINDOW, 0), Q_BLOCK)
        k_w = kv_ref[pl.ds(w0, span), (4 + g) * HEAD_PAD:(5 + g) * HEAD_PAD]
        v_w = kv_ref[pl.ds(w0, span), (6 + g) * HEAD_PAD:(7 + g) * HEAD_PAD]
        s_w = _dot_nt(qs, k_w) * scale
        kpos_w = w0 + lax.broadcasted_iota(jnp.int32, (1, span), 1)
        ok_w = (kpos_w <= t_row) & (kpos_w > t_row - WINDOW)
        s_w = jnp.where(ok_w, s_w, NEG_BIG)
        m_w = jnp.max(s_w, axis=-1, keepdims=True)
        p_w = jnp.exp(s_w - m_w)
        l_w = jnp.sum(p_w, axis=-1, keepdims=True)
        o_win = _dot(p_w, v_w) * (1.0 / l_w)

        for r in range(NSA_REP):
            head = g * NSA_REP + r
            c0 = 8 + head * 3
            gate = _sigmoid(small[:, c0:c0 + 3])
            sl = slice(r * Q_BLOCK, (r + 1) * Q_BLOCK)
            o_h = gate[:, 0:1] * o_cmp[sl] + gate[:, 1:2] * o_slc[sl] + gate[:, 2:3] * o_win[sl]
            o_ref[:, head * HEAD_PAD:(head + 1) * HEAD_PAD] = o_h.astype(o_ref.dtype)


def _nsa(nq, nkv, kc, vc, small, overlap, bsz, seq):
    nqb = seq // Q_BLOCK
    in_specs = [pl.BlockSpec((Q_BLOCK, NSA_HEADS * HEAD_PAD), lambda b, i: (b * nqb + i, 0)),
                pl.BlockSpec((seq, nkv.shape[1]), lambda b, i: (b, 0)),
                pl.BlockSpec((1,) + kc.shape[1:], lambda b, i: (b, 0, 0, 0)),
                pl.BlockSpec((1,) + vc.shape[1:], lambda b, i: (b, 0, 0, 0)),
                pl.BlockSpec((Q_BLOCK, LANE), lambda b, i: (b * nqb + i, 0)),
                _full_spec(overlap.shape)]
    return pl.pallas_call(
        functools.partial(_nsa_kernel, seq=seq), grid=(bsz, nqb), in_specs=in_specs,
        out_specs=pl.BlockSpec((Q_BLOCK, NSA_HEADS * HEAD_PAD), lambda b, i: (b * nqb + i, 0)),
        out_shape=jax.ShapeDtypeStruct((bsz * seq, NSA_HEADS * HEAD_PAD), BF16),
        compiler_params=_cparams(("parallel", "parallel")), name="nsa_attention",
    )(nq, nkv, kc, vc, small, overlap)


def _merge_kernel(oa_ref, ob_ref, mix_ref, x_ref, wa_ref, wb_ref, wo_ref, g_ref, b_ref, h_ref):
    ya = jnp.dot(oa_ref[...], wa_ref[...], preferred_element_type=F32)
    yb = jnp.dot(ob_ref[...], wb_ref[...], preferred_element_type=F32)
    mix = mix_ref[...].astype(F32)
    mixed = _sigmoid(mix[:, 0:D_MODEL]) * ya + _sigmoid(mix[:, D_MODEL:2 * D_MODEL]) * yb
    y = _dot(mixed, wo_ref[...])
    h_ref[...] = _layer_norm(DN_ALPHA * x_ref[...] + y, g_ref[...], b_ref[...])


def _merge(oa, ob, mix, x2d, wa, wb, wo, g, b):
    m = x2d.shape[0]
    tm = ROW_TILE
    row = lambda n: pl.BlockSpec((tm, n), lambda i: (i, 0))
    in_specs = [row(oa.shape[1]), row(ob.shape[1]), row(mix.shape[1]), row(D_MODEL),
                _full_spec(wa.shape), _full_spec(wb.shape), _full_spec(wo.shape),
                _full_spec(g.shape), _full_spec(b.shape)]
    return pl.pallas_call(
        _merge_kernel, grid=(m // tm,), in_specs=in_specs, out_specs=row(D_MODEL),
        out_shape=jax.ShapeDtypeStruct((m, D_MODEL), F32),
        compiler_params=_cparams(("parallel",)), name="merge_ln",
    )(oa, ob, mix, x2d, wa, wb, wo, g, b)


def _ffn_kernel(h_ref, halo_ref, wg_ref, wv_ref, cg_ref, cv_ref, wd_ref, g_ref, b_ref, o_ref,
                hb, acc, ug, uv, *, seq, tm, halo):
    i = pl.program_id(0)
    j = pl.program_id(1)

    @pl.when(j == 0)
    def _():
        first = lax.rem(i * tm, seq) == 0
        hb[0:halo, :] = jnp.where(first, 0.0, halo_ref[...]).astype(BF16)
        hb[halo:, :] = h_ref[...].astype(BF16)
        acc[...] = jnp.zeros_like(acc)

    ug[...] = jnp.dot(hb[...], wg_ref[...], preferred_element_type=F32)
    uv[...] = jnp.dot(hb[...], wv_ref[...], preferred_element_type=F32)

    def conv(u_ref, c_ref):
        out = None
        for tap in range(FFN_CONV):
            term = c_ref[tap:tap + 1, :] * u_ref[pl.ds(halo - (FFN_CONV - 1) + tap, tm), :]
            out = term if out is None else out + term
        return out

    a = conv(ug, cg_ref)
    act = a * _sigmoid(a) * conv(uv, cv_ref)
    acc[...] += _dot(act, wd_ref[...])

    @pl.when(j == pl.num_programs(1) - 1)
    def _():
        o_ref[...] = _layer_norm(DN_ALPHA * h_ref[...] + acc[...], g_ref[...], b_ref[...])


def _ffn(h, w_up, conv_w, w_down, g, b, seq):
    m = h.shape[0]
    tm = ROW_TILE
    halo = 16
    nch = D_FF // FFN_CHUNK
    in_specs = [pl.BlockSpec((tm, D_MODEL), lambda i, j: (i, 0)),
                pl.BlockSpec((halo, D_MODEL), lambda i, j: (jnp.maximum(i * (tm // halo) - 1, 0), 0)),
                pl.BlockSpec((D_MODEL, FFN_CHUNK), lambda i, j: (0, j)),
                pl.BlockSpec((D_MODEL, FFN_CHUNK), lambda i, j: (0, nch + j)),
                pl.BlockSpec((FFN_CONV, FFN_CHUNK), lambda i, j: (0, j)),
                pl.BlockSpec((FFN_CONV, FFN_CHUNK), lambda i, j: (0, nch + j)),
                pl.BlockSpec((FFN_CHUNK, D_MODEL), lambda i, j: (j, 0)),
                _full_spec(g.shape), _full_spec(b.shape)]
    return pl.pallas_call(
        functools.partial(_ffn_kernel, seq=seq, tm=tm, halo=halo), grid=(m // tm, nch),
        in_specs=in_specs, out_specs=pl.BlockSpec((tm, D_MODEL), lambda i, j: (i, 0)),
        out_shape=jax.ShapeDtypeStruct((m, D_MODEL), F32),
        scratch_shapes=[pltpu.VMEM((tm + halo, D_MODEL), BF16), pltpu.VMEM((tm, D_MODEL), F32),
                        pltpu.VMEM((tm + halo, FFN_CHUNK), F32), pltpu.VMEM((tm + halo, FFN_CHUNK), F32)],
        compiler_params=_cparams(("parallel", "arbitrary")), name="conv_ffn_ln",
    )(h, h, w_up, w_up, conv_w, conv_w, w_down, g, b)


def _pad_heads(w, n_heads):
    lead = w.shape[:-1]
    w = w.reshape(lead + (n_heads, NSA_DH))
    w = jnp.pad(w, [(0, 0)] * len(lead) + [(0, 0), (0, HEAD_PAD - NSA_DH)])
    return w.reshape(lead + (n_heads * HEAD_PAD,))


def _compress_weights(w1):
    half = CMP_BLOCK // 2
    w1r = w1.reshape(2, half, NSA_DH, CMP_HIDDEN)
    eye = jnp.eye(NSA_KV_HEADS, dtype=w1.dtype)
    ex = jnp.einsum("aidh,gk->agikdh", w1r, eye)
    ex = ex.reshape(2 * NSA_KV_HEADS, half * NSA_KVW, CMP_HIDDEN)
    return jnp.concatenate([ex[n] for n in range(2 * NSA_KV_HEADS)], axis=1)


def _overlap_matrix(seq):
    n_cmp = (seq - CMP_BLOCK) // CMP_STRIDE + 1
    n_sel = seq // SEL_BLOCK
    starts = np.arange(n_cmp) * CMP_STRIDE
    jb = np.arange(n_sel) * SEL_BLOCK
    ov = ((starts[:, None] < jb[None] + SEL_BLOCK) & (starts[:, None] + CMP_BLOCK > jb[None])).astype(np.float32)
    out = np.zeros((seq // CMP_STRIDE, LANE), np.float32)
    out[:n_cmp, :n_sel] = ov
    return jnp.asarray(out, BF16)


def kernel(x, w_in, gdn_conv_w, gdn_a_log, gdn_dt_bias, gdn_norm_w, cmp_pos_k, cmp_w1_k, cmp_w2_k,
           cmp_pos_v, cmp_w1_v, cmp_w2_v, w_branch_gdn, w_branch_nsa, w_out, ln1_g, ln1_b, w_up,
           ffn_conv_w, w_down, ln2_g, ln2_b):
    bsz, seq, _ = x.shape
    m = bsz * seq
    for i in range(DEPTH):
        x2d = x.reshape(m, D_MODEL)
        w = w_in[i]
        o_gdn = 2 * GDN_QK + 2 * GDN_VW
        o_nq = o_gdn + 2 * GDN_HEADS
        o_kv = o_nq + NSA_QW
        o_gate = o_kv + 6 * NSA_KVW
        o_mix = o_gate + 3 * NSA_HEADS
        kv = [w[:, o_kv + n * NSA_KVW:o_kv + (n + 1) * NSA_KVW] for n in range(6)]
        small_w = jnp.concatenate([w[:, o_gdn:o_nq], w[:, o_gate:o_mix]], axis=1)
        small_w = jnp.pad(small_w, ((0, 0), (0, LANE - small_w.shape[1])))
        weights = [w[:, :o_gdn],
                   _pad_heads(w[:, o_nq:o_kv], NSA_HEADS),
                   jnp.concatenate([_pad_heads(t, NSA_KV_HEADS) for t in kv[2:]], axis=1),
                   jnp.concatenate(kv[:2], axis=1),
                   w[:, o_mix:],
                   small_w]
        weights = [t.astype(BF16) for t in weights]
        gqkvz, nq, nkv, kvc, mix, small = _inproj(x2d, weights, [BF16, BF16, BF16, BF16, BF16, F32])

        nchunk = seq // GDN_CHUNK
        gates = small[:, :2 * GDN_HEADS].reshape(bsz, seq, 2, GDN_HEADS)
        gcol = gates.transpose(0, 3, 1, 2)
        grow = gcol.reshape(bsz, GDN_HEADS, nchunk, GDN_CHUNK, 2).transpose(0, 1, 2, 4, 3)
        alog = jnp.broadcast_to(gdn_a_log[i][:, None, None], (GDN_HEADS, 1, LANE))
        dtb = jnp.broadcast_to(gdn_dt_bias[i][:, None, None], (GDN_HEADS, 1, LANE))
        o_a = _gdn(gqkvz, gdn_conv_w[i], gcol, grow, alog, dtb, gdn_norm_w[i][None, :], bsz, seq)

        grp = seq // CMP_STRIDE
        k16 = kvc[:, :NSA_KVW].reshape(bsz, grp, CMP_STRIDE * NSA_KVW)
        v16 = kvc[:, NSA_KVW:].reshape(bsz, grp, CMP_STRIDE * NSA_KVW)
        pos_rows = lambda p: jnp.pad(p.reshape(1, CMP_BLOCK * NSA_DH), ((0, 7), (0, 0))).astype(BF16)
        w2_pad = lambda t: jnp.pad(t, ((0, 0), (0, HEAD_PAD - NSA_DH))).astype(BF16)
        kc, vc = _compress(k16, v16,
                           _compress_weights(cmp_w1_k[i]).astype(BF16), _compress_weights(cmp_w1_v[i]).astype(BF16),
                           cmp_w1_k[i].astype(BF16), cmp_w1_v[i].astype(BF16),
                           pos_rows(cmp_pos_k[i]), pos_rows(cmp_pos_v[i]),
                           w2_pad(cmp_w2_k[i]), w2_pad(cmp_w2_v[i]))
        o_b = _nsa(nq, nkv, kc, vc, small, _overlap_matrix(seq), bsz, seq)

        wb = jnp.pad(w_branch_nsa[i].reshape(NSA_HEADS, NSA_DH, D_MODEL),
                     ((0, 0), (0, HEAD_PAD - NSA_DH), (0, 0))).reshape(NSA_HEADS * HEAD_PAD, D_MODEL)
        h = _merge(o_a, o_b, mix, x2d, w_branch_gdn[i].astype(BF16), wb.astype(BF16), w_out[i].astype(BF16),
                   ln1_g[i][None, :], ln1_b[i][None, :])

        x = _ffn(h, w_up[i].astype(BF16), ffn_conv_w[i], w_down[i].astype(BF16),
                 ln2_g[i][None, :], ln2_b[i][None, :], seq).reshape(bsz, seq, D_MODEL)
    return x
```

```python
import functools
import math

import numpy as np
import jax
import jax.numpy as jnp
from jax import lax
from jax.experimental import pallas as pl
from jax.experimental.pallas import tpu as pltpu

F32 = jnp.float32
BF16 = jnp.bfloat16

D_MODEL = 1024
GDN_HEADS = 4
GDN_DK = 128
GDN_DV = 128
GDN_CONV = 4
GDN_CHUNK = 64
NSA_HEADS = 8
NSA_KV_HEADS = 2
NSA_REP = NSA_HEADS // NSA_KV_HEADS
NSA_DH = 64
CMP_BLOCK = 32
CMP_STRIDE = 16
CMP_HIDDEN = 256
SEL_BLOCK = 64
SEL_SHIFT = 6
SEL_TOPK = 16
WINDOW = 512
FORCE_SCORE = 1e9
D_FF = 2816
FFN_CONV = 3
DEPTH = 1
DN_ALPHA = (2 * DEPTH) ** 0.25
LN_EPS = 1e-5
RMS_EPS = 1e-6

GDN_QK = GDN_HEADS * GDN_DK
GDN_VW = GDN_HEADS * GDN_DV
NSA_QW = NSA_HEADS * NSA_DH
NSA_KVW = NSA_KV_HEADS * NSA_DH

LANE = 128
HEAD_PAD = LANE
NEG_BIG = -1e30
VMEM_LIMIT = 56 * 1024 * 1024

Q_BLOCK = 128
SEL_KEY_BLOCK = 256
FFN_CHUNK = 256
ROW_TILE = 512
GDN_CHUNK_SHIFT = 6
GDN_PAIR = 2 * GDN_CHUNK
GDN_UNROLL = 4


def _cparams(sem):
    return pltpu.CompilerParams(dimension_semantics=sem, vmem_limit_bytes=VMEM_LIMIT)


def _dot(a, b):
    return jnp.dot(a.astype(BF16), b.astype(BF16), preferred_element_type=F32)


def _dot_nt(a, b):
    return lax.dot_general(a.astype(BF16), b.astype(BF16), (((1,), (1,)), ((), ())),
                           preferred_element_type=F32)


def _dot_tn(a, b):
    return lax.dot_general(a.astype(BF16), b.astype(BF16), (((0,), (0,)), ((), ())),
                           preferred_element_type=F32)


def _sigmoid(x):
    return jax.nn.sigmoid(x)


def _softplus(x):
    return jnp.maximum(x, 0.0) + jnp.log1p(jnp.exp(-jnp.abs(x)))


def _layer_norm(x, g, b):
    mu = jnp.mean(x, axis=-1, keepdims=True)
    xc = x - mu
    var = jnp.mean(xc * xc, axis=-1, keepdims=True)
    return xc * lax.rsqrt(var + LN_EPS) * g + b


def _full_spec(shape):
    nd = len(shape)
    return pl.BlockSpec(shape, lambda *_: (0,) * nd)


def _inproj_kernel(x_ref, *refs):
    n = len(refs) // 2
    xb = x_ref[...].astype(BF16)
    for w_ref, o_ref in zip(refs[:n], refs[n:]):
        o_ref[...] = jnp.dot(xb, w_ref[...], preferred_element_type=F32).astype(o_ref.dtype)


def _inproj(x2d, weights, out_dtypes):
    m = x2d.shape[0]
    tm = ROW_TILE
    in_specs = [pl.BlockSpec((tm, D_MODEL), lambda i: (i, 0))]
    in_specs += [_full_spec(w.shape) for w in weights]
    out_specs = [pl.BlockSpec((tm, w.shape[1]), lambda i: (i, 0)) for w in weights]
    out_shape = [jax.ShapeDtypeStruct((m, w.shape[1]), dt) for w, dt in zip(weights, out_dtypes)]
    return pl.pallas_call(
        _inproj_kernel, grid=(m // tm,), in_specs=in_specs, out_specs=out_specs, out_shape=out_shape,
        compiler_params=_cparams(("parallel",)), name="inproj")(x2d, *weights)


def _gdn_kernel(q_ref, k_ref, v_ref, z_ref, cwq_ref, cwk_ref, cwv_ref, gcol_ref, grow_ref,
                alog_ref, dtb_ref, nw_ref, o_ref,
                xpad, qh, kh, vh, xp, qk_s, rhs, kd, qd, gl, sol, mq, ns, o0, st, *, seq):
    c = GDN_CHUNK
    pr = GDN_PAIR
    npair = seq // pr
    head = pl.program_id(1)
    pad = 8
    conv_rows = 256
    xpad[0:pad, :] = jnp.zeros((pad, 3 * LANE), F32)
    xpad[pad:, 0:LANE] = q_ref[...].astype(F32)
    xpad[pad:, LANE:2 * LANE] = k_ref[...].astype(F32)
    xpad[pad:, 2 * LANE:3 * LANE] = v_ref[...].astype(F32)
    cw = jnp.concatenate([cwq_ref[...], cwk_ref[...], cwv_ref[...]], axis=1)

    for t0 in range(0, seq, conv_rows):
        acc = jnp.zeros((conv_rows, 3 * LANE), F32)
        for j in range(GDN_CONV):
            xs = xpad[t0 + pad - (GDN_CONV - 1) + j:t0 + pad - (GDN_CONV - 1) + j + conv_rows, :]
            acc = acc + xs * cw[j:j + 1, :]
        y = acc * _sigmoid(acc)
        yq, yk, yv = y[:, 0:LANE], y[:, LANE:2 * LANE], y[:, 2 * LANE:3 * LANE]
        yq = yq * lax.rsqrt(jnp.sum(yq * yq, axis=-1, keepdims=True) + RMS_EPS) * (GDN_DK ** -0.5)
        yk = yk * lax.rsqrt(jnp.sum(yk * yk, axis=-1, keepdims=True) + RMS_EPS)
        qh[t0:t0 + conv_rows, :] = yq
        kh[t0:t0 + conv_rows, :] = yk
        vh[t0:t0 + conv_rows, :] = yv

    neg_a = -jnp.exp(alog_ref[0, :, 0:1])
    dtb = dtb_ref[0, :, 0:1]
    ii = lax.broadcasted_iota(jnp.int32, (pr, pr), 0)
    jj = lax.broadcasted_iota(jnp.int32, (pr, pr), 1)
    same = jnp.right_shift(ii, GDN_CHUNK_SHIFT) == jnp.right_shift(jj, GDN_CHUNK_SHIFT)
    incl = same & (jj <= ii)
    strict = same & (jj < ii)
    upper = same & (ii <= jj)
    eye = (ii == jj).astype(F32)
    sub8 = lax.broadcasted_iota(jnp.int32, (8, LANE), 0)

    def over_pairs(body):
        def run(gi, carry):
            for uu in range(GDN_UNROLL):
                body(gi * GDN_UNROLL + uu)
            return carry
        lax.fori_loop(0, npair // GDN_UNROLL, run, 0)

    def stage_gates(p):
        rows = pl.ds(pl.multiple_of(p * pr, pr), pr)
        q = qh[rows, :]
        k = kh[rows, :]
        v = vh[rows, :]
        gcol = gcol_ref[0, 0, rows, :]
        grow = grow_ref[0, 0, p]
        beta = _sigmoid(gcol[:, 0:1])
        g_c = neg_a * _softplus(gcol[:, 1:2] + dtb)
        g_r = neg_a * _softplus(grow[1:2, :] + dtb)
        gc_col = jnp.sum(jnp.where(incl, g_r, 0.0), axis=1, keepdims=True)
        gc_row = jnp.sum(jnp.where(upper, g_c, 0.0), axis=0, keepdims=True)
        gt_col = jnp.sum(jnp.where(same, g_r, 0.0), axis=1, keepdims=True)
        decay = jnp.where(incl, jnp.exp(jnp.where(incl, gc_col - gc_row, 0.0)), 0.0)
        kq_k = _dot_nt(jnp.concatenate([k, q], axis=0), k)
        a_mat = jnp.where(strict, kq_k[0:pr] * decay * beta, 0.0)
        xp[p, 0:pr, :] = eye - a_mat
        xp[p, pr:2 * pr, :] = _dot(a_mat, a_mat)
        qk_s[p] = (kq_k[pr:2 * pr] * decay).astype(qk_s.dtype)
        e_gc = jnp.exp(gc_col)
        rhs[rows, 0:GDN_DV] = (v * beta).astype(rhs.dtype)
        rhs[rows, GDN_DV:GDN_DV + GDN_DK] = (k * (beta * e_gc)).astype(rhs.dtype)
        kd[rows, :] = (k * jnp.exp(gt_col - gc_col)).astype(kd.dtype)
        qd[rows, :] = q * e_gc
        g_last = jnp.exp(gt_col)
        gl[p] = jnp.where(sub8 == 0, g_last[0:1, :], g_last[c:c + 1, :])

    def stage_double(p):
        y = _dot(xp[p], xp[p, pr:2 * pr, :])
        xp[p, 0:pr, :] = xp[p, 0:pr, :] + y[0:pr]
        xp[p, pr:2 * pr, :] = y[pr:2 * pr]

    def stage_solve(p):
        rows = pl.ds(pl.multiple_of(p * pr, pr), pr)
        x_inv = xp[p, 0:pr, :]
        x_inv = x_inv + _dot(x_inv, xp[p, pr:2 * pr, :])
        sol[rows, :] = _dot(x_inv, rhs[rows, :])

    def stage_affine(p):
        r0 = pl.multiple_of(p * pr, pr)
        rows = pl.ds(r0, pr)
        e = _dot(qk_s[p], sol[rows, :])
        qe = qd[rows, :] - e[:, GDN_DV:GDN_DV + GDN_DK]
        g_rows = gl[p]
        for cc in range(pr // c):
            rc = pl.ds(r0 + cc * c, c)
            mn = _dot_tn(kd[rc, :], sol[rc, :])
            idx = p * (pr // c) + cc
            mq[head, idx, 0:GDN_DK, :] = (eye * g_rows[cc:cc + 1, :] - mn[:, GDN_DV:GDN_DV + GDN_DK]).astype(mq.dtype)
            mq[head, idx, GDN_DK:GDN_DK + c, :] = qe[cc * c:(cc + 1) * c].astype(mq.dtype)
            ns[head, idx] = mn[:, 0:GDN_DV]
            o0[head, idx] = e[cc * c:(cc + 1) * c, 0:GDN_DV]

    over_pairs(stage_gates)
    for _ in range(int(math.log2(c)) - 2):
        over_pairs(stage_double)
    over_pairs(stage_solve)
    over_pairs(stage_affine)

    @pl.when(head == GDN_HEADS - 1)
    def _():
        st[...] = jnp.zeros_like(st)
        nw = nw_ref[...]

        def step(ci, carry):
            r0 = pl.multiple_of(ci * c, c)
            for h in range(GDN_HEADS):
                cols = slice(h * LANE, (h + 1) * LANE)
                ys = jnp.dot(mq[h, ci], st[h], preferred_element_type=F32)
                st[h] = (ys[0:GDN_DK] + ns[h, ci]).astype(st.dtype)
                o = ys[GDN_DK:GDN_DK + c] + o0[h, ci]
                o = o * lax.rsqrt(jnp.mean(o * o, axis=-1, keepdims=True) + RMS_EPS) * nw
                z = z_ref[pl.ds(r0, c), cols].astype(F32)
                o_ref[pl.ds(r0, c), cols] = (o * (z * _sigmoid(z))).astype(o_ref.dtype)
            return carry

        lax.fori_loop(0, seq // c, step, 0)


def _gdn(gqkvz, conv_w, gcol, grow, alog, dtb, nw, bsz, seq):
    assert GDN_DK == GDN_PAIR == GDN_DV == LANE
    h = GDN_HEADS
    npair = seq // GDN_PAIR
    nchunk = seq // GDN_CHUNK
    blk = lambda off: pl.BlockSpec((seq, LANE), lambda b, hh: (b, off + hh))
    cwb = lambda off: pl.BlockSpec((GDN_CONV, LANE), lambda b, hh: (0, off + hh))
    in_specs = [blk(0), blk(h), blk(2 * h),
                pl.BlockSpec((seq, GDN_VW), lambda b, hh: (b, 3)),
                cwb(0), cwb(h), cwb(2 * h),
                pl.BlockSpec((1, 1, seq, 2), lambda b, hh: (b, hh, 0, 0)),
                pl.BlockSpec((1, 1, npair, 2, GDN_PAIR), lambda b, hh: (b, hh, 0, 0, 0)),
                pl.BlockSpec((1, 1, LANE), lambda b, hh: (hh, 0, 0)),
                pl.BlockSpec((1, 1, LANE), lambda b, hh: (hh, 0, 0)),
                pl.BlockSpec((1, LANE), lambda b, hh: (0, 0))]
    scratch = [pltpu.VMEM((seq + 8, 3 * LANE), F32),
               pltpu.VMEM((seq, LANE), F32), pltpu.VMEM((seq, LANE), F32), pltpu.VMEM((seq, LANE), F32),
               pltpu.VMEM((npair, 2 * GDN_PAIR, GDN_PAIR), F32),
               pltpu.VMEM((npair, GDN_PAIR, GDN_PAIR), BF16),
               pltpu.VMEM((seq, GDN_DV + GDN_DK), BF16),
               pltpu.VMEM((seq, GDN_DK), BF16),
               pltpu.VMEM((seq, GDN_DK), F32),
               pltpu.VMEM((npair, 8, LANE), F32),
               pltpu.VMEM((seq, GDN_DV + GDN_DK), F32),
               pltpu.VMEM((h, nchunk, GDN_DK + GDN_CHUNK, GDN_DK), BF16),
               pltpu.VMEM((h, nchunk, GDN_DK, GDN_DV), F32),
               pltpu.VMEM((h, nchunk, GDN_CHUNK, GDN_DV), F32),
               pltpu.VMEM((h, GDN_DK, GDN_DV), BF16)]
    return pl.pallas_call(
        functools.partial(_gdn_kernel, seq=seq), grid=(bsz, h), in_specs=in_specs,
        out_specs=pl.BlockSpec((seq, GDN_VW), lambda b, hh: (b, 0)),
        out_shape=jax.ShapeDtypeStruct((bsz * seq, GDN_VW), BF16),
        scratch_shapes=scratch,
        compiler_params=_cparams(("parallel", "arbitrary")), name="gdn",
    )(gqkvz, gqkvz, gqkvz, gqkvz, conv_w, conv_w, conv_w, gcol, grow, alog, dtb, nw)


def _gelu_tanh(x):
    return 0.5 * x * (1.0 + jnp.tanh(math.sqrt(2.0 / math.pi) * (x + 0.044715 * x * x * x)))


def _compress_kernel(k16_ref, v16_ref, wkc_ref, wvc_ref, w1k_ref, w1v_ref, pk_ref, pv_ref,
                     w2k_ref, w2v_ref, kc_ref, vc_ref):
    nblk = k16_ref.shape[1]
    for x_ref, wc_ref, w1_ref, p_ref, w2_ref, o_ref in (
            (k16_ref, wkc_ref, w1k_ref, pk_ref, w2k_ref, kc_ref),
            (v16_ref, wvc_ref, w1v_ref, pv_ref, w2v_ref, vc_ref)):
        y = jnp.dot(x_ref[0], wc_ref[...], preferred_element_type=F32)
        pos_term = jnp.dot(p_ref[...], w1_ref[...], preferred_element_type=F32)[0:1, :]
        for g in range(NSA_KV_HEADS):
            top = y[:, g * CMP_HIDDEN:(g + 1) * CMP_HIDDEN]
            bot = y[:, (NSA_KV_HEADS + g) * CMP_HIDDEN:(NSA_KV_HEADS + g + 1) * CMP_HIDDEN]
            hid = top + pltpu.roll(bot, nblk - 1, 0) + pos_term
            o_ref[0, g] = _dot(_gelu_tanh(hid), w2_ref[...]).astype(o_ref.dtype)


def _compress(k16, v16, wkc, wvc, w1k, w1v, pk, pv, w2k, w2v):
    bsz, nblk, width = k16.shape
    in_specs = [pl.BlockSpec((1, nblk, width), lambda b: (b, 0, 0))] * 2
    in_specs += [_full_spec(a.shape) for a in (wkc, wvc, w1k, w1v, pk, pv, w2k, w2v)]
    out_spec = pl.BlockSpec((1, NSA_KV_HEADS, nblk, HEAD_PAD), lambda b: (b, 0, 0, 0))
    out_shape = jax.ShapeDtypeStruct((bsz, NSA_KV_HEADS, nblk, HEAD_PAD), BF16)
    return pl.pallas_call(
        _compress_kernel, grid=(bsz,), in_specs=in_specs, out_specs=[out_spec, out_spec],
        out_shape=[out_shape, out_shape], compiler_params=_cparams(("parallel",)), name="nsa_compress",
    )(k16, v16, wkc, wvc, w1k, w1v, pk, pv, w2k, w2v)


def _nsa_kernel(q_ref, kv_ref, kc_ref, vc_ref, sm_ref, ov_ref, o_ref, *, seq):
    qb = pl.program_id(1)
    q0 = qb * Q_BLOCK
    rows = NSA_REP * Q_BLOCK
    scale = NSA_DH ** -0.5
    n_cmp_pad = kc_ref.shape[2]
    n_sel_blk = seq // SEL_BLOCK
    t_tok = q0 + lax.broadcasted_iota(jnp.int32, (Q_BLOCK, 1), 0)
    t_row = jnp.concatenate([t_tok] * NSA_REP, axis=0)
    small = sm_ref[...]

    for g in range(NSA_KV_HEADS):
        qs = jnp.concatenate(
            [q_ref[:, (g * NSA_REP + r) * HEAD_PAD:(g * NSA_REP + r + 1) * HEAD_PAD] for r in range(NSA_REP)],
            axis=0)

        s_c = _dot_nt(qs, kc_ref[0, g]) * scale
        n_idx = lax.broadcasted_iota(jnp.int32, (1, n_cmp_pad), 1)
        valid = (n_idx * CMP_STRIDE + (CMP_BLOCK - 1)) <= t_row
        m_c = jnp.max(jnp.where(valid, s_c, NEG_BIG), axis=-1, keepdims=True)
        p_c = jnp.where(valid, jnp.exp(s_c - m_c), 0.0)
        d_c = jnp.sum(p_c, axis=-1, keepdims=True)
        p_c = p_c * (1.0 / jnp.where(d_c > 0, d_c, 1.0))
        o_cmp = _dot(p_c, vc_ref[0, g])

        p_sum = p_c[0:Q_BLOCK]
        for r in range(1, NSA_REP):
            p_sum = p_sum + p_c[r * Q_BLOCK:(r + 1) * Q_BLOCK]
        imp = _dot(p_sum, ov_ref[...])
        j_idx = lax.broadcasted_iota(jnp.int32, (1, LANE), 1)
        cur = jnp.right_shift(t_tok, SEL_SHIFT)
        forced = (j_idx == 0) | (j_idx == cur) | (j_idx == cur - 1)
        causal_blk = (j_idx * SEL_BLOCK) <= t_tok
        imp = jnp.where(forced, FORCE_SCORE, jnp.where(causal_blk, imp, -jnp.inf))
        rank = jnp.zeros((Q_BLOCK, LANE), F32)
        for jp in range(n_sel_blk):
            col = imp[:, jp:jp + 1]
            ahead = (col > imp) | ((col == imp) & (j_idx > jp))
            rank = rank + ahead.astype(F32)
        sel = ((rank < SEL_TOPK) & (j_idx < n_sel_blk)).astype(BF16)

        def sel_step(kb, carry):
            m_i, l_i, acc = carry
            k0 = pl.multiple_of(kb * SEL_KEY_BLOCK, SEL_KEY_BLOCK)
            k_blk = kv_ref[pl.ds(k0, SEL_KEY_BLOCK), g * HEAD_PAD:(g + 1) * HEAD_PAD]
            v_blk = kv_ref[pl.ds(k0, SEL_KEY_BLOCK), (2 + g) * HEAD_PAD:(3 + g) * HEAD_PAD]
            s = _dot_nt(qs, k_blk) * scale
            eb = lax.broadcasted_iota(jnp.int32, (LANE, SEL_KEY_BLOCK), 0)
            ec = lax.broadcasted_iota(jnp.int32, (LANE, SEL_KEY_BLOCK), 1)
            expand = (eb == kb * (SEL_KEY_BLOCK // SEL_BLOCK) + jnp.right_shift(ec, SEL_SHIFT)).astype(BF16)
            sel_k = jnp.dot(sel, expand, preferred_element_type=F32)
            kpos = k0 + lax.broadcasted_iota(jnp.int32, (1, SEL_KEY_BLOCK), 1)
            bias = jnp.where((sel_k > 0.5) & (kpos <= t_tok), 0.0, NEG_BIG)
            s = s + jnp.concatenate([bias] * NSA_REP, axis=0)
            m_new = jnp.maximum(m_i, jnp.max(s, axis=-1, keepdims=True))
            alpha = jnp.exp(m_i - m_new)
            p = jnp.exp(s - m_new)
            l_new = alpha * l_i + jnp.sum(p, axis=-1, keepdims=True)
            acc_new = alpha * acc + _dot(p, v_blk)
            return m_new, l_new, acc_new

        n_kb = lax.div(q0 + (Q_BLOCK + SEL_KEY_BLOCK - 1), SEL_KEY_BLOCK)
        init = (jnp.full((rows, 1), NEG_BIG, F32), jnp.zeros((rows, 1), F32), jnp.zeros((rows, HEAD_PAD), F32))
        _, l_s, acc_s = lax.fori_loop(0, n_kb, sel_step, init)
        o_slc = acc_s * (1.0 / l_s)

        span = WINDOW + Q_BLOCK
        w0 = pl.multiple_of(jnp.maximum(q0 - WINDOW, 0), Q_BLOCK)
        k_w = kv_ref[pl.ds(w0, span), (4 + g) * HEAD_PAD:(5 + g) * HEAD_PAD]
        v_w = kv_ref[pl.ds(w0, span), (6 + g) * HEAD_PAD:(7 + g) * HEAD_PAD]
        s_w = _dot_nt(qs, k_w) * scale
        kpos_w = w0 + lax.broadcasted_iota(jnp.int32, (1, span), 1)
        ok_w = (kpos_w <= t_row) & (kpos_w > t_row - WINDOW)
        s_w = jnp.where(ok_w, s_w, NEG_BIG)
        m_w = jnp.max(s_w, axis=-1, keepdims=True)
        p_w = jnp.exp(s_w - m_w)
        l_w = jnp.sum(p_w, axis=-1, keepdims=True)
        o_win = _dot(p_w, v_w) * (1.0 / l_w)

        for r in range(NSA_REP):
            head = g * NSA_REP + r
            c0 = 8 + head * 3
            gate = _sigmoid(small[:, c0:c0 + 3])
            sl = slice(r * Q_BLOCK, (r + 1) * Q_BLOCK)
            o_h = gate[:, 0:1] * o_cmp[sl] + gate[:, 1:2] * o_slc[sl] + gate[:, 2:3] * o_win[sl]
            o_ref[:, head * HEAD_PAD:(head + 1) * HEAD_PAD] = o_h.astype(o_ref.dtype)


def _nsa(nq, nkv, kc, vc, small, overlap, bsz, seq):
    nqb = seq // Q_BLOCK
    in_specs = [pl.BlockSpec((Q_BLOCK, NSA_HEADS * HEAD_PAD), lambda b, i: (b * nqb + i, 0)),
                pl.BlockSpec((seq, nkv.shape[1]), lambda b, i: (b, 0)),
                pl.BlockSpec((1,) + kc.shape[1:], lambda b, i: (b, 0, 0, 0)),
                pl.BlockSpec((1,) + vc.shape[1:], lambda b, i: (b, 0, 0, 0)),
                pl.BlockSpec((Q_BLOCK, LANE), lambda b, i: (b * nqb + i, 0)),
                _full_spec(overlap.shape)]
    return pl.pallas_call(
        functools.partial(_nsa_kernel, seq=seq), grid=(bsz, nqb), in_specs=in_specs,
        out_specs=pl.BlockSpec((Q_BLOCK, NSA_HEADS * HEAD_PAD), lambda b, i: (b * nqb + i, 0)),
        out_shape=jax.ShapeDtypeStruct((bsz * seq, NSA_HEADS * HEAD_PAD), BF16),
        compiler_params=_cparams(("parallel", "parallel")), name="nsa_attention",
    )(nq, nkv, kc, vc, small, overlap)


def _merge_kernel(oa_ref, ob_ref, mix_ref, x_ref, wa_ref, wb_ref, wo_ref, g_ref, b_ref, h_ref):
    ya = jnp.dot(oa_ref[...], wa_ref[...], preferred_element_type=F32)
    yb = jnp.dot(ob_ref[...], wb_ref[...], preferred_element_type=F32)
    mix = mix_ref[...].astype(F32)
    mixed = _sigmoid(mix[:, 0:D_MODEL]) * ya + _sigmoid(mix[:, D_MODEL:2 * D_MODEL]) * yb
    y = _dot(mixed, wo_ref[...])
    h_ref[...] = _layer_norm(DN_ALPHA * x_ref[...] + y, g_ref[...], b_ref[...])


def _merge(oa, ob, mix, x2d, wa, wb, wo, g, b):
    m = x2d.shape[0]
    tm = ROW_TILE
    row = lambda n: pl.BlockSpec((tm, n), lambda i: (i, 0))
    in_specs = [row(oa.shape[1]), row(ob.shape[1]), row(mix.shape[1]), row(D_MODEL),
                _full_spec(wa.shape), _full_spec(wb.shape), _full_spec(wo.shape),
                _full_spec(g.shape), _full_spec(b.shape)]
    return pl.pallas_call(
        _merge_kernel, grid=(m // tm,), in_specs=in_specs, out_specs=row(D_MODEL),
        out_shape=jax.ShapeDtypeStruct((m, D_MODEL), F32),
        compiler_params=_cparams(("parallel",)), name="merge_ln",
    )(oa, ob, mix, x2d, wa, wb, wo, g, b)


def _ffn_kernel(h_ref, halo_ref, wg_ref, wv_ref, cg_ref, cv_ref, wd_ref, g_ref, b_ref, o_ref,
                hb, acc, ug, uv, *, seq, tm, halo):
    i = pl.program_id(0)
    j = pl.program_id(1)

    @pl.when(j == 0)
    def _():
        first = lax.rem(i * tm, seq) == 0
        hb[0:halo, :] = jnp.where(first, 0.0, halo_ref[...]).astype(BF16)
        hb[halo:, :] = h_ref[...].astype(BF16)
        acc[...] = jnp.zeros_like(acc)

    ug[...] = jnp.dot(hb[...], wg_ref[...], preferred_element_type=F32)
    uv[...] = jnp.dot(hb[...], wv_ref[...], preferred_element_type=F32)

    def conv(u_ref, c_ref):
        out = None
        for tap in range(FFN_CONV):
            term = c_ref[tap:tap + 1, :] * u_ref[pl.ds(halo - (FFN_CONV - 1) + tap, tm), :]
            out = term if out is None else out + term
        return out

    a = conv(ug, cg_ref)
    act = a * _sigmoid(a) * conv(uv, cv_ref)
    acc[...] += _dot(act, wd_ref[...])

    @pl.when(j == pl.num_programs(1) - 1)
    def _():
        o_ref[...] = _layer_norm(DN_ALPHA * h_ref[...] + acc[...], g_ref[...], b_ref[...])


def _ffn(h, w_up, conv_w, w_down, g, b, seq):
    m = h.shape[0]
    tm = ROW_TILE
    halo = 16
    nch = D_FF // FFN_CHUNK
    in_specs = [pl.BlockSpec((tm, D_MODEL), lambda i, j: (i, 0)),
                pl.BlockSpec((halo, D_MODEL), lambda i, j: (jnp.maximum(i * (tm // halo) - 1, 0), 0)),
                pl.BlockSpec((D_MODEL, FFN_CHUNK), lambda i, j: (0, j)),
                pl.BlockSpec((D_MODEL, FFN_CHUNK), lambda i, j: (0, nch + j)),
                pl.BlockSpec((FFN_CONV, FFN_CHUNK), lambda i, j: (0, j)),
                pl.BlockSpec((FFN_CONV, FFN_CHUNK), lambda i, j: (0, nch + j)),
                pl.BlockSpec((FFN_CHUNK, D_MODEL), lambda i, j: (j, 0)),
                _full_spec(g.shape), _full_spec(b.shape)]
    return pl.pallas_call(
        functools.partial(_ffn_kernel, seq=seq, tm=tm, halo=halo), grid=(m // tm, nch),
        in_specs=in_specs, out_specs=pl.BlockSpec((tm, D_MODEL), lambda i, j: (i, 0)),
        out_shape=jax.ShapeDtypeStruct((m, D_MODEL), F32),
        scratch_shapes=[pltpu.VMEM((tm + halo, D_MODEL), BF16), pltpu.VMEM((tm, D_MODEL), F32),
                        pltpu.VMEM((tm + halo, FFN_CHUNK), F32), pltpu.VMEM((tm + halo, FFN_CHUNK), F32)],
        compiler_params=_cparams(("parallel", "arbitrary")), name="conv_ffn_ln",
    )(h, h, w_up, w_up, conv_w, conv_w, w_down, g, b)


def _pad_heads(w, n_heads):
    lead = w.shape[:-1]
    w = w.reshape(lead + (n_heads, NSA_DH))
    w = jnp.pad(w, [(0, 0)] * len(lead) + [(0, 0), (0, HEAD_PAD - NSA_DH)])
    return w.reshape(lead + (n_heads * HEAD_PAD,))


def _compress_weights(w1):
    half = CMP_BLOCK // 2
    w1r = w1.reshape(2, half, NSA_DH, CMP_HIDDEN)
    eye = jnp.eye(NSA_KV_HEADS, dtype=w1.dtype)
    ex = jnp.einsum("aidh,gk->agikdh", w1r, eye)
    ex = ex.reshape(2 * NSA_KV_HEADS, half * NSA_KVW, CMP_HIDDEN)
    return jnp.concatenate([ex[n] for n in range(2 * NSA_KV_HEADS)], axis=1)


def _overlap_matrix(seq):
    n_cmp = (seq - CMP_BLOCK) // CMP_STRIDE + 1
    n_sel = seq // SEL_BLOCK
    starts = np.arange(n_cmp) * CMP_STRIDE
    jb = np.arange(n_sel) * SEL_BLOCK
    ov = ((starts[:, None] < jb[None] + SEL_BLOCK) & (starts[:, None] + CMP_BLOCK > jb[None])).astype(np.float32)
    out = np.zeros((seq // CMP_STRIDE, LANE), np.float32)
    out[:n_cmp, :n_sel] = ov
    return jnp.asarray(out, BF16)


def kernel(x, w_in, gdn_conv_w, gdn_a_log, gdn_dt_bias, gdn_norm_w, cmp_pos_k, cmp_w1_k, cmp_w2_k,
           cmp_pos_v, cmp_w1_v, cmp_w2_v, w_branch_gdn, w_branch_nsa, w_out, ln1_g, ln1_b, w_up,
           ffn_conv_w, w_down, ln2_g, ln2_b):
    bsz, seq, _ = x.shape
    m = bsz * seq
    for i in range(DEPTH):
        x2d = x.reshape(m, D_MODEL)
        w = w_in[i]
        o_gdn = 2 * GDN_QK + 2 * GDN_VW
        o_nq = o_gdn + 2 * GDN_HEADS
        o_kv = o_nq + NSA_QW
        o_gate = o_kv + 6 * NSA_KVW
        o_mix = o_gate + 3 * NSA_HEADS
        kv = [w[:, o_kv + n * NSA_KVW:o_kv + (n + 1) * NSA_KVW] for n in range(6)]
        small_w = jnp.concatenate([w[:, o_gdn:o_nq], w[:, o_gate:o_mix]], axis=1)
        small_w = jnp.pad(small_w, ((0, 0), (0, LANE - small_w.shape[1])))
        weights = [w[:, :o_gdn],
                   _pad_heads(w[:, o_nq:o_kv], NSA_HEADS),
                   jnp.concatenate([_pad_heads(t, NSA_KV_HEADS) for t in kv[2:]], axis=1),
                   jnp.concatenate(kv[:2], axis=1),
                   w[:, o_mix:],
                   small_w]
        weights = [t.astype(BF16) for t in weights]
        gqkvz, nq, nkv, kvc, mix, small = _inproj(x2d, weights, [BF16, BF16, BF16, BF16, BF16, F32])

        gates = small[:, :2 * GDN_HEADS].reshape(bsz, seq, 2, GDN_HEADS)
        gcol = gates.transpose(0, 3, 1, 2)
        grow = gcol.reshape(bsz, GDN_HEADS, seq // GDN_PAIR, GDN_PAIR, 2).transpose(0, 1, 2, 4, 3)
        alog = jnp.broadcast_to(gdn_a_log[i][:, None, None], (GDN_HEADS, 1, LANE))
        dtb = jnp.broadcast_to(gdn_dt_bias[i][:, None, None], (GDN_HEADS, 1, LANE))
        o_a = _gdn(gqkvz, gdn_conv_w[i], gcol, grow, alog, dtb, gdn_norm_w[i][None, :], bsz, seq)

        grp = seq // CMP_STRIDE
        k16 = kvc[:, :NSA_KVW].reshape(bsz, grp, CMP_STRIDE * NSA_KVW)
        v16 = kvc[:, NSA_KVW:].reshape(bsz, grp, CMP_STRIDE * NSA_KVW)
        pos_rows = lambda p: jnp.pad(p.reshape(1, CMP_BLOCK * NSA_DH), ((0, 7), (0, 0))).astype(BF16)
        w2_pad = lambda t: jnp.pad(t, ((0, 0), (0, HEAD_PAD - NSA_DH))).astype(BF16)
        kc, vc = _compress(k16, v16,
                           _compress_weights(cmp_w1_k[i]).astype(BF16), _compress_weights(cmp_w1_v[i]).astype(BF16),
                           cmp_w1_k[i].astype(BF16), cmp_w1_v[i].astype(BF16),
                           pos_rows(cmp_pos_k[i]), pos_rows(cmp_pos_v[i]),
                           w2_pad(cmp_w2_k[i]), w2_pad(cmp_w2_v[i]))
        o_b = _nsa(nq, nkv, kc, vc, small, _overlap_matrix(seq), bsz, seq)

        wb = jnp.pad(w_branch_nsa[i].reshape(NSA_HEADS, NSA_DH, D_MODEL),
                     ((0, 0), (0, HEAD_PAD - NSA_DH), (0, 0))).reshape(NSA_HEADS * HEAD_PAD, D_MODEL)
        h = _merge(o_a, o_b, mix, x2d, w_branch_gdn[i].astype(BF16), wb.astype(BF16), w_out[i].astype(BF16),
                   ln1_g[i][None, :], ln1_b[i][None, :])

        x = _ffn(h, w_up[i].astype(BF16), ffn_conv_w[i], w_down[i].astype(BF16),
                 ln2_g[i][None, :], ln2_b[i][None, :], seq).reshape(bsz, seq, D_MODEL)
    return x
```

```python
import functools
import math

import numpy as np
import jax
import jax.numpy as jnp
from jax import lax
from jax.experimental import pallas as pl
from jax.experimental.pallas import tpu as pltpu

F32 = jnp.float32
BF16 = jnp.bfloat16

D_MODEL = 1024
GDN_HEADS = 4
GDN_DK = 128
GDN_DV = 128
GDN_CONV = 4
GDN_CHUNK = 64
NSA_HEADS = 8
NSA_KV_HEADS = 2
NSA_REP = NSA_HEADS // NSA_KV_HEADS
NSA_DH = 64
CMP_BLOCK = 32
CMP_STRIDE = 16
CMP_HIDDEN = 256
SEL_BLOCK = 64
SEL_SHIFT = 6
SEL_TOPK = 16
WINDOW = 512
FORCE_SCORE = 1e9
D_FF = 2816
FFN_CONV = 3
DEPTH = 1
DN_ALPHA = (2 * DEPTH) ** 0.25
LN_EPS = 1e-5
RMS_EPS = 1e-6

GDN_QK = GDN_HEADS * GDN_DK
GDN_VW = GDN_HEADS * GDN_DV
NSA_QW = NSA_HEADS * NSA_DH
NSA_KVW = NSA_KV_HEADS * NSA_DH

LANE = 128
HEAD_PAD = LANE
NEG_BIG = -1e30
VMEM_LIMIT = 56 * 1024 * 1024

Q_BLOCK = 128
SEL_KEY_BLOCK = 256
FFN_CHUNK = 256
ROW_TILE = 512
GDN_CHUNK_SHIFT = 6
GDN_PAIR = 2 * GDN_CHUNK
GDN_UNROLL = 4


def _cparams(sem):
    return pltpu.CompilerParams(dimension_semantics=sem, vmem_limit_bytes=VMEM_LIMIT)


def _dot(a, b):
    return jnp.dot(a.astype(BF16), b.astype(BF16), preferred_element_type=F32)


def _dot_nt(a, b):
    return lax.dot_general(a.astype(BF16), b.astype(BF16), (((1,), (1,)), ((), ())),
                           preferred_element_type=F32)


def _dot_tn(a, b):
    return lax.dot_general(a.astype(BF16), b.astype(BF16), (((0,), (0,)), ((), ())),
                           preferred_element_type=F32)


def _sigmoid(x):
    return jax.nn.sigmoid(x)


def _softplus(x):
    return jnp.maximum(x, 0.0) + jnp.log1p(jnp.exp(-jnp.abs(x)))


def _layer_norm(x, g, b):
    mu = jnp.mean(x, axis=-1, keepdims=True)
    xc = x - mu
    var = jnp.mean(xc * xc, axis=-1, keepdims=True)
    return xc * lax.rsqrt(var + LN_EPS) * g + b


def _full_spec(shape):
    nd = len(shape)
    return pl.BlockSpec(shape, lambda *_: (0,) * nd)


def _inproj_kernel(x_ref, *refs):
    n = len(refs) // 2
    xb = x_ref[...].astype(BF16)
    for w_ref, o_ref in zip(refs[:n], refs[n:]):
        o_ref[...] = jnp.dot(xb, w_ref[...], preferred_element_type=F32).astype(o_ref.dtype)


def _inproj(x2d, weights, out_dtypes):
    m = x2d.shape[0]
    tm = ROW_TILE
    in_specs = [pl.BlockSpec((tm, D_MODEL), lambda i: (i, 0))]
    in_specs += [_full_spec(w.shape) for w in weights]
    out_specs = [pl.BlockSpec((tm, w.shape[1]), lambda i: (i, 0)) for w in weights]
    out_shape = [jax.ShapeDtypeStruct((m, w.shape[1]), dt) for w, dt in zip(weights, out_dtypes)]
    return pl.pallas_call(
        _inproj_kernel, grid=(m // tm,), in_specs=in_specs, out_specs=out_specs, out_shape=out_shape,
        compiler_params=_cparams(("parallel",)), name="inproj")(x2d, *weights)


def _gdn_kernel(q_ref, k_ref, v_ref, z_ref, cwq_ref, cwk_ref, cwv_ref, gcol_ref, grow_ref,
                alog_ref, dtb_ref, nw_ref, o_ref,
                xpad, qh, kh, vh, xp, qk_s, rhs, kd, qd, gl, sol, mq, ns, o0, st, *, seq):
    c = GDN_CHUNK
    pr = GDN_PAIR
    npair = seq // pr
    head = pl.program_id(1)
    pad = 8
    conv_rows = 256
    xpad[0:pad, :] = jnp.zeros((pad, 3 * LANE), F32)
    xpad[pad:, 0:LANE] = q_ref[...].astype(F32)
    xpad[pad:, LANE:2 * LANE] = k_ref[...].astype(F32)
    xpad[pad:, 2 * LANE:3 * LANE] = v_ref[...].astype(F32)
    cw = jnp.concatenate([cwq_ref[...], cwk_ref[...], cwv_ref[...]], axis=1)

    for t0 in range(0, seq, conv_rows):
        acc = jnp.zeros((conv_rows, 3 * LANE), F32)
        for j in range(GDN_CONV):
            xs = xpad[t0 + pad - (GDN_CONV - 1) + j:t0 + pad - (GDN_CONV - 1) + j + conv_rows, :]
            acc = acc + xs * cw[j:j + 1, :]
        y = acc * _sigmoid(acc)
        yq, yk, yv = y[:, 0:LANE], y[:, LANE:2 * LANE], y[:, 2 * LANE:3 * LANE]
        yq = yq * lax.rsqrt(jnp.sum(yq * yq, axis=-1, keepdims=True) + RMS_EPS) * (GDN_DK ** -0.5)
        yk = yk * lax.rsqrt(jnp.sum(yk * yk, axis=-1, keepdims=True) + RMS_EPS)
        qh[t0:t0 + conv_rows, :] = yq
        kh[t0:t0 + conv_rows, :] = yk
        vh[t0:t0 + conv_rows, :] = yv

    neg_a = -jnp.exp(alog_ref[0, :, 0:1])
    dtb = dtb_ref[0, :, 0:1]
    ii = lax.broadcasted_iota(jnp.int32, (pr, pr), 0)
    jj = lax.broadcasted_iota(jnp.int32, (pr, pr), 1)
    same = jnp.right_shift(ii, GDN_CHUNK_SHIFT) == jnp.right_shift(jj, GDN_CHUNK_SHIFT)
    incl = same & (jj <= ii)
    strict = same & (jj < ii)
    upper = same & (ii <= jj)
    eye = (ii == jj).astype(F32)
    sub8 = lax.broadcasted_iota(jnp.int32, (8, LANE), 0)

    def over_pairs(body):
        def run(gi, carry):
            for uu in range(GDN_UNROLL):
                body(gi * GDN_UNROLL + uu)
            return carry
        lax.fori_loop(0, npair // GDN_UNROLL, run, 0)

    def stage_gates(p):
        rows = pl.ds(pl.multiple_of(p * pr, pr), pr)
        q = qh[rows, :]
        k = kh[rows, :]
        v = vh[rows, :]
        gcol = gcol_ref[0, 0, rows, :]
        grow = grow_ref[0, 0, p]
        beta = _sigmoid(gcol[:, 0:1])
        g_c = neg_a * _softplus(gcol[:, 1:2] + dtb)
        g_r = neg_a * _softplus(grow[1:2, :] + dtb)
        gc_col = jnp.sum(jnp.where(incl, g_r, 0.0), axis=1, keepdims=True)
        gc_row = jnp.sum(jnp.where(upper, g_c, 0.0), axis=0, keepdims=True)
        gt_col = jnp.sum(jnp.where(same, g_r, 0.0), axis=1, keepdims=True)
        decay = jnp.where(incl, jnp.exp(jnp.where(incl, gc_col - gc_row, 0.0)), 0.0)
        kq_k = _dot_nt(jnp.concatenate([k, q], axis=0), k)
        a_mat = jnp.where(strict, kq_k[0:pr] * decay * beta, 0.0)
        xp[p, 0:pr, :] = eye - a_mat
        xp[p, pr:2 * pr, :] = _dot(a_mat, a_mat)
        qk_s[p] = (kq_k[pr:2 * pr] * decay).astype(qk_s.dtype)
        e_gc = jnp.exp(gc_col)
        rhs[rows, 0:GDN_DV] = (v * beta).astype(rhs.dtype)
        rhs[rows, GDN_DV:GDN_DV + GDN_DK] = (k * (beta * e_gc)).astype(rhs.dtype)
        kd[rows, :] = (k * jnp.exp(gt_col - gc_col)).astype(kd.dtype)
        qd[rows, :] = q * e_gc
        g_last = jnp.exp(gt_col)
        gl[p] = jnp.where(sub8 == 0, g_last[0:1, :], g_last[c:c + 1, :])

    def stage_double(p):
        y = _dot(xp[p], xp[p, pr:2 * pr, :])
        xp[p, 0:pr, :] = xp[p, 0:pr, :] + y[0:pr]
        xp[p, pr:2 * pr, :] = y[pr:2 * pr]

    def stage_solve(p):
        rows = pl.ds(pl.multiple_of(p * pr, pr), pr)
        x_inv = xp[p, 0:pr, :]
        x_inv = x_inv + _dot(x_inv, xp[p, pr:2 * pr, :])
        sol[rows, :] = _dot(x_inv, rhs[rows, :])

    def stage_affine(p):
        r0 = pl.multiple_of(p * pr, pr)
        rows = pl.ds(r0, pr)
        e = _dot(qk_s[p], sol[rows, :])
        qe = qd[rows, :] - e[:, GDN_DV:GDN_DV + GDN_DK]
        g_rows = gl[p]
        for cc in range(pr // c):
            rc = pl.ds(r0 + cc * c, c)
            mn = _dot_tn(kd[rc, :], sol[rc, :])
            idx = p * (pr // c) + cc
            mq[head, idx, 0:GDN_DK, :] = (eye * g_rows[cc:cc + 1, :] - mn[:, GDN_DV:GDN_DV + GDN_DK]).astype(mq.dtype)
            mq[head, idx, GDN_DK:GDN_DK + c, :] = qe[cc * c:(cc + 1) * c].astype(mq.dtype)
            ns[head, idx] = mn[:, 0:GDN_DV]
            o0[head, idx] = e[cc * c:(cc + 1) * c, 0:GDN_DV]

    over_pairs(stage_gates)
    for _ in range(int(math.log2(c)) - 2):
        over_pairs(stage_double)
    over_pairs(stage_solve)
    over_pairs(stage_affine)

    @pl.when(head == GDN_HEADS - 1)
    def _():
        st[...] = jnp.zeros_like(st)
        nw = nw_ref[...]

        def step(ci, carry):
            r0 = pl.multiple_of(ci * c, c)
            for h in range(GDN_HEADS):
                cols = slice(h * LANE, (h + 1) * LANE)
                ys = jnp.dot(mq[h, ci], st[h], preferred_element_type=F32)
                st[h] = (ys[0:GDN_DK] + ns[h, ci]).astype(st.dtype)
                o = ys[GDN_DK:GDN_DK + c] + o0[h, ci]
                o = o * lax.rsqrt(jnp.mean(o * o, axis=-1, keepdims=True) + RMS_EPS) * nw
                z = z_ref[pl.ds(r0, c), cols].astype(F32)
                o_ref[pl.ds(r0, c), cols] = (o * (z * _sigmoid(z))).astype(o_ref.dtype)
            return carry

        lax.fori_loop(0, seq // c, step, 0)


def _gdn(gqkvz, conv_w, gcol, grow, alog, dtb, nw, bsz, seq):
    assert GDN_DK == GDN_PAIR == GDN_DV == LANE
    h = GDN_HEADS
    npair = seq // GDN_PAIR
    nchunk = seq // GDN_CHUNK
    blk = lambda off: pl.BlockSpec((seq, LANE), lambda b, hh: (b, off + hh))
    cwb = lambda off: pl.BlockSpec((GDN_CONV, LANE), lambda b, hh: (0, off + hh))
    in_specs = [blk(0), blk(h), blk(2 * h),
                pl.BlockSpec((seq, GDN_VW), lambda b, hh: (b, 3)),
                cwb(0), cwb(h), cwb(2 * h),
                pl.BlockSpec((1, 1, seq, 2), lambda b, hh: (b, hh, 0, 0)),
                pl.BlockSpec((1, 1, npair, 2, GDN_PAIR), lambda b, hh: (b, hh, 0, 0, 0)),
                pl.BlockSpec((1, 1, LANE), lambda b, hh: (hh, 0, 0)),
                pl.BlockSpec((1, 1, LANE), lambda b, hh: (hh, 0, 0)),
                pl.BlockSpec((1, LANE), lambda b, hh: (0, 0))]
    scratch = [pltpu.VMEM((seq + 8, 3 * LANE), F32),
               pltpu.VMEM((seq, LANE), F32), pltpu.VMEM((seq, LANE), F32), pltpu.VMEM((seq, LANE), F32),
               pltpu.VMEM((npair, 2 * GDN_PAIR, GDN_PAIR), F32),
               pltpu.VMEM((npair, GDN_PAIR, GDN_PAIR), BF16),
               pltpu.VMEM((seq, GDN_DV + GDN_DK), BF16),
               pltpu.VMEM((seq, GDN_DK), BF16),
               pltpu.VMEM((seq, GDN_DK), F32),
               pltpu.VMEM((npair, 8, LANE), F32),
               pltpu.VMEM((seq, GDN_DV + GDN_DK), F32),
               pltpu.VMEM((h, nchunk, GDN_DK + GDN_CHUNK, GDN_DK), BF16),
               pltpu.VMEM((h, nchunk, GDN_DK, GDN_DV), F32),
               pltpu.VMEM((h, nchunk, GDN_CHUNK, GDN_DV), F32),
               pltpu.VMEM((h, GDN_DK, GDN_DV), BF16)]
    return pl.pallas_call(
        functools.partial(_gdn_kernel, seq=seq), grid=(bsz, h), in_specs=in_specs,
        out_specs=pl.BlockSpec((seq, GDN_VW), lambda b, hh: (b, 0)),
        out_shape=jax.ShapeDtypeStruct((bsz * seq, GDN_VW), BF16),
        scratch_shapes=scratch,
        compiler_params=_cparams(("parallel", "arbitrary")), name="gdn",
    )(gqkvz, gqkvz, gqkvz, gqkvz, conv_w, conv_w, conv_w, gcol, grow, alog, dtb, nw)


def _gelu_tanh(x):
    return 0.5 * x * (1.0 + jnp.tanh(math.sqrt(2.0 / math.pi) * (x + 0.044715 * x * x * x)))


def _compress_kernel(k16_ref, v16_ref, wkc_ref, wvc_ref, w1k_ref, w1v_ref, pk_ref, pv_ref,
                     w2k_ref, w2v_ref, kc_ref, vc_ref):
    nblk = k16_ref.shape[1]
    for x_ref, wc_ref, w1_ref, p_ref, w2_ref, o_ref in (
            (k16_ref, wkc_ref, w1k_ref, pk_ref, w2k_ref, kc_ref),
            (v16_ref, wvc_ref, w1v_ref, pv_ref, w2v_ref, vc_ref)):
        y = jnp.dot(x_ref[0], wc_ref[...], preferred_element_type=F32)
        pos_term = jnp.dot(p_ref[...], w1_ref[...], preferred_element_type=F32)[0:1, :]
        for g in range(NSA_KV_HEADS):
            top = y[:, g * CMP_HIDDEN:(g + 1) * CMP_HIDDEN]
            bot = y[:, (NSA_KV_HEADS + g) * CMP_HIDDEN:(NSA_KV_HEADS + g + 1) * CMP_HIDDEN]
            hid = top + pltpu.roll(bot, nblk - 1, 0) + pos_term
            o_ref[0, g] = _dot(_gelu_tanh(hid), w2_ref[...]).astype(o_ref.dtype)


def _compress(k16, v16, wkc, wvc, w1k, w1v, pk, pv, w2k, w2v):
    bsz, nblk, width = k16.shape
    in_specs = [pl.BlockSpec((1, nblk, width), lambda b: (b, 0, 0))] * 2
    in_specs += [_full_spec(a.shape) for a in (wkc, wvc, w1k, w1v, pk, pv, w2k, w2v)]
    out_spec = pl.BlockSpec((1, NSA_KV_HEADS, nblk, HEAD_PAD), lambda b: (b, 0, 0, 0))
    out_shape = jax.ShapeDtypeStruct((bsz, NSA_KV_HEADS, nblk, HEAD_PAD), BF16)
    return pl.pallas_call(
        _compress_kernel, grid=(bsz,), in_specs=in_specs, out_specs=[out_spec, out_spec],
        out_shape=[out_shape, out_shape], compiler_params=_cparams(("parallel",)), name="nsa_compress",
    )(k16, v16, wkc, wvc, w1k, w1v, pk, pv, w2k, w2v)


def _nsa_kernel(q_ref, kv_ref, kc_ref, vc_ref, sm_ref, ovt_ref, o_ref, *, seq):
    qb = pl.program_id(1)
    q0 = qb * Q_BLOCK
    rows = NSA_REP * Q_BLOCK
    scale = NSA_DH ** -0.5
    n_cmp_pad = kc_ref.shape[2]
    n_sel_blk = seq // SEL_BLOCK
    t_tok = q0 + lax.broadcasted_iota(jnp.int32, (Q_BLOCK, 1), 0)
    t_row = jnp.concatenate([t_tok] * NSA_REP, axis=0)
    t_lane = q0 + lax.broadcasted_iota(jnp.int32, (1, Q_BLOCK), 1)
    small = sm_ref[...]

    def per_head(fn):
        return jnp.concatenate([fn(slice(r * Q_BLOCK, (r + 1) * Q_BLOCK)) for r in range(NSA_REP)], axis=0)

    qs_g, sel_g, o_cmp_g = [], [], []
    for g in range(NSA_KV_HEADS):
        qs = jnp.concatenate(
            [q_ref[:, (g * NSA_REP + r) * HEAD_PAD:(g * NSA_REP + r + 1) * HEAD_PAD] for r in range(NSA_REP)],
            axis=0) * scale
        qs_g.append(qs)

        s_c = _dot_nt(qs, kc_ref[0, g])
        n_idx = lax.broadcasted_iota(jnp.int32, (1, n_cmp_pad), 1)
        valid = (n_idx * CMP_STRIDE + (CMP_BLOCK - 1)) <= t_row
        m_c = jnp.max(jnp.where(valid, s_c, NEG_BIG), axis=-1, keepdims=True)
        p_c = jnp.where(valid, jnp.exp(s_c - m_c), 0.0)
        d_c = jnp.sum(p_c, axis=-1, keepdims=True)
        p_c = p_c * (1.0 / jnp.where(d_c > 0, d_c, 1.0))
        o_cmp_g.append(_dot(p_c, vc_ref[0, g]))

        p_sum = p_c[0:Q_BLOCK]
        for r in range(1, NSA_REP):
            p_sum = p_sum + p_c[r * Q_BLOCK:(r + 1) * Q_BLOCK]
        imp_t = _dot_nt(ovt_ref[...], p_sum)

        def rank_select(imp_t=imp_t):
            j_sub = lax.broadcasted_iota(jnp.int32, (n_sel_blk, 1), 0)
            cur = jnp.right_shift(t_lane, SEL_SHIFT)
            forced = (j_sub == 0) | (j_sub == cur) | (j_sub == cur - 1)
            causal_blk = (j_sub * SEL_BLOCK) <= t_lane
            imp = jnp.where(forced, FORCE_SCORE, jnp.where(causal_blk, imp_t, -jnp.inf))
            rank = jnp.zeros((n_sel_blk, Q_BLOCK), F32)
            for jp in range(n_sel_blk):
                row = imp[jp:jp + 1, :]
                ahead = (row > imp) | ((row == imp) & (j_sub > jp))
                rank = rank + ahead.astype(F32)
            return (rank < SEL_TOPK).astype(BF16)

        sel_g.append(lax.cond(q0 + Q_BLOCK <= SEL_TOPK * SEL_BLOCK,
                              lambda: jnp.ones((n_sel_blk, Q_BLOCK), BF16), rank_select))

    def sel_step(kb, carry):
        k0 = pl.multiple_of(kb * SEL_KEY_BLOCK, SEL_KEY_BLOCK)
        kpos = k0 + lax.broadcasted_iota(jnp.int32, (1, SEL_KEY_BLOCK), 1)
        causal = kpos <= t_tok
        eb = lax.broadcasted_iota(jnp.int32, (n_sel_blk, SEL_KEY_BLOCK), 0)
        ec = lax.broadcasted_iota(jnp.int32, (n_sel_blk, SEL_KEY_BLOCK), 1)
        expand = (eb == kb * (SEL_KEY_BLOCK // SEL_BLOCK) + jnp.right_shift(ec, SEL_SHIFT)).astype(BF16)
        out = []
        for g in range(NSA_KV_HEADS):
            m_i, l_i, acc = carry[g]
            k_blk = kv_ref[pl.ds(k0, SEL_KEY_BLOCK), g * HEAD_PAD:(g + 1) * HEAD_PAD]
            v_blk = kv_ref[pl.ds(k0, SEL_KEY_BLOCK), (2 + g) * HEAD_PAD:(3 + g) * HEAD_PAD]
            s = _dot_nt(qs_g[g], k_blk)
            sel_k = _dot_tn(sel_g[g], expand)
            bias = jnp.where((sel_k > 0.5) & causal, 0.0, NEG_BIG)
            s = per_head(lambda sl: s[sl] + bias)
            m_new = jnp.maximum(m_i, jnp.max(s, axis=-1, keepdims=True))
            alpha = jnp.exp(m_i - m_new)
            p = jnp.exp(s - m_new)
            l_new = alpha * l_i + jnp.sum(p, axis=-1, keepdims=True)
            out.append((m_new, l_new, alpha * acc + _dot(p, v_blk)))
        return tuple(out)

    n_kb = lax.div(q0 + (Q_BLOCK + SEL_KEY_BLOCK - 1), SEL_KEY_BLOCK)
    init = (jnp.full((rows, 1), NEG_BIG, F32), jnp.zeros((rows, 1), F32), jnp.zeros((rows, HEAD_PAD), F32))
    sel_out = lax.fori_loop(0, n_kb, sel_step, (init,) * NSA_KV_HEADS)

    span = WINDOW + Q_BLOCK
    w0 = pl.multiple_of(jnp.maximum(q0 - WINDOW, 0), Q_BLOCK)
    kpos_w = w0 + lax.broadcasted_iota(jnp.int32, (1, span), 1)
    ok_w = (kpos_w <= t_row) & (kpos_w > t_row - WINDOW)
    for g in range(NSA_KV_HEADS):
        _, l_s, acc_s = sel_out[g]
        o_slc = acc_s * (1.0 / l_s)

        k_w = kv_ref[pl.ds(w0, span), (4 + g) * HEAD_PAD:(5 + g) * HEAD_PAD]
        v_w = kv_ref[pl.ds(w0, span), (6 + g) * HEAD_PAD:(7 + g) * HEAD_PAD]
        s_w = jnp.where(ok_w, _dot_nt(qs_g[g], k_w), NEG_BIG)
        m_w = jnp.max(s_w, axis=-1, keepdims=True)
        p_w = jnp.exp(s_w - m_w)
        l_w = jnp.sum(p_w, axis=-1, keepdims=True)
        o_win = _dot(p_w, v_w) * (1.0 / l_w)

        for r in range(NSA_REP):
            head = g * NSA_REP + r
            c0 = 8 + head * 3
            gate = _sigmoid(small[:, c0:c0 + 3])
            sl = slice(r * Q_BLOCK, (r + 1) * Q_BLOCK)
            o_h = gate[:, 0:1] * o_cmp_g[g][sl] + gate[:, 1:2] * o_slc[sl] + gate[:, 2:3] * o_win[sl]
            o_ref[:, head * HEAD_PAD:(head + 1) * HEAD_PAD] = o_h.astype(o_ref.dtype)


def _nsa(nq, nkv, kc, vc, small, overlap, bsz, seq):
    nqb = seq // Q_BLOCK
    in_specs = [pl.BlockSpec((Q_BLOCK, NSA_HEADS * HEAD_PAD), lambda b, i: (b * nqb + i, 0)),
                pl.BlockSpec((seq, nkv.shape[1]), lambda b, i: (b, 0)),
                pl.BlockSpec((1,) + kc.shape[1:], lambda b, i: (b, 0, 0, 0)),
                pl.BlockSpec((1,) + vc.shape[1:], lambda b, i: (b, 0, 0, 0)),
                pl.BlockSpec((Q_BLOCK, LANE), lambda b, i: (b * nqb + i, 0)),
                _full_spec(overlap.shape)]
    return pl.pallas_call(
        functools.partial(_nsa_kernel, seq=seq), grid=(bsz, nqb), in_specs=in_specs,
        out_specs=pl.BlockSpec((Q_BLOCK, NSA_HEADS * HEAD_PAD), lambda b, i: (b * nqb + i, 0)),
        out_shape=jax.ShapeDtypeStruct((bsz * seq, NSA_HEADS * HEAD_PAD), BF16),
        compiler_params=_cparams(("parallel", "parallel")), name="nsa_attention",
    )(nq, nkv, kc, vc, small, overlap)


def _merge_kernel(oa_ref, ob_ref, mix_ref, x_ref, wa_ref, wb_ref, wo_ref, g_ref, b_ref, h_ref):
    ya = jnp.dot(oa_ref[...], wa_ref[...], preferred_element_type=F32)
    yb = jnp.dot(ob_ref[...], wb_ref[...], preferred_element_type=F32)
    mix = mix_ref[...].astype(F32)
    mixed = _sigmoid(mix[:, 0:D_MODEL]) * ya + _sigmoid(mix[:, D_MODEL:2 * D_MODEL]) * yb
    y = _dot(mixed, wo_ref[...])
    h_ref[...] = _layer_norm(DN_ALPHA * x_ref[...] + y, g_ref[...], b_ref[...])


def _merge(oa, ob, mix, x2d, wa, wb, wo, g, b):
    m = x2d.shape[0]
    tm = ROW_TILE
    row = lambda n: pl.BlockSpec((tm, n), lambda i: (i, 0))
    in_specs = [row(oa.shape[1]), row(ob.shape[1]), row(mix.shape[1]), row(D_MODEL),
                _full_spec(wa.shape), _full_spec(wb.shape), _full_spec(wo.shape),
                _full_spec(g.shape), _full_spec(b.shape)]
    return pl.pallas_call(
        _merge_kernel, grid=(m // tm,), in_specs=in_specs, out_specs=row(D_MODEL),
        out_shape=jax.ShapeDtypeStruct((m, D_MODEL), F32),
        compiler_params=_cparams(("parallel",)), name="merge_ln",
    )(oa, ob, mix, x2d, wa, wb, wo, g, b)


def _ffn_kernel(h_ref, halo_ref, wg_ref, wv_ref, cg_ref, cv_ref, wd_ref, g_ref, b_ref, o_ref,
                hb, acc, ug, uv, *, seq, tm, halo):
    i = pl.program_id(0)
    j = pl.program_id(1)

    @pl.when(j == 0)
    def _():
        first = lax.rem(i * tm, seq) == 0
        hb[0:halo, :] = jnp.where(first, 0.0, halo_ref[...]).astype(BF16)
        hb[halo:, :] = h_ref[...].astype(BF16)
        acc[...] = jnp.zeros_like(acc)

    ug[...] = jnp.dot(hb[...], wg_ref[...], preferred_element_type=F32)
    uv[...] = jnp.dot(hb[...], wv_ref[...], preferred_element_type=F32)

    def conv(u_ref, c_ref):
        out = None
        for tap in range(FFN_CONV):
            term = c_ref[tap:tap + 1, :] * u_ref[pl.ds(halo - (FFN_CONV - 1) + tap, tm), :]
            out = term if out is None else out + term
        return out

    a = conv(ug, cg_ref)
    act = a * _sigmoid(a) * conv(uv, cv_ref)
    acc[...] += _dot(act, wd_ref[...])

    @pl.when(j == pl.num_programs(1) - 1)
    def _():
        o_ref[...] = _layer_norm(DN_ALPHA * h_ref[...] + acc[...], g_ref[...], b_ref[...])


def _ffn(h, w_up, conv_w, w_down, g, b, seq):
    m = h.shape[0]
    tm = ROW_TILE
    halo = 16
    nch = D_FF // FFN_CHUNK
    in_specs = [pl.BlockSpec((tm, D_MODEL), lambda i, j: (i, 0)),
                pl.BlockSpec((halo, D_MODEL), lambda i, j: (jnp.maximum(i * (tm // halo) - 1, 0), 0)),
                pl.BlockSpec((D_MODEL, FFN_CHUNK), lambda i, j: (0, j)),
                pl.BlockSpec((D_MODEL, FFN_CHUNK), lambda i, j: (0, nch + j)),
                pl.BlockSpec((FFN_CONV, FFN_CHUNK), lambda i, j: (0, j)),
                pl.BlockSpec((FFN_CONV, FFN_CHUNK), lambda i, j: (0, nch + j)),
                pl.BlockSpec((FFN_CHUNK, D_MODEL), lambda i, j: (j, 0)),
                _full_spec(g.shape), _full_spec(b.shape)]
    return pl.pallas_call(
        functools.partial(_ffn_kernel, seq=seq, tm=tm, halo=halo), grid=(m // tm, nch),
        in_specs=in_specs, out_specs=pl.BlockSpec((tm, D_MODEL), lambda i, j: (i, 0)),
        out_shape=jax.ShapeDtypeStruct((m, D_MODEL), F32),
        scratch_shapes=[pltpu.VMEM((tm + halo, D_MODEL), BF16), pltpu.VMEM((tm, D_MODEL), F32),
                        pltpu.VMEM((tm + halo, FFN_CHUNK), F32), pltpu.VMEM((tm + halo, FFN_CHUNK), F32)],
        compiler_params=_cparams(("parallel", "arbitrary")), name="conv_ffn_ln",
    )(h, h, w_up, w_up, conv_w, conv_w, w_down, g, b)


def _pad_heads(w, n_heads):
    lead = w.shape[:-1]
    w = w.reshape(lead + (n_heads, NSA_DH))
    w = jnp.pad(w, [(0, 0)] * len(lead) + [(0, 0), (0, HEAD_PAD - NSA_DH)])
    return w.reshape(lead + (n_heads * HEAD_PAD,))


def _compress_weights(w1):
    half = CMP_BLOCK // 2
    w1r = w1.reshape(2, half, NSA_DH, CMP_HIDDEN)
    eye = jnp.eye(NSA_KV_HEADS, dtype=w1.dtype)
    ex = jnp.einsum("aidh,gk->agikdh", w1r, eye)
    ex = ex.reshape(2 * NSA_KV_HEADS, half * NSA_KVW, CMP_HIDDEN)
    return jnp.concatenate([ex[n] for n in range(2 * NSA_KV_HEADS)], axis=1)


def _overlap_matrix(seq):
    n_cmp = (seq - CMP_BLOCK) // CMP_STRIDE + 1
    n_sel = seq // SEL_BLOCK
    starts = np.arange(n_cmp) * CMP_STRIDE
    jb = np.arange(n_sel) * SEL_BLOCK
    ov = ((starts[:, None] < jb[None] + SEL_BLOCK) & (starts[:, None] + CMP_BLOCK > jb[None])).astype(np.float32)
    out = np.zeros((n_sel, seq // CMP_STRIDE), np.float32)
    out[:, :n_cmp] = ov.T
    return jnp.asarray(out, BF16)


def kernel(x, w_in, gdn_conv_w, gdn_a_log, gdn_dt_bias, gdn_norm_w, cmp_pos_k, cmp_w1_k, cmp_w2_k,
           cmp_pos_v, cmp_w1_v, cmp_w2_v, w_branch_gdn, w_branch_nsa, w_out, ln1_g, ln1_b, w_up,
           ffn_conv_w, w_down, ln2_g, ln2_b):
    bsz, seq, _ = x.shape
    m = bsz * seq
    for i in range(DEPTH):
        x2d = x.reshape(m, D_MODEL)
        w = w_in[i]
        o_gdn = 2 * GDN_QK + 2 * GDN_VW
        o_nq = o_gdn + 2 * GDN_HEADS
        o_kv = o_nq + NSA_QW
        o_gate = o_kv + 6 * NSA_KVW
        o_mix = o_gate + 3 * NSA_HEADS
        kv = [w[:, o_kv + n * NSA_KVW:o_kv + (n + 1) * NSA_KVW] for n in range(6)]
        small_w = jnp.concatenate([w[:, o_gdn:o_nq], w[:, o_gate:o_mix]], axis=1)
        small_w = jnp.pad(small_w, ((0, 0), (0, LANE - small_w.shape[1])))
        weights = [w[:, :o_gdn],
                   _pad_heads(w[:, o_nq:o_kv], NSA_HEADS),
                   jnp.concatenate([_pad_heads(t, NSA_KV_HEADS) for t in kv[2:]], axis=1),
                   jnp.concatenate(kv[:2], axis=1),
                   w[:, o_mix:],
                   small_w]
        weights = [t.astype(BF16) for t in weights]
        gqkvz, nq, nkv, kvc, mix, small = _inproj(x2d, weights, [BF16, BF16, BF16, BF16, BF16, F32])

        gates = small[:, :2 * GDN_HEADS].reshape(bsz, seq, 2, GDN_HEADS)
        gcol = gates.transpose(0, 3, 1, 2)
        grow = gcol.reshape(bsz, GDN_HEADS, seq // GDN_PAIR, GDN_PAIR, 2).transpose(0, 1, 2, 4, 3)
        alog = jnp.broadcast_to(gdn_a_log[i][:, None, None], (GDN_HEADS, 1, LANE))
        dtb = jnp.broadcast_to(gdn_dt_bias[i][:, None, None], (GDN_HEADS, 1, LANE))
        o_a = _gdn(gqkvz, gdn_conv_w[i], gcol, grow, alog, dtb, gdn_norm_w[i][None, :], bsz, seq)

        grp = seq // CMP_STRIDE
        k16 = kvc[:, :NSA_KVW].reshape(bsz, grp, CMP_STRIDE * NSA_KVW)
        v16 = kvc[:, NSA_KVW:].reshape(bsz, grp, CMP_STRIDE * NSA_KVW)
        pos_rows = lambda p: jnp.pad(p.reshape(1, CMP_BLOCK * NSA_DH), ((0, 7), (0, 0))).astype(BF16)
        w2_pad = lambda t: jnp.pad(t, ((0, 0), (0, HEAD_PAD - NSA_DH))).astype(BF16)
        kc, vc = _compress(k16, v16,
                           _compress_weights(cmp_w1_k[i]).astype(BF16), _compress_weights(cmp_w1_v[i]).astype(BF16),
                           cmp_w1_k[i].astype(BF16), cmp_w1_v[i].astype(BF16),
                           pos_rows(cmp_pos_k[i]), pos_rows(cmp_pos_v[i]),
                           w2_pad(cmp_w2_k[i]), w2_pad(cmp_w2_v[i]))
        o_b = _nsa(nq, nkv, kc, vc, small, _overlap_matrix(seq), bsz, seq)

        wb = jnp.pad(w_branch_nsa[i].reshape(NSA_HEADS, NSA_DH, D_MODEL),
                     ((0, 0), (0, HEAD_PAD - NSA_DH), (0, 0))).reshape(NSA_HEADS * HEAD_PAD, D_MODEL)
        h = _merge(o_a, o_b, mix, x2d, w_branch_gdn[i].astype(BF16), wb.astype(BF16), w_out[i].astype(BF16),
                   ln1_g[i][None, :], ln1_b[i][None, :])

        x = _ffn(h, w_up[i].astype(BF16), ffn_conv_w[i], w_down[i].astype(BF16),
                 ln2_g[i][None, :], ln2_b[i][None, :], seq).reshape(bsz, seq, D_MODEL)
    return x
```

```python
import functools
import math

import numpy as np
import jax
import jax.numpy as jnp
from jax import lax
from jax.experimental import pallas as pl
from jax.experimental.pallas import tpu as pltpu

F32 = jnp.float32
BF16 = jnp.bfloat16

D_MODEL = 1024
GDN_HEADS = 4
GDN_DK = 128
GDN_DV = 128
GDN_CONV = 4
GDN_CHUNK = 64
NSA_HEADS = 8
NSA_KV_HEADS = 2
NSA_REP = NSA_HEADS // NSA_KV_HEADS
NSA_DH = 64
CMP_BLOCK = 32
CMP_STRIDE = 16
CMP_HIDDEN = 256
SEL_BLOCK = 64
SEL_SHIFT = 6
SEL_TOPK = 16
WINDOW = 512
FORCE_SCORE = 1e9
D_FF = 2816
FFN_CONV = 3
DEPTH = 1
DN_ALPHA = (2 * DEPTH) ** 0.25
LN_EPS = 1e-5
RMS_EPS = 1e-6

GDN_QK = GDN_HEADS * GDN_DK
GDN_VW = GDN_HEADS * GDN_DV
NSA_QW = NSA_HEADS * NSA_DH
NSA_KVW = NSA_KV_HEADS * NSA_DH

LANE = 128
HEAD_PAD = LANE
NEG_BIG = -1e30
VMEM_LIMIT = 56 * 1024 * 1024

Q_BLOCK = 128
SEL_KEY_BLOCK = 256
SEL_COL_SPLIT = 4
FFN_CHUNK = 256
ROW_TILE = 512
GDN_CHUNK_SHIFT = 6
GDN_PAIR = 2 * GDN_CHUNK
GDN_UNROLL = 4


def _cparams(sem):
    return pltpu.CompilerParams(dimension_semantics=sem, vmem_limit_bytes=VMEM_LIMIT)


def _dot(a, b):
    return jnp.dot(a.astype(BF16), b.astype(BF16), preferred_element_type=F32)


def _dot_nt(a, b):
    return lax.dot_general(a.astype(BF16), b.astype(BF16), (((1,), (1,)), ((), ())),
                           preferred_element_type=F32)


def _dot_tn(a, b):
    return lax.dot_general(a.astype(BF16), b.astype(BF16), (((0,), (0,)), ((), ())),
                           preferred_element_type=F32)


def _sigmoid(x):
    return jax.nn.sigmoid(x)


def _softplus(x):
    return jnp.maximum(x, 0.0) + jnp.log1p(jnp.exp(-jnp.abs(x)))


def _layer_norm(x, g, b):
    mu = jnp.mean(x, axis=-1, keepdims=True)
    xc = x - mu
    var = jnp.mean(xc * xc, axis=-1, keepdims=True)
    return xc * lax.rsqrt(var + LN_EPS) * g + b


def _full_spec(shape):
    nd = len(shape)
    return pl.BlockSpec(shape, lambda *_: (0,) * nd)


def _inproj_kernel(x_ref, *refs):
    n = len(refs) // 2
    xb = x_ref[...].astype(BF16)
    for w_ref, o_ref in zip(refs[:n], refs[n:]):
        o_ref[...] = jnp.dot(xb, w_ref[...], preferred_element_type=F32).astype(o_ref.dtype)


def _inproj(x2d, weights, out_dtypes):
    m = x2d.shape[0]
    tm = ROW_TILE
    in_specs = [pl.BlockSpec((tm, D_MODEL), lambda i: (i, 0))]
    in_specs += [_full_spec(w.shape) for w in weights]
    out_specs = [pl.BlockSpec((tm, w.shape[1]), lambda i: (i, 0)) for w in weights]
    out_shape = [jax.ShapeDtypeStruct((m, w.shape[1]), dt) for w, dt in zip(weights, out_dtypes)]
    return pl.pallas_call(
        _inproj_kernel, grid=(m // tm,), in_specs=in_specs, out_specs=out_specs, out_shape=out_shape,
        compiler_params=_cparams(("parallel",)), name="inproj")(x2d, *weights)


def _gdn_kernel(q_ref, k_ref, v_ref, z_ref, cwq_ref, cwk_ref, cwv_ref, gcol_ref, grow_ref,
                alog_ref, dtb_ref, nw_ref, o_ref,
                xpad, qh, kh, vh, xp, qk_s, rhs, kd, qd, gl, sol, mq, ns, o0, st, *, seq):
    c = GDN_CHUNK
    pr = GDN_PAIR
    npair = seq // pr
    head = pl.program_id(1)
    pad = 8
    conv_rows = 256
    xpad[0:pad, :] = jnp.zeros((pad, 3 * LANE), F32)
    xpad[pad:, 0:LANE] = q_ref[...].astype(F32)
    xpad[pad:, LANE:2 * LANE] = k_ref[...].astype(F32)
    xpad[pad:, 2 * LANE:3 * LANE] = v_ref[...].astype(F32)
    cw = jnp.concatenate([cwq_ref[...], cwk_ref[...], cwv_ref[...]], axis=1)

    for t0 in range(0, seq, conv_rows):
        acc = jnp.zeros((conv_rows, 3 * LANE), F32)
        for j in range(GDN_CONV):
            xs = xpad[t0 + pad - (GDN_CONV - 1) + j:t0 + pad - (GDN_CONV - 1) + j + conv_rows, :]
            acc = acc + xs * cw[j:j + 1, :]
        y = acc * _sigmoid(acc)
        yq, yk, yv = y[:, 0:LANE], y[:, LANE:2 * LANE], y[:, 2 * LANE:3 * LANE]
        yq = yq * lax.rsqrt(jnp.sum(yq * yq, axis=-1, keepdims=True) + RMS_EPS) * (GDN_DK ** -0.5)
        yk = yk * lax.rsqrt(jnp.sum(yk * yk, axis=-1, keepdims=True) + RMS_EPS)
        qh[t0:t0 + conv_rows, :] = yq
        kh[t0:t0 + conv_rows, :] = yk
        vh[t0:t0 + conv_rows, :] = yv

    neg_a = -jnp.exp(alog_ref[0, :, 0:1])
    dtb = dtb_ref[0, :, 0:1]
    ii = lax.broadcasted_iota(jnp.int32, (pr, pr), 0)
    jj = lax.broadcasted_iota(jnp.int32, (pr, pr), 1)
    same = jnp.right_shift(ii, GDN_CHUNK_SHIFT) == jnp.right_shift(jj, GDN_CHUNK_SHIFT)
    incl = same & (jj <= ii)
    strict = same & (jj < ii)
    upper = same & (ii <= jj)
    eye = (ii == jj).astype(F32)
    sub8 = lax.broadcasted_iota(jnp.int32, (8, LANE), 0)

    def over_pairs(body):
        def run(gi, carry):
            for uu in range(GDN_UNROLL):
                body(gi * GDN_UNROLL + uu)
            return carry
        lax.fori_loop(0, npair // GDN_UNROLL, run, 0)

    def stage_gates(p):
        rows = pl.ds(pl.multiple_of(p * pr, pr), pr)
        q = qh[rows, :]
        k = kh[rows, :]
        v = vh[rows, :]
        gcol = gcol_ref[0, 0, rows, :]
        grow = grow_ref[0, 0, p]
        beta = _sigmoid(gcol[:, 0:1])
        g_c = neg_a * _softplus(gcol[:, 1:2] + dtb)
        g_r = neg_a * _softplus(grow[1:2, :] + dtb)
        gc_col = jnp.sum(jnp.where(incl, g_r, 0.0), axis=1, keepdims=True)
        gc_row = jnp.sum(jnp.where(upper, g_c, 0.0), axis=0, keepdims=True)
        gt_col = jnp.sum(jnp.where(same, g_r, 0.0), axis=1, keepdims=True)
        decay = jnp.where(incl, jnp.exp(jnp.where(incl, gc_col - gc_row, 0.0)), 0.0)
        kq_k = _dot_nt(jnp.concatenate([k, q], axis=0), k)
        a_mat = jnp.where(strict, kq_k[0:pr] * decay * beta, 0.0)
        xp[p, 0:pr, :] = eye - a_mat
        xp[p, pr:2 * pr, :] = _dot(a_mat, a_mat)
        qk_s[p] = (kq_k[pr:2 * pr] * decay).astype(qk_s.dtype)
        e_gc = jnp.exp(gc_col)
        rhs[rows, 0:GDN_DV] = (v * beta).astype(rhs.dtype)
        rhs[rows, GDN_DV:GDN_DV + GDN_DK] = (k * (beta * e_gc)).astype(rhs.dtype)
        kd[rows, :] = (k * jnp.exp(gt_col - gc_col)).astype(kd.dtype)
        qd[rows, :] = q * e_gc
        g_last = jnp.exp(gt_col)
        gl[p] = jnp.where(sub8 == 0, g_last[0:1, :], g_last[c:c + 1, :])

    def stage_double(p):
        y = _dot(xp[p], xp[p, pr:2 * pr, :])
        xp[p, 0:pr, :] = xp[p, 0:pr, :] + y[0:pr]
        xp[p, pr:2 * pr, :] = y[pr:2 * pr]

    def stage_solve(p):
        rows = pl.ds(pl.multiple_of(p * pr, pr), pr)
        x_inv = xp[p, 0:pr, :]
        x_inv = x_inv + _dot(x_inv, xp[p, pr:2 * pr, :])
        sol[rows, :] = _dot(x_inv, rhs[rows, :])

    def stage_affine(p):
        r0 = pl.multiple_of(p * pr, pr)
        rows = pl.ds(r0, pr)
        e = _dot(qk_s[p], sol[rows, :])
        qe = qd[rows, :] - e[:, GDN_DV:GDN_DV + GDN_DK]
        g_rows = gl[p]
        for cc in range(pr // c):
            rc = pl.ds(r0 + cc * c, c)
            mn = _dot_tn(kd[rc, :], sol[rc, :])
            idx = p * (pr // c) + cc
            mq[head, idx, 0:GDN_DK, :] = (eye * g_rows[cc:cc + 1, :] - mn[:, GDN_DV:GDN_DV + GDN_DK]).astype(mq.dtype)
            mq[head, idx, GDN_DK:GDN_DK + c, :] = qe[cc * c:(cc + 1) * c].astype(mq.dtype)
            ns[head, idx] = mn[:, 0:GDN_DV]
            o0[head, idx] = e[cc * c:(cc + 1) * c, 0:GDN_DV]

    over_pairs(stage_gates)
    for _ in range(int(math.log2(c)) - 2):
        over_pairs(stage_double)
    over_pairs(stage_solve)
    over_pairs(stage_affine)

    @pl.when(head == GDN_HEADS - 1)
    def _():
        st[...] = jnp.zeros_like(st)
        nw = nw_ref[...]

        def step(ci, carry):
            r0 = pl.multiple_of(ci * c, c)
            for h in range(GDN_HEADS):
                cols = slice(h * LANE, (h + 1) * LANE)
                ys = jnp.dot(mq[h, ci], st[h], preferred_element_type=F32)
                st[h] = (ys[0:GDN_DK] + ns[h, ci]).astype(st.dtype)
                o = ys[GDN_DK:GDN_DK + c] + o0[h, ci]
                o = o * lax.rsqrt(jnp.mean(o * o, axis=-1, keepdims=True) + RMS_EPS) * nw
                z = z_ref[pl.ds(r0, c), cols].astype(F32)
                o_ref[pl.ds(r0, c), cols] = (o * (z * _sigmoid(z))).astype(o_ref.dtype)
            return carry

        lax.fori_loop(0, seq // c, step, 0)


def _gdn(gqkvz, conv_w, gcol, grow, alog, dtb, nw, bsz, seq):
    assert GDN_DK == GDN_PAIR == GDN_DV == LANE
    h = GDN_HEADS
    npair = seq // GDN_PAIR
    nchunk = seq // GDN_CHUNK
    blk = lambda off: pl.BlockSpec((seq, LANE), lambda b, hh: (b, off + hh))
    cwb = lambda off: pl.BlockSpec((GDN_CONV, LANE), lambda b, hh: (0, off + hh))
    in_specs = [blk(0), blk(h), blk(2 * h),
                pl.BlockSpec((seq, GDN_VW), lambda b, hh: (b, 3)),
                cwb(0), cwb(h), cwb(2 * h),
                pl.BlockSpec((1, 1, seq, 2), lambda b, hh: (b, hh, 0, 0)),
                pl.BlockSpec((1, 1, npair, 2, GDN_PAIR), lambda b, hh: (b, hh, 0, 0, 0)),
                pl.BlockSpec((1, 1, LANE), lambda b, hh: (hh, 0, 0)),
                pl.BlockSpec((1, 1, LANE), lambda b, hh: (hh, 0, 0)),
                pl.BlockSpec((1, LANE), lambda b, hh: (0, 0))]
    scratch = [pltpu.VMEM((seq + 8, 3 * LANE), F32),
               pltpu.VMEM((seq, LANE), F32), pltpu.VMEM((seq, LANE), F32), pltpu.VMEM((seq, LANE), F32),
               pltpu.VMEM((npair, 2 * GDN_PAIR, GDN_PAIR), F32),
               pltpu.VMEM((npair, GDN_PAIR, GDN_PAIR), BF16),
               pltpu.VMEM((seq, GDN_DV + GDN_DK), BF16),
               pltpu.VMEM((seq, GDN_DK), BF16),
               pltpu.VMEM((seq, GDN_DK), F32),
               pltpu.VMEM((npair, 8, LANE), F32),
               pltpu.VMEM((seq, GDN_DV + GDN_DK), F32),
               pltpu.VMEM((h, nchunk, GDN_DK + GDN_CHUNK, GDN_DK), BF16),
               pltpu.VMEM((h, nchunk, GDN_DK, GDN_DV), F32),
               pltpu.VMEM((h, nchunk, GDN_CHUNK, GDN_DV), F32),
               pltpu.VMEM((h, GDN_DK, GDN_DV), BF16)]
    return pl.pallas_call(
        functools.partial(_gdn_kernel, seq=seq), grid=(bsz, h), in_specs=in_specs,
        out_specs=pl.BlockSpec((seq, GDN_VW), lambda b, hh: (b, 0)),
        out_shape=jax.ShapeDtypeStruct((bsz * seq, GDN_VW), BF16),
        scratch_shapes=scratch,
        compiler_params=_cparams(("parallel", "arbitrary")), name="gdn",
    )(gqkvz, gqkvz, gqkvz, gqkvz, conv_w, conv_w, conv_w, gcol, grow, alog, dtb, nw)


def _gelu_tanh(x):
    return 0.5 * x * (1.0 + jnp.tanh(math.sqrt(2.0 / math.pi) * (x + 0.044715 * x * x * x)))


def _compress_kernel(k16_ref, v16_ref, wkc_ref, wvc_ref, w1k_ref, w1v_ref, pk_ref, pv_ref,
                     w2k_ref, w2v_ref, kc_ref, vc_ref):
    nblk = k16_ref.shape[1]
    for x_ref, wc_ref, w1_ref, p_ref, w2_ref, o_ref in (
            (k16_ref, wkc_ref, w1k_ref, pk_ref, w2k_ref, kc_ref),
            (v16_ref, wvc_ref, w1v_ref, pv_ref, w2v_ref, vc_ref)):
        y = jnp.dot(x_ref[0], wc_ref[...], preferred_element_type=F32)
        pos_term = jnp.dot(p_ref[...], w1_ref[...], preferred_element_type=F32)[0:1, :]
        for g in range(NSA_KV_HEADS):
            top = y[:, g * CMP_HIDDEN:(g + 1) * CMP_HIDDEN]
            bot = y[:, (NSA_KV_HEADS + g) * CMP_HIDDEN:(NSA_KV_HEADS + g + 1) * CMP_HIDDEN]
            hid = top + pltpu.roll(bot, nblk - 1, 0) + pos_term
            o_ref[0, g] = _dot(_gelu_tanh(hid), w2_ref[...]).astype(o_ref.dtype)


def _compress(k16, v16, wkc, wvc, w1k, w1v, pk, pv, w2k, w2v):
    bsz, nblk, width = k16.shape
    in_specs = [pl.BlockSpec((1, nblk, width), lambda b: (b, 0, 0))] * 2
    in_specs += [_full_spec(a.shape) for a in (wkc, wvc, w1k, w1v, pk, pv, w2k, w2v)]
    out_spec = pl.BlockSpec((1, NSA_KV_HEADS, nblk, HEAD_PAD), lambda b: (b, 0, 0, 0))
    out_shape = jax.ShapeDtypeStruct((bsz, NSA_KV_HEADS, nblk, HEAD_PAD), BF16)
    return pl.pallas_call(
        _compress_kernel, grid=(bsz,), in_specs=in_specs, out_specs=[out_spec, out_spec],
        out_shape=[out_shape, out_shape], compiler_params=_cparams(("parallel",)), name="nsa_compress",
    )(k16, v16, wkc, wvc, w1k, w1v, pk, pv, w2k, w2v)


def _nsa_kernel(q_ref, kv_ref, kc_ref, vc_ref, sm_ref, ovt_ref, o_ref, ksa, vst, *, seq):
    qb = pl.program_id(1)
    q0 = qb * Q_BLOCK
    cols = NSA_REP * Q_BLOCK
    scale = NSA_DH ** -0.5
    n_cmp_pad = kc_ref.shape[2]
    n_sel_blk = seq // SEL_BLOCK
    kb_rows = SEL_KEY_BLOCK

    @pl.when(qb == 0)
    def _():
        lane = lax.broadcasted_iota(jnp.int32, (1, NSA_KV_HEADS * HEAD_PAD), 1) & (HEAD_PAD - 1)

        def fill(bi, carry):
            r0 = pl.multiple_of(bi * kb_rows, kb_rows)
            blk = jnp.right_shift(r0 + lax.broadcasted_iota(jnp.int32, (kb_rows, 1), 0), SEL_SHIFT)
            onehot = (lane == NSA_DH + blk).astype(ksa.dtype)
            ksa[pl.ds(r0, kb_rows), :] = kv_ref[pl.ds(r0, kb_rows), 0:NSA_KV_HEADS * HEAD_PAD] + onehot
            for g in range(NSA_KV_HEADS):
                v_blk = kv_ref[pl.ds(r0, kb_rows), (2 + g) * HEAD_PAD:(3 + g) * HEAD_PAD]
                vst[bi, g] = jnp.transpose(v_blk.astype(F32)).astype(vst.dtype)
            return carry

        lax.fori_loop(0, seq // kb_rows, fill, 0)

    t_col = q0 + (lax.broadcasted_iota(jnp.int32, (1, cols), 1) & (Q_BLOCK - 1))
    t_lane = q0 + lax.broadcasted_iota(jnp.int32, (1, Q_BLOCK), 1)
    gates_t = jnp.transpose(_sigmoid(sm_ref[...]))
    lane_q = lax.broadcasted_iota(jnp.int32, (1, HEAD_PAD), 1)
    pen_lanes = (lane_q >= NSA_DH) & (lane_q < NSA_DH + n_sel_blk)
    place = (lax.broadcasted_iota(jnp.int32, (n_sel_blk, HEAD_PAD), 1)
             == NSA_DH + lax.broadcasted_iota(jnp.int32, (n_sel_blk, HEAD_PAD), 0)).astype(BF16)

    def softmax_cols(s, ok):
        m = jnp.max(jnp.where(ok, s, NEG_BIG), axis=0, keepdims=True)
        p = jnp.where(ok, jnp.exp(s - m), 0.0)
        d = jnp.sum(p, axis=0, keepdims=True)
        return p * (1.0 / jnp.where(d > 0, d, 1.0))

    qp_g, o_cmp_g = [], []
    for g in range(NSA_KV_HEADS):
        qs = jnp.concatenate(
            [q_ref[:, (g * NSA_REP + r) * HEAD_PAD:(g * NSA_REP + r + 1) * HEAD_PAD] for r in range(NSA_REP)],
            axis=0) * scale

        s_c = _dot_nt(kc_ref[0, g], qs)
        n_sub = lax.broadcasted_iota(jnp.int32, (n_cmp_pad, 1), 0)
        p_c = softmax_cols(s_c, (n_sub * CMP_STRIDE + (CMP_BLOCK - 1)) <= t_col)
        o_cmp_g.append(_dot_tn(vc_ref[0, g], p_c))

        p_sum = p_c[:, 0:Q_BLOCK]
        for r in range(1, NSA_REP):
            p_sum = p_sum + p_c[:, r * Q_BLOCK:(r + 1) * Q_BLOCK]
        imp_t = _dot(ovt_ref[...], p_sum)

        def rank_select(imp_t=imp_t):
            j_sub = lax.broadcasted_iota(jnp.int32, (n_sel_blk, 1), 0)
            cur = jnp.right_shift(t_lane, SEL_SHIFT)
            forced = (j_sub == 0) | (j_sub == cur) | (j_sub == cur - 1)
            causal_blk = (j_sub * SEL_BLOCK) <= t_lane
            imp = jnp.where(forced, FORCE_SCORE, jnp.where(causal_blk, imp_t, -jnp.inf))
            rank = jnp.zeros((n_sel_blk, Q_BLOCK), F32)
            for jp in range(n_sel_blk):
                row = imp[jp:jp + 1, :]
                ahead = (row > imp) | ((row == imp) & (j_sub > jp))
                rank = rank + ahead.astype(F32)
            return (rank < SEL_TOPK).astype(BF16)

        sel_t = lax.cond(q0 + Q_BLOCK <= SEL_TOPK * SEL_BLOCK,
                         lambda: jnp.ones((n_sel_blk, Q_BLOCK), BF16), rank_select)
        sel_lanes = _dot_tn(sel_t, place)
        pen = jnp.where(pen_lanes, (sel_lanes - 1.0) * (-NEG_BIG), 0.0).astype(BF16)
        qp_g.append(qs + jnp.concatenate([pen] * NSA_REP, axis=0))

    half = cols // SEL_COL_SPLIT
    t_half = t_col[:, 0:half]

    def sel_block(kb, carry, causal):
        k0 = pl.multiple_of(kb * kb_rows, kb_rows)
        out = []
        for g in range(NSA_KV_HEADS):
            k_blk = ksa[pl.ds(k0, kb_rows), g * HEAD_PAD:(g + 1) * HEAD_PAD]
            v_t = vst[kb, g]
            for hf in range(SEL_COL_SPLIT):
                m_i, l_i, acc = carry[g * SEL_COL_SPLIT + hf]
                s = _dot_nt(k_blk, qp_g[g][hf * half:(hf + 1) * half])
                if causal:
                    kpos = k0 + lax.broadcasted_iota(jnp.int32, (kb_rows, 1), 0)
                    s = jnp.where(kpos <= t_half, s, NEG_BIG)
                m_new = jnp.maximum(m_i, jnp.max(s, axis=0, keepdims=True))
                alpha = jnp.exp(m_i - m_new)
                p = jnp.exp(s - m_new)
                l_new = alpha * l_i + jnp.sum(p, axis=0, keepdims=True)
                out.append((m_new, l_new, alpha * acc + _dot(v_t, p)))
        return tuple(out)

    n_full = lax.div(q0, kb_rows)
    init = (jnp.full((1, half), NEG_BIG, F32), jnp.zeros((1, half), F32), jnp.zeros((HEAD_PAD, half), F32))
    carry = lax.fori_loop(0, n_full, lambda kb, c: sel_block(kb, c, False),
                          (init,) * (NSA_KV_HEADS * SEL_COL_SPLIT))
    sel_out = sel_block(n_full, carry, True)

    span = WINDOW + Q_BLOCK
    w0 = pl.multiple_of(jnp.maximum(q0 - WINDOW, 0), Q_BLOCK)
    kpos_w = w0 + lax.broadcasted_iota(jnp.int32, (span, 1), 0)
    ok_w = (kpos_w <= t_col) & (kpos_w > t_col - WINDOW)
    for g in range(NSA_KV_HEADS):
        o_slc = jnp.concatenate([sel_out[g * SEL_COL_SPLIT + hf][2] * (1.0 / sel_out[g * SEL_COL_SPLIT + hf][1])
                                 for hf in range(SEL_COL_SPLIT)], axis=1)

        k_w = kv_ref[pl.ds(w0, span), (4 + g) * HEAD_PAD:(5 + g) * HEAD_PAD]
        v_w = kv_ref[pl.ds(w0, span), (6 + g) * HEAD_PAD:(7 + g) * HEAD_PAD]
        o_win = _dot_tn(v_w, softmax_cols(_dot_nt(k_w, qp_g[g]), ok_w))

        for r in range(NSA_REP):
            head = g * NSA_REP + r
            c0 = 8 + head * 3
            sl = slice(r * Q_BLOCK, (r + 1) * Q_BLOCK)
            o_t = (gates_t[c0:c0 + 1, :] * o_cmp_g[g][:, sl] + gates_t[c0 + 1:c0 + 2, :] * o_slc[:, sl]
                   + gates_t[c0 + 2:c0 + 3, :] * o_win[:, sl])
            o_ref[:, head * HEAD_PAD:(head + 1) * HEAD_PAD] = jnp.transpose(o_t).astype(o_ref.dtype)


def _nsa(nq, nkv, kc, vc, small, overlap, bsz, seq):
    nqb = seq // Q_BLOCK
    in_specs = [pl.BlockSpec((Q_BLOCK, NSA_HEADS * HEAD_PAD), lambda b, i: (b * nqb + i, 0)),
                pl.BlockSpec((seq, nkv.shape[1]), lambda b, i: (b, 0)),
                pl.BlockSpec((1,) + kc.shape[1:], lambda b, i: (b, 0, 0, 0)),
                pl.BlockSpec((1,) + vc.shape[1:], lambda b, i: (b, 0, 0, 0)),
                pl.BlockSpec((Q_BLOCK, LANE), lambda b, i: (b * nqb + i, 0)),
                _full_spec(overlap.shape)]
    return pl.pallas_call(
        functools.partial(_nsa_kernel, seq=seq), grid=(bsz, nqb), in_specs=in_specs,
        out_specs=pl.BlockSpec((Q_BLOCK, NSA_HEADS * HEAD_PAD), lambda b, i: (b * nqb + i, 0)),
        out_shape=jax.ShapeDtypeStruct((bsz * seq, NSA_HEADS * HEAD_PAD), BF16),
        scratch_shapes=[pltpu.VMEM((seq, NSA_KV_HEADS * HEAD_PAD), BF16),
                        pltpu.VMEM((seq // SEL_KEY_BLOCK, NSA_KV_HEADS, HEAD_PAD, SEL_KEY_BLOCK), BF16)],
        compiler_params=_cparams(("parallel", "arbitrary")), name="nsa_attention",
    )(nq, nkv, kc, vc, small, overlap)


def _merge_kernel(oa_ref, ob_ref, mix_ref, x_ref, wa_ref, wb_ref, wo_ref, g_ref, b_ref, h_ref):
    ya = jnp.dot(oa_ref[...], wa_ref[...], preferred_element_type=F32)
    yb = jnp.dot(ob_ref[...], wb_ref[...], preferred_element_type=F32)
    mix = mix_ref[...].astype(F32)
    mixed = _sigmoid(mix[:, 0:D_MODEL]) * ya + _sigmoid(mix[:, D_MODEL:2 * D_MODEL]) * yb
    y = _dot(mixed, wo_ref[...])
    h_ref[...] = _layer_norm(DN_ALPHA * x_ref[...] + y, g_ref[...], b_ref[...])


def _merge(oa, ob, mix, x2d, wa, wb, wo, g, b):
    m = x2d.shape[0]
    tm = ROW_TILE
    row = lambda n: pl.BlockSpec((tm, n), lambda i: (i, 0))
    in_specs = [row(oa.shape[1]), row(ob.shape[1]), row(mix.shape[1]), row(D_MODEL),
                _full_spec(wa.shape), _full_spec(wb.shape), _full_spec(wo.shape),
                _full_spec(g.shape), _full_spec(b.shape)]
    return pl.pallas_call(
        _merge_kernel, grid=(m // tm,), in_specs=in_specs, out_specs=row(D_MODEL),
        out_shape=jax.ShapeDtypeStruct((m, D_MODEL), F32),
        compiler_params=_cparams(("parallel",)), name="merge_ln",
    )(oa, ob, mix, x2d, wa, wb, wo, g, b)


def _ffn_kernel(h_ref, halo_ref, wup_ref, cw_ref, wd_ref, g_ref, b_ref, o_ref,
                hb, ug, uv, act, *, seq, tm, halo):
    i = pl.program_id(0)
    first = lax.rem(i * tm, seq) == 0
    hb[0:halo, :] = jnp.where(first, 0.0, halo_ref[...]).astype(BF16)
    hb[halo:, :] = h_ref[...].astype(BF16)

    def conv(u_ref, slot, cols):
        out = None
        for tap in range(FFN_CONV):
            term = cw_ref[tap:tap + 1, cols] * u_ref[slot, pl.ds(halo - (FFN_CONV - 1) + tap, tm), :]
            out = term if out is None else out + term
        return out

    for j in range(D_FF // FFN_CHUNK):
        slot = j % 2
        cols_g = slice(j * FFN_CHUNK, (j + 1) * FFN_CHUNK)
        cols_v = slice(D_FF + j * FFN_CHUNK, D_FF + (j + 1) * FFN_CHUNK)
        ug[slot] = jnp.dot(hb[...], wup_ref[:, cols_g], preferred_element_type=F32)
        uv[slot] = jnp.dot(hb[...], wup_ref[:, cols_v], preferred_element_type=F32)
        a = conv(ug, slot, cols_g)
        act[:, cols_g] = (a * _sigmoid(a) * conv(uv, slot, cols_v)).astype(act.dtype)

    f = jnp.dot(act[...], wd_ref[...], preferred_element_type=F32)
    o_ref[...] = _layer_norm(DN_ALPHA * h_ref[...] + f, g_ref[...], b_ref[...])


def _ffn(h, w_up, conv_w, w_down, g, b, seq):
    m = h.shape[0]
    tm = ROW_TILE
    halo = 16
    resident = lambda shape: pl.BlockSpec(shape, lambda i: (0, 0), pipeline_mode=pl.Buffered(1))
    in_specs = [pl.BlockSpec((tm, D_MODEL), lambda i: (i, 0)),
                pl.BlockSpec((halo, D_MODEL), lambda i: (jnp.maximum(i * (tm // halo) - 1, 0), 0)),
                resident(w_up.shape), resident(conv_w.shape), resident(w_down.shape),
                resident(g.shape), resident(b.shape)]
    return pl.pallas_call(
        functools.partial(_ffn_kernel, seq=seq, tm=tm, halo=halo), grid=(m // tm,),
        in_specs=in_specs, out_specs=pl.BlockSpec((tm, D_MODEL), lambda i: (i, 0)),
        out_shape=jax.ShapeDtypeStruct((m, D_MODEL), F32),
        scratch_shapes=[pltpu.VMEM((tm + halo, D_MODEL), BF16),
                        pltpu.VMEM((2, tm + halo, FFN_CHUNK), F32), pltpu.VMEM((2, tm + halo, FFN_CHUNK), F32),
                        pltpu.VMEM((tm, D_FF), BF16)],
        compiler_params=_cparams(("parallel",)), name="conv_ffn_ln",
    )(h, h, w_up, conv_w, w_down, g, b)


def _pad_heads(w, n_heads):
    lead = w.shape[:-1]
    w = w.reshape(lead + (n_heads, NSA_DH))
    w = jnp.pad(w, [(0, 0)] * len(lead) + [(0, 0), (0, HEAD_PAD - NSA_DH)])
    return w.reshape(lead + (n_heads * HEAD_PAD,))


def _compress_weights(w1):
    half = CMP_BLOCK // 2
    w1r = w1.reshape(2, half, NSA_DH, CMP_HIDDEN)
    eye = jnp.eye(NSA_KV_HEADS, dtype=w1.dtype)
    ex = jnp.einsum("aidh,gk->agikdh", w1r, eye)
    ex = ex.reshape(2 * NSA_KV_HEADS, half * NSA_KVW, CMP_HIDDEN)
    return jnp.concatenate([ex[n] for n in range(2 * NSA_KV_HEADS)], axis=1)


def _overlap_matrix(seq):
    n_cmp = (seq - CMP_BLOCK) // CMP_STRIDE + 1
    n_sel = seq // SEL_BLOCK
    starts = np.arange(n_cmp) * CMP_STRIDE
    jb = np.arange(n_sel) * SEL_BLOCK
    ov = ((starts[:, None] < jb[None] + SEL_BLOCK) & (starts[:, None] + CMP_BLOCK > jb[None])).astype(np.float32)
    out = np.zeros((n_sel, seq // CMP_STRIDE), np.float32)
    out[:, :n_cmp] = ov.T
    return jnp.asarray(out, BF16)


def kernel(x, w_in, gdn_conv_w, gdn_a_log, gdn_dt_bias, gdn_norm_w, cmp_pos_k, cmp_w1_k, cmp_w2_k,
           cmp_pos_v, cmp_w1_v, cmp_w2_v, w_branch_gdn, w_branch_nsa, w_out, ln1_g, ln1_b, w_up,
           ffn_conv_w, w_down, ln2_g, ln2_b):
    bsz, seq, _ = x.shape
    m = bsz * seq
    for i in range(DEPTH):
        x2d = x.reshape(m, D_MODEL)
        w = w_in[i]
        o_gdn = 2 * GDN_QK + 2 * GDN_VW
        o_nq = o_gdn + 2 * GDN_HEADS
        o_kv = o_nq + NSA_QW
        o_gate = o_kv + 6 * NSA_KVW
        o_mix = o_gate + 3 * NSA_HEADS
        kv = [w[:, o_kv + n * NSA_KVW:o_kv + (n + 1) * NSA_KVW] for n in range(6)]
        small_w = jnp.concatenate([w[:, o_gdn:o_nq], w[:, o_gate:o_mix]], axis=1)
        small_w = jnp.pad(small_w, ((0, 0), (0, LANE - small_w.shape[1])))
        weights = [w[:, :o_gdn],
                   _pad_heads(w[:, o_nq:o_kv], NSA_HEADS),
                   jnp.concatenate([_pad_heads(t, NSA_KV_HEADS) for t in kv[2:]], axis=1),
                   jnp.concatenate(kv[:2], axis=1),
                   w[:, o_mix:],
                   small_w]
        weights = [t.astype(BF16) for t in weights]
        gqkvz, nq, nkv, kvc, mix, small = _inproj(x2d, weights, [BF16, BF16, BF16, BF16, BF16, F32])

        gates = small[:, :2 * GDN_HEADS].reshape(bsz, seq, 2, GDN_HEADS)
        gcol = gates.transpose(0, 3, 1, 2)
        grow = gcol.reshape(bsz, GDN_HEADS, seq // GDN_PAIR, GDN_PAIR, 2).transpose(0, 1, 2, 4, 3)
        alog = jnp.broadcast_to(gdn_a_log[i][:, None, None], (GDN_HEADS, 1, LANE))
        dtb = jnp.broadcast_to(gdn_dt_bias[i][:, None, None], (GDN_HEADS, 1, LANE))
        o_a = _gdn(gqkvz, gdn_conv_w[i], gcol, grow, alog, dtb, gdn_norm_w[i][None, :], bsz, seq)

        grp = seq // CMP_STRIDE
        k16 = kvc[:, :NSA_KVW].reshape(bsz, grp, CMP_STRIDE * NSA_KVW)
        v16 = kvc[:, NSA_KVW:].reshape(bsz, grp, CMP_STRIDE * NSA_KVW)
        pos_rows = lambda p: jnp.pad(p.reshape(1, CMP_BLOCK * NSA_DH), ((0, 7), (0, 0))).astype(BF16)
        w2_pad = lambda t: jnp.pad(t, ((0, 0), (0, HEAD_PAD - NSA_DH))).astype(BF16)
        kc, vc = _compress(k16, v16,
                           _compress_weights(cmp_w1_k[i]).astype(BF16), _compress_weights(cmp_w1_v[i]).astype(BF16),
                           cmp_w1_k[i].astype(BF16), cmp_w1_v[i].astype(BF16),
                           pos_rows(cmp_pos_k[i]), pos_rows(cmp_pos_v[i]),
                           w2_pad(cmp_w2_k[i]), w2_pad(cmp_w2_v[i]))
        o_b = _nsa(nq, nkv, kc, vc, small, _overlap_matrix(seq), bsz, seq)

        wb = jnp.pad(w_branch_nsa[i].reshape(NSA_HEADS, NSA_DH, D_MODEL),
                     ((0, 0), (0, HEAD_PAD - NSA_DH), (0, 0))).reshape(NSA_HEADS * HEAD_PAD, D_MODEL)
        h = _merge(o_a, o_b, mix, x2d, w_branch_gdn[i].astype(BF16), wb.astype(BF16), w_out[i].astype(BF16),
                   ln1_g[i][None, :], ln1_b[i][None, :])

        x = _ffn(h, w_up[i].astype(BF16), ffn_conv_w[i], w_down[i].astype(BF16),
                 ln2_g[i][None, :], ln2_b[i][None, :], seq).reshape(bsz, seq, D_MODEL)
    return x
```

```python
import functools
import math

import numpy as np
import jax
import jax.numpy as jnp
from jax import lax
from jax.experimental import pallas as pl
from jax.experimental.pallas import tpu as pltpu

F32 = jnp.float32
BF16 = jnp.bfloat16

D_MODEL = 1024
GDN_HEADS = 4
GDN_DK = 128
GDN_DV = 128
GDN_CONV = 4
GDN_CHUNK = 64
NSA_HEADS = 8
NSA_KV_HEADS = 2
NSA_REP = NSA_HEADS // NSA_KV_HEADS
NSA_DH = 64
CMP_BLOCK = 32
CMP_STRIDE = 16
CMP_HIDDEN = 256
SEL_BLOCK = 64
SEL_SHIFT = 6
SEL_TOPK = 16
WINDOW = 512
FORCE_SCORE = 1e9
D_FF = 2816
FFN_CONV = 3
DEPTH = 1
DN_ALPHA = (2 * DEPTH) ** 0.25
LN_EPS = 1e-5
RMS_EPS = 1e-6
LOG2E = math.log2(math.e)

GDN_QK = GDN_HEADS * GDN_DK
GDN_VW = GDN_HEADS * GDN_DV
NSA_QW = NSA_HEADS * NSA_DH
NSA_KVW = NSA_KV_HEADS * NSA_DH

LANE = 128
HEAD_PAD = LANE
NEG_BIG = -1e30
VMEM_LIMIT = 56 * 1024 * 1024

Q_BLOCK = 128
SEL_KEY_BLOCK = 256
SEL_COL_SPLIT = 4
FFN_CHUNK = 256
ROW_TILE = 512
GDN_CHUNK_SHIFT = 6
GDN_PAIR = 2 * GDN_CHUNK
GDN_UNROLL = 4


def _cparams(sem):
    return pltpu.CompilerParams(dimension_semantics=sem, vmem_limit_bytes=VMEM_LIMIT)


def _dot(a, b):
    return jnp.dot(a.astype(BF16), b.astype(BF16), preferred_element_type=F32)


def _dot_nt(a, b):
    return lax.dot_general(a.astype(BF16), b.astype(BF16), (((1,), (1,)), ((), ())),
                           preferred_element_type=F32)


def _dot_tn(a, b):
    return lax.dot_general(a.astype(BF16), b.astype(BF16), (((0,), (0,)), ((), ())),
                           preferred_element_type=F32)


def _sigmoid(x):
    return jax.nn.sigmoid(x)


def _softplus(x):
    return jnp.maximum(x, 0.0) + jnp.log1p(jnp.exp(-jnp.abs(x)))


def _layer_norm(x, g, b):
    mu = jnp.mean(x, axis=-1, keepdims=True)
    xc = x - mu
    var = jnp.mean(xc * xc, axis=-1, keepdims=True)
    return xc * lax.rsqrt(var + LN_EPS) * g + b


def _full_spec(shape):
    nd = len(shape)
    return pl.BlockSpec(shape, lambda *_: (0,) * nd)


def _inproj_kernel(x_ref, *refs):
    n = len(refs) // 2
    xb = x_ref[...].astype(BF16)
    for w_ref, o_ref in zip(refs[:n], refs[n:]):
        o_ref[...] = jnp.dot(xb, w_ref[...], preferred_element_type=F32).astype(o_ref.dtype)


def _inproj(x2d, weights, out_dtypes):
    m = x2d.shape[0]
    tm = ROW_TILE
    in_specs = [pl.BlockSpec((tm, D_MODEL), lambda i: (i, 0))]
    in_specs += [_full_spec(w.shape) for w in weights]
    out_specs = [pl.BlockSpec((tm, w.shape[1]), lambda i: (i, 0)) for w in weights]
    out_shape = [jax.ShapeDtypeStruct((m, w.shape[1]), dt) for w, dt in zip(weights, out_dtypes)]
    return pl.pallas_call(
        _inproj_kernel, grid=(m // tm,), in_specs=in_specs, out_specs=out_specs, out_shape=out_shape,
        compiler_params=_cparams(("parallel",)), name="inproj")(x2d, *weights)


def _gdn_kernel(q_ref, k_ref, v_ref, z_ref, cwq_ref, cwk_ref, cwv_ref, gcol_ref, grow_ref,
                alog_ref, dtb_ref, nw_ref, o_ref,
                xpad, qh, kh, vh, xp, qk_s, rhs, kd, qd, gl, sol, mq, ns, o0, st, *, seq):
    c = GDN_CHUNK
    pr = GDN_PAIR
    npair = seq // pr
    head = pl.program_id(1)
    pad = 8
    conv_rows = 256
    xpad[0:pad, :] = jnp.zeros((pad, 3 * LANE), F32)
    xpad[pad:, 0:LANE] = q_ref[...].astype(F32)
    xpad[pad:, LANE:2 * LANE] = k_ref[...].astype(F32)
    xpad[pad:, 2 * LANE:3 * LANE] = v_ref[...].astype(F32)
    cw = jnp.concatenate([cwq_ref[...], cwk_ref[...], cwv_ref[...]], axis=1)

    for t0 in range(0, seq, conv_rows):
        acc = jnp.zeros((conv_rows, 3 * LANE), F32)
        for j in range(GDN_CONV):
            xs = xpad[t0 + pad - (GDN_CONV - 1) + j:t0 + pad - (GDN_CONV - 1) + j + conv_rows, :]
            acc = acc + xs * cw[j:j + 1, :]
        y = acc * _sigmoid(acc)
        yq, yk, yv = y[:, 0:LANE], y[:, LANE:2 * LANE], y[:, 2 * LANE:3 * LANE]
        yq = yq * lax.rsqrt(jnp.sum(yq * yq, axis=-1, keepdims=True) + RMS_EPS) * (GDN_DK ** -0.5)
        yk = yk * lax.rsqrt(jnp.sum(yk * yk, axis=-1, keepdims=True) + RMS_EPS)
        qh[t0:t0 + conv_rows, :] = yq
        kh[t0:t0 + conv_rows, :] = yk
        vh[t0:t0 + conv_rows, :] = yv

    neg_a = -jnp.exp(alog_ref[0, :, 0:1])
    dtb = dtb_ref[0, :, 0:1]
    ii = lax.broadcasted_iota(jnp.int32, (pr, pr), 0)
    jj = lax.broadcasted_iota(jnp.int32, (pr, pr), 1)
    same = jnp.right_shift(ii, GDN_CHUNK_SHIFT) == jnp.right_shift(jj, GDN_CHUNK_SHIFT)
    incl = same & (jj <= ii)
    strict = same & (jj < ii)
    upper = same & (ii <= jj)
    eye = (ii == jj).astype(F32)
    sub8 = lax.broadcasted_iota(jnp.int32, (8, LANE), 0)
    lane_id = lax.broadcasted_iota(jnp.int32, (1, LANE), 1)

    def over_pairs(body):
        def run(gi, carry):
            for uu in range(GDN_UNROLL):
                body(gi * GDN_UNROLL + uu)
            return carry
        lax.fori_loop(0, npair // GDN_UNROLL, run, 0)

    def stage_gates(p):
        rows = pl.ds(pl.multiple_of(p * pr, pr), pr)
        q = qh[rows, :]
        k = kh[rows, :]
        v = vh[rows, :]
        gcol = gcol_ref[rows, :]
        b_logit = jnp.sum(jnp.where(lane_id == head, gcol, 0.0), axis=1, keepdims=True)
        a_logit = jnp.sum(jnp.where(lane_id == GDN_HEADS + head, gcol, 0.0), axis=1, keepdims=True)
        a_row = jnp.sum(jnp.where(sub8[:, 0:1] == GDN_HEADS + head, grow_ref[0, p], 0.0), axis=0, keepdims=True)
        beta = _sigmoid(b_logit)
        g_c = neg_a * _softplus(a_logit + dtb)
        g_r = neg_a * _softplus(a_row + dtb)
        gc_col = jnp.sum(jnp.where(incl, g_r, 0.0), axis=1, keepdims=True)
        gc_row = jnp.sum(jnp.where(upper, g_c, 0.0), axis=0, keepdims=True)
        gt_col = jnp.sum(jnp.where(same, g_r, 0.0), axis=1, keepdims=True)
        decay = jnp.where(incl, jnp.exp(jnp.where(incl, gc_col - gc_row, 0.0)), 0.0)
        kq_k = _dot_nt(jnp.concatenate([k, q], axis=0), k)
        a_mat = jnp.where(strict, kq_k[0:pr] * decay * beta, 0.0)
        xp[p, 0:pr, :] = eye - a_mat
        xp[p, pr:2 * pr, :] = _dot(a_mat, a_mat)
        qk_s[p] = (kq_k[pr:2 * pr] * decay).astype(qk_s.dtype)
        e_gc = jnp.exp(gc_col)
        rhs[rows, 0:GDN_DV] = (v * beta).astype(rhs.dtype)
        rhs[rows, GDN_DV:GDN_DV + GDN_DK] = (k * (beta * e_gc)).astype(rhs.dtype)
        kd[rows, :] = (k * jnp.exp(gt_col - gc_col)).astype(kd.dtype)
        qd[rows, :] = q * e_gc
        g_last = jnp.exp(gt_col)
        gl[p] = jnp.where(sub8 == 0, g_last[0:1, :], g_last[c:c + 1, :])

    def stage_double(p):
        y = _dot(xp[p], xp[p, pr:2 * pr, :])
        xp[p, 0:pr, :] = xp[p, 0:pr, :] + y[0:pr]
        xp[p, pr:2 * pr, :] = y[pr:2 * pr]

    def stage_solve(p):
        rows = pl.ds(pl.multiple_of(p * pr, pr), pr)
        x_inv = xp[p, 0:pr, :]
        x_inv = x_inv + _dot(x_inv, xp[p, pr:2 * pr, :])
        sol[rows, :] = _dot(x_inv, rhs[rows, :])

    def stage_affine(p):
        r0 = pl.multiple_of(p * pr, pr)
        rows = pl.ds(r0, pr)
        e = _dot(qk_s[p], sol[rows, :])
        qe = qd[rows, :] - e[:, GDN_DV:GDN_DV + GDN_DK]
        g_rows = gl[p]
        for cc in range(pr // c):
            rc = pl.ds(r0 + cc * c, c)
            mn = _dot_tn(kd[rc, :], sol[rc, :])
            idx = p * (pr // c) + cc
            mq[head, idx, 0:GDN_DK, :] = (eye * g_rows[cc:cc + 1, :] - mn[:, GDN_DV:GDN_DV + GDN_DK]).astype(mq.dtype)
            mq[head, idx, GDN_DK:GDN_DK + c, :] = qe[cc * c:(cc + 1) * c].astype(mq.dtype)
            ns[head, idx] = mn[:, 0:GDN_DV]
            o0[head, idx] = e[cc * c:(cc + 1) * c, 0:GDN_DV]

    over_pairs(stage_gates)
    for _ in range(int(math.log2(c)) - 2):
        over_pairs(stage_double)
    over_pairs(stage_solve)
    over_pairs(stage_affine)

    @pl.when(head == GDN_HEADS - 1)
    def _():
        st[...] = jnp.zeros_like(st)
        nw = nw_ref[...]

        def step(ci, carry):
            r0 = pl.multiple_of(ci * c, c)
            for h in range(GDN_HEADS):
                cols = slice(h * LANE, (h + 1) * LANE)
                ys = jnp.dot(mq[h, ci], st[h], preferred_element_type=F32)
                st[h] = (ys[0:GDN_DK] + ns[h, ci]).astype(st.dtype)
                o = ys[GDN_DK:GDN_DK + c] + o0[h, ci]
                o = o * lax.rsqrt(jnp.mean(o * o, axis=-1, keepdims=True) + RMS_EPS) * nw
                z = z_ref[pl.ds(r0, c), cols].astype(F32)
                o_ref[pl.ds(r0, c), cols] = (o * (z * _sigmoid(z))).astype(o_ref.dtype)
            return carry

        lax.fori_loop(0, seq // c, step, 0)


def _gdn(gqkvz, conv_w, gcol, grow, alog, dtb, nw, bsz, seq):
    assert GDN_DK == GDN_PAIR == GDN_DV == LANE
    h = GDN_HEADS
    npair = seq // GDN_PAIR
    nchunk = seq // GDN_CHUNK
    blk = lambda off: pl.BlockSpec((seq, LANE), lambda b, hh: (b, off + hh))
    cwb = lambda off: pl.BlockSpec((GDN_CONV, LANE), lambda b, hh: (0, off + hh))
    in_specs = [blk(0), blk(h), blk(2 * h),
                pl.BlockSpec((seq, GDN_VW), lambda b, hh: (b, 3)),
                cwb(0), cwb(h), cwb(2 * h),
                pl.BlockSpec((seq, LANE), lambda b, hh: (b, 0)),
                pl.BlockSpec((1, npair, 8, GDN_PAIR), lambda b, hh: (b, 0, 0, 0)),
                pl.BlockSpec((1, 1, LANE), lambda b, hh: (hh, 0, 0)),
                pl.BlockSpec((1, 1, LANE), lambda b, hh: (hh, 0, 0)),
                pl.BlockSpec((1, LANE), lambda b, hh: (0, 0))]
    scratch = [pltpu.VMEM((seq + 8, 3 * LANE), F32),
               pltpu.VMEM((seq, LANE), F32), pltpu.VMEM((seq, LANE), F32), pltpu.VMEM((seq, LANE), F32),
               pltpu.VMEM((npair, 2 * GDN_PAIR, GDN_PAIR), F32),
               pltpu.VMEM((npair, GDN_PAIR, GDN_PAIR), BF16),
               pltpu.VMEM((seq, GDN_DV + GDN_DK), BF16),
               pltpu.VMEM((seq, GDN_DK), BF16),
               pltpu.VMEM((seq, GDN_DK), F32),
               pltpu.VMEM((npair, 8, LANE), F32),
               pltpu.VMEM((seq, GDN_DV + GDN_DK), F32),
               pltpu.VMEM((h, nchunk, GDN_DK + GDN_CHUNK, GDN_DK), BF16),
               pltpu.VMEM((h, nchunk, GDN_DK, GDN_DV), F32),
               pltpu.VMEM((h, nchunk, GDN_CHUNK, GDN_DV), F32),
               pltpu.VMEM((h, GDN_DK, GDN_DV), BF16)]
    return pl.pallas_call(
        functools.partial(_gdn_kernel, seq=seq), grid=(bsz, h), in_specs=in_specs,
        out_specs=pl.BlockSpec((seq, GDN_VW), lambda b, hh: (b, 0)),
        out_shape=jax.ShapeDtypeStruct((bsz * seq, GDN_VW), BF16),
        scratch_shapes=scratch,
        compiler_params=_cparams(("parallel", "arbitrary")), name="gdn",
    )(gqkvz, gqkvz, gqkvz, gqkvz, conv_w, conv_w, conv_w, gcol, grow, alog, dtb, nw)


def _gelu_tanh(x):
    return 0.5 * x * (1.0 + jnp.tanh(math.sqrt(2.0 / math.pi) * (x + 0.044715 * x * x * x)))


def _compress_kernel(k16_ref, v16_ref, wkc_ref, wvc_ref, w1k_ref, w1v_ref, pk_ref, pv_ref,
                     w2k_ref, w2v_ref, kc_ref, vc_ref):
    nblk = k16_ref.shape[1]
    for x_ref, wc_ref, w1_ref, p_ref, w2_ref, o_ref in (
            (k16_ref, wkc_ref, w1k_ref, pk_ref, w2k_ref, kc_ref),
            (v16_ref, wvc_ref, w1v_ref, pv_ref, w2v_ref, vc_ref)):
        y = jnp.dot(x_ref[0], wc_ref[...], preferred_element_type=F32)
        pos_term = jnp.dot(p_ref[...], w1_ref[...], preferred_element_type=F32)[0:1, :]
        for g in range(NSA_KV_HEADS):
            top = y[:, g * CMP_HIDDEN:(g + 1) * CMP_HIDDEN]
            bot = y[:, (NSA_KV_HEADS + g) * CMP_HIDDEN:(NSA_KV_HEADS + g + 1) * CMP_HIDDEN]
            hid = top + pltpu.roll(bot, nblk - 1, 0) + pos_term
            o_ref[0, g] = _dot(_gelu_tanh(hid), w2_ref[...]).astype(o_ref.dtype)


def _compress(k16, v16, wkc, wvc, w1k, w1v, pk, pv, w2k, w2v):
    bsz, nblk, width = k16.shape
    in_specs = [pl.BlockSpec((1, nblk, width), lambda b: (b, 0, 0))] * 2
    in_specs += [_full_spec(a.shape) for a in (wkc, wvc, w1k, w1v, pk, pv, w2k, w2v)]
    out_spec = pl.BlockSpec((1, NSA_KV_HEADS, nblk, HEAD_PAD), lambda b: (b, 0, 0, 0))
    out_shape = jax.ShapeDtypeStruct((bsz, NSA_KV_HEADS, nblk, HEAD_PAD), BF16)
    return pl.pallas_call(
        _compress_kernel, grid=(bsz,), in_specs=in_specs, out_specs=[out_spec, out_spec],
        out_shape=[out_shape, out_shape], compiler_params=_cparams(("parallel",)), name="nsa_compress",
    )(k16, v16, wkc, wvc, w1k, w1v, pk, pv, w2k, w2v)


def _nsa_kernel(q_ref, kv_ref, kc_ref, vc_ref, sm_ref, ovt_ref, o_ref, ksa, vst, vwa, *, seq):
    qb = pl.program_id(1)
    q0 = qb * Q_BLOCK
    cols = NSA_REP * Q_BLOCK
    n_cmp_pad = kc_ref.shape[2]
    n_sel_blk = seq // SEL_BLOCK
    kb_rows = SEL_KEY_BLOCK
    one_row = NSA_DH

    @pl.when(qb == 0)
    def _():
        lane = lax.broadcasted_iota(jnp.int32, (1, NSA_KV_HEADS * HEAD_PAD), 1) & (HEAD_PAD - 1)
        ones_lane = (lane == one_row).astype(vwa.dtype)
        row_t = lax.broadcasted_iota(jnp.int32, (HEAD_PAD, 1), 0)

        def fill(bi, carry):
            r0 = pl.multiple_of(bi * kb_rows, kb_rows)
            rows = pl.ds(r0, kb_rows)
            blk = jnp.right_shift(r0 + lax.broadcasted_iota(jnp.int32, (kb_rows, 1), 0), SEL_SHIFT)
            ksa[rows, :] = kv_ref[rows, 0:2 * HEAD_PAD] + (lane == NSA_DH + blk).astype(ksa.dtype)
            vwa[rows, :] = kv_ref[rows, 6 * HEAD_PAD:8 * HEAD_PAD] + ones_lane
            for g in range(NSA_KV_HEADS):
                v_t = jnp.transpose(kv_ref[rows, (2 + g) * HEAD_PAD:(3 + g) * HEAD_PAD].astype(F32))
                vst[bi, g] = jnp.where(row_t == one_row, 1.0, v_t).astype(vst.dtype)
            return carry

        lax.fori_loop(0, seq // kb_rows, fill, 0)

    tok = lax.broadcasted_iota(jnp.int32, (1, Q_BLOCK), 1)
    t_lane = q0 + tok
    gates_t = jnp.transpose(_sigmoid(sm_ref[...]))
    lane_q = lax.broadcasted_iota(jnp.int32, (1, HEAD_PAD), 1)
    pen_lanes = (lane_q >= NSA_DH) & (lane_q < NSA_DH + n_sel_blk)
    place = (lax.broadcasted_iota(jnp.int32, (n_sel_blk, HEAD_PAD), 1)
             == NSA_DH + lax.broadcasted_iota(jnp.int32, (n_sel_blk, HEAD_PAD), 0)).astype(BF16)

    def add_bias(s, bias):
        return jnp.concatenate([s[:, r * Q_BLOCK:(r + 1) * Q_BLOCK] + bias for r in range(s.shape[1] // Q_BLOCK)],
                               axis=1)

    n_sub = lax.broadcasted_iota(jnp.int32, (n_cmp_pad, 1), 0)
    cmp_ok = (n_sub * CMP_STRIDE + (CMP_BLOCK - 1)) <= t_lane
    cmp_bias = jnp.where(cmp_ok, 0.0, NEG_BIG)
    cmp_any = jnp.concatenate([(t_lane >= CMP_BLOCK - 1).astype(F32)] * NSA_REP, axis=1)

    qp_g, o_cmp_g = [], []
    for g in range(NSA_KV_HEADS):
        qs = jnp.concatenate(
            [q_ref[:, (g * NSA_REP + r) * HEAD_PAD:(g * NSA_REP + r + 1) * HEAD_PAD] for r in range(NSA_REP)],
            axis=0)

        s_c = add_bias(_dot_nt(kc_ref[0, g], qs), cmp_bias)
        e_c = jnp.exp2(s_c - jnp.max(s_c, axis=0, keepdims=True))
        p_c = e_c * (cmp_any / jnp.sum(e_c, axis=0, keepdims=True))
        o_cmp_g.append(_dot_tn(vc_ref[0, g], p_c))

        p_sum = p_c[:, 0:Q_BLOCK]
        for r in range(1, NSA_REP):
            p_sum = p_sum + p_c[:, r * Q_BLOCK:(r + 1) * Q_BLOCK]
        imp_t = _dot(ovt_ref[...], p_sum)

        def rank_select(imp_t=imp_t):
            j_sub = lax.broadcasted_iota(jnp.int32, (n_sel_blk, 1), 0)
            cur = jnp.right_shift(t_lane, SEL_SHIFT)
            forced = (j_sub == 0) | (j_sub == cur) | (j_sub == cur - 1)
            causal_blk = (j_sub * SEL_BLOCK) <= t_lane
            imp = jnp.where(forced, FORCE_SCORE, jnp.where(causal_blk, imp_t, -jnp.inf))
            rank = jnp.zeros((n_sel_blk, Q_BLOCK), F32)
            for jp in range(n_sel_blk):
                row = imp[jp:jp + 1, :]
                ahead = (row > imp) | ((row == imp) & (j_sub > jp))
                rank = rank + ahead.astype(F32)
            return (rank < SEL_TOPK).astype(BF16)

        sel_t = lax.cond(q0 + Q_BLOCK <= SEL_TOPK * SEL_BLOCK,
                         lambda: jnp.ones((n_sel_blk, Q_BLOCK), BF16), rank_select)
        sel_lanes = _dot_tn(sel_t, place)
        pen = jnp.where(pen_lanes, (sel_lanes - 1.0) * (-NEG_BIG), 0.0).astype(BF16)
        qp_g.append(qs + jnp.concatenate([pen] * NSA_REP, axis=0))

    half = cols // SEL_COL_SPLIT

    def sel_block(kb, carry, causal_bias):
        k0 = pl.multiple_of(kb * kb_rows, kb_rows)
        out = []
        for g in range(NSA_KV_HEADS):
            k_blk = ksa[pl.ds(k0, kb_rows), g * HEAD_PAD:(g + 1) * HEAD_PAD]
            v_t = vst[kb, g]
            for hf in range(SEL_COL_SPLIT):
                m_i, acc = carry[g * SEL_COL_SPLIT + hf]
                s = _dot_nt(k_blk, qp_g[g][hf * half:(hf + 1) * half])
                if causal_bias is not None:
                    s = add_bias(s, causal_bias)
                m_new = jnp.maximum(m_i, jnp.max(s, axis=0, keepdims=True))
                p = jnp.exp2(s - m_new)
                out.append((m_new, jnp.exp2(m_i - m_new) * acc + _dot(v_t, p)))
        return tuple(out)

    n_full = lax.div(q0, kb_rows)
    init = (jnp.full((1, half), NEG_BIG, F32), jnp.zeros((HEAD_PAD, half), F32))
    carry = lax.fori_loop(0, n_full, lambda kb, c: sel_block(kb, c, None),
                          (init,) * (NSA_KV_HEADS * SEL_COL_SPLIT))
    kpos_last = n_full * kb_rows + lax.broadcasted_iota(jnp.int32, (kb_rows, 1), 0)
    sel_out = sel_block(n_full, carry, jnp.where(kpos_last <= t_lane, 0.0, NEG_BIG))

    span = WINDOW + Q_BLOCK
    w0 = pl.multiple_of(jnp.maximum(q0 - WINDOW, 0), Q_BLOCK)
    kpos_w = w0 + lax.broadcasted_iota(jnp.int32, (span, 1), 0)
    win_bias = jnp.where((kpos_w <= t_lane) & (kpos_w > t_lane - WINDOW), 0.0, NEG_BIG)
    for g in range(NSA_KV_HEADS):
        acc_s = jnp.concatenate([sel_out[g * SEL_COL_SPLIT + hf][1] for hf in range(SEL_COL_SPLIT)], axis=1)
        o_slc = acc_s * (1.0 / acc_s[one_row:one_row + 1, :])

        k_w = kv_ref[pl.ds(w0, span), (4 + g) * HEAD_PAD:(5 + g) * HEAD_PAD]
        s_w = add_bias(_dot_nt(k_w, qp_g[g]), win_bias)
        acc_w = _dot_tn(vwa[pl.ds(w0, span), g * HEAD_PAD:(g + 1) * HEAD_PAD],
                        jnp.exp2(s_w - jnp.max(s_w, axis=0, keepdims=True)))
        o_win = acc_w * (1.0 / acc_w[one_row:one_row + 1, :])

        for r in range(NSA_REP):
            head = g * NSA_REP + r
            c0 = 8 + head * 3
            sl = slice(r * Q_BLOCK, (r + 1) * Q_BLOCK)
            o_t = (gates_t[c0:c0 + 1, :] * o_cmp_g[g][:, sl] + gates_t[c0 + 1:c0 + 2, :] * o_slc[:, sl]
                   + gates_t[c0 + 2:c0 + 3, :] * o_win[:, sl])
            o_ref[:, head * HEAD_PAD:(head + 1) * HEAD_PAD] = jnp.transpose(o_t).astype(o_ref.dtype)


def _nsa(nq, nkv, kc, vc, small, overlap, bsz, seq):
    nqb = seq // Q_BLOCK
    in_specs = [pl.BlockSpec((Q_BLOCK, NSA_HEADS * HEAD_PAD), lambda b, i: (b * nqb + i, 0)),
                pl.BlockSpec((seq, nkv.shape[1]), lambda b, i: (b, 0)),
                pl.BlockSpec((1,) + kc.shape[1:], lambda b, i: (b, 0, 0, 0)),
                pl.BlockSpec((1,) + vc.shape[1:], lambda b, i: (b, 0, 0, 0)),
                pl.BlockSpec((Q_BLOCK, LANE), lambda b, i: (b * nqb + i, 0)),
                _full_spec(overlap.shape)]
    return pl.pallas_call(
        functools.partial(_nsa_kernel, seq=seq), grid=(bsz, nqb), in_specs=in_specs,
        out_specs=pl.BlockSpec((Q_BLOCK, NSA_HEADS * HEAD_PAD), lambda b, i: (b * nqb + i, 0)),
        out_shape=jax.ShapeDtypeStruct((bsz * seq, NSA_HEADS * HEAD_PAD), BF16),
        scratch_shapes=[pltpu.VMEM((seq, NSA_KV_HEADS * HEAD_PAD), BF16),
                        pltpu.VMEM((seq // SEL_KEY_BLOCK, NSA_KV_HEADS, HEAD_PAD, SEL_KEY_BLOCK), BF16),
                        pltpu.VMEM((seq, NSA_KV_HEADS * HEAD_PAD), BF16)],
        compiler_params=_cparams(("parallel", "arbitrary")), name="nsa_attention",
    )(nq, nkv, kc, vc, small, overlap)


def _merge_kernel(oa_ref, ob_ref, mix_ref, x_ref, wa_ref, wb_ref, wo_ref, g_ref, b_ref, h_ref):
    ya = jnp.dot(oa_ref[...], wa_ref[...], preferred_element_type=F32)
    yb = jnp.dot(ob_ref[...], wb_ref[...], preferred_element_type=F32)
    mix = mix_ref[...].astype(F32)
    mixed = _sigmoid(mix[:, 0:D_MODEL]) * ya + _sigmoid(mix[:, D_MODEL:2 * D_MODEL]) * yb
    y = _dot(mixed, wo_ref[...])
    h_ref[...] = _layer_norm(DN_ALPHA * x_ref[...] + y, g_ref[...], b_ref[...])


def _merge(oa, ob, mix, x2d, wa, wb, wo, g, b):
    m = x2d.shape[0]
    tm = ROW_TILE
    row = lambda n: pl.BlockSpec((tm, n), lambda i: (i, 0))
    in_specs = [row(oa.shape[1]), row(ob.shape[1]), row(mix.shape[1]), row(D_MODEL),
                _full_spec(wa.shape), _full_spec(wb.shape), _full_spec(wo.shape),
                _full_spec(g.shape), _full_spec(b.shape)]
    return pl.pallas_call(
        _merge_kernel, grid=(m // tm,), in_specs=in_specs, out_specs=row(D_MODEL),
        out_shape=jax.ShapeDtypeStruct((m, D_MODEL), F32),
        compiler_params=_cparams(("parallel",)), name="merge_ln",
    )(oa, ob, mix, x2d, wa, wb, wo, g, b)


def _ffn_kernel(h_ref, halo_ref, wup_ref, cw_ref, wd_ref, g_ref, b_ref, o_ref,
                hb, ug, uv, act, *, seq, tm, halo):
    i = pl.program_id(0)
    first = lax.rem(i * tm, seq) == 0
    hb[0:halo, :] = jnp.where(first, 0.0, halo_ref[...]).astype(BF16)
    hb[halo:, :] = h_ref[...].astype(BF16)

    def conv(u_ref, slot, cols):
        out = None
        for tap in range(FFN_CONV):
            term = cw_ref[tap:tap + 1, cols] * u_ref[slot, pl.ds(halo - (FFN_CONV - 1) + tap, tm), :]
            out = term if out is None else out + term
        return out

    for j in range(D_FF // FFN_CHUNK):
        slot = j % 2
        cols_g = slice(j * FFN_CHUNK, (j + 1) * FFN_CHUNK)
        cols_v = slice(D_FF + j * FFN_CHUNK, D_FF + (j + 1) * FFN_CHUNK)
        ug[slot] = jnp.dot(hb[...], wup_ref[:, cols_g], preferred_element_type=F32)
        uv[slot] = jnp.dot(hb[...], wup_ref[:, cols_v], preferred_element_type=F32)
        a = conv(ug, slot, cols_g)
        act[:, cols_g] = (a * _sigmoid(a) * conv(uv, slot, cols_v)).astype(act.dtype)

    f = jnp.dot(act[...], wd_ref[...], preferred_element_type=F32)
    o_ref[...] = _layer_norm(DN_ALPHA * h_ref[...] + f, g_ref[...], b_ref[...])


def _ffn(h, w_up, conv_w, w_down, g, b, seq):
    m = h.shape[0]
    tm = ROW_TILE
    halo = 16
    resident = lambda shape: pl.BlockSpec(shape, lambda i: (0, 0), pipeline_mode=pl.Buffered(1))
    in_specs = [pl.BlockSpec((tm, D_MODEL), lambda i: (i, 0)),
                pl.BlockSpec((halo, D_MODEL), lambda i: (jnp.maximum(i * (tm // halo) - 1, 0), 0)),
                resident(w_up.shape), resident(conv_w.shape), resident(w_down.shape),
                resident(g.shape), resident(b.shape)]
    return pl.pallas_call(
        functools.partial(_ffn_kernel, seq=seq, tm=tm, halo=halo), grid=(m // tm,),
        in_specs=in_specs, out_specs=pl.BlockSpec((tm, D_MODEL), lambda i: (i, 0)),
        out_shape=jax.ShapeDtypeStruct((m, D_MODEL), F32),
        scratch_shapes=[pltpu.VMEM((tm + halo, D_MODEL), BF16),
                        pltpu.VMEM((2, tm + halo, FFN_CHUNK), F32), pltpu.VMEM((2, tm + halo, FFN_CHUNK), F32),
                        pltpu.VMEM((tm, D_FF), BF16)],
        compiler_params=_cparams(("parallel",)), name="conv_ffn_ln",
    )(h, h, w_up, conv_w, w_down, g, b)


def _pad_heads(w, n_heads):
    lead = w.shape[:-1]
    w = w.reshape(lead + (n_heads, NSA_DH))
    w = jnp.pad(w, [(0, 0)] * len(lead) + [(0, 0), (0, HEAD_PAD - NSA_DH)])
    return w.reshape(lead + (n_heads * HEAD_PAD,))


def _compress_weights(w1):
    half = CMP_BLOCK // 2
    w1r = w1.reshape(2, half, NSA_DH, CMP_HIDDEN)
    eye = jnp.eye(NSA_KV_HEADS, dtype=w1.dtype)
    ex = jnp.einsum("aidh,gk->agikdh", w1r, eye)
    ex = ex.reshape(2 * NSA_KV_HEADS, half * NSA_KVW, CMP_HIDDEN)
    return jnp.concatenate([ex[n] for n in range(2 * NSA_KV_HEADS)], axis=1)


def _overlap_matrix(seq):
    n_cmp = (seq - CMP_BLOCK) // CMP_STRIDE + 1
    n_sel = seq // SEL_BLOCK
    starts = np.arange(n_cmp) * CMP_STRIDE
    jb = np.arange(n_sel) * SEL_BLOCK
    ov = ((starts[:, None] < jb[None] + SEL_BLOCK) & (starts[:, None] + CMP_BLOCK > jb[None])).astype(np.float32)
    out = np.zeros((n_sel, seq // CMP_STRIDE), np.float32)
    out[:, :n_cmp] = ov.T
    return jnp.asarray(out, BF16)


def kernel(x, w_in, gdn_conv_w, gdn_a_log, gdn_dt_bias, gdn_norm_w, cmp_pos_k, cmp_w1_k, cmp_w2_k,
           cmp_pos_v, cmp_w1_v, cmp_w2_v, w_branch_gdn, w_branch_nsa, w_out, ln1_g, ln1_b, w_up,
           ffn_conv_w, w_down, ln2_g, ln2_b):
    bsz, seq, _ = x.shape
    m = bsz * seq
    for i in range(DEPTH):
        x2d = x.reshape(m, D_MODEL)
        w = w_in[i]
        o_gdn = 2 * GDN_QK + 2 * GDN_VW
        o_nq = o_gdn + 2 * GDN_HEADS
        o_kv = o_nq + NSA_QW
        o_gate = o_kv + 6 * NSA_KVW
        o_mix = o_gate + 3 * NSA_HEADS
        kv = [w[:, o_kv + n * NSA_KVW:o_kv + (n + 1) * NSA_KVW] for n in range(6)]
        small_w = jnp.concatenate([w[:, o_gdn:o_nq], w[:, o_gate:o_mix]], axis=1)
        small_w = jnp.pad(small_w, ((0, 0), (0, LANE - small_w.shape[1])))
        weights = [w[:, :o_gdn],
                   _pad_heads(w[:, o_nq:o_kv] * (NSA_DH ** -0.5 * LOG2E), NSA_HEADS),
                   jnp.concatenate([_pad_heads(t, NSA_KV_HEADS) for t in kv[2:]], axis=1),
                   jnp.concatenate(kv[:2], axis=1),
                   w[:, o_mix:],
                   small_w]
        weights = [t.astype(BF16) for t in weights]
        gqkvz, nq, nkv, kvc, mix, small = _inproj(x2d, weights, [BF16, BF16, BF16, BF16, BF16, F32])

        grow = small[:, :2 * GDN_HEADS].reshape(bsz, seq // GDN_PAIR, GDN_PAIR, 2 * GDN_HEADS).transpose(0, 1, 3, 2)
        alog = jnp.broadcast_to(gdn_a_log[i][:, None, None], (GDN_HEADS, 1, LANE))
        dtb = jnp.broadcast_to(gdn_dt_bias[i][:, None, None], (GDN_HEADS, 1, LANE))
        o_a = _gdn(gqkvz, gdn_conv_w[i], small, grow, alog, dtb, gdn_norm_w[i][None, :], bsz, seq)

        grp = seq // CMP_STRIDE
        k16 = kvc[:, :NSA_KVW].reshape(bsz, grp, CMP_STRIDE * NSA_KVW)
        v16 = kvc[:, NSA_KVW:].reshape(bsz, grp, CMP_STRIDE * NSA_KVW)
        pos_rows = lambda p: jnp.pad(p.reshape(1, CMP_BLOCK * NSA_DH), ((0, 7), (0, 0))).astype(BF16)
        w2_pad = lambda t: jnp.pad(t, ((0, 0), (0, HEAD_PAD - NSA_DH))).astype(BF16)
        kc, vc = _compress(k16, v16,
                           _compress_weights(cmp_w1_k[i]).astype(BF16), _compress_weights(cmp_w1_v[i]).astype(BF16),
                           cmp_w1_k[i].astype(BF16), cmp_w1_v[i].astype(BF16),
                           pos_rows(cmp_pos_k[i]), pos_rows(cmp_pos_v[i]),
                           w2_pad(cmp_w2_k[i]), w2_pad(cmp_w2_v[i]))
        o_b = _nsa(nq, nkv, kc, vc, small, _overlap_matrix(seq), bsz, seq)

        wb = jnp.pad(w_branch_nsa[i].reshape(NSA_HEADS, NSA_DH, D_MODEL),
                     ((0, 0), (0, HEAD_PAD - NSA_DH), (0, 0))).reshape(NSA_HEADS * HEAD_PAD, D_MODEL)
        h = _merge(o_a, o_b, mix, x2d, w_branch_gdn[i].astype(BF16), wb.astype(BF16), w_out[i].astype(BF16),
                   ln1_g[i][None, :], ln1_b[i][None, :])

        x = _ffn(h, w_up[i].astype(BF16), ffn_conv_w[i], w_down[i].astype(BF16),
                 ln2_g[i][None, :], ln2_b[i][None, :], seq).reshape(bsz, seq, D_MODEL)
    return x
```

```python
import functools
import math

import numpy as np
import jax
import jax.numpy as jnp
from jax import lax
from jax.experimental import pallas as pl
from jax.experimental.pallas import tpu as pltpu

F32 = jnp.float32
BF16 = jnp.bfloat16

D_MODEL = 1024
GDN_HEADS = 4
GDN_DK = 128
GDN_DV = 128
GDN_CONV = 4
GDN_CHUNK = 64
NSA_HEADS = 8
NSA_KV_HEADS = 2
NSA_REP = NSA_HEADS // NSA_KV_HEADS
NSA_DH = 64
CMP_BLOCK = 32
CMP_STRIDE = 16
CMP_HIDDEN = 256
SEL_BLOCK = 64
SEL_SHIFT = 6
SEL_TOPK = 16
WINDOW = 512
FORCE_SCORE = 1e9
D_FF = 2816
FFN_CONV = 3
DEPTH = 1
DN_ALPHA = (2 * DEPTH) ** 0.25
LN_EPS = 1e-5
RMS_EPS = 1e-6
LOG2E = math.log2(math.e)

GDN_QK = GDN_HEADS * GDN_DK
GDN_VW = GDN_HEADS * GDN_DV
NSA_QW = NSA_HEADS * NSA_DH
NSA_KVW = NSA_KV_HEADS * NSA_DH

LANE = 128
HEAD_PAD = LANE
NEG_BIG = -1e30
VMEM_LIMIT = 56 * 1024 * 1024

Q_BLOCK = 128
SEL_KEY_BLOCK = 256
FFN_CHUNK = 256
ROW_TILE = 512
GDN_CHUNK_SHIFT = 6
GDN_PAIR = 2 * GDN_CHUNK
GDN_UNROLL = 16


def _cparams(sem):
    return pltpu.CompilerParams(dimension_semantics=sem, vmem_limit_bytes=VMEM_LIMIT)


def _dot(a, b):
    return jnp.dot(a.astype(BF16), b.astype(BF16), preferred_element_type=F32)


def _dot_nt(a, b):
    return lax.dot_general(a.astype(BF16), b.astype(BF16), (((1,), (1,)), ((), ())),
                           preferred_element_type=F32)


def _dot_tn(a, b):
    return lax.dot_general(a.astype(BF16), b.astype(BF16), (((0,), (0,)), ((), ())),
                           preferred_element_type=F32)


def _sigmoid(x):
    return jax.nn.sigmoid(x)


def _softplus(x):
    return jnp.maximum(x, 0.0) + jnp.log1p(jnp.exp(-jnp.abs(x)))


def _layer_norm(x, g, b):
    mu = jnp.mean(x, axis=-1, keepdims=True)
    xc = x - mu
    var = jnp.mean(xc * xc, axis=-1, keepdims=True)
    return xc * lax.rsqrt(var + LN_EPS) * g + b


def _full_spec(shape):
    nd = len(shape)
    return pl.BlockSpec(shape, lambda *_: (0,) * nd)


def _inproj_kernel(x_ref, *refs):
    n = len(refs) // 2
    xb = x_ref[...].astype(BF16)
    for w_ref, o_ref in zip(refs[:n], refs[n:]):
        o_ref[...] = jnp.dot(xb, w_ref[...], preferred_element_type=F32).astype(o_ref.dtype)


def _inproj(x2d, weights, out_dtypes):
    m = x2d.shape[0]
    tm = ROW_TILE
    in_specs = [pl.BlockSpec((tm, D_MODEL), lambda i: (i, 0))]
    in_specs += [_full_spec(w.shape) for w in weights]
    out_specs = [pl.BlockSpec((tm, w.shape[1]), lambda i: (i, 0)) for w in weights]
    out_shape = [jax.ShapeDtypeStruct((m, w.shape[1]), dt) for w, dt in zip(weights, out_dtypes)]
    return pl.pallas_call(
        _inproj_kernel, grid=(m // tm,), in_specs=in_specs, out_specs=out_specs, out_shape=out_shape,
        compiler_params=_cparams(("parallel",)), name="inproj")(x2d, *weights)


def _gdn_kernel(q_ref, k_ref, v_ref, z_ref, cwq_ref, cwk_ref, cwv_ref, gcol_ref, grow_ref,
                alog_ref, dtb_ref, nw_ref, o_ref,
                xpad, qh, kh, vh, xp, qk_s, rhs, kd, qd, gl, sol, mq, ns, o0, st, *, seq):
    c = GDN_CHUNK
    pr = GDN_PAIR
    npair = seq // pr
    head = pl.program_id(1)
    pad = 8
    conv_rows = 256
    xpad[0:pad, :] = jnp.zeros((pad, 3 * LANE), F32)
    xpad[pad:, 0:LANE] = q_ref[...].astype(F32)
    xpad[pad:, LANE:2 * LANE] = k_ref[...].astype(F32)
    xpad[pad:, 2 * LANE:3 * LANE] = v_ref[...].astype(F32)
    cw = jnp.concatenate([cwq_ref[...], cwk_ref[...], cwv_ref[...]], axis=1)

    for t0 in range(0, seq, conv_rows):
        acc = jnp.zeros((conv_rows, 3 * LANE), F32)
        for j in range(GDN_CONV):
            xs = xpad[t0 + pad - (GDN_CONV - 1) + j:t0 + pad - (GDN_CONV - 1) + j + conv_rows, :]
            acc = acc + xs * cw[j:j + 1, :]
        y = acc * _sigmoid(acc)
        yq, yk, yv = y[:, 0:LANE], y[:, LANE:2 * LANE], y[:, 2 * LANE:3 * LANE]
        yq = yq * lax.rsqrt(jnp.sum(yq * yq, axis=-1, keepdims=True) + RMS_EPS) * (GDN_DK ** -0.5)
        yk = yk * lax.rsqrt(jnp.sum(yk * yk, axis=-1, keepdims=True) + RMS_EPS)
        qh[t0:t0 + conv_rows, :] = yq
        kh[t0:t0 + conv_rows, :] = yk
        vh[t0:t0 + conv_rows, :] = yv

    neg_a = -jnp.exp(alog_ref[0, :, 0:1])
    dtb = dtb_ref[0, :, 0:1]
    ii = lax.broadcasted_iota(jnp.int32, (pr, pr), 0)
    jj = lax.broadcasted_iota(jnp.int32, (pr, pr), 1)
    same = jnp.right_shift(ii, GDN_CHUNK_SHIFT) == jnp.right_shift(jj, GDN_CHUNK_SHIFT)
    incl = same & (jj <= ii)
    strict = same & (jj < ii)
    upper = same & (ii <= jj)
    eye = (ii == jj).astype(F32)
    sub8 = lax.broadcasted_iota(jnp.int32, (8, LANE), 0)
    lane_id = lax.broadcasted_iota(jnp.int32, (1, LANE), 1)

    def over_pairs(body):
        def run(gi, carry):
            for uu in range(GDN_UNROLL):
                body(gi * GDN_UNROLL + uu)
            return carry
        lax.fori_loop(0, npair // GDN_UNROLL, run, 0)

    def stage_gates(p):
        rows = pl.ds(pl.multiple_of(p * pr, pr), pr)
        q = qh[rows, :]
        k = kh[rows, :]
        v = vh[rows, :]
        gcol = gcol_ref[rows, :]
        b_logit = jnp.sum(jnp.where(lane_id == head, gcol, 0.0), axis=1, keepdims=True)
        a_logit = jnp.sum(jnp.where(lane_id == GDN_HEADS + head, gcol, 0.0), axis=1, keepdims=True)
        a_row = jnp.sum(jnp.where(sub8[:, 0:1] == GDN_HEADS + head, grow_ref[0, p], 0.0), axis=0, keepdims=True)
        beta = _sigmoid(b_logit)
        g_c = neg_a * _softplus(a_logit + dtb)
        g_r = neg_a * _softplus(a_row + dtb)
        gc_col = jnp.sum(jnp.where(incl, g_r, 0.0), axis=1, keepdims=True)
        gc_row = jnp.sum(jnp.where(upper, g_c, 0.0), axis=0, keepdims=True)
        gt_col = jnp.sum(jnp.where(same, g_r, 0.0), axis=1, keepdims=True)
        decay = jnp.where(incl, jnp.exp(jnp.where(incl, gc_col - gc_row, 0.0)), 0.0)
        kq_k = _dot_nt(jnp.concatenate([k, q], axis=0), k)
        a_mat = jnp.where(strict, kq_k[0:pr] * decay * beta, 0.0)
        xp[p, 0:pr, :] = eye - a_mat
        xp[p, pr:2 * pr, :] = _dot(a_mat, a_mat)
        qk_s[p] = (kq_k[pr:2 * pr] * decay).astype(qk_s.dtype)
        e_gc = jnp.exp(gc_col)
        rhs[rows, 0:GDN_DV] = (v * beta).astype(rhs.dtype)
        rhs[rows, GDN_DV:GDN_DV + GDN_DK] = (k * (beta * e_gc)).astype(rhs.dtype)
        kd[rows, :] = (k * jnp.exp(gt_col - gc_col)).astype(kd.dtype)
        qd[rows, :] = q * e_gc
        g_last = jnp.exp(gt_col)
        gl[p] = jnp.where(sub8 == 0, g_last[0:1, :], g_last[c:c + 1, :])

    def stage_double(p):
        y = _dot(xp[p], xp[p, pr:2 * pr, :])
        xp[p, 0:pr, :] = xp[p, 0:pr, :] + y[0:pr]
        xp[p, pr:2 * pr, :] = y[pr:2 * pr]

    def stage_solve(p):
        rows = pl.ds(pl.multiple_of(p * pr, pr), pr)
        x_inv = xp[p, 0:pr, :]
        x_inv = x_inv + _dot(x_inv, xp[p, pr:2 * pr, :])
        sol[rows, :] = _dot(x_inv, rhs[rows, :])

    def stage_affine(p):
        r0 = pl.multiple_of(p * pr, pr)
        rows = pl.ds(r0, pr)
        e = _dot(qk_s[p], sol[rows, :])
        qe = qd[rows, :] - e[:, GDN_DV:GDN_DV + GDN_DK]
        g_rows = gl[p]
        for cc in range(pr // c):
            rc = pl.ds(r0 + cc * c, c)
            mn = _dot_tn(kd[rc, :], sol[rc, :])
            idx = p * (pr // c) + cc
            mq[head, idx, 0:GDN_DK, :] = (eye * g_rows[cc:cc + 1, :] - mn[:, GDN_DV:GDN_DV + GDN_DK]).astype(mq.dtype)
            mq[head, idx, GDN_DK:GDN_DK + c, :] = qe[cc * c:(cc + 1) * c].astype(mq.dtype)
            ns[head, idx] = mn[:, 0:GDN_DV]
            o0[head, idx] = e[cc * c:(cc + 1) * c, 0:GDN_DV]

    over_pairs(stage_gates)
    for _ in range(int(math.log2(c)) - 2):
        over_pairs(stage_double)
    over_pairs(stage_solve)
    over_pairs(stage_affine)

    @pl.when(head == GDN_HEADS - 1)
    def _():
        st[...] = jnp.zeros_like(st)
        nw = nw_ref[...]

        def step(ci, carry):
            r0 = pl.multiple_of(ci * c, c)
            for h in range(GDN_HEADS):
                cols = slice(h * LANE, (h + 1) * LANE)
                ys = jnp.dot(mq[h, ci], st[h], preferred_element_type=F32)
                st[h] = (ys[0:GDN_DK] + ns[h, ci]).astype(st.dtype)
                o = ys[GDN_DK:GDN_DK + c] + o0[h, ci]
                o = o * lax.rsqrt(jnp.mean(o * o, axis=-1, keepdims=True) + RMS_EPS) * nw
                z = z_ref[pl.ds(r0, c), cols].astype(F32)
                o_ref[pl.ds(r0, c), cols] = (o * (z * _sigmoid(z))).astype(o_ref.dtype)
            return carry

        lax.fori_loop(0, seq // c, step, 0)


def _gdn(gqkvz, conv_w, gcol, grow, alog, dtb, nw, bsz, seq):
    assert GDN_DK == GDN_PAIR == GDN_DV == LANE
    h = GDN_HEADS
    npair = seq // GDN_PAIR
    nchunk = seq // GDN_CHUNK
    blk = lambda off: pl.BlockSpec((seq, LANE), lambda b, hh: (b, off + hh))
    cwb = lambda off: pl.BlockSpec((GDN_CONV, LANE), lambda b, hh: (0, off + hh))
    in_specs = [blk(0), blk(h), blk(2 * h),
                pl.BlockSpec((seq, GDN_VW), lambda b, hh: (b, 3)),
                cwb(0), cwb(h), cwb(2 * h),
                pl.BlockSpec((seq, LANE), lambda b, hh: (b, 0)),
                pl.BlockSpec((1, npair, 8, GDN_PAIR), lambda b, hh: (b, 0, 0, 0)),
                pl.BlockSpec((1, 1, LANE), lambda b, hh: (hh, 0, 0)),
                pl.BlockSpec((1, 1, LANE), lambda b, hh: (hh, 0, 0)),
                pl.BlockSpec((1, LANE), lambda b, hh: (0, 0))]
    scratch = [pltpu.VMEM((seq + 8, 3 * LANE), F32),
               pltpu.VMEM((seq, LANE), F32), pltpu.VMEM((seq, LANE), F32), pltpu.VMEM((seq, LANE), F32),
               pltpu.VMEM((npair, 2 * GDN_PAIR, GDN_PAIR), F32),
               pltpu.VMEM((npair, GDN_PAIR, GDN_PAIR), BF16),
               pltpu.VMEM((seq, GDN_DV + GDN_DK), BF16),
               pltpu.VMEM((seq, GDN_DK), BF16),
               pltpu.VMEM((seq, GDN_DK), F32),
               pltpu.VMEM((npair, 8, LANE), F32),
               pltpu.VMEM((seq, GDN_DV + GDN_DK), F32),
               pltpu.VMEM((h, nchunk, GDN_DK + GDN_CHUNK, GDN_DK), BF16),
               pltpu.VMEM((h, nchunk, GDN_DK, GDN_DV), F32),
               pltpu.VMEM((h, nchunk, GDN_CHUNK, GDN_DV), F32),
               pltpu.VMEM((h, GDN_DK, GDN_DV), BF16)]
    return pl.pallas_call(
        functools.partial(_gdn_kernel, seq=seq), grid=(bsz, h), in_specs=in_specs,
        out_specs=pl.BlockSpec((seq, GDN_VW), lambda b, hh: (b, 0)),
        out_shape=jax.ShapeDtypeStruct((bsz * seq, GDN_VW), BF16),
        scratch_shapes=scratch,
        compiler_params=_cparams(("parallel", "arbitrary")), name="gdn",
    )(gqkvz, gqkvz, gqkvz, gqkvz, conv_w, conv_w, conv_w, gcol, grow, alog, dtb, nw)


def _gelu_tanh(x):
    return 0.5 * x * (1.0 + jnp.tanh(math.sqrt(2.0 / math.pi) * (x + 0.044715 * x * x * x)))


def _compress_kernel(k16_ref, v16_ref, wkc_ref, wvc_ref, w1k_ref, w1v_ref, pk_ref, pv_ref,
                     w2k_ref, w2v_ref, kc_ref, vc_ref):
    nblk = k16_ref.shape[1]
    for x_ref, wc_ref, w1_ref, p_ref, w2_ref, o_ref in (
            (k16_ref, wkc_ref, w1k_ref, pk_ref, w2k_ref, kc_ref),
            (v16_ref, wvc_ref, w1v_ref, pv_ref, w2v_ref, vc_ref)):
        y = jnp.dot(x_ref[0], wc_ref[...], preferred_element_type=F32)
        pos_term = jnp.dot(p_ref[...], w1_ref[...], preferred_element_type=F32)[0:1, :]
        for g in range(NSA_KV_HEADS):
            top = y[:, g * CMP_HIDDEN:(g + 1) * CMP_HIDDEN]
            bot = y[:, (NSA_KV_HEADS + g) * CMP_HIDDEN:(NSA_KV_HEADS + g + 1) * CMP_HIDDEN]
            hid = top + pltpu.roll(bot, nblk - 1, 0) + pos_term
            o_ref[0, g] = _dot(_gelu_tanh(hid), w2_ref[...]).astype(o_ref.dtype)


def _compress(k16, v16, wkc, wvc, w1k, w1v, pk, pv, w2k, w2v):
    bsz, nblk, width = k16.shape
    in_specs = [pl.BlockSpec((1, nblk, width), lambda b: (b, 0, 0))] * 2
    in_specs += [_full_spec(a.shape) for a in (wkc, wvc, w1k, w1v, pk, pv, w2k, w2v)]
    out_spec = pl.BlockSpec((1, NSA_KV_HEADS, nblk, HEAD_PAD), lambda b: (b, 0, 0, 0))
    out_shape = jax.ShapeDtypeStruct((bsz, NSA_KV_HEADS, nblk, HEAD_PAD), BF16)
    return pl.pallas_call(
        _compress_kernel, grid=(bsz,), in_specs=in_specs, out_specs=[out_spec, out_spec],
        out_shape=[out_shape, out_shape], compiler_params=_cparams(("parallel",)), name="nsa_compress",
    )(k16, v16, wkc, wvc, w1k, w1v, pk, pv, w2k, w2v)


def _nsa_kernel(q_ref, kv_ref, kc_ref, vc_ref, sm_ref, ovt_ref, o_ref, ksa, vst, vwa, *, seq):
    qb = pl.program_id(1)
    q0 = qb * Q_BLOCK
    cols = NSA_REP * Q_BLOCK
    n_cmp_pad = kc_ref.shape[2]
    n_sel_blk = seq // SEL_BLOCK
    kb_rows = SEL_KEY_BLOCK
    one_row = NSA_DH

    @pl.when(qb == 0)
    def _():
        lane = lax.broadcasted_iota(jnp.int32, (1, NSA_KV_HEADS * HEAD_PAD), 1) & (HEAD_PAD - 1)
        ones_lane = (lane == one_row).astype(vwa.dtype)
        row_t = lax.broadcasted_iota(jnp.int32, (HEAD_PAD, 1), 0)

        def fill(bi, carry):
            r0 = pl.multiple_of(bi * kb_rows, kb_rows)
            rows = pl.ds(r0, kb_rows)
            blk = jnp.right_shift(r0 + lax.broadcasted_iota(jnp.int32, (kb_rows, 1), 0), SEL_SHIFT)
            ksa[rows, :] = kv_ref[rows, 0:2 * HEAD_PAD] + (lane == NSA_DH + blk).astype(ksa.dtype)
            vwa[rows, :] = kv_ref[rows, 6 * HEAD_PAD:8 * HEAD_PAD] + ones_lane
            for g in range(NSA_KV_HEADS):
                v_t = jnp.transpose(kv_ref[rows, (2 + g) * HEAD_PAD:(3 + g) * HEAD_PAD].astype(F32))
                vst[bi, g] = jnp.where(row_t == one_row, 1.0, v_t).astype(vst.dtype)
            return carry

        lax.fori_loop(0, seq // kb_rows, fill, 0)

    tok = lax.broadcasted_iota(jnp.int32, (1, Q_BLOCK), 1)
    t_lane = q0 + tok
    gates_t = jnp.transpose(_sigmoid(sm_ref[...]))
    lane_q = lax.broadcasted_iota(jnp.int32, (1, HEAD_PAD), 1)
    pen_lanes = (lane_q >= NSA_DH) & (lane_q < NSA_DH + n_sel_blk)
    place = (lax.broadcasted_iota(jnp.int32, (n_sel_blk, HEAD_PAD), 1)
             == NSA_DH + lax.broadcasted_iota(jnp.int32, (n_sel_blk, HEAD_PAD), 0)).astype(BF16)

    def add_bias(s, bias):
        return jnp.concatenate([s[:, r * Q_BLOCK:(r + 1) * Q_BLOCK] + bias for r in range(s.shape[1] // Q_BLOCK)],
                               axis=1)

    n_sub = lax.broadcasted_iota(jnp.int32, (n_cmp_pad, 1), 0)
    cmp_ok = (n_sub * CMP_STRIDE + (CMP_BLOCK - 1)) <= t_lane
    cmp_bias = jnp.where(cmp_ok, 0.0, NEG_BIG)
    cmp_any = jnp.concatenate([(t_lane >= CMP_BLOCK - 1).astype(F32)] * NSA_REP, axis=1)

    qp_g, o_cmp_g = [], []
    for g in range(NSA_KV_HEADS):
        qs = jnp.concatenate(
            [q_ref[:, (g * NSA_REP + r) * HEAD_PAD:(g * NSA_REP + r + 1) * HEAD_PAD] for r in range(NSA_REP)],
            axis=0)

        s_c = add_bias(_dot_nt(kc_ref[0, g], qs), cmp_bias)
        e_c = jnp.exp2(s_c - jnp.max(s_c, axis=0, keepdims=True))
        p_c = e_c * (cmp_any / jnp.sum(e_c, axis=0, keepdims=True))
        o_cmp_g.append(_dot_tn(vc_ref[0, g], p_c))

        p_sum = p_c[:, 0:Q_BLOCK]
        for r in range(1, NSA_REP):
            p_sum = p_sum + p_c[:, r * Q_BLOCK:(r + 1) * Q_BLOCK]
        imp_t = _dot(ovt_ref[...], p_sum)

        def rank_select(imp_t=imp_t):
            j_sub = lax.broadcasted_iota(jnp.int32, (n_sel_blk, 1), 0)
            cur = jnp.right_shift(t_lane, SEL_SHIFT)
            forced = (j_sub == 0) | (j_sub == cur) | (j_sub == cur - 1)
            causal_blk = (j_sub * SEL_BLOCK) <= t_lane
            imp = jnp.where(forced, FORCE_SCORE, jnp.where(causal_blk, imp_t, -jnp.inf))
            rank = jnp.zeros((n_sel_blk, Q_BLOCK), F32)
            for jp in range(n_sel_blk):
                row = imp[jp:jp + 1, :]
                ahead = (row > imp) | ((row == imp) & (j_sub > jp))
                rank = rank + ahead.astype(F32)
            return (rank < SEL_TOPK).astype(BF16)

        sel_t = lax.cond(q0 + Q_BLOCK <= SEL_TOPK * SEL_BLOCK,
                         lambda: jnp.ones((n_sel_blk, Q_BLOCK), BF16), rank_select)
        sel_lanes = _dot_tn(sel_t, place)
        pen = jnp.where(pen_lanes, (sel_lanes - 1.0) * (-NEG_BIG), 0.0).astype(BF16)
        qp_g.append(qs + jnp.concatenate([pen] * NSA_REP, axis=0))

    groups = range(NSA_KV_HEADS)
    n_full = lax.div(q0, kb_rows)

    def scores(kb, g):
        k0 = pl.multiple_of(kb * kb_rows, kb_rows)
        return _dot_nt(ksa[pl.ds(k0, kb_rows), g * HEAD_PAD:(g + 1) * HEAD_PAD], qp_g[g])

    def absorb(state, s, kb, g):
        m_i, acc = state
        m_new = jnp.maximum(m_i, jnp.max(s, axis=0, keepdims=True))
        return m_new, jnp.exp2(m_i - m_new) * acc + _dot(vst[kb, g], jnp.exp2(s - m_new))

    kpos_last = n_full * kb_rows + lax.broadcasted_iota(jnp.int32, (kb_rows, 1), 0)
    causal_bias = jnp.where(kpos_last <= t_lane, 0.0, NEG_BIG)
    first = []
    for g in groups:
        s = add_bias(scores(n_full, g), causal_bias)
        m = jnp.max(s, axis=0, keepdims=True)
        first.append((m, _dot(vst[n_full, g], jnp.exp2(s - m))))

    def full_blocks(state):
        def step(kb, carry):
            st, s_cur = carry
            s_next = tuple(scores(kb + 1, g) for g in groups)
            return tuple(absorb(st[g], s_cur[g], kb, g) for g in groups), s_next

        st, s_cur = lax.fori_loop(0, n_full - 1, step, (state, tuple(scores(0, g) for g in groups)))
        return tuple(absorb(st[g], s_cur[g], n_full - 1, g) for g in groups)

    sel_out = lax.cond(n_full > 0, full_blocks, lambda st: st, tuple(first))

    span = WINDOW + Q_BLOCK
    w0 = pl.multiple_of(jnp.maximum(q0 - WINDOW, 0), Q_BLOCK)
    kpos_w = w0 + lax.broadcasted_iota(jnp.int32, (span, 1), 0)
    win_bias = jnp.where((kpos_w <= t_lane) & (kpos_w > t_lane - WINDOW), 0.0, NEG_BIG)
    for g in range(NSA_KV_HEADS):
        acc_s = sel_out[g][1]
        o_slc = acc_s * (1.0 / acc_s[one_row:one_row + 1, :])

        k_w = kv_ref[pl.ds(w0, span), (4 + g) * HEAD_PAD:(5 + g) * HEAD_PAD]
        s_w = add_bias(_dot_nt(k_w, qp_g[g]), win_bias)
        acc_w = _dot_tn(vwa[pl.ds(w0, span), g * HEAD_PAD:(g + 1) * HEAD_PAD],
                        jnp.exp2(s_w - jnp.max(s_w, axis=0, keepdims=True)))
        o_win = acc_w * (1.0 / acc_w[one_row:one_row + 1, :])

        for r in range(NSA_REP):
            head = g * NSA_REP + r
            c0 = 8 + head * 3
            sl = slice(r * Q_BLOCK, (r + 1) * Q_BLOCK)
            o_t = (gates_t[c0:c0 + 1, :] * o_cmp_g[g][:, sl] + gates_t[c0 + 1:c0 + 2, :] * o_slc[:, sl]
                   + gates_t[c0 + 2:c0 + 3, :] * o_win[:, sl])
            o_ref[:, head * HEAD_PAD:(head + 1) * HEAD_PAD] = jnp.transpose(o_t).astype(o_ref.dtype)


def _nsa(nq, nkv, kc, vc, small, overlap, bsz, seq):
    nqb = seq // Q_BLOCK
    in_specs = [pl.BlockSpec((Q_BLOCK, NSA_HEADS * HEAD_PAD), lambda b, i: (b * nqb + i, 0)),
                pl.BlockSpec((seq, nkv.shape[1]), lambda b, i: (b, 0)),
                pl.BlockSpec((1,) + kc.shape[1:], lambda b, i: (b, 0, 0, 0)),
                pl.BlockSpec((1,) + vc.shape[1:], lambda b, i: (b, 0, 0, 0)),
                pl.BlockSpec((Q_BLOCK, LANE), lambda b, i: (b * nqb + i, 0)),
                _full_spec(overlap.shape)]
    return pl.pallas_call(
        functools.partial(_nsa_kernel, seq=seq), grid=(bsz, nqb), in_specs=in_specs,
        out_specs=pl.BlockSpec((Q_BLOCK, NSA_HEADS * HEAD_PAD), lambda b, i: (b * nqb + i, 0)),
        out_shape=jax.ShapeDtypeStruct((bsz * seq, NSA_HEADS * HEAD_PAD), BF16),
        scratch_shapes=[pltpu.VMEM((seq, NSA_KV_HEADS * HEAD_PAD), BF16),
                        pltpu.VMEM((seq // SEL_KEY_BLOCK, NSA_KV_HEADS, HEAD_PAD, SEL_KEY_BLOCK), BF16),
                        pltpu.VMEM((seq, NSA_KV_HEADS * HEAD_PAD), BF16)],
        compiler_params=_cparams(("parallel", "arbitrary")), name="nsa_attention",
    )(nq, nkv, kc, vc, small, overlap)


def _merge_kernel(oa_ref, ob_ref, mix_ref, x_ref, wa_ref, wb_ref, wo_ref, g_ref, b_ref, h_ref):
    ya = jnp.dot(oa_ref[...], wa_ref[...], preferred_element_type=F32)
    yb = jnp.dot(ob_ref[...], wb_ref[...], preferred_element_type=F32)
    mix = mix_ref[...].astype(F32)
    mixed = _sigmoid(mix[:, 0:D_MODEL]) * ya + _sigmoid(mix[:, D_MODEL:2 * D_MODEL]) * yb
    y = _dot(mixed, wo_ref[...])
    h_ref[...] = _layer_norm(DN_ALPHA * x_ref[...] + y, g_ref[...], b_ref[...])


def _merge(oa, ob, mix, x2d, wa, wb, wo, g, b):
    m = x2d.shape[0]
    tm = ROW_TILE
    row = lambda n: pl.BlockSpec((tm, n), lambda i: (i, 0))
    in_specs = [row(oa.shape[1]), row(ob.shape[1]), row(mix.shape[1]), row(D_MODEL),
                _full_spec(wa.shape), _full_spec(wb.shape), _full_spec(wo.shape),
                _full_spec(g.shape), _full_spec(b.shape)]
    return pl.pallas_call(
        _merge_kernel, grid=(m // tm,), in_specs=in_specs, out_specs=row(D_MODEL),
        out_shape=jax.ShapeDtypeStruct((m, D_MODEL), F32),
        compiler_params=_cparams(("parallel",)), name="merge_ln",
    )(oa, ob, mix, x2d, wa, wb, wo, g, b)


def _ffn_kernel(h_ref, halo_ref, wup_ref, cw_ref, wd_ref, g_ref, b_ref, o_ref,
                hb, ug, uv, act, *, seq, tm, halo):
    i = pl.program_id(0)
    first = lax.rem(i * tm, seq) == 0
    hb[0:halo, :] = jnp.where(first, 0.0, halo_ref[...]).astype(BF16)
    hb[halo:, :] = h_ref[...].astype(BF16)

    def conv(u_ref, slot, cols):
        out = None
        for tap in range(FFN_CONV):
            term = cw_ref[tap:tap + 1, cols] * u_ref[slot, pl.ds(halo - (FFN_CONV - 1) + tap, tm), :]
            out = term if out is None else out + term
        return out

    for j in range(D_FF // FFN_CHUNK):
        slot = j % 2
        cols_g = slice(j * FFN_CHUNK, (j + 1) * FFN_CHUNK)
        cols_v = slice(D_FF + j * FFN_CHUNK, D_FF + (j + 1) * FFN_CHUNK)
        ug[slot] = jnp.dot(hb[...], wup_ref[:, cols_g], preferred_element_type=F32)
        uv[slot] = jnp.dot(hb[...], wup_ref[:, cols_v], preferred_element_type=F32)
        a = conv(ug, slot, cols_g)
        act[:, cols_g] = (a * _sigmoid(a) * conv(uv, slot, cols_v)).astype(act.dtype)

    f = jnp.dot(act[...], wd_ref[...], preferred_element_type=F32)
    o_ref[...] = _layer_norm(DN_ALPHA * h_ref[...] + f, g_ref[...], b_ref[...])


def _ffn(h, w_up, conv_w, w_down, g, b, seq):
    m = h.shape[0]
    tm = ROW_TILE
    halo = 16
    resident = lambda shape: pl.BlockSpec(shape, lambda i: (0, 0), pipeline_mode=pl.Buffered(1))
    in_specs = [pl.BlockSpec((tm, D_MODEL), lambda i: (i, 0)),
                pl.BlockSpec((halo, D_MODEL), lambda i: (jnp.maximum(i * (tm // halo) - 1, 0), 0)),
                resident(w_up.shape), resident(conv_w.shape), resident(w_down.shape),
                resident(g.shape), resident(b.shape)]
    return pl.pallas_call(
        functools.partial(_ffn_kernel, seq=seq, tm=tm, halo=halo), grid=(m // tm,),
        in_specs=in_specs, out_specs=pl.BlockSpec((tm, D_MODEL), lambda i: (i, 0)),
        out_shape=jax.ShapeDtypeStruct((m, D_MODEL), F32),
        scratch_shapes=[pltpu.VMEM((tm + halo, D_MODEL), BF16),
                        pltpu.VMEM((2, tm + halo, FFN_CHUNK), F32), pltpu.VMEM((2, tm + halo, FFN_CHUNK), F32),
                        pltpu.VMEM((tm, D_FF), BF16)],
        compiler_params=_cparams(("parallel",)), name="conv_ffn_ln",
    )(h, h, w_up, conv_w, w_down, g, b)


def _pad_heads(w, n_heads):
    lead = w.shape[:-1]
    w = w.reshape(lead + (n_heads, NSA_DH))
    w = jnp.pad(w, [(0, 0)] * len(lead) + [(0, 0), (0, HEAD_PAD - NSA_DH)])
    return w.reshape(lead + (n_heads * HEAD_PAD,))


def _compress_weights(w1):
    half = CMP_BLOCK // 2
    w1r = w1.reshape(2, half, NSA_DH, CMP_HIDDEN)
    eye = jnp.eye(NSA_KV_HEADS, dtype=w1.dtype)
    ex = jnp.einsum("aidh,gk->agikdh", w1r, eye)
    ex = ex.reshape(2 * NSA_KV_HEADS, half * NSA_KVW, CMP_HIDDEN)
    return jnp.concatenate([ex[n] for n in range(2 * NSA_KV_HEADS)], axis=1)


def _overlap_matrix(seq):
    n_cmp = (seq - CMP_BLOCK) // CMP_STRIDE + 1
    n_sel = seq // SEL_BLOCK
    starts = np.arange(n_cmp) * CMP_STRIDE
    jb = np.arange(n_sel) * SEL_BLOCK
    ov = ((starts[:, None] < jb[None] + SEL_BLOCK) & (starts[:, None] + CMP_BLOCK > jb[None])).astype(np.float32)
    out = np.zeros((n_sel, seq // CMP_STRIDE), np.float32)
    out[:, :n_cmp] = ov.T
    return jnp.asarray(out, BF16)


def kernel(x, w_in, gdn_conv_w, gdn_a_log, gdn_dt_bias, gdn_norm_w, cmp_pos_k, cmp_w1_k, cmp_w2_k,
           cmp_pos_v, cmp_w1_v, cmp_w2_v, w_branch_gdn, w_branch_nsa, w_out, ln1_g, ln1_b, w_up,
           ffn_conv_w, w_down, ln2_g, ln2_b):
    bsz, seq, _ = x.shape
    m = bsz * seq
    for i in range(DEPTH):
        x2d = x.reshape(m, D_MODEL)
        w = w_in[i]
        o_gdn = 2 * GDN_QK + 2 * GDN_VW
        o_nq = o_gdn + 2 * GDN_HEADS
        o_kv = o_nq + NSA_QW
        o_gate = o_kv + 6 * NSA_KVW
        o_mix = o_gate + 3 * NSA_HEADS
        kv = [w[:, o_kv + n * NSA_KVW:o_kv + (n + 1) * NSA_KVW] for n in range(6)]
        small_w = jnp.concatenate([w[:, o_gdn:o_nq], w[:, o_gate:o_mix]], axis=1)
        small_w = jnp.pad(small_w, ((0, 0), (0, LANE - small_w.shape[1])))
        weights = [w[:, :o_gdn],
                   _pad_heads(w[:, o_nq:o_kv] * (NSA_DH ** -0.5 * LOG2E), NSA_HEADS),
                   jnp.concatenate([_pad_heads(t, NSA_KV_HEADS) for t in kv[2:]], axis=1),
                   jnp.concatenate(kv[:2], axis=1),
                   w[:, o_mix:],
                   small_w]
        weights = [t.astype(BF16) for t in weights]
        gqkvz, nq, nkv, kvc, mix, small = _inproj(x2d, weights, [BF16, BF16, BF16, BF16, BF16, F32])

        grow = small[:, :2 * GDN_HEADS].reshape(bsz, seq // GDN_PAIR, GDN_PAIR, 2 * GDN_HEADS).transpose(0, 1, 3, 2)
        alog = jnp.broadcast_to(gdn_a_log[i][:, None, None], (GDN_HEADS, 1, LANE))
        dtb = jnp.broadcast_to(gdn_dt_bias[i][:, None, None], (GDN_HEADS, 1, LANE))
        o_a = _gdn(gqkvz, gdn_conv_w[i], small, grow, alog, dtb, gdn_norm_w[i][None, :], bsz, seq)

        grp = seq // CMP_STRIDE
        k16 = kvc[:, :NSA_KVW].reshape(bsz, grp, CMP_STRIDE * NSA_KVW)
        v16 = kvc[:, NSA_KVW:].reshape(bsz, grp, CMP_STRIDE * NSA_KVW)
        pos_rows = lambda p: jnp.pad(p.reshape(1, CMP_BLOCK * NSA_DH), ((0, 7), (0, 0))).astype(BF16)
        w2_pad = lambda t: jnp.pad(t, ((0, 0), (0, HEAD_PAD - NSA_DH))).astype(BF16)
        kc, vc = _compress(k16, v16,
                           _compress_weights(cmp_w1_k[i]).astype(BF16), _compress_weights(cmp_w1_v[i]).astype(BF16),
                           cmp_w1_k[i].astype(BF16), cmp_w1_v[i].astype(BF16),
                           pos_rows(cmp_pos_k[i]), pos_rows(cmp_pos_v[i]),
                           w2_pad(cmp_w2_k[i]), w2_pad(cmp_w2_v[i]))
        o_b = _nsa(nq, nkv, kc, vc, small, _overlap_matrix(seq), bsz, seq)

        wb = jnp.pad(w_branch_nsa[i].reshape(NSA_HEADS, NSA_DH, D_MODEL),
                     ((0, 0), (0, HEAD_PAD - NSA_DH), (0, 0))).reshape(NSA_HEADS * HEAD_PAD, D_MODEL)
        h = _merge(o_a, o_b, mix, x2d, w_branch_gdn[i].astype(BF16), wb.astype(BF16), w_out[i].astype(BF16),
                   ln1_g[i][None, :], ln1_b[i][None, :])

        x = _ffn(h, w_up[i].astype(BF16), ffn_conv_w[i], w_down[i].astype(BF16),
                 ln2_g[i][None, :], ln2_b[i][None, :], seq).reshape(bsz, seq, D_MODEL)
    return x
```

```python
import functools
import math

import numpy as np
import jax
import jax.numpy as jnp
from jax import lax
from jax.experimental import pallas as pl
from jax.experimental.pallas import tpu as pltpu

F32 = jnp.float32
BF16 = jnp.bfloat16

D_MODEL = 1024
GDN_HEADS = 4
GDN_DK = 128
GDN_DV = 128
GDN_CONV = 4
GDN_CHUNK = 64
NSA_HEADS = 8
NSA_KV_HEADS = 2
NSA_REP = NSA_HEADS // NSA_KV_HEADS
NSA_DH = 64
CMP_BLOCK = 32
CMP_STRIDE = 16
CMP_HIDDEN = 256
SEL_BLOCK = 64
SEL_SHIFT = 6
SEL_TOPK = 16
WINDOW = 512
FORCE_SCORE = 1e9
D_FF = 2816
FFN_CONV = 3
DEPTH = 1
DN_ALPHA = (2 * DEPTH) ** 0.25
LN_EPS = 1e-5
RMS_EPS = 1e-6
LOG2E = math.log2(math.e)

GDN_QK = GDN_HEADS * GDN_DK
GDN_VW = GDN_HEADS * GDN_DV
NSA_QW = NSA_HEADS * NSA_DH
NSA_KVW = NSA_KV_HEADS * NSA_DH

LANE = 128
HEAD_PAD = LANE
NEG_BIG = -1e30
VMEM_LIMIT = 56 * 1024 * 1024

Q_BLOCK = 128
NSA_Q_SUB = 2
SEL_KEY_BLOCK = 256
FFN_CHUNK = 256
ROW_TILE = 512
GDN_CHUNK_SHIFT = 6
GDN_PAIR = 2 * GDN_CHUNK
GDN_UNROLL = 16


def _cparams(sem):
    return pltpu.CompilerParams(dimension_semantics=sem, vmem_limit_bytes=VMEM_LIMIT)


def _dot(a, b):
    return jnp.dot(a.astype(BF16), b.astype(BF16), preferred_element_type=F32)


def _dot_nt(a, b):
    return lax.dot_general(a.astype(BF16), b.astype(BF16), (((1,), (1,)), ((), ())),
                           preferred_element_type=F32)


def _dot_tn(a, b):
    return lax.dot_general(a.astype(BF16), b.astype(BF16), (((0,), (0,)), ((), ())),
                           preferred_element_type=F32)


def _sigmoid(x):
    return jax.nn.sigmoid(x)


def _softplus(x):
    return jnp.maximum(x, 0.0) + jnp.log1p(jnp.exp(-jnp.abs(x)))


def _layer_norm(x, g, b):
    mu = jnp.mean(x, axis=-1, keepdims=True)
    xc = x - mu
    var = jnp.mean(xc * xc, axis=-1, keepdims=True)
    return xc * lax.rsqrt(var + LN_EPS) * g + b


def _full_spec(shape):
    nd = len(shape)
    return pl.BlockSpec(shape, lambda *_: (0,) * nd)


def _inproj_kernel(x_ref, wt_ref, *refs):
    n = (len(refs) - 1) // 2
    xb = x_ref[...].astype(BF16)
    for w_ref, o_ref in zip(refs[:n], refs[n:2 * n]):
        o_ref[...] = jnp.dot(xb, w_ref[...], preferred_element_type=F32).astype(o_ref.dtype)
    t_ref = refs[2 * n]
    for j in range(t_ref.shape[0]):
        t_ref[j] = _dot_nt(wt_ref[...], xb[j * LANE:(j + 1) * LANE])


def _inproj(x2d, weights, out_dtypes, w_rows):
    m = x2d.shape[0]
    tm = ROW_TILE
    in_specs = [pl.BlockSpec((tm, D_MODEL), lambda i: (i, 0)), _full_spec(w_rows.shape)]
    in_specs += [_full_spec(w.shape) for w in weights]
    out_specs = [pl.BlockSpec((tm, w.shape[1]), lambda i: (i, 0)) for w in weights]
    out_specs.append(pl.BlockSpec((tm // LANE, w_rows.shape[0], LANE), lambda i: (i, 0, 0)))
    out_shape = [jax.ShapeDtypeStruct((m, w.shape[1]), dt) for w, dt in zip(weights, out_dtypes)]
    out_shape.append(jax.ShapeDtypeStruct((m // LANE, w_rows.shape[0], LANE), F32))
    return pl.pallas_call(
        _inproj_kernel, grid=(m // tm,), in_specs=in_specs, out_specs=out_specs, out_shape=out_shape,
        compiler_params=_cparams(("parallel",)), name="inproj")(x2d, w_rows, *weights)


def _gdn_kernel(q_ref, k_ref, v_ref, z_ref, cwq_ref, cwk_ref, cwv_ref, gcol_ref, grow_ref,
                alog_ref, dtb_ref, nw_ref, o_ref,
                xpad, qh, kh, vh, xp, qk_s, rhs, kd, qd, gl, sol, mq, ns, o0, st, *, seq):
    c = GDN_CHUNK
    pr = GDN_PAIR
    npair = seq // pr
    head = pl.program_id(1)
    pad = 8
    conv_rows = 256
    xpad[0:pad, :] = jnp.zeros((pad, 3 * LANE), F32)
    xpad[pad:, 0:LANE] = q_ref[...].astype(F32)
    xpad[pad:, LANE:2 * LANE] = k_ref[...].astype(F32)
    xpad[pad:, 2 * LANE:3 * LANE] = v_ref[...].astype(F32)
    cw = jnp.concatenate([cwq_ref[...], cwk_ref[...], cwv_ref[...]], axis=1)

    for t0 in range(0, seq, conv_rows):
        acc = jnp.zeros((conv_rows, 3 * LANE), F32)
        for j in range(GDN_CONV):
            xs = xpad[t0 + pad - (GDN_CONV - 1) + j:t0 + pad - (GDN_CONV - 1) + j + conv_rows, :]
            acc = acc + xs * cw[j:j + 1, :]
        y = acc * _sigmoid(acc)
        yq, yk, yv = y[:, 0:LANE], y[:, LANE:2 * LANE], y[:, 2 * LANE:3 * LANE]
        yq = yq * lax.rsqrt(jnp.sum(yq * yq, axis=-1, keepdims=True) + RMS_EPS) * (GDN_DK ** -0.5)
        yk = yk * lax.rsqrt(jnp.sum(yk * yk, axis=-1, keepdims=True) + RMS_EPS)
        qh[t0:t0 + conv_rows, :] = yq
        kh[t0:t0 + conv_rows, :] = yk
        vh[t0:t0 + conv_rows, :] = yv

    neg_a = -jnp.exp(alog_ref[0, :, 0:1])
    dtb = dtb_ref[0, :, 0:1]
    ii = lax.broadcasted_iota(jnp.int32, (pr, pr), 0)
    jj = lax.broadcasted_iota(jnp.int32, (pr, pr), 1)
    same = jnp.right_shift(ii, GDN_CHUNK_SHIFT) == jnp.right_shift(jj, GDN_CHUNK_SHIFT)
    incl = same & (jj <= ii)
    strict = same & (jj < ii)
    upper = same & (ii <= jj)
    eye = (ii == jj).astype(F32)
    sub8 = lax.broadcasted_iota(jnp.int32, (8, LANE), 0)
    lane_id = lax.broadcasted_iota(jnp.int32, (1, LANE), 1)

    def over_pairs(body):
        def run(gi, carry):
            for uu in range(GDN_UNROLL):
                body(gi * GDN_UNROLL + uu)
            return carry
        lax.fori_loop(0, npair // GDN_UNROLL, run, 0)

    def stage_gates(p):
        rows = pl.ds(pl.multiple_of(p * pr, pr), pr)
        q = qh[rows, :]
        k = kh[rows, :]
        v = vh[rows, :]
        gcol = gcol_ref[rows, :]
        b_logit = jnp.sum(jnp.where(lane_id == head, gcol, 0.0), axis=1, keepdims=True)
        a_logit = jnp.sum(jnp.where(lane_id == GDN_HEADS + head, gcol, 0.0), axis=1, keepdims=True)
        a_row = jnp.sum(jnp.where(sub8[:, 0:1] == GDN_HEADS + head, grow_ref[p], 0.0), axis=0, keepdims=True)
        beta = _sigmoid(b_logit)
        g_c = neg_a * _softplus(a_logit + dtb)
        g_r = neg_a * _softplus(a_row + dtb)
        gc_col = jnp.sum(jnp.where(incl, g_r, 0.0), axis=1, keepdims=True)
        gc_row = jnp.sum(jnp.where(upper, g_c, 0.0), axis=0, keepdims=True)
        gt_col = jnp.sum(jnp.where(same, g_r, 0.0), axis=1, keepdims=True)
        decay = jnp.where(incl, jnp.exp(jnp.where(incl, gc_col - gc_row, 0.0)), 0.0)
        kq_k = _dot_nt(jnp.concatenate([k, q], axis=0), k)
        a_mat = jnp.where(strict, kq_k[0:pr] * decay * beta, 0.0)
        xp[p, 0:pr, :] = eye - a_mat
        xp[p, pr:2 * pr, :] = _dot(a_mat, a_mat)
        qk_s[p] = (kq_k[pr:2 * pr] * decay).astype(qk_s.dtype)
        e_gc = jnp.exp(gc_col)
        rhs[rows, 0:GDN_DV] = (v * beta).astype(rhs.dtype)
        rhs[rows, GDN_DV:GDN_DV + GDN_DK] = (k * (beta * e_gc)).astype(rhs.dtype)
        kd[rows, :] = (k * jnp.exp(gt_col - gc_col)).astype(kd.dtype)
        qd[rows, :] = q * e_gc
        g_last = jnp.exp(gt_col)
        gl[p] = jnp.where(sub8 == 0, g_last[0:1, :], g_last[c:c + 1, :])

    def stage_double(p):
        y = _dot(xp[p], xp[p, pr:2 * pr, :])
        xp[p, 0:pr, :] = xp[p, 0:pr, :] + y[0:pr]
        xp[p, pr:2 * pr, :] = y[pr:2 * pr]

    def stage_solve(p):
        rows = pl.ds(pl.multiple_of(p * pr, pr), pr)
        x_inv = xp[p, 0:pr, :]
        x_inv = x_inv + _dot(x_inv, xp[p, pr:2 * pr, :])
        sol[rows, :] = _dot(x_inv, rhs[rows, :])

    def stage_affine(p):
        r0 = pl.multiple_of(p * pr, pr)
        rows = pl.ds(r0, pr)
        e = _dot(qk_s[p], sol[rows, :])
        qe = qd[rows, :] - e[:, GDN_DV:GDN_DV + GDN_DK]
        g_rows = gl[p]
        for cc in range(pr // c):
            rc = pl.ds(r0 + cc * c, c)
            mn = _dot_tn(kd[rc, :], sol[rc, :])
            idx = p * (pr // c) + cc
            mq[head, idx, 0:GDN_DK, :] = (eye * g_rows[cc:cc + 1, :] - mn[:, GDN_DV:GDN_DV + GDN_DK]).astype(mq.dtype)
            mq[head, idx, GDN_DK:GDN_DK + c, :] = qe[cc * c:(cc + 1) * c].astype(mq.dtype)
            ns[head, idx] = mn[:, 0:GDN_DV]
            o0[head, idx] = e[cc * c:(cc + 1) * c, 0:GDN_DV]

    over_pairs(stage_gates)
    for _ in range(int(math.log2(c)) - 2):
        over_pairs(stage_double)
    over_pairs(stage_solve)
    over_pairs(stage_affine)

    @pl.when(head == GDN_HEADS - 1)
    def _():
        st[...] = jnp.zeros_like(st)
        nw = nw_ref[...]

        def step(ci, carry):
            r0 = pl.multiple_of(ci * c, c)
            for h in range(GDN_HEADS):
                cols = slice(h * LANE, (h + 1) * LANE)
                ys = jnp.dot(mq[h, ci], st[h], preferred_element_type=F32)
                st[h] = (ys[0:GDN_DK] + ns[h, ci]).astype(st.dtype)
                o = ys[GDN_DK:GDN_DK + c] + o0[h, ci]
                o = o * lax.rsqrt(jnp.mean(o * o, axis=-1, keepdims=True) + RMS_EPS) * nw
                z = z_ref[pl.ds(r0, c), cols].astype(F32)
                o_ref[pl.ds(r0, c), cols] = (o * (z * _sigmoid(z))).astype(o_ref.dtype)
            return carry

        lax.fori_loop(0, seq // c, step, 0)


def _gdn(gqkvz, conv_w, gcol, grow, alog, dtb, nw, bsz, seq):
    assert GDN_DK == GDN_PAIR == GDN_DV == LANE
    h = GDN_HEADS
    npair = seq // GDN_PAIR
    nchunk = seq // GDN_CHUNK
    blk = lambda off: pl.BlockSpec((seq, LANE), lambda b, hh: (b, off + hh))
    cwb = lambda off: pl.BlockSpec((GDN_CONV, LANE), lambda b, hh: (0, off + hh))
    in_specs = [blk(0), blk(h), blk(2 * h),
                pl.BlockSpec((seq, GDN_VW), lambda b, hh: (b, 3)),
                cwb(0), cwb(h), cwb(2 * h),
                pl.BlockSpec((seq, LANE), lambda b, hh: (b, 0)),
                pl.BlockSpec((npair, 8, GDN_PAIR), lambda b, hh: (b, 0, 0)),
                pl.BlockSpec((1, 1, LANE), lambda b, hh: (hh, 0, 0)),
                pl.BlockSpec((1, 1, LANE), lambda b, hh: (hh, 0, 0)),
                pl.BlockSpec((1, LANE), lambda b, hh: (0, 0))]
    scratch = [pltpu.VMEM((seq + 8, 3 * LANE), F32),
               pltpu.VMEM((seq, LANE), F32), pltpu.VMEM((seq, LANE), F32), pltpu.VMEM((seq, LANE), F32),
               pltpu.VMEM((npair, 2 * GDN_PAIR, GDN_PAIR), F32),
               pltpu.VMEM((npair, GDN_PAIR, GDN_PAIR), BF16),
               pltpu.VMEM((seq, GDN_DV + GDN_DK), BF16),
               pltpu.VMEM((seq, GDN_DK), BF16),
               pltpu.VMEM((seq, GDN_DK), F32),
               pltpu.VMEM((npair, 8, LANE), F32),
               pltpu.VMEM((seq, GDN_DV + GDN_DK), F32),
               pltpu.VMEM((h, nchunk, GDN_DK + GDN_CHUNK, GDN_DK), BF16),
               pltpu.VMEM((h, nchunk, GDN_DK, GDN_DV), F32),
               pltpu.VMEM((h, nchunk, GDN_CHUNK, GDN_DV), F32),
               pltpu.VMEM((h, GDN_DK, GDN_DV), BF16)]
    return pl.pallas_call(
        functools.partial(_gdn_kernel, seq=seq), grid=(bsz, h), in_specs=in_specs,
        out_specs=pl.BlockSpec((seq, GDN_VW), lambda b, hh: (b, 0)),
        out_shape=jax.ShapeDtypeStruct((bsz * seq, GDN_VW), BF16),
        scratch_shapes=scratch,
        compiler_params=_cparams(("parallel", "arbitrary")), name="gdn",
    )(gqkvz, gqkvz, gqkvz, gqkvz, conv_w, conv_w, conv_w, gcol, grow, alog, dtb, nw)


def _gelu_tanh(x):
    return 0.5 * x * (1.0 + jnp.tanh(math.sqrt(2.0 / math.pi) * (x + 0.044715 * x * x * x)))


def _compress_kernel(k16_ref, v16_ref, wkc_ref, wvc_ref, w1k_ref, w1v_ref, pk_ref, pv_ref,
                     w2k_ref, w2v_ref, kc_ref, vc_ref):
    nblk = k16_ref.shape[1]
    for x_ref, wc_ref, w1_ref, p_ref, w2_ref, o_ref in (
            (k16_ref, wkc_ref, w1k_ref, pk_ref, w2k_ref, kc_ref),
            (v16_ref, wvc_ref, w1v_ref, pv_ref, w2v_ref, vc_ref)):
        y = jnp.dot(x_ref[0], wc_ref[...], preferred_element_type=F32)
        pos_term = jnp.dot(p_ref[...], w1_ref[...], preferred_element_type=F32)[0:1, :]
        for g in range(NSA_KV_HEADS):
            top = y[:, g * CMP_HIDDEN:(g + 1) * CMP_HIDDEN]
            bot = y[:, (NSA_KV_HEADS + g) * CMP_HIDDEN:(NSA_KV_HEADS + g + 1) * CMP_HIDDEN]
            hid = top + pltpu.roll(bot, nblk - 1, 0) + pos_term
            o_ref[0, g] = _dot(_gelu_tanh(hid), w2_ref[...]).astype(o_ref.dtype)


def _compress(k16, v16, wkc, wvc, w1k, w1v, pk, pv, w2k, w2v):
    bsz, nblk, width = k16.shape
    in_specs = [pl.BlockSpec((1, nblk, width), lambda b: (b, 0, 0))] * 2
    in_specs += [_full_spec(a.shape) for a in (wkc, wvc, w1k, w1v, pk, pv, w2k, w2v)]
    out_spec = pl.BlockSpec((1, NSA_KV_HEADS, nblk, HEAD_PAD), lambda b: (b, 0, 0, 0))
    out_shape = jax.ShapeDtypeStruct((bsz, NSA_KV_HEADS, nblk, HEAD_PAD), BF16)
    return pl.pallas_call(
        _compress_kernel, grid=(bsz,), in_specs=in_specs, out_specs=[out_spec, out_spec],
        out_shape=[out_shape, out_shape], compiler_params=_cparams(("parallel",)), name="nsa_compress",
    )(k16, v16, wkc, wvc, w1k, w1v, pk, pv, w2k, w2v)


def _nsa_kernel(q_ref, kv_ref, kc_ref, vc_ref, sm_ref, ovt_ref, o_ref, ksa, vst, vwa, *, seq):
    step = pl.program_id(1)
    n_cmp_pad = kc_ref.shape[2]
    n_sel_blk = seq // SEL_BLOCK
    kb_rows = SEL_KEY_BLOCK
    one_row = NSA_DH
    groups = range(NSA_KV_HEADS)
    units = [(sub, g) for sub in range(NSA_Q_SUB) for g in groups]

    @pl.when(step == 0)
    def _():
        lane = lax.broadcasted_iota(jnp.int32, (1, NSA_KV_HEADS * HEAD_PAD), 1) & (HEAD_PAD - 1)
        ones_lane = (lane == one_row).astype(vwa.dtype)
        row_t = lax.broadcasted_iota(jnp.int32, (HEAD_PAD, 1), 0)

        def fill(bi, carry):
            r0 = pl.multiple_of(bi * kb_rows, kb_rows)
            rows = pl.ds(r0, kb_rows)
            blk = jnp.right_shift(r0 + lax.broadcasted_iota(jnp.int32, (kb_rows, 1), 0), SEL_SHIFT)
            ksa[rows, :] = kv_ref[rows, 0:2 * HEAD_PAD] + (lane == NSA_DH + blk).astype(ksa.dtype)
            vwa[rows, :] = kv_ref[rows, 6 * HEAD_PAD:8 * HEAD_PAD] + ones_lane
            for g in range(NSA_KV_HEADS):
                v_t = jnp.transpose(kv_ref[rows, (2 + g) * HEAD_PAD:(3 + g) * HEAD_PAD].astype(F32))
                vst[bi, g] = jnp.where(row_t == one_row, 1.0, v_t).astype(vst.dtype)
            return carry

        lax.fori_loop(0, seq // kb_rows, fill, 0)

    tok = lax.broadcasted_iota(jnp.int32, (1, Q_BLOCK), 1)
    lane_q = lax.broadcasted_iota(jnp.int32, (1, HEAD_PAD), 1)
    pen_lanes = (lane_q >= NSA_DH) & (lane_q < NSA_DH + n_sel_blk)
    place = (lax.broadcasted_iota(jnp.int32, (n_sel_blk, HEAD_PAD), 1)
             == NSA_DH + lax.broadcasted_iota(jnp.int32, (n_sel_blk, HEAD_PAD), 0)).astype(BF16)
    n_sub = lax.broadcasted_iota(jnp.int32, (n_cmp_pad, 1), 0)

    def add_bias(s, bias):
        return jnp.concatenate([s[:, r * Q_BLOCK:(r + 1) * Q_BLOCK] + bias for r in range(s.shape[1] // Q_BLOCK)],
                               axis=1)

    q0_s, t_lane_s, gates_s, qp, o_cmp = [], [], [], {}, {}
    for sub in range(NSA_Q_SUB):
        q0 = (step * NSA_Q_SUB + sub) * Q_BLOCK
        tok_rows = slice(sub * Q_BLOCK, (sub + 1) * Q_BLOCK)
        t_lane = q0 + tok
        q0_s.append(q0)
        t_lane_s.append(t_lane)
        gates_s.append(jnp.transpose(_sigmoid(sm_ref[tok_rows, :])))

        cmp_bias = jnp.where((n_sub * CMP_STRIDE + (CMP_BLOCK - 1)) <= t_lane, 0.0, NEG_BIG)
        cmp_any = jnp.concatenate([(t_lane >= CMP_BLOCK - 1).astype(F32)] * NSA_REP, axis=1)

        for g in groups:
            qs = jnp.concatenate(
                [q_ref[tok_rows, (g * NSA_REP + r) * HEAD_PAD:(g * NSA_REP + r + 1) * HEAD_PAD]
                 for r in range(NSA_REP)], axis=0)

            s_c = add_bias(_dot_nt(kc_ref[0, g], qs), cmp_bias)
            e_c = jnp.exp2(s_c - jnp.max(s_c, axis=0, keepdims=True))
            p_c = e_c * (cmp_any / jnp.sum(e_c, axis=0, keepdims=True))
            o_cmp[sub, g] = _dot_tn(vc_ref[0, g], p_c)

            p_sum = p_c[:, 0:Q_BLOCK]
            for r in range(1, NSA_REP):
                p_sum = p_sum + p_c[:, r * Q_BLOCK:(r + 1) * Q_BLOCK]
            imp_t = _dot(ovt_ref[...], p_sum)

            def rank_select(imp_t=imp_t, t_lane=t_lane):
                j_sub = lax.broadcasted_iota(jnp.int32, (n_sel_blk, 1), 0)
                cur = jnp.right_shift(t_lane, SEL_SHIFT)
                forced = (j_sub == 0) | (j_sub == cur) | (j_sub == cur - 1)
                causal_blk = (j_sub * SEL_BLOCK) <= t_lane
                imp = jnp.where(forced, FORCE_SCORE, jnp.where(causal_blk, imp_t, -jnp.inf))
                rank = jnp.zeros((n_sel_blk, Q_BLOCK), F32)
                for jp in range(n_sel_blk):
                    row = imp[jp:jp + 1, :]
                    ahead = (row > imp) | ((row == imp) & (j_sub > jp))
                    rank = rank + ahead.astype(F32)
                return (rank < SEL_TOPK).astype(BF16)

            sel_t = lax.cond(q0 + Q_BLOCK <= SEL_TOPK * SEL_BLOCK,
                             lambda: jnp.ones((n_sel_blk, Q_BLOCK), BF16), rank_select)
            sel_lanes = _dot_tn(sel_t, place)
            pen = jnp.where(pen_lanes, (sel_lanes - 1.0) * (-NEG_BIG), 0.0).astype(BF16)
            qp[sub, g] = qs + jnp.concatenate([pen] * NSA_REP, axis=0)

    n_full = step

    def scores(kb, unit):
        k0 = pl.multiple_of(kb * kb_rows, kb_rows)
        g = unit[1]
        return _dot_nt(ksa[pl.ds(k0, kb_rows), g * HEAD_PAD:(g + 1) * HEAD_PAD], qp[unit])

    def absorb(state, s, kb, unit):
        m_i, acc = state
        m_new = jnp.maximum(m_i, jnp.max(s, axis=0, keepdims=True))
        return m_new, jnp.exp2(m_i - m_new) * acc + _dot(vst[kb, unit[1]], jnp.exp2(s - m_new))

    kpos_last = n_full * kb_rows + lax.broadcasted_iota(jnp.int32, (kb_rows, 1), 0)
    first = []
    for unit in units:
        s = add_bias(scores(n_full, unit), jnp.where(kpos_last <= t_lane_s[unit[0]], 0.0, NEG_BIG))
        m = jnp.max(s, axis=0, keepdims=True)
        first.append((m, _dot(vst[n_full, unit[1]], jnp.exp2(s - m))))

    def full_blocks(state):
        def loop_body(kb, carry):
            st, s_cur = carry
            s_next = tuple(scores(kb + 1, u) for u in units)
            return tuple(absorb(st[i], s_cur[i], kb, u) for i, u in enumerate(units)), s_next

        st, s_cur = lax.fori_loop(0, n_full - 1, loop_body, (state, tuple(scores(0, u) for u in units)))
        return tuple(absorb(st[i], s_cur[i], n_full - 1, u) for i, u in enumerate(units))

    sel_out = lax.cond(n_full > 0, full_blocks, lambda st: st, tuple(first))

    span = WINDOW + Q_BLOCK
    for i, (sub, g) in enumerate(units):
        q0, t_lane = q0_s[sub], t_lane_s[sub]
        tok_rows = slice(sub * Q_BLOCK, (sub + 1) * Q_BLOCK)
        w0 = pl.multiple_of(jnp.maximum(q0 - WINDOW, 0), Q_BLOCK)
        kpos_w = w0 + lax.broadcasted_iota(jnp.int32, (span, 1), 0)
        win_bias = jnp.where((kpos_w <= t_lane) & (kpos_w > t_lane - WINDOW), 0.0, NEG_BIG)

        acc_s = sel_out[i][1]
        o_slc = acc_s[0:NSA_DH] * (1.0 / acc_s[one_row:one_row + 1, :])

        k_w = kv_ref[pl.ds(w0, span), (4 + g) * HEAD_PAD:(5 + g) * HEAD_PAD]
        s_w = add_bias(_dot_nt(k_w, qp[sub, g]), win_bias)
        acc_w = _dot_tn(vwa[pl.ds(w0, span), g * HEAD_PAD:(g + 1) * HEAD_PAD],
                        jnp.exp2(s_w - jnp.max(s_w, axis=0, keepdims=True)))
        o_win = acc_w[0:NSA_DH] * (1.0 / acc_w[one_row:one_row + 1, :])

        gates_t = gates_s[sub]
        o_heads = []
        for r in range(NSA_REP):
            c0 = 8 + (g * NSA_REP + r) * 3
            sl = slice(r * Q_BLOCK, (r + 1) * Q_BLOCK)
            o_t = (gates_t[c0:c0 + 1, :] * o_cmp[sub, g][0:NSA_DH, sl] + gates_t[c0 + 1:c0 + 2, :] * o_slc[:, sl]
                   + gates_t[c0 + 2:c0 + 3, :] * o_win[:, sl])
            o_heads.append(o_t)
        for pair in range(NSA_REP // 2):
            lanes = slice((g * NSA_REP + 2 * pair) * NSA_DH, (g * NSA_REP + 2 * pair + 2) * NSA_DH)
            o_ref[tok_rows, lanes] = jnp.transpose(jnp.concatenate(o_heads[2 * pair:2 * pair + 2], axis=0)
                                                   ).astype(o_ref.dtype)


def _nsa(nq, nkv, kc, vc, small, overlap, bsz, seq):
    assert NSA_Q_SUB * Q_BLOCK == SEL_KEY_BLOCK
    rows = NSA_Q_SUB * Q_BLOCK
    nstep = seq // rows
    in_specs = [pl.BlockSpec((rows, NSA_HEADS * HEAD_PAD), lambda b, i: (b * nstep + i, 0)),
                pl.BlockSpec((seq, nkv.shape[1]), lambda b, i: (b, 0)),
                pl.BlockSpec((1,) + kc.shape[1:], lambda b, i: (b, 0, 0, 0)),
                pl.BlockSpec((1,) + vc.shape[1:], lambda b, i: (b, 0, 0, 0)),
                pl.BlockSpec((rows, LANE), lambda b, i: (b * nstep + i, 0)),
                _full_spec(overlap.shape)]
    return pl.pallas_call(
        functools.partial(_nsa_kernel, seq=seq), grid=(bsz, nstep), in_specs=in_specs,
        out_specs=pl.BlockSpec((rows, NSA_QW), lambda b, i: (b * nstep + i, 0)),
        out_shape=jax.ShapeDtypeStruct((bsz * seq, NSA_QW), BF16),
        scratch_shapes=[pltpu.VMEM((seq, NSA_KV_HEADS * HEAD_PAD), BF16),
                        pltpu.VMEM((seq // SEL_KEY_BLOCK, NSA_KV_HEADS, HEAD_PAD, SEL_KEY_BLOCK), BF16),
                        pltpu.VMEM((seq, NSA_KV_HEADS * HEAD_PAD), BF16)],
        compiler_params=_cparams(("parallel", "arbitrary")), name="nsa_attention",
    )(nq, nkv, kc, vc, small, overlap)


def _merge_kernel(oa_ref, ob_ref, mix_ref, x_ref, wa_ref, wb_ref, wo_ref, g_ref, b_ref, h_ref):
    ya = jnp.dot(oa_ref[...], wa_ref[...], preferred_element_type=F32)
    yb = jnp.dot(ob_ref[...], wb_ref[...], preferred_element_type=F32)
    mix = mix_ref[...].astype(F32)
    mixed = _sigmoid(mix[:, 0:D_MODEL]) * ya + _sigmoid(mix[:, D_MODEL:2 * D_MODEL]) * yb
    y = _dot(mixed, wo_ref[...])
    h_ref[...] = _layer_norm(DN_ALPHA * x_ref[...] + y, g_ref[...], b_ref[...])


def _merge(oa, ob, mix, x2d, wa, wb, wo, g, b):
    m = x2d.shape[0]
    tm = ROW_TILE
    row = lambda n: pl.BlockSpec((tm, n), lambda i: (i, 0))
    in_specs = [row(oa.shape[1]), row(ob.shape[1]), row(mix.shape[1]), row(D_MODEL),
                _full_spec(wa.shape), _full_spec(wb.shape), _full_spec(wo.shape),
                _full_spec(g.shape), _full_spec(b.shape)]
    return pl.pallas_call(
        _merge_kernel, grid=(m // tm,), in_specs=in_specs, out_specs=row(D_MODEL),
        out_shape=jax.ShapeDtypeStruct((m, D_MODEL), F32),
        compiler_params=_cparams(("parallel",)), name="merge_ln",
    )(oa, ob, mix, x2d, wa, wb, wo, g, b)


def _ffn_kernel(h_ref, halo_ref, wup_ref, cw_ref, wd_ref, g_ref, b_ref, o_ref,
                hb, ug, uv, act, *, seq, tm, halo):
    i = pl.program_id(0)
    first = lax.rem(i * tm, seq) == 0
    hb[0:halo, :] = jnp.where(first, 0.0, halo_ref[...]).astype(BF16)
    hb[halo:, :] = h_ref[...].astype(BF16)

    def conv(u_ref, slot, cols):
        out = None
        for tap in range(FFN_CONV):
            term = cw_ref[tap:tap + 1, cols] * u_ref[slot, pl.ds(halo - (FFN_CONV - 1) + tap, tm), :]
            out = term if out is None else out + term
        return out

    for j in range(D_FF // FFN_CHUNK):
        slot = j % 2
        cols_g = slice(j * FFN_CHUNK, (j + 1) * FFN_CHUNK)
        cols_v = slice(D_FF + j * FFN_CHUNK, D_FF + (j + 1) * FFN_CHUNK)
        ug[slot] = jnp.dot(hb[...], wup_ref[:, cols_g], preferred_element_type=F32)
        uv[slot] = jnp.dot(hb[...], wup_ref[:, cols_v], preferred_element_type=F32)
        a = conv(ug, slot, cols_g)
        act[:, cols_g] = (a * _sigmoid(a) * conv(uv, slot, cols_v)).astype(act.dtype)

    f = jnp.dot(act[...], wd_ref[...], preferred_element_type=F32)
    o_ref[...] = _layer_norm(DN_ALPHA * h_ref[...] + f, g_ref[...], b_ref[...])


def _ffn(h, w_up, conv_w, w_down, g, b, seq):
    m = h.shape[0]
    tm = ROW_TILE
    halo = 16
    resident = lambda shape: pl.BlockSpec(shape, lambda i: (0, 0), pipeline_mode=pl.Buffered(1))
    in_specs = [pl.BlockSpec((tm, D_MODEL), lambda i: (i, 0)),
                pl.BlockSpec((halo, D_MODEL), lambda i: (jnp.maximum(i * (tm // halo) - 1, 0), 0)),
                resident(w_up.shape), resident(conv_w.shape), resident(w_down.shape),
                resident(g.shape), resident(b.shape)]
    return pl.pallas_call(
        functools.partial(_ffn_kernel, seq=seq, tm=tm, halo=halo), grid=(m // tm,),
        in_specs=in_specs, out_specs=pl.BlockSpec((tm, D_MODEL), lambda i: (i, 0)),
        out_shape=jax.ShapeDtypeStruct((m, D_MODEL), F32),
        scratch_shapes=[pltpu.VMEM((tm + halo, D_MODEL), BF16),
                        pltpu.VMEM((2, tm + halo, FFN_CHUNK), F32), pltpu.VMEM((2, tm + halo, FFN_CHUNK), F32),
                        pltpu.VMEM((tm, D_FF), BF16)],
        compiler_params=_cparams(("parallel",)), name="conv_ffn_ln",
    )(h, h, w_up, conv_w, w_down, g, b)


def _pad_heads(w, n_heads):
    lead = w.shape[:-1]
    w = w.reshape(lead + (n_heads, NSA_DH))
    w = jnp.pad(w, [(0, 0)] * len(lead) + [(0, 0), (0, HEAD_PAD - NSA_DH)])
    return w.reshape(lead + (n_heads * HEAD_PAD,))


def _compress_weights(w1):
    half = CMP_BLOCK // 2
    w1r = w1.reshape(2, half, NSA_DH, CMP_HIDDEN)
    eye = jnp.eye(NSA_KV_HEADS, dtype=w1.dtype)
    ex = jnp.einsum("aidh,gk->agikdh", w1r, eye)
    ex = ex.reshape(2 * NSA_KV_HEADS, half * NSA_KVW, CMP_HIDDEN)
    return jnp.concatenate([ex[n] for n in range(2 * NSA_KV_HEADS)], axis=1)


def _overlap_matrix(seq):
    n_cmp = (seq - CMP_BLOCK) // CMP_STRIDE + 1
    n_sel = seq // SEL_BLOCK
    starts = np.arange(n_cmp) * CMP_STRIDE
    jb = np.arange(n_sel) * SEL_BLOCK
    ov = ((starts[:, None] < jb[None] + SEL_BLOCK) & (starts[:, None] + CMP_BLOCK > jb[None])).astype(np.float32)
    out = np.zeros((n_sel, seq // CMP_STRIDE), np.float32)
    out[:, :n_cmp] = ov.T
    return jnp.asarray(out, BF16)


def kernel(x, w_in, gdn_conv_w, gdn_a_log, gdn_dt_bias, gdn_norm_w, cmp_pos_k, cmp_w1_k, cmp_w2_k,
           cmp_pos_v, cmp_w1_v, cmp_w2_v, w_branch_gdn, w_branch_nsa, w_out, ln1_g, ln1_b, w_up,
           ffn_conv_w, w_down, ln2_g, ln2_b):
    bsz, seq, _ = x.shape
    m = bsz * seq
    for i in range(DEPTH):
        x2d = x.reshape(m, D_MODEL)
        w = w_in[i]
        o_gdn = 2 * GDN_QK + 2 * GDN_VW
        o_nq = o_gdn + 2 * GDN_HEADS
        o_kv = o_nq + NSA_QW
        o_gate = o_kv + 6 * NSA_KVW
        o_mix = o_gate + 3 * NSA_HEADS
        kv = [w[:, o_kv + n * NSA_KVW:o_kv + (n + 1) * NSA_KVW] for n in range(6)]
        small_w = jnp.concatenate([w[:, o_gdn:o_nq], w[:, o_gate:o_mix]], axis=1)
        small_w = jnp.pad(small_w, ((0, 0), (0, LANE - small_w.shape[1])))
        weights = [w[:, :o_gdn],
                   _pad_heads(w[:, o_nq:o_kv] * (NSA_DH ** -0.5 * LOG2E), NSA_HEADS),
                   jnp.concatenate([_pad_heads(t, NSA_KV_HEADS) for t in kv[2:]], axis=1),
                   kv[0], kv[1],
                   w[:, o_mix:],
                   small_w]
        weights = [t.astype(BF16) for t in weights]
        gate_rows = w[:, o_gdn:o_nq].T.astype(BF16)
        gqkvz, nq, nkv, kcp, vcp, mix, small, grow = _inproj(x2d, weights, [BF16] * 6 + [F32], gate_rows)

        alog = jnp.broadcast_to(gdn_a_log[i][:, None, None], (GDN_HEADS, 1, LANE))
        dtb = jnp.broadcast_to(gdn_dt_bias[i][:, None, None], (GDN_HEADS, 1, LANE))
        o_a = _gdn(gqkvz, gdn_conv_w[i], small, grow, alog, dtb, gdn_norm_w[i][None, :], bsz, seq)

        grp = seq // CMP_STRIDE
        k16 = kcp.reshape(bsz, grp, CMP_STRIDE * NSA_KVW)
        v16 = vcp.reshape(bsz, grp, CMP_STRIDE * NSA_KVW)
        pos_rows = lambda p: jnp.pad(p.reshape(1, CMP_BLOCK * NSA_DH), ((0, 7), (0, 0))).astype(BF16)
        w2_pad = lambda t: jnp.pad(t, ((0, 0), (0, HEAD_PAD - NSA_DH))).astype(BF16)
        kc, vc = _compress(k16, v16,
                           _compress_weights(cmp_w1_k[i]).astype(BF16), _compress_weights(cmp_w1_v[i]).astype(BF16),
                           cmp_w1_k[i].astype(BF16), cmp_w1_v[i].astype(BF16),
                           pos_rows(cmp_pos_k[i]), pos_rows(cmp_pos_v[i]),
                           w2_pad(cmp_w2_k[i]), w2_pad(cmp_w2_v[i]))
        o_b = _nsa(nq, nkv, kc, vc, small, _overlap_matrix(seq), bsz, seq)

        h = _merge(o_a, o_b, mix, x2d, w_branch_gdn[i].astype(BF16), w_branch_nsa[i].astype(BF16), w_out[i].astype(BF16),
                   ln1_g[i][None, :], ln1_b[i][None, :])

        x = _ffn(h, w_up[i].astype(BF16), ffn_conv_w[i], w_down[i].astype(BF16),
                 ln2_g[i][None, :], ln2_b[i][None, :], seq).reshape(bsz, seq, D_MODEL)
    return x
```

```python
import functools
import math

import numpy as np
import jax
import jax.numpy as jnp
from jax import lax
from jax.experimental import pallas as pl
from jax.experimental.pallas import tpu as pltpu

F32 = jnp.float32
BF16 = jnp.bfloat16

D_MODEL = 1024
GDN_HEADS = 4
GDN_DK = 128
GDN_DV = 128
GDN_CONV = 4
GDN_CHUNK = 64
NSA_HEADS = 8
NSA_KV_HEADS = 2
NSA_REP = NSA_HEADS // NSA_KV_HEADS
NSA_DH = 64
CMP_BLOCK = 32
CMP_STRIDE = 16
CMP_HIDDEN = 256
SEL_BLOCK = 64
SEL_SHIFT = 6
SEL_TOPK = 16
WINDOW = 512
FORCE_SCORE = 1e9
D_FF = 2816
FFN_CONV = 3
DEPTH = 1
DN_ALPHA = (2 * DEPTH) ** 0.25
LN_EPS = 1e-5
RMS_EPS = 1e-6
LOG2E = math.log2(math.e)

GDN_QK = GDN_HEADS * GDN_DK
GDN_VW = GDN_HEADS * GDN_DV
NSA_QW = NSA_HEADS * NSA_DH
NSA_KVW = NSA_KV_HEADS * NSA_DH

LANE = 128
HEAD_PAD = LANE
NEG_BIG = -1e30
VMEM_LIMIT = 56 * 1024 * 1024

Q_BLOCK = 128
NSA_Q_SUB = 2
SEL_KEY_BLOCK = 256
FFN_CHUNK = 256
ROW_TILE = 512
GDN_CHUNK_SHIFT = 6
GDN_PAIR = 2 * GDN_CHUNK
GDN_UNROLL = 16


def _cparams(sem):
    return pltpu.CompilerParams(dimension_semantics=sem, vmem_limit_bytes=VMEM_LIMIT)


def _dot(a, b):
    return jnp.dot(a.astype(BF16), b.astype(BF16), preferred_element_type=F32)


def _dot_nt(a, b):
    return lax.dot_general(a.astype(BF16), b.astype(BF16), (((1,), (1,)), ((), ())),
                           preferred_element_type=F32)


def _dot_tn(a, b):
    return lax.dot_general(a.astype(BF16), b.astype(BF16), (((0,), (0,)), ((), ())),
                           preferred_element_type=F32)


def _sigmoid(x):
    return jax.nn.sigmoid(x)


def _softplus(x):
    return jnp.maximum(x, 0.0) + jnp.log1p(jnp.exp(-jnp.abs(x)))


def _layer_norm(x, g, b):
    mu = jnp.mean(x, axis=-1, keepdims=True)
    xc = x - mu
    var = jnp.mean(xc * xc, axis=-1, keepdims=True)
    return xc * lax.rsqrt(var + LN_EPS) * g + b


def _full_spec(shape):
    nd = len(shape)
    return pl.BlockSpec(shape, lambda *_: (0,) * nd)


def _inproj_kernel(x_ref, *refs, t_rows):
    n = (len(refs) - 1) // 2
    xb = x_ref[...].astype(BF16)
    for w_ref, o_ref in zip(refs[:n], refs[n:2 * n]):
        res = jnp.dot(xb, w_ref[...], preferred_element_type=F32)
        o_ref[...] = res.astype(o_ref.dtype)
    t_ref = refs[2 * n]
    for j in range(t_ref.shape[0]):
        t_ref[j] = jnp.transpose(res[j * LANE:(j + 1) * LANE, :])[0:t_rows, :]


def _inproj(x2d, weights, out_dtypes, t_rows):
    assert weights[-1].shape[1] == LANE and out_dtypes[-1] == F32
    m = x2d.shape[0]
    tm = ROW_TILE
    in_specs = [pl.BlockSpec((tm, D_MODEL), lambda i: (i, 0))]
    in_specs += [_full_spec(w.shape) for w in weights]
    out_specs = [pl.BlockSpec((tm, w.shape[1]), lambda i: (i, 0)) for w in weights]
    out_specs.append(pl.BlockSpec((tm // LANE, t_rows, LANE), lambda i: (i, 0, 0)))
    out_shape = [jax.ShapeDtypeStruct((m, w.shape[1]), dt) for w, dt in zip(weights, out_dtypes)]
    out_shape.append(jax.ShapeDtypeStruct((m // LANE, t_rows, LANE), F32))
    return pl.pallas_call(
        functools.partial(_inproj_kernel, t_rows=t_rows), grid=(m // tm,), in_specs=in_specs, out_specs=out_specs,
        out_shape=out_shape, compiler_params=_cparams(("parallel",)), name="inproj")(x2d, *weights)


def _gdn_kernel(q_ref, k_ref, v_ref, z_ref, cwq_ref, cwk_ref, cwv_ref, gcol_ref, grow_ref,
                alog_ref, dtb_ref, nw_ref, o_ref,
                xpad, qh, kh, vh, xp, qk_s, rhs, kd, qd, gl, sol, mq, ns, o0, st, *, seq):
    c = GDN_CHUNK
    pr = GDN_PAIR
    npair = seq // pr
    head = pl.program_id(1)
    pad = 8
    conv_rows = 256
    xpad[0:pad, :] = jnp.zeros((pad, 3 * LANE), F32)
    xpad[pad:, 0:LANE] = q_ref[...].astype(F32)
    xpad[pad:, LANE:2 * LANE] = k_ref[...].astype(F32)
    xpad[pad:, 2 * LANE:3 * LANE] = v_ref[...].astype(F32)
    cw = jnp.concatenate([cwq_ref[...], cwk_ref[...], cwv_ref[...]], axis=1)

    for t0 in range(0, seq, conv_rows):
        acc = jnp.zeros((conv_rows, 3 * LANE), F32)
        for j in range(GDN_CONV):
            xs = xpad[t0 + pad - (GDN_CONV - 1) + j:t0 + pad - (GDN_CONV - 1) + j + conv_rows, :]
            acc = acc + xs * cw[j:j + 1, :]
        y = acc * _sigmoid(acc)
        yq, yk, yv = y[:, 0:LANE], y[:, LANE:2 * LANE], y[:, 2 * LANE:3 * LANE]
        yq = yq * lax.rsqrt(jnp.sum(yq * yq, axis=-1, keepdims=True) + RMS_EPS) * (GDN_DK ** -0.5)
        yk = yk * lax.rsqrt(jnp.sum(yk * yk, axis=-1, keepdims=True) + RMS_EPS)
        qh[t0:t0 + conv_rows, :] = yq
        kh[t0:t0 + conv_rows, :] = yk
        vh[t0:t0 + conv_rows, :] = yv

    neg_a = -jnp.exp(alog_ref[0, :, 0:1])
    dtb = dtb_ref[0, :, 0:1]
    ii = lax.broadcasted_iota(jnp.int32, (pr, pr), 0)
    jj = lax.broadcasted_iota(jnp.int32, (pr, pr), 1)
    same = jnp.right_shift(ii, GDN_CHUNK_SHIFT) == jnp.right_shift(jj, GDN_CHUNK_SHIFT)
    incl = same & (jj <= ii)
    strict = same & (jj < ii)
    upper = same & (ii <= jj)
    eye = (ii == jj).astype(F32)
    sub8 = lax.broadcasted_iota(jnp.int32, (8, LANE), 0)
    lane_id = lax.broadcasted_iota(jnp.int32, (1, LANE), 1)

    def over_pairs(body):
        def run(gi, carry):
            for uu in range(GDN_UNROLL):
                body(gi * GDN_UNROLL + uu)
            return carry
        lax.fori_loop(0, npair // GDN_UNROLL, run, 0)

    def stage_gates(p):
        rows = pl.ds(pl.multiple_of(p * pr, pr), pr)
        q = qh[rows, :]
        k = kh[rows, :]
        v = vh[rows, :]
        gcol = gcol_ref[rows, :]
        b_logit = jnp.sum(jnp.where(lane_id == head, gcol, 0.0), axis=1, keepdims=True)
        a_logit = jnp.sum(jnp.where(lane_id == GDN_HEADS + head, gcol, 0.0), axis=1, keepdims=True)
        a_row = jnp.sum(jnp.where(sub8[:, 0:1] == GDN_HEADS + head, grow_ref[p], 0.0), axis=0, keepdims=True)
        beta = _sigmoid(b_logit)
        g_c = neg_a * _softplus(a_logit + dtb)
        g_r = neg_a * _softplus(a_row + dtb)
        gc_col = jnp.sum(jnp.where(incl, g_r, 0.0), axis=1, keepdims=True)
        gc_row = jnp.sum(jnp.where(upper, g_c, 0.0), axis=0, keepdims=True)
        gt_col = jnp.sum(jnp.where(same, g_r, 0.0), axis=1, keepdims=True)
        decay = jnp.where(incl, jnp.exp(jnp.where(incl, gc_col - gc_row, 0.0)), 0.0)
        kq_k = _dot_nt(jnp.concatenate([k, q], axis=0), k)
        a_mat = jnp.where(strict, kq_k[0:pr] * decay * beta, 0.0)
        xp[p, 0:pr, :] = eye - a_mat
        xp[p, pr:2 * pr, :] = _dot(a_mat, a_mat)
        qk_s[p] = (kq_k[pr:2 * pr] * decay).astype(qk_s.dtype)
        e_gc = jnp.exp(gc_col)
        rhs[rows, 0:GDN_DV] = (v * beta).astype(rhs.dtype)
        rhs[rows, GDN_DV:GDN_DV + GDN_DK] = (k * (beta * e_gc)).astype(rhs.dtype)
        kd[rows, :] = (k * jnp.exp(gt_col - gc_col)).astype(kd.dtype)
        qd[rows, :] = q * e_gc
        g_last = jnp.exp(gt_col)
        gl[p] = jnp.where(sub8 == 0, g_last[0:1, :], g_last[c:c + 1, :])

    def stage_double(p):
        y = _dot(xp[p], xp[p, pr:2 * pr, :])
        xp[p, 0:pr, :] = xp[p, 0:pr, :] + y[0:pr]
        xp[p, pr:2 * pr, :] = y[pr:2 * pr]

    def stage_solve(p):
        rows = pl.ds(pl.multiple_of(p * pr, pr), pr)
        x_inv = xp[p, 0:pr, :]
        x_inv = x_inv + _dot(x_inv, xp[p, pr:2 * pr, :])
        sol[rows, :] = _dot(x_inv, rhs[rows, :])

    def stage_affine(p):
        r0 = pl.multiple_of(p * pr, pr)
        rows = pl.ds(r0, pr)
        e = _dot(qk_s[p], sol[rows, :])
        qe = qd[rows, :] - e[:, GDN_DV:GDN_DV + GDN_DK]
        g_rows = gl[p]
        for cc in range(pr // c):
            rc = pl.ds(r0 + cc * c, c)
            mn = _dot_tn(kd[rc, :], sol[rc, :])
            idx = p * (pr // c) + cc
            mq[head, idx, 0:GDN_DK, :] = (eye * g_rows[cc:cc + 1, :] - mn[:, GDN_DV:GDN_DV + GDN_DK]).astype(mq.dtype)
            mq[head, idx, GDN_DK:GDN_DK + c, :] = qe[cc * c:(cc + 1) * c].astype(mq.dtype)
            ns[head, idx] = mn[:, 0:GDN_DV]
            o0[head, idx] = e[cc * c:(cc + 1) * c, 0:GDN_DV]

    over_pairs(stage_gates)
    for _ in range(int(math.log2(c)) - 2):
        over_pairs(stage_double)
    over_pairs(stage_solve)
    over_pairs(stage_affine)

    @pl.when(head == GDN_HEADS - 1)
    def _():
        st[...] = jnp.zeros_like(st)
        nw = nw_ref[...]

        def step(ci, carry):
            r0 = pl.multiple_of(ci * c, c)
            for h in range(GDN_HEADS):
                cols = slice(h * LANE, (h + 1) * LANE)
                ys = jnp.dot(mq[h, ci], st[h], preferred_element_type=F32)
                st[h] = (ys[0:GDN_DK] + ns[h, ci]).astype(st.dtype)
                o = ys[GDN_DK:GDN_DK + c] + o0[h, ci]
                o = o * lax.rsqrt(jnp.mean(o * o, axis=-1, keepdims=True) + RMS_EPS) * nw
                z = z_ref[pl.ds(r0, c), cols].astype(F32)
                o_ref[pl.ds(r0, c), cols] = (o * (z * _sigmoid(z))).astype(o_ref.dtype)
            return carry

        lax.fori_loop(0, seq // c, step, 0)


def _gdn(gqkvz, conv_w, gcol, grow, alog, dtb, nw, bsz, seq):
    assert GDN_DK == GDN_PAIR == GDN_DV == LANE
    h = GDN_HEADS
    npair = seq // GDN_PAIR
    nchunk = seq // GDN_CHUNK
    blk = lambda off: pl.BlockSpec((seq, LANE), lambda b, hh: (b, off + hh))
    cwb = lambda off: pl.BlockSpec((GDN_CONV, LANE), lambda b, hh: (0, off + hh))
    in_specs = [blk(0), blk(h), blk(2 * h),
                pl.BlockSpec((seq, GDN_VW), lambda b, hh: (b, 3)),
                cwb(0), cwb(h), cwb(2 * h),
                pl.BlockSpec((seq, LANE), lambda b, hh: (b, 0)),
                pl.BlockSpec((npair, 8, GDN_PAIR), lambda b, hh: (b, 0, 0)),
                pl.BlockSpec((1, 1, LANE), lambda b, hh: (hh, 0, 0)),
                pl.BlockSpec((1, 1, LANE), lambda b, hh: (hh, 0, 0)),
                pl.BlockSpec((1, LANE), lambda b, hh: (0, 0))]
    scratch = [pltpu.VMEM((seq + 8, 3 * LANE), F32),
               pltpu.VMEM((seq, LANE), F32), pltpu.VMEM((seq, LANE), F32), pltpu.VMEM((seq, LANE), F32),
               pltpu.VMEM((npair, 2 * GDN_PAIR, GDN_PAIR), F32),
               pltpu.VMEM((npair, GDN_PAIR, GDN_PAIR), BF16),
               pltpu.VMEM((seq, GDN_DV + GDN_DK), BF16),
               pltpu.VMEM((seq, GDN_DK), BF16),
               pltpu.VMEM((seq, GDN_DK), F32),
               pltpu.VMEM((npair, 8, LANE), F32),
               pltpu.VMEM((seq, GDN_DV + GDN_DK), F32),
               pltpu.VMEM((h, nchunk, GDN_DK + GDN_CHUNK, GDN_DK), BF16),
               pltpu.VMEM((h, nchunk, GDN_DK, GDN_DV), F32),
               pltpu.VMEM((h, nchunk, GDN_CHUNK, GDN_DV), F32),
               pltpu.VMEM((h, GDN_DK, GDN_DV), BF16)]
    return pl.pallas_call(
        functools.partial(_gdn_kernel, seq=seq), grid=(bsz, h), in_specs=in_specs,
        out_specs=pl.BlockSpec((seq, GDN_VW), lambda b, hh: (b, 0)),
        out_shape=jax.ShapeDtypeStruct((bsz * seq, GDN_VW), BF16),
        scratch_shapes=scratch,
        compiler_params=_cparams(("parallel", "arbitrary")), name="gdn",
    )(gqkvz, gqkvz, gqkvz, gqkvz, conv_w, conv_w, conv_w, gcol, grow, alog, dtb, nw)


def _gelu_tanh(x):
    return 0.5 * x * (1.0 + jnp.tanh(math.sqrt(2.0 / math.pi) * (x + 0.044715 * x * x * x)))


def _compress_kernel(k16_ref, v16_ref, wkc_ref, wvc_ref, w1k_ref, w1v_ref, pk_ref, pv_ref,
                     w2k_ref, w2v_ref, kc_ref, vc_ref):
    nblk = k16_ref.shape[1]
    for x_ref, wc_ref, w1_ref, p_ref, w2_ref, o_ref in (
            (k16_ref, wkc_ref, w1k_ref, pk_ref, w2k_ref, kc_ref),
            (v16_ref, wvc_ref, w1v_ref, pv_ref, w2v_ref, vc_ref)):
        y = jnp.dot(x_ref[0], wc_ref[...], preferred_element_type=F32)
        pos_term = jnp.dot(p_ref[...], w1_ref[...], preferred_element_type=F32)[0:1, :]
        for g in range(NSA_KV_HEADS):
            top = y[:, g * CMP_HIDDEN:(g + 1) * CMP_HIDDEN]
            bot = y[:, (NSA_KV_HEADS + g) * CMP_HIDDEN:(NSA_KV_HEADS + g + 1) * CMP_HIDDEN]
            hid = top + pltpu.roll(bot, nblk - 1, 0) + pos_term
            o_ref[0, g] = _dot(_gelu_tanh(hid), w2_ref[...]).astype(o_ref.dtype)


def _compress(k16, v16, wkc, wvc, w1k, w1v, pk, pv, w2k, w2v):
    bsz, nblk, width = k16.shape
    in_specs = [pl.BlockSpec((1, nblk, width), lambda b: (b, 0, 0))] * 2
    in_specs += [_full_spec(a.shape) for a in (wkc, wvc, w1k, w1v, pk, pv, w2k, w2v)]
    out_spec = pl.BlockSpec((1, NSA_KV_HEADS, nblk, HEAD_PAD), lambda b: (b, 0, 0, 0))
    out_shape = jax.ShapeDtypeStruct((bsz, NSA_KV_HEADS, nblk, HEAD_PAD), BF16)
    return pl.pallas_call(
        _compress_kernel, grid=(bsz,), in_specs=in_specs, out_specs=[out_spec, out_spec],
        out_shape=[out_shape, out_shape], compiler_params=_cparams(("parallel",)), name="nsa_compress",
    )(k16, v16, wkc, wvc, w1k, w1v, pk, pv, w2k, w2v)


def _nsa_kernel(q_ref, kv_ref, kc_ref, vc_ref, sm_ref, ovt_ref, o_ref, ksa, vst, vwa, *, seq):
    step = pl.program_id(1)
    n_cmp_pad = kc_ref.shape[2]
    n_sel_blk = seq // SEL_BLOCK
    kb_rows = SEL_KEY_BLOCK
    one_row = NSA_DH
    groups = range(NSA_KV_HEADS)
    units = [(sub, g) for sub in range(NSA_Q_SUB) for g in groups]

    @pl.when(step == 0)
    def _():
        lane = lax.broadcasted_iota(jnp.int32, (1, NSA_KV_HEADS * HEAD_PAD), 1) & (HEAD_PAD - 1)
        ones_lane = (lane == one_row).astype(vwa.dtype)
        row_t = lax.broadcasted_iota(jnp.int32, (HEAD_PAD, 1), 0)

        def fill(bi, carry):
            r0 = pl.multiple_of(bi * kb_rows, kb_rows)
            rows = pl.ds(r0, kb_rows)
            blk = jnp.right_shift(r0 + lax.broadcasted_iota(jnp.int32, (kb_rows, 1), 0), SEL_SHIFT)
            ksa[rows, :] = kv_ref[rows, 0:2 * HEAD_PAD] + (lane == NSA_DH + blk).astype(ksa.dtype)
            vwa[rows, :] = kv_ref[rows, 6 * HEAD_PAD:8 * HEAD_PAD] + ones_lane
            for g in range(NSA_KV_HEADS):
                v_t = jnp.transpose(kv_ref[rows, (2 + g) * HEAD_PAD:(3 + g) * HEAD_PAD].astype(F32))
                vst[bi, g] = jnp.where(row_t == one_row, 1.0, v_t).astype(vst.dtype)
            return carry

        lax.fori_loop(0, seq // kb_rows, fill, 0)

    tok = lax.broadcasted_iota(jnp.int32, (1, Q_BLOCK), 1)
    lane_q = lax.broadcasted_iota(jnp.int32, (1, HEAD_PAD), 1)
    pen_lanes = (lane_q >= NSA_DH) & (lane_q < NSA_DH + n_sel_blk)
    place = (lax.broadcasted_iota(jnp.int32, (n_sel_blk, HEAD_PAD), 1)
             == NSA_DH + lax.broadcasted_iota(jnp.int32, (n_sel_blk, HEAD_PAD), 0)).astype(BF16)
    n_sub = lax.broadcasted_iota(jnp.int32, (n_cmp_pad, 1), 0)

    def add_bias(s, bias):
        return jnp.concatenate([s[:, r * Q_BLOCK:(r + 1) * Q_BLOCK] + bias for r in range(s.shape[1] // Q_BLOCK)],
                               axis=1)

    q0_s, t_lane_s, gates_s, qp, o_cmp = [], [], [], {}, {}
    for sub in range(NSA_Q_SUB):
        q0 = (step * NSA_Q_SUB + sub) * Q_BLOCK
        tok_rows = slice(sub * Q_BLOCK, (sub + 1) * Q_BLOCK)
        t_lane = q0 + tok
        q0_s.append(q0)
        t_lane_s.append(t_lane)
        gates_s.append(jnp.transpose(_sigmoid(sm_ref[tok_rows, :])))

        cmp_bias = jnp.where((n_sub * CMP_STRIDE + (CMP_BLOCK - 1)) <= t_lane, 0.0, NEG_BIG)
        cmp_any = jnp.concatenate([(t_lane >= CMP_BLOCK - 1).astype(F32)] * NSA_REP, axis=1)

        for g in groups:
            qs = jnp.concatenate(
                [q_ref[tok_rows, (g * NSA_REP + r) * HEAD_PAD:(g * NSA_REP + r + 1) * HEAD_PAD]
                 for r in range(NSA_REP)], axis=0)

            s_c = add_bias(_dot_nt(kc_ref[0, g], qs), cmp_bias)
            e_c = jnp.exp2(s_c - jnp.max(s_c, axis=0, keepdims=True))
            p_c = e_c * (cmp_any / jnp.sum(e_c, axis=0, keepdims=True))
            o_cmp[sub, g] = _dot_tn(vc_ref[0, g], p_c)[0:NSA_DH]

            p_sum = p_c[:, 0:Q_BLOCK]
            for r in range(1, NSA_REP):
                p_sum = p_sum + p_c[:, r * Q_BLOCK:(r + 1) * Q_BLOCK]
            imp_t = _dot(ovt_ref[...], p_sum)

            def rank_select(imp_t=imp_t, t_lane=t_lane):
                j_sub = lax.broadcasted_iota(jnp.int32, (n_sel_blk, 1), 0)
                cur = jnp.right_shift(t_lane, SEL_SHIFT)
                forced = (j_sub == 0) | (j_sub == cur) | (j_sub == cur - 1)
                causal_blk = (j_sub * SEL_BLOCK) <= t_lane
                imp = jnp.where(forced, FORCE_SCORE, jnp.where(causal_blk, imp_t, -jnp.inf))
                rank = jnp.zeros((n_sel_blk, Q_BLOCK), F32)
                for jp in range(n_sel_blk):
                    row = imp[jp:jp + 1, :]
                    ahead = (row > imp) | ((row == imp) & (j_sub > jp))
                    rank = rank + ahead.astype(F32)
                return (rank < SEL_TOPK).astype(BF16)

            sel_t = lax.cond(q0 + Q_BLOCK <= SEL_TOPK * SEL_BLOCK,
                             lambda: jnp.ones((n_sel_blk, Q_BLOCK), BF16), rank_select)
            sel_lanes = _dot_tn(sel_t, place)
            pen = jnp.where(pen_lanes, (sel_lanes - 1.0) * (-NEG_BIG), 0.0).astype(BF16)
            qp[sub, g] = qs + jnp.concatenate([pen] * NSA_REP, axis=0)

    n_full = step

    def scores(kb, unit):
        k0 = pl.multiple_of(kb * kb_rows, kb_rows)
        g = unit[1]
        return _dot_nt(ksa[pl.ds(k0, kb_rows), g * HEAD_PAD:(g + 1) * HEAD_PAD], qp[unit])

    def absorb(state, s, kb, unit):
        m_i, acc = state
        m_new = jnp.maximum(m_i, jnp.max(s, axis=0, keepdims=True))
        return m_new, jnp.exp2(m_i - m_new) * acc + _dot(vst[kb, unit[1]], jnp.exp2(s - m_new))

    kpos_last = n_full * kb_rows + lax.broadcasted_iota(jnp.int32, (kb_rows, 1), 0)
    first = []
    for unit in units:
        s = add_bias(scores(n_full, unit), jnp.where(kpos_last <= t_lane_s[unit[0]], 0.0, NEG_BIG))
        m = jnp.max(s, axis=0, keepdims=True)
        first.append((m, _dot(vst[n_full, unit[1]], jnp.exp2(s - m))))

    def full_blocks(state):
        def loop_body(kb, carry):
            st, s_cur = carry
            s_next = tuple(scores(kb + 1, u) for u in units)
            return tuple(absorb(st[i], s_cur[i], kb, u) for i, u in enumerate(units)), s_next

        st, s_cur = lax.fori_loop(0, n_full - 1, loop_body, (state, tuple(scores(0, u) for u in units)))
        return tuple(absorb(st[i], s_cur[i], n_full - 1, u) for i, u in enumerate(units))

    sel_out = lax.cond(n_full > 0, full_blocks, lambda st: st, tuple(first))

    span = WINDOW + Q_BLOCK
    for i, (sub, g) in enumerate(units):
        q0, t_lane = q0_s[sub], t_lane_s[sub]
        tok_rows = slice(sub * Q_BLOCK, (sub + 1) * Q_BLOCK)
        w0 = pl.multiple_of(jnp.maximum(q0 - WINDOW, 0), Q_BLOCK)
        kpos_w = w0 + lax.broadcasted_iota(jnp.int32, (span, 1), 0)
        win_bias = jnp.where((kpos_w <= t_lane) & (kpos_w > t_lane - WINDOW), 0.0, NEG_BIG)

        acc_s = sel_out[i][1]
        o_slc = acc_s[0:NSA_DH] * (1.0 / acc_s[one_row:one_row + 1, :])

        k_w = kv_ref[pl.ds(w0, span), (4 + g) * HEAD_PAD:(5 + g) * HEAD_PAD]
        s_w = add_bias(_dot_nt(k_w, qp[sub, g]), win_bias)
        acc_w = _dot_tn(vwa[pl.ds(w0, span), g * HEAD_PAD:(g + 1) * HEAD_PAD],
                        jnp.exp2(s_w - jnp.max(s_w, axis=0, keepdims=True)))
        o_win = acc_w[0:NSA_DH] * (1.0 / acc_w[one_row:one_row + 1, :])

        gates_t = gates_s[sub]
        o_heads = []
        for r in range(NSA_REP):
            c0 = 8 + (g * NSA_REP + r) * 3
            sl = slice(r * Q_BLOCK, (r + 1) * Q_BLOCK)
            o_t = (gates_t[c0:c0 + 1, :] * o_cmp[sub, g][:, sl] + gates_t[c0 + 1:c0 + 2, :] * o_slc[:, sl]
                   + gates_t[c0 + 2:c0 + 3, :] * o_win[:, sl])
            o_heads.append(o_t)
        for pair in range(NSA_REP // 2):
            lanes = slice((g * NSA_REP + 2 * pair) * NSA_DH, (g * NSA_REP + 2 * pair + 2) * NSA_DH)
            o_ref[tok_rows, lanes] = jnp.transpose(jnp.concatenate(o_heads[2 * pair:2 * pair + 2], axis=0)
                                                   ).astype(o_ref.dtype)


def _nsa(nq, nkv, kc, vc, small, overlap, bsz, seq):
    assert NSA_Q_SUB * Q_BLOCK == SEL_KEY_BLOCK
    rows = NSA_Q_SUB * Q_BLOCK
    nstep = seq // rows
    in_specs = [pl.BlockSpec((rows, NSA_HEADS * HEAD_PAD), lambda b, i: (b * nstep + i, 0)),
                pl.BlockSpec((seq, nkv.shape[1]), lambda b, i: (b, 0)),
                pl.BlockSpec((1,) + kc.shape[1:], lambda b, i: (b, 0, 0, 0)),
                pl.BlockSpec((1,) + vc.shape[1:], lambda b, i: (b, 0, 0, 0)),
                pl.BlockSpec((rows, LANE), lambda b, i: (b * nstep + i, 0)),
                _full_spec(overlap.shape)]
    return pl.pallas_call(
        functools.partial(_nsa_kernel, seq=seq), grid=(bsz, nstep), in_specs=in_specs,
        out_specs=pl.BlockSpec((rows, NSA_QW), lambda b, i: (b * nstep + i, 0)),
        out_shape=jax.ShapeDtypeStruct((bsz * seq, NSA_QW), BF16),
        scratch_shapes=[pltpu.VMEM((seq, NSA_KV_HEADS * HEAD_PAD), BF16),
                        pltpu.VMEM((seq // SEL_KEY_BLOCK, NSA_KV_HEADS, HEAD_PAD, SEL_KEY_BLOCK), BF16),
                        pltpu.VMEM((seq, NSA_KV_HEADS * HEAD_PAD), BF16)],
        compiler_params=_cparams(("parallel", "arbitrary")), name="nsa_attention",
    )(nq, nkv, kc, vc, small, overlap)


def _merge_kernel(oa_ref, ob_ref, mix_ref, x_ref, wa_ref, wb_ref, wo_ref, g_ref, b_ref, h_ref):
    ya = jnp.dot(oa_ref[...], wa_ref[...], preferred_element_type=F32)
    yb = jnp.dot(ob_ref[...], wb_ref[...], preferred_element_type=F32)
    mix = mix_ref[...].astype(F32)
    mixed = _sigmoid(mix[:, 0:D_MODEL]) * ya + _sigmoid(mix[:, D_MODEL:2 * D_MODEL]) * yb
    y = _dot(mixed, wo_ref[...])
    h_ref[...] = _layer_norm(DN_ALPHA * x_ref[...] + y, g_ref[...], b_ref[...])


def _merge(oa, ob, mix, x2d, wa, wb, wo, g, b):
    m = x2d.shape[0]
    tm = ROW_TILE
    row = lambda n: pl.BlockSpec((tm, n), lambda i: (i, 0))
    in_specs = [row(oa.shape[1]), row(ob.shape[1]), row(mix.shape[1]), row(D_MODEL),
                _full_spec(wa.shape), _full_spec(wb.shape), _full_spec(wo.shape),
                _full_spec(g.shape), _full_spec(b.shape)]
    return pl.pallas_call(
        _merge_kernel, grid=(m // tm,), in_specs=in_specs, out_specs=row(D_MODEL),
        out_shape=jax.ShapeDtypeStruct((m, D_MODEL), F32),
        compiler_params=_cparams(("parallel",)), name="merge_ln",
    )(oa, ob, mix, x2d, wa, wb, wo, g, b)


def _ffn_kernel(h_ref, halo_ref, wup_ref, cw_ref, wd_ref, g_ref, b_ref, o_ref,
                hb, ug, uv, act, *, seq, tm, halo):
    i = pl.program_id(0)
    first = lax.rem(i * tm, seq) == 0
    hb[0:halo, :] = jnp.where(first, 0.0, halo_ref[...]).astype(BF16)
    hb[halo:, :] = h_ref[...].astype(BF16)

    def conv(u_ref, slot, cols):
        out = None
        for tap in range(FFN_CONV):
            term = cw_ref[tap:tap + 1, cols] * u_ref[slot, pl.ds(halo - (FFN_CONV - 1) + tap, tm), :]
            out = term if out is None else out + term
        return out

    for j in range(D_FF // FFN_CHUNK):
        slot = j % 2
        cols_g = slice(j * FFN_CHUNK, (j + 1) * FFN_CHUNK)
        cols_v = slice(D_FF + j * FFN_CHUNK, D_FF + (j + 1) * FFN_CHUNK)
        ug[slot] = jnp.dot(hb[...], wup_ref[:, cols_g], preferred_element_type=F32)
        uv[slot] = jnp.dot(hb[...], wup_ref[:, cols_v], preferred_element_type=F32)
        a = conv(ug, slot, cols_g)
        act[:, cols_g] = (a * _sigmoid(a) * conv(uv, slot, cols_v)).astype(act.dtype)

    f = jnp.dot(act[...], wd_ref[...], preferred_element_type=F32)
    o_ref[...] = _layer_norm(DN_ALPHA * h_ref[...] + f, g_ref[...], b_ref[...])


def _ffn(h, w_up, conv_w, w_down, g, b, seq):
    m = h.shape[0]
    tm = ROW_TILE
    halo = 16
    resident = lambda shape: pl.BlockSpec(shape, lambda i: (0, 0), pipeline_mode=pl.Buffered(1))
    in_specs = [pl.BlockSpec((tm, D_MODEL), lambda i: (i, 0)),
                pl.BlockSpec((halo, D_MODEL), lambda i: (jnp.maximum(i * (tm // halo) - 1, 0), 0)),
                resident(w_up.shape), resident(conv_w.shape), resident(w_down.shape),
                resident(g.shape), resident(b.shape)]
    return pl.pallas_call(
        functools.partial(_ffn_kernel, seq=seq, tm=tm, halo=halo), grid=(m // tm,),
        in_specs=in_specs, out_specs=pl.BlockSpec((tm, D_MODEL), lambda i: (i, 0)),
        out_shape=jax.ShapeDtypeStruct((m, D_MODEL), F32),
        scratch_shapes=[pltpu.VMEM((tm + halo, D_MODEL), BF16),
                        pltpu.VMEM((2, tm + halo, FFN_CHUNK), F32), pltpu.VMEM((2, tm + halo, FFN_CHUNK), F32),
                        pltpu.VMEM((tm, D_FF), BF16)],
        compiler_params=_cparams(("parallel",)), name="conv_ffn_ln",
    )(h, h, w_up, conv_w, w_down, g, b)


def _pad_heads(w, n_heads):
    zeros = jnp.zeros(w.shape[:-1] + (HEAD_PAD - NSA_DH,), w.dtype)
    parts = []
    for hd in range(n_heads):
        parts += [w[..., hd * NSA_DH:(hd + 1) * NSA_DH], zeros]
    return jnp.concatenate(parts, axis=-1)


def _compress_weights(w1):
    half = CMP_BLOCK // 2
    w1r = w1.reshape(2, half, NSA_DH, CMP_HIDDEN)
    eye = jnp.eye(NSA_KV_HEADS, dtype=w1.dtype)
    ex = jnp.einsum("aidh,gk->agikdh", w1r, eye)
    ex = ex.reshape(2 * NSA_KV_HEADS, half * NSA_KVW, CMP_HIDDEN)
    return jnp.concatenate([ex[n] for n in range(2 * NSA_KV_HEADS)], axis=1)


def _overlap_matrix(seq):
    n_cmp = (seq - CMP_BLOCK) // CMP_STRIDE + 1
    n_sel = seq // SEL_BLOCK
    starts = np.arange(n_cmp) * CMP_STRIDE
    jb = np.arange(n_sel) * SEL_BLOCK
    ov = ((starts[:, None] < jb[None] + SEL_BLOCK) & (starts[:, None] + CMP_BLOCK > jb[None])).astype(np.float32)
    out = np.zeros((n_sel, seq // CMP_STRIDE), np.float32)
    out[:, :n_cmp] = ov.T
    return jnp.asarray(out, BF16)


def kernel(x, w_in, gdn_conv_w, gdn_a_log, gdn_dt_bias, gdn_norm_w, cmp_pos_k, cmp_w1_k, cmp_w2_k,
           cmp_pos_v, cmp_w1_v, cmp_w2_v, w_branch_gdn, w_branch_nsa, w_out, ln1_g, ln1_b, w_up,
           ffn_conv_w, w_down, ln2_g, ln2_b):
    bsz, seq, _ = x.shape
    m = bsz * seq
    for i in range(DEPTH):
        x2d = x.reshape(m, D_MODEL)
        w = w_in[i]
        o_gdn = 2 * GDN_QK + 2 * GDN_VW
        o_nq = o_gdn + 2 * GDN_HEADS
        o_kv = o_nq + NSA_QW
        o_gate = o_kv + 6 * NSA_KVW
        o_mix = o_gate + 3 * NSA_HEADS
        kv = [w[:, o_kv + n * NSA_KVW:o_kv + (n + 1) * NSA_KVW] for n in range(6)]
        small_w = jnp.concatenate([w[:, o_gdn:o_nq], w[:, o_gate:o_mix]], axis=1)
        small_w = jnp.pad(small_w, ((0, 0), (0, LANE - small_w.shape[1])))
        weights = [w[:, :o_gdn],
                   _pad_heads(w[:, o_nq:o_kv] * (NSA_DH ** -0.5 * LOG2E), NSA_HEADS),
                   jnp.concatenate([_pad_heads(t, NSA_KV_HEADS) for t in kv[2:]], axis=1),
                   kv[0], kv[1],
                   w[:, o_mix:],
                   small_w]
        weights = [t.astype(BF16) for t in weights]
        gqkvz, nq, nkv, kcp, vcp, mix, small, grow = _inproj(x2d, weights, [BF16] * 6 + [F32], 2 * GDN_HEADS)

        alog = jnp.broadcast_to(gdn_a_log[i][:, None, None], (GDN_HEADS, 1, LANE))
        dtb = jnp.broadcast_to(gdn_dt_bias[i][:, None, None], (GDN_HEADS, 1, LANE))
        o_a = _gdn(gqkvz, gdn_conv_w[i], small, grow, alog, dtb, gdn_norm_w[i][None, :], bsz, seq)

        grp = seq // CMP_STRIDE
        k16 = kcp.reshape(bsz, grp, CMP_STRIDE * NSA_KVW)
        v16 = vcp.reshape(bsz, grp, CMP_STRIDE * NSA_KVW)
        pos_rows = lambda p: jnp.pad(p.reshape(1, CMP_BLOCK * NSA_DH), ((0, 7), (0, 0))).astype(BF16)
        w2_pad = lambda t: jnp.pad(t, ((0, 0), (0, HEAD_PAD - NSA_DH))).astype(BF16)
        kc, vc = _compress(k16, v16,
                           _compress_weights(cmp_w1_k[i]).astype(BF16), _compress_weights(cmp_w1_v[i]).astype(BF16),
                           cmp_w1_k[i].astype(BF16), cmp_w1_v[i].astype(BF16),
                           pos_rows(cmp_pos_k[i]), pos_rows(cmp_pos_v[i]),
                           w2_pad(cmp_w2_k[i]), w2_pad(cmp_w2_v[i]))
        o_b = _nsa(nq, nkv, kc, vc, small, _overlap_matrix(seq), bsz, seq)

        h = _merge(o_a, o_b, mix, x2d, w_branch_gdn[i].astype(BF16), w_branch_nsa[i].astype(BF16), w_out[i].astype(BF16),
                   ln1_g[i][None, :], ln1_b[i][None, :])

        x = _ffn(h, w_up[i].astype(BF16), ffn_conv_w[i], w_down[i].astype(BF16),
                 ln2_g[i][None, :], ln2_b[i][None, :], seq).reshape(bsz, seq, D_MODEL)
    return x
```

```python
import functools
import math

import numpy as np
import jax
import jax.numpy as jnp
from jax import lax
from jax.experimental import pallas as pl
from jax.experimental.pallas import tpu as pltpu

F32 = jnp.float32
BF16 = jnp.bfloat16

D_MODEL = 1024
GDN_HEADS = 4
GDN_DK = 128
GDN_DV = 128
GDN_CONV = 4
GDN_CHUNK = 64
NSA_HEADS = 8
NSA_KV_HEADS = 2
NSA_REP = NSA_HEADS // NSA_KV_HEADS
NSA_DH = 64
CMP_BLOCK = 32
CMP_STRIDE = 16
CMP_HIDDEN = 256
SEL_BLOCK = 64
SEL_SHIFT = 6
SEL_TOPK = 16
WINDOW = 512
FORCE_SCORE = 1e9
D_FF = 2816
FFN_CONV = 3
DEPTH = 1
DN_ALPHA = (2 * DEPTH) ** 0.25
LN_EPS = 1e-5
RMS_EPS = 1e-6
LOG2E = math.log2(math.e)

GDN_QK = GDN_HEADS * GDN_DK
GDN_VW = GDN_HEADS * GDN_DV
NSA_QW = NSA_HEADS * NSA_DH
NSA_KVW = NSA_KV_HEADS * NSA_DH

LANE = 128
HEAD_PAD = LANE
NEG_BIG = -1e30
VMEM_LIMIT = 56 * 1024 * 1024

Q_BLOCK = 128
NSA_Q_SUB = 2
SEL_KEY_BLOCK = 256
FFN_CHUNK = 256
ROW_TILE = 512
GDN_CHUNK_SHIFT = 6
GDN_PAIR = 2 * GDN_CHUNK
GDN_UNROLL = 16


def _cparams(sem):
    return pltpu.CompilerParams(dimension_semantics=sem, vmem_limit_bytes=VMEM_LIMIT)


def _dot(a, b):
    return jnp.dot(a.astype(BF16), b.astype(BF16), preferred_element_type=F32)


def _dot_nt(a, b):
    return lax.dot_general(a.astype(BF16), b.astype(BF16), (((1,), (1,)), ((), ())),
                           preferred_element_type=F32)


def _dot_tn(a, b):
    return lax.dot_general(a.astype(BF16), b.astype(BF16), (((0,), (0,)), ((), ())),
                           preferred_element_type=F32)


def _sigmoid(x):
    return jax.nn.sigmoid(x)


def _softplus(x):
    return jnp.maximum(x, 0.0) + jnp.log1p(jnp.exp(-jnp.abs(x)))


def _layer_norm(x, g, b):
    mu = jnp.mean(x, axis=-1, keepdims=True)
    xc = x - mu
    var = jnp.mean(xc * xc, axis=-1, keepdims=True)
    return xc * lax.rsqrt(var + LN_EPS) * g + b


def _full_spec(shape):
    nd = len(shape)
    return pl.BlockSpec(shape, lambda *_: (0,) * nd)


def _inproj_kernel(x_ref, *refs, t_rows):
    n = (len(refs) - 1) // 2
    xb = x_ref[...].astype(BF16)
    for w_ref, o_ref in zip(refs[:n], refs[n:2 * n]):
        res = jnp.dot(xb, w_ref[...], preferred_element_type=F32)
        if o_ref.shape[1] == w_ref.shape[1]:
            o_ref[...] = res.astype(o_ref.dtype)
        else:
            out = res.astype(o_ref.dtype)
            zeros = jnp.zeros((out.shape[0], HEAD_PAD - NSA_DH), o_ref.dtype)
            for hd in range(w_ref.shape[1] // NSA_DH):
                o_ref[:, hd * HEAD_PAD:hd * HEAD_PAD + NSA_DH] = out[:, hd * NSA_DH:(hd + 1) * NSA_DH]
                o_ref[:, hd * HEAD_PAD + NSA_DH:(hd + 1) * HEAD_PAD] = zeros
    t_ref = refs[2 * n]
    for j in range(t_ref.shape[0]):
        t_ref[j] = jnp.transpose(res[j * LANE:(j + 1) * LANE, :])[0:t_rows, :]


def _inproj(x2d, weights, out_dtypes, out_widths, t_rows):
    assert weights[-1].shape[1] == LANE and out_dtypes[-1] == F32
    m = x2d.shape[0]
    tm = ROW_TILE
    in_specs = [pl.BlockSpec((tm, D_MODEL), lambda i: (i, 0))]
    in_specs += [_full_spec(w.shape) for w in weights]
    out_specs = [pl.BlockSpec((tm, n), lambda i: (i, 0)) for n in out_widths]
    out_specs.append(pl.BlockSpec((tm // LANE, t_rows, LANE), lambda i: (i, 0, 0)))
    out_shape = [jax.ShapeDtypeStruct((m, n), dt) for n, dt in zip(out_widths, out_dtypes)]
    out_shape.append(jax.ShapeDtypeStruct((m // LANE, t_rows, LANE), F32))
    return pl.pallas_call(
        functools.partial(_inproj_kernel, t_rows=t_rows), grid=(m // tm,), in_specs=in_specs, out_specs=out_specs,
        out_shape=out_shape, compiler_params=_cparams(("parallel",)), name="inproj")(x2d, *weights)


def _gdn_kernel(q_ref, k_ref, v_ref, z_ref, cwq_ref, cwk_ref, cwv_ref, gcol_ref, grow_ref,
                alog_ref, dtb_ref, nw_ref, o_ref,
                xpad, qh, kh, vh, xp, qk_s, rhs, kd, qd, gl, sol, mq, ns, o0, st, *, seq):
    c = GDN_CHUNK
    pr = GDN_PAIR
    npair = seq // pr
    head = pl.program_id(1)
    pad = 8
    conv_rows = 256
    xpad[0:pad, :] = jnp.zeros((pad, 3 * LANE), F32)
    xpad[pad:, 0:LANE] = q_ref[...].astype(F32)
    xpad[pad:, LANE:2 * LANE] = k_ref[...].astype(F32)
    xpad[pad:, 2 * LANE:3 * LANE] = v_ref[...].astype(F32)
    cw = jnp.concatenate([cwq_ref[...], cwk_ref[...], cwv_ref[...]], axis=1)

    for t0 in range(0, seq, conv_rows):
        acc = jnp.zeros((conv_rows, 3 * LANE), F32)
        for j in range(GDN_CONV):
            xs = xpad[t0 + pad - (GDN_CONV - 1) + j:t0 + pad - (GDN_CONV - 1) + j + conv_rows, :]
            acc = acc + xs * cw[j:j + 1, :]
        y = acc * _sigmoid(acc)
        yq, yk, yv = y[:, 0:LANE], y[:, LANE:2 * LANE], y[:, 2 * LANE:3 * LANE]
        yq = yq * lax.rsqrt(jnp.sum(yq * yq, axis=-1, keepdims=True) + RMS_EPS) * (GDN_DK ** -0.5)
        yk = yk * lax.rsqrt(jnp.sum(yk * yk, axis=-1, keepdims=True) + RMS_EPS)
        qh[t0:t0 + conv_rows, :] = yq
        kh[t0:t0 + conv_rows, :] = yk
        vh[t0:t0 + conv_rows, :] = yv

    neg_a = -jnp.exp(alog_ref[0, :, 0:1])
    dtb = dtb_ref[0, :, 0:1]
    ii = lax.broadcasted_iota(jnp.int32, (pr, pr), 0)
    jj = lax.broadcasted_iota(jnp.int32, (pr, pr), 1)
    same = jnp.right_shift(ii, GDN_CHUNK_SHIFT) == jnp.right_shift(jj, GDN_CHUNK_SHIFT)
    incl = same & (jj <= ii)
    strict = same & (jj < ii)
    upper = same & (ii <= jj)
    eye = (ii == jj).astype(F32)
    sub8 = lax.broadcasted_iota(jnp.int32, (8, LANE), 0)
    lane_id = lax.broadcasted_iota(jnp.int32, (1, LANE), 1)

    def over_pairs(body):
        def run(gi, carry):
            for uu in range(GDN_UNROLL):
                body(gi * GDN_UNROLL + uu)
            return carry
        lax.fori_loop(0, npair // GDN_UNROLL, run, 0)

    def stage_gates(p):
        rows = pl.ds(pl.multiple_of(p * pr, pr), pr)
        q = qh[rows, :]
        k = kh[rows, :]
        v = vh[rows, :]
        gcol = gcol_ref[rows, :]
        b_logit = jnp.sum(jnp.where(lane_id == head, gcol, 0.0), axis=1, keepdims=True)
        a_logit = jnp.sum(jnp.where(lane_id == GDN_HEADS + head, gcol, 0.0), axis=1, keepdims=True)
        a_row = jnp.sum(jnp.where(sub8[:, 0:1] == GDN_HEADS + head, grow_ref[p], 0.0), axis=0, keepdims=True)
        beta = _sigmoid(b_logit)
        g_c = neg_a * _softplus(a_logit + dtb)
        g_r = neg_a * _softplus(a_row + dtb)
        gc_col = jnp.sum(jnp.where(incl, g_r, 0.0), axis=1, keepdims=True)
        gc_row = jnp.sum(jnp.where(upper, g_c, 0.0), axis=0, keepdims=True)
        gt_col = jnp.sum(jnp.where(same, g_r, 0.0), axis=1, keepdims=True)
        decay = jnp.where(incl, jnp.exp(jnp.where(incl, gc_col - gc_row, 0.0)), 0.0)
        kq_k = _dot_nt(jnp.concatenate([k, q], axis=0), k)
        a_mat = jnp.where(strict, kq_k[0:pr] * decay * beta, 0.0)
        xp[p, 0:pr, :] = eye - a_mat
        xp[p, pr:2 * pr, :] = _dot(a_mat, a_mat)
        qk_s[p] = (kq_k[pr:2 * pr] * decay).astype(qk_s.dtype)
        e_gc = jnp.exp(gc_col)
        rhs[rows, 0:GDN_DV] = (v * beta).astype(rhs.dtype)
        rhs[rows, GDN_DV:GDN_DV + GDN_DK] = (k * (beta * e_gc)).astype(rhs.dtype)
        kd[rows, :] = (k * jnp.exp(gt_col - gc_col)).astype(kd.dtype)
        qd[rows, :] = q * e_gc
        g_last = jnp.exp(gt_col)
        gl[p] = jnp.where(sub8 == 0, g_last[0:1, :], g_last[c:c + 1, :])

    def stage_double(p):
        y = _dot(xp[p], xp[p, pr:2 * pr, :])
        xp[p, 0:pr, :] = xp[p, 0:pr, :] + y[0:pr]
        xp[p, pr:2 * pr, :] = y[pr:2 * pr]

    def stage_solve(p):
        rows = pl.ds(pl.multiple_of(p * pr, pr), pr)
        x_inv = xp[p, 0:pr, :]
        x_inv = x_inv + _dot(x_inv, xp[p, pr:2 * pr, :])
        sol[rows, :] = _dot(x_inv, rhs[rows, :])

    def stage_affine(p):
        r0 = pl.multiple_of(p * pr, pr)
        rows = pl.ds(r0, pr)
        e = _dot(qk_s[p], sol[rows, :])
        qe = qd[rows, :] - e[:, GDN_DV:GDN_DV + GDN_DK]
        g_rows = gl[p]
        for cc in range(pr // c):
            rc = pl.ds(r0 + cc * c, c)
            mn = _dot_tn(kd[rc, :], sol[rc, :])
            idx = p * (pr // c) + cc
            mq[head, idx, 0:GDN_DK, :] = (eye * g_rows[cc:cc + 1, :] - mn[:, GDN_DV:GDN_DV + GDN_DK]).astype(mq.dtype)
            mq[head, idx, GDN_DK:GDN_DK + c, :] = qe[cc * c:(cc + 1) * c].astype(mq.dtype)
            ns[head, idx] = mn[:, 0:GDN_DV]
            o0[head, idx] = e[cc * c:(cc + 1) * c, 0:GDN_DV]

    over_pairs(stage_gates)
    for _ in range(int(math.log2(c)) - 2):
        over_pairs(stage_double)
    over_pairs(stage_solve)
    over_pairs(stage_affine)

    @pl.when(head == GDN_HEADS - 1)
    def _():
        st[...] = jnp.zeros_like(st)
        nw = nw_ref[...]

        def step(ci, carry):
            r0 = pl.multiple_of(ci * c, c)
            for h in range(GDN_HEADS):
                cols = slice(h * LANE, (h + 1) * LANE)
                ys = jnp.dot(mq[h, ci], st[h], preferred_element_type=F32)
                st[h] = (ys[0:GDN_DK] + ns[h, ci]).astype(st.dtype)
                o = ys[GDN_DK:GDN_DK + c] + o0[h, ci]
                o = o * lax.rsqrt(jnp.mean(o * o, axis=-1, keepdims=True) + RMS_EPS) * nw
                z = z_ref[pl.ds(r0, c), cols].astype(F32)
                o_ref[pl.ds(r0, c), cols] = (o * (z * _sigmoid(z))).astype(o_ref.dtype)
            return carry

        lax.fori_loop(0, seq // c, step, 0)


def _gdn(gqkvz, conv_w, gcol, grow, alog, dtb, nw, bsz, seq):
    assert GDN_DK == GDN_PAIR == GDN_DV == LANE
    h = GDN_HEADS
    npair = seq // GDN_PAIR
    nchunk = seq // GDN_CHUNK
    blk = lambda off: pl.BlockSpec((seq, LANE), lambda b, hh: (b, off + hh))
    cwb = lambda off: pl.BlockSpec((GDN_CONV, LANE), lambda b, hh: (0, off + hh))
    in_specs = [blk(0), blk(h), blk(2 * h),
                pl.BlockSpec((seq, GDN_VW), lambda b, hh: (b, 3)),
                cwb(0), cwb(h), cwb(2 * h),
                pl.BlockSpec((seq, LANE), lambda b, hh: (b, 0)),
                pl.BlockSpec((npair, 8, GDN_PAIR), lambda b, hh: (b, 0, 0)),
                pl.BlockSpec((1, 1, LANE), lambda b, hh: (hh, 0, 0)),
                pl.BlockSpec((1, 1, LANE), lambda b, hh: (hh, 0, 0)),
                pl.BlockSpec((1, LANE), lambda b, hh: (0, 0))]
    scratch = [pltpu.VMEM((seq + 8, 3 * LANE), F32),
               pltpu.VMEM((seq, LANE), F32), pltpu.VMEM((seq, LANE), F32), pltpu.VMEM((seq, LANE), F32),
               pltpu.VMEM((npair, 2 * GDN_PAIR, GDN_PAIR), F32),
               pltpu.VMEM((npair, GDN_PAIR, GDN_PAIR), BF16),
               pltpu.VMEM((seq, GDN_DV + GDN_DK), BF16),
               pltpu.VMEM((seq, GDN_DK), BF16),
               pltpu.VMEM((seq, GDN_DK), F32),
               pltpu.VMEM((npair, 8, LANE), F32),
               pltpu.VMEM((seq, GDN_DV + GDN_DK), F32),
               pltpu.VMEM((h, nchunk, GDN_DK + GDN_CHUNK, GDN_DK), BF16),
               pltpu.VMEM((h, nchunk, GDN_DK, GDN_DV), F32),
               pltpu.VMEM((h, nchunk, GDN_CHUNK, GDN_DV), F32),
               pltpu.VMEM((h, GDN_DK, GDN_DV), BF16)]
    return pl.pallas_call(
        functools.partial(_gdn_kernel, seq=seq), grid=(bsz, h), in_specs=in_specs,
        out_specs=pl.BlockSpec((seq, GDN_VW), lambda b, hh: (b, 0)),
        out_shape=jax.ShapeDtypeStruct((bsz * seq, GDN_VW), BF16),
        scratch_shapes=scratch,
        compiler_params=_cparams(("parallel", "arbitrary")), name="gdn",
    )(gqkvz, gqkvz, gqkvz, gqkvz, conv_w, conv_w, conv_w, gcol, grow, alog, dtb, nw)


def _gelu_tanh(x):
    return 0.5 * x * (1.0 + jnp.tanh(math.sqrt(2.0 / math.pi) * (x + 0.044715 * x * x * x)))


def _compress_kernel(k16_ref, v16_ref, wkc_ref, wvc_ref, w1k_ref, w1v_ref, pk_ref, pv_ref,
                     w2k_ref, w2v_ref, kc_ref, vc_ref):
    nblk = k16_ref.shape[1]
    for x_ref, wc_ref, w1_ref, p_ref, w2_ref, o_ref in (
            (k16_ref, wkc_ref, w1k_ref, pk_ref, w2k_ref, kc_ref),
            (v16_ref, wvc_ref, w1v_ref, pv_ref, w2v_ref, vc_ref)):
        y = jnp.dot(x_ref[0], wc_ref[...], preferred_element_type=F32)
        pos_term = jnp.dot(p_ref[...], w1_ref[...], preferred_element_type=F32)[0:1, :]
        for g in range(NSA_KV_HEADS):
            top = y[:, g * CMP_HIDDEN:(g + 1) * CMP_HIDDEN]
            bot = y[:, (NSA_KV_HEADS + g) * CMP_HIDDEN:(NSA_KV_HEADS + g + 1) * CMP_HIDDEN]
            hid = top + pltpu.roll(bot, nblk - 1, 0) + pos_term
            o_ref[0, g] = _dot(_gelu_tanh(hid), w2_ref[...]).astype(o_ref.dtype)


def _compress(k16, v16, wkc, wvc, w1k, w1v, pk, pv, w2k, w2v):
    bsz, nblk, width = k16.shape
    in_specs = [pl.BlockSpec((1, nblk, width), lambda b: (b, 0, 0))] * 2
    in_specs += [_full_spec(a.shape) for a in (wkc, wvc, w1k, w1v, pk, pv, w2k, w2v)]
    out_spec = pl.BlockSpec((1, NSA_KV_HEADS, nblk, HEAD_PAD), lambda b: (b, 0, 0, 0))
    out_shape = jax.ShapeDtypeStruct((bsz, NSA_KV_HEADS, nblk, HEAD_PAD), BF16)
    return pl.pallas_call(
        _compress_kernel, grid=(bsz,), in_specs=in_specs, out_specs=[out_spec, out_spec],
        out_shape=[out_shape, out_shape], compiler_params=_cparams(("parallel",)), name="nsa_compress",
    )(k16, v16, wkc, wvc, w1k, w1v, pk, pv, w2k, w2v)


def _nsa_kernel(q_ref, kv_ref, kc_ref, vc_ref, sm_ref, ovt_ref, o_ref, ksa, vst, vwa, *, seq):
    step = pl.program_id(1)
    n_cmp_pad = kc_ref.shape[2]
    n_sel_blk = seq // SEL_BLOCK
    kb_rows = SEL_KEY_BLOCK
    one_row = NSA_DH
    groups = range(NSA_KV_HEADS)
    units = [(sub, g) for sub in range(NSA_Q_SUB) for g in groups]

    @pl.when(step == 0)
    def _():
        lane = lax.broadcasted_iota(jnp.int32, (1, NSA_KV_HEADS * HEAD_PAD), 1) & (HEAD_PAD - 1)
        ones_lane = (lane == one_row).astype(vwa.dtype)
        row_t = lax.broadcasted_iota(jnp.int32, (HEAD_PAD, 1), 0)

        def fill(bi, carry):
            r0 = pl.multiple_of(bi * kb_rows, kb_rows)
            rows = pl.ds(r0, kb_rows)
            blk = jnp.right_shift(r0 + lax.broadcasted_iota(jnp.int32, (kb_rows, 1), 0), SEL_SHIFT)
            ksa[rows, :] = kv_ref[rows, 0:2 * HEAD_PAD] + (lane == NSA_DH + blk).astype(ksa.dtype)
            vwa[rows, :] = kv_ref[rows, 6 * HEAD_PAD:8 * HEAD_PAD] + ones_lane
            for g in range(NSA_KV_HEADS):
                v_t = jnp.transpose(kv_ref[rows, (2 + g) * HEAD_PAD:(3 + g) * HEAD_PAD].astype(F32))
                vst[bi, g] = jnp.where(row_t == one_row, 1.0, v_t).astype(vst.dtype)
            return carry

        lax.fori_loop(0, seq // kb_rows, fill, 0)

    tok = lax.broadcasted_iota(jnp.int32, (1, Q_BLOCK), 1)
    lane_q = lax.broadcasted_iota(jnp.int32, (1, HEAD_PAD), 1)
    pen_lanes = (lane_q >= NSA_DH) & (lane_q < NSA_DH + n_sel_blk)
    place = (lax.broadcasted_iota(jnp.int32, (n_sel_blk, HEAD_PAD), 1)
             == NSA_DH + lax.broadcasted_iota(jnp.int32, (n_sel_blk, HEAD_PAD), 0)).astype(BF16)
    n_sub = lax.broadcasted_iota(jnp.int32, (n_cmp_pad, 1), 0)

    def add_bias(s, bias):
        return jnp.concatenate([s[:, r * Q_BLOCK:(r + 1) * Q_BLOCK] + bias for r in range(s.shape[1] // Q_BLOCK)],
                               axis=1)

    q0_s, t_lane_s, gates_s, qp, o_cmp = [], [], [], {}, {}
    for sub in range(NSA_Q_SUB):
        q0 = (step * NSA_Q_SUB + sub) * Q_BLOCK
        tok_rows = slice(sub * Q_BLOCK, (sub + 1) * Q_BLOCK)
        t_lane = q0 + tok
        q0_s.append(q0)
        t_lane_s.append(t_lane)
        gates_s.append(jnp.transpose(_sigmoid(sm_ref[tok_rows, :])))

        cmp_bias = jnp.where((n_sub * CMP_STRIDE + (CMP_BLOCK - 1)) <= t_lane, 0.0, NEG_BIG)
        cmp_any = jnp.concatenate([(t_lane >= CMP_BLOCK - 1).astype(F32)] * NSA_REP, axis=1)

        for g in groups:
            qs = jnp.concatenate(
                [q_ref[tok_rows, (g * NSA_REP + r) * HEAD_PAD:(g * NSA_REP + r + 1) * HEAD_PAD]
                 for r in range(NSA_REP)], axis=0)

            s_c = add_bias(_dot_nt(kc_ref[0, g], qs), cmp_bias)
            e_c = jnp.exp2(s_c - jnp.max(s_c, axis=0, keepdims=True))
            p_c = e_c * (cmp_any / jnp.sum(e_c, axis=0, keepdims=True))
            o_cmp[sub, g] = _dot_tn(vc_ref[0, g], p_c)[0:NSA_DH]

            p_sum = p_c[:, 0:Q_BLOCK]
            for r in range(1, NSA_REP):
                p_sum = p_sum + p_c[:, r * Q_BLOCK:(r + 1) * Q_BLOCK]
            imp_t = _dot(ovt_ref[...], p_sum)

            def rank_select(imp_t=imp_t, t_lane=t_lane):
                j_sub = lax.broadcasted_iota(jnp.int32, (n_sel_blk, 1), 0)
                cur = jnp.right_shift(t_lane, SEL_SHIFT)
                forced = (j_sub == 0) | (j_sub == cur) | (j_sub == cur - 1)
                causal_blk = (j_sub * SEL_BLOCK) <= t_lane
                imp = jnp.where(forced, FORCE_SCORE, jnp.where(causal_blk, imp_t, -jnp.inf))
                rank = jnp.zeros((n_sel_blk, Q_BLOCK), F32)
                for jp in range(n_sel_blk):
                    row = imp[jp:jp + 1, :]
                    ahead = (row > imp) | ((row == imp) & (j_sub > jp))
                    rank = rank + ahead.astype(F32)
                return (rank < SEL_TOPK).astype(BF16)

            sel_t = lax.cond(q0 + Q_BLOCK <= SEL_TOPK * SEL_BLOCK,
                             lambda: jnp.ones((n_sel_blk, Q_BLOCK), BF16), rank_select)
            sel_lanes = _dot_tn(sel_t, place)
            pen = jnp.where(pen_lanes, (sel_lanes - 1.0) * (-NEG_BIG), 0.0).astype(BF16)
            qp[sub, g] = qs + jnp.concatenate([pen] * NSA_REP, axis=0)

    n_full = step

    def scores(kb, unit):
        k0 = pl.multiple_of(kb * kb_rows, kb_rows)
        g = unit[1]
        return _dot_nt(ksa[pl.ds(k0, kb_rows), g * HEAD_PAD:(g + 1) * HEAD_PAD], qp[unit])

    def absorb(state, s, kb, unit):
        m_i, acc = state
        m_new = jnp.maximum(m_i, jnp.max(s, axis=0, keepdims=True))
        return m_new, jnp.exp2(m_i - m_new) * acc + _dot(vst[kb, unit[1]], jnp.exp2(s - m_new))

    kpos_last = n_full * kb_rows + lax.broadcasted_iota(jnp.int32, (kb_rows, 1), 0)
    first = []
    for unit in units:
        s = add_bias(scores(n_full, unit), jnp.where(kpos_last <= t_lane_s[unit[0]], 0.0, NEG_BIG))
        m = jnp.max(s, axis=0, keepdims=True)
        first.append((m, _dot(vst[n_full, unit[1]], jnp.exp2(s - m))))

    def full_blocks(state):
        def loop_body(kb, carry):
            st, s_cur = carry
            s_next = tuple(scores(kb + 1, u) for u in units)
            return tuple(absorb(st[i], s_cur[i], kb, u) for i, u in enumerate(units)), s_next

        st, s_cur = lax.fori_loop(0, n_full - 1, loop_body, (state, tuple(scores(0, u) for u in units)))
        return tuple(absorb(st[i], s_cur[i], n_full - 1, u) for i, u in enumerate(units))

    sel_out = lax.cond(n_full > 0, full_blocks, lambda st: st, tuple(first))

    span = WINDOW + Q_BLOCK
    for i, (sub, g) in enumerate(units):
        q0, t_lane = q0_s[sub], t_lane_s[sub]
        tok_rows = slice(sub * Q_BLOCK, (sub + 1) * Q_BLOCK)
        w0 = pl.multiple_of(jnp.maximum(q0 - WINDOW, 0), Q_BLOCK)
        kpos_w = w0 + lax.broadcasted_iota(jnp.int32, (span, 1), 0)
        win_bias = jnp.where((kpos_w <= t_lane) & (kpos_w > t_lane - WINDOW), 0.0, NEG_BIG)

        acc_s = sel_out[i][1]
        o_slc = acc_s[0:NSA_DH] * (1.0 / acc_s[one_row:one_row + 1, :])

        k_w = kv_ref[pl.ds(w0, span), (4 + g) * HEAD_PAD:(5 + g) * HEAD_PAD]
        s_w = add_bias(_dot_nt(k_w, qp[sub, g]), win_bias)
        acc_w = _dot_tn(vwa[pl.ds(w0, span), g * HEAD_PAD:(g + 1) * HEAD_PAD],
                        jnp.exp2(s_w - jnp.max(s_w, axis=0, keepdims=True)))
        o_win = acc_w[0:NSA_DH] * (1.0 / acc_w[one_row:one_row + 1, :])

        gates_t = gates_s[sub]
        o_heads = []
        for r in range(NSA_REP):
            c0 = 8 + (g * NSA_REP + r) * 3
            sl = slice(r * Q_BLOCK, (r + 1) * Q_BLOCK)
            o_t = (gates_t[c0:c0 + 1, :] * o_cmp[sub, g][:, sl] + gates_t[c0 + 1:c0 + 2, :] * o_slc[:, sl]
                   + gates_t[c0 + 2:c0 + 3, :] * o_win[:, sl])
            o_heads.append(o_t)
        for pair in range(NSA_REP // 2):
            lanes = slice((g * NSA_REP + 2 * pair) * NSA_DH, (g * NSA_REP + 2 * pair + 2) * NSA_DH)
            o_ref[tok_rows, lanes] = jnp.transpose(jnp.concatenate(o_heads[2 * pair:2 * pair + 2], axis=0)
                                                   ).astype(o_ref.dtype)


def _nsa(nq, nkv, kc, vc, small, overlap, bsz, seq):
    assert NSA_Q_SUB * Q_BLOCK == SEL_KEY_BLOCK
    rows = NSA_Q_SUB * Q_BLOCK
    nstep = seq // rows
    in_specs = [pl.BlockSpec((rows, NSA_HEADS * HEAD_PAD), lambda b, i: (b * nstep + i, 0)),
                pl.BlockSpec((seq, nkv.shape[1]), lambda b, i: (b, 0)),
                pl.BlockSpec((1,) + kc.shape[1:], lambda b, i: (b, 0, 0, 0)),
                pl.BlockSpec((1,) + vc.shape[1:], lambda b, i: (b, 0, 0, 0)),
                pl.BlockSpec((rows, LANE), lambda b, i: (b * nstep + i, 0)),
                _full_spec(overlap.shape)]
    return pl.pallas_call(
        functools.partial(_nsa_kernel, seq=seq), grid=(bsz, nstep), in_specs=in_specs,
        out_specs=pl.BlockSpec((rows, NSA_QW), lambda b, i: (b * nstep + i, 0)),
        out_shape=jax.ShapeDtypeStruct((bsz * seq, NSA_QW), BF16),
        scratch_shapes=[pltpu.VMEM((seq, NSA_KV_HEADS * HEAD_PAD), BF16),
                        pltpu.VMEM((seq // SEL_KEY_BLOCK, NSA_KV_HEADS, HEAD_PAD, SEL_KEY_BLOCK), BF16),
                        pltpu.VMEM((seq, NSA_KV_HEADS * HEAD_PAD), BF16)],
        compiler_params=_cparams(("parallel", "arbitrary")), name="nsa_attention",
    )(nq, nkv, kc, vc, small, overlap)


def _merge_kernel(oa_ref, ob_ref, mix_ref, x_ref, wa_ref, wb_ref, wo_ref, g_ref, b_ref, h_ref):
    ya = jnp.dot(oa_ref[...], wa_ref[...], preferred_element_type=F32)
    yb = jnp.dot(ob_ref[...], wb_ref[...], preferred_element_type=F32)
    mix = mix_ref[...].astype(F32)
    mixed = _sigmoid(mix[:, 0:D_MODEL]) * ya + _sigmoid(mix[:, D_MODEL:2 * D_MODEL]) * yb
    y = _dot(mixed, wo_ref[...])
    h_ref[...] = _layer_norm(DN_ALPHA * x_ref[...] + y, g_ref[...], b_ref[...])


def _merge(oa, ob, mix, x2d, wa, wb, wo, g, b):
    m = x2d.shape[0]
    tm = ROW_TILE
    row = lambda n: pl.BlockSpec((tm, n), lambda i: (i, 0))
    in_specs = [row(oa.shape[1]), row(ob.shape[1]), row(mix.shape[1]), row(D_MODEL),
                _full_spec(wa.shape), _full_spec(wb.shape), _full_spec(wo.shape),
                _full_spec(g.shape), _full_spec(b.shape)]
    return pl.pallas_call(
        _merge_kernel, grid=(m // tm,), in_specs=in_specs, out_specs=row(D_MODEL),
        out_shape=jax.ShapeDtypeStruct((m, D_MODEL), F32),
        compiler_params=_cparams(("parallel",)), name="merge_ln",
    )(oa, ob, mix, x2d, wa, wb, wo, g, b)


def _ffn_kernel(h_ref, halo_ref, wup_ref, cw_ref, wd_ref, g_ref, b_ref, o_ref,
                hb, ug, uv, act, *, seq, tm, halo):
    i = pl.program_id(0)
    first = lax.rem(i * tm, seq) == 0
    hb[0:halo, :] = jnp.where(first, 0.0, halo_ref[...]).astype(BF16)
    hb[halo:, :] = h_ref[...].astype(BF16)

    def conv(u_ref, slot, cols):
        out = None
        for tap in range(FFN_CONV):
            term = cw_ref[tap:tap + 1, cols] * u_ref[slot, pl.ds(halo - (FFN_CONV - 1) + tap, tm), :]
            out = term if out is None else out + term
        return out

    for j in range(D_FF // FFN_CHUNK):
        slot = j % 2
        cols_g = slice(j * FFN_CHUNK, (j + 1) * FFN_CHUNK)
        cols_v = slice(D_FF + j * FFN_CHUNK, D_FF + (j + 1) * FFN_CHUNK)
        ug[slot] = jnp.dot(hb[...], wup_ref[:, cols_g], preferred_element_type=F32)
        uv[slot] = jnp.dot(hb[...], wup_ref[:, cols_v], preferred_element_type=F32)
        a = conv(ug, slot, cols_g)
        act[:, cols_g] = (a * _sigmoid(a) * conv(uv, slot, cols_v)).astype(act.dtype)

    f = jnp.dot(act[...], wd_ref[...], preferred_element_type=F32)
    o_ref[...] = _layer_norm(DN_ALPHA * h_ref[...] + f, g_ref[...], b_ref[...])


def _ffn(h, w_up, conv_w, w_down, g, b, seq):
    m = h.shape[0]
    tm = ROW_TILE
    halo = 16
    resident = lambda shape: pl.BlockSpec(shape, lambda i: (0, 0), pipeline_mode=pl.Buffered(1))
    in_specs = [pl.BlockSpec((tm, D_MODEL), lambda i: (i, 0)),
                pl.BlockSpec((halo, D_MODEL), lambda i: (jnp.maximum(i * (tm // halo) - 1, 0), 0)),
                resident(w_up.shape), resident(conv_w.shape), resident(w_down.shape),
                resident(g.shape), resident(b.shape)]
    return pl.pallas_call(
        functools.partial(_ffn_kernel, seq=seq, tm=tm, halo=halo), grid=(m // tm,),
        in_specs=in_specs, out_specs=pl.BlockSpec((tm, D_MODEL), lambda i: (i, 0)),
        out_shape=jax.ShapeDtypeStruct((m, D_MODEL), F32),
        scratch_shapes=[pltpu.VMEM((tm + halo, D_MODEL), BF16),
                        pltpu.VMEM((2, tm + halo, FFN_CHUNK), F32), pltpu.VMEM((2, tm + halo, FFN_CHUNK), F32),
                        pltpu.VMEM((tm, D_FF), BF16)],
        compiler_params=_cparams(("parallel",)), name="conv_ffn_ln",
    )(h, h, w_up, conv_w, w_down, g, b)


def _compress_weights(w1):
    half = CMP_BLOCK // 2
    w1r = w1.reshape(2, half, NSA_DH, CMP_HIDDEN)
    eye = jnp.eye(NSA_KV_HEADS, dtype=w1.dtype)
    ex = jnp.einsum("aidh,gk->agikdh", w1r, eye)
    ex = ex.reshape(2 * NSA_KV_HEADS, half * NSA_KVW, CMP_HIDDEN)
    return jnp.concatenate([ex[n] for n in range(2 * NSA_KV_HEADS)], axis=1)


def _overlap_matrix(seq):
    n_cmp = (seq - CMP_BLOCK) // CMP_STRIDE + 1
    n_sel = seq // SEL_BLOCK
    starts = np.arange(n_cmp) * CMP_STRIDE
    jb = np.arange(n_sel) * SEL_BLOCK
    ov = ((starts[:, None] < jb[None] + SEL_BLOCK) & (starts[:, None] + CMP_BLOCK > jb[None])).astype(np.float32)
    out = np.zeros((n_sel, seq // CMP_STRIDE), np.float32)
    out[:, :n_cmp] = ov.T
    return jnp.asarray(out, BF16)


def kernel(x, w_in, gdn_conv_w, gdn_a_log, gdn_dt_bias, gdn_norm_w, cmp_pos_k, cmp_w1_k, cmp_w2_k,
           cmp_pos_v, cmp_w1_v, cmp_w2_v, w_branch_gdn, w_branch_nsa, w_out, ln1_g, ln1_b, w_up,
           ffn_conv_w, w_down, ln2_g, ln2_b):
    bsz, seq, _ = x.shape
    m = bsz * seq
    for i in range(DEPTH):
        x2d = x.reshape(m, D_MODEL)
        w = w_in[i]
        o_gdn = 2 * GDN_QK + 2 * GDN_VW
        o_nq = o_gdn + 2 * GDN_HEADS
        o_kv = o_nq + NSA_QW
        o_gate = o_kv + 6 * NSA_KVW
        o_mix = o_gate + 3 * NSA_HEADS
        kv = [w[:, o_kv + n * NSA_KVW:o_kv + (n + 1) * NSA_KVW] for n in range(6)]
        small_w = jnp.concatenate([w[:, o_gdn:o_nq], w[:, o_gate:o_mix]], axis=1)
        small_w = jnp.pad(small_w, ((0, 0), (0, LANE - small_w.shape[1])))
        weights = [w[:, :o_gdn],
                   w[:, o_nq:o_kv] * (NSA_DH ** -0.5 * LOG2E),
                   w[:, o_kv + 2 * NSA_KVW:o_gate],
                   kv[0], kv[1],
                   w[:, o_mix:],
                   small_w]
        weights = [t.astype(BF16) for t in weights]
        widths = [o_gdn, NSA_HEADS * HEAD_PAD, 4 * NSA_KV_HEADS * HEAD_PAD, NSA_KVW, NSA_KVW, 2 * D_MODEL, LANE]
        gqkvz, nq, nkv, kcp, vcp, mix, small, grow = _inproj(x2d, weights, [BF16] * 6 + [F32], widths,
                                                             2 * GDN_HEADS)

        alog = jnp.broadcast_to(gdn_a_log[i][:, None, None], (GDN_HEADS, 1, LANE))
        dtb = jnp.broadcast_to(gdn_dt_bias[i][:, None, None], (GDN_HEADS, 1, LANE))
        o_a = _gdn(gqkvz, gdn_conv_w[i], small, grow, alog, dtb, gdn_norm_w[i][None, :], bsz, seq)

        grp = seq // CMP_STRIDE
        k16 = kcp.reshape(bsz, grp, CMP_STRIDE * NSA_KVW)
        v16 = vcp.reshape(bsz, grp, CMP_STRIDE * NSA_KVW)
        pos_rows = lambda p: jnp.pad(p.reshape(1, CMP_BLOCK * NSA_DH), ((0, 7), (0, 0))).astype(BF16)
        w2_pad = lambda t: jnp.pad(t, ((0, 0), (0, HEAD_PAD - NSA_DH))).astype(BF16)
        kc, vc = _compress(k16, v16,
                           _compress_weights(cmp_w1_k[i]).astype(BF16), _compress_weights(cmp_w1_v[i]).astype(BF16),
                           cmp_w1_k[i].astype(BF16), cmp_w1_v[i].astype(BF16),
                           pos_rows(cmp_pos_k[i]), pos_rows(cmp_pos_v[i]),
                           w2_pad(cmp_w2_k[i]), w2_pad(cmp_w2_v[i]))
        o_b = _nsa(nq, nkv, kc, vc, small, _overlap_matrix(seq), bsz, seq)

        h = _merge(o_a, o_b, mix, x2d, w_branch_gdn[i].astype(BF16), w_branch_nsa[i].astype(BF16), w_out[i].astype(BF16),
                   ln1_g[i][None, :], ln1_b[i][None, :])

        x = _ffn(h, w_up[i].astype(BF16), ffn_conv_w[i], w_down[i].astype(BF16),
                 ln2_g[i][None, :], ln2_b[i][None, :], seq).reshape(bsz, seq, D_MODEL)
    return x
```

```python
import functools
import math

import numpy as np
import jax
import jax.numpy as jnp
from jax import lax
from jax.experimental import pallas as pl
from jax.experimental.pallas import tpu as pltpu

F32 = jnp.float32
BF16 = jnp.bfloat16

D_MODEL = 1024
GDN_HEADS = 4
GDN_DK = 128
GDN_DV = 128
GDN_CONV = 4
GDN_CHUNK = 64
NSA_HEADS = 8
NSA_KV_HEADS = 2
NSA_REP = NSA_HEADS // NSA_KV_HEADS
NSA_DH = 64
CMP_BLOCK = 32
CMP_STRIDE = 16
CMP_HIDDEN = 256
SEL_BLOCK = 64
SEL_SHIFT = 6
SEL_TOPK = 16
WINDOW = 512
FORCE_SCORE = 1e9
D_FF = 2816
FFN_CONV = 3
DEPTH = 1
DN_ALPHA = (2 * DEPTH) ** 0.25
LN_EPS = 1e-5
RMS_EPS = 1e-6
LOG2E = math.log2(math.e)

GDN_QK = GDN_HEADS * GDN_DK
GDN_VW = GDN_HEADS * GDN_DV
NSA_QW = NSA_HEADS * NSA_DH
NSA_KVW = NSA_KV_HEADS * NSA_DH

LANE = 128
HEAD_PAD = LANE
NEG_BIG = -1e30
VMEM_LIMIT = 56 * 1024 * 1024

Q_BLOCK = 128
NSA_Q_SUB = 2
SEL_KEY_BLOCK = 256
FFN_CHUNK = 256
ROW_TILE = 512
GDN_CHUNK_SHIFT = 6
GDN_PAIR = 2 * GDN_CHUNK
GDN_UNROLL = 16


def _cparams(sem):
    return pltpu.CompilerParams(dimension_semantics=sem, vmem_limit_bytes=VMEM_LIMIT)


def _dot(a, b):
    return jnp.dot(a.astype(BF16), b.astype(BF16), preferred_element_type=F32)


def _dot_nt(a, b):
    return lax.dot_general(a.astype(BF16), b.astype(BF16), (((1,), (1,)), ((), ())),
                           preferred_element_type=F32)


def _dot_tn(a, b):
    return lax.dot_general(a.astype(BF16), b.astype(BF16), (((0,), (0,)), ((), ())),
                           preferred_element_type=F32)


def _sigmoid(x):
    return jax.nn.sigmoid(x)


def _softplus(x):
    return jnp.maximum(x, 0.0) + jnp.log1p(jnp.exp(-jnp.abs(x)))


def _layer_norm(x, g, b):
    mu = jnp.mean(x, axis=-1, keepdims=True)
    xc = x - mu
    var = jnp.mean(xc * xc, axis=-1, keepdims=True)
    return xc * lax.rsqrt(var + LN_EPS) * g + b


def _full_spec(shape):
    nd = len(shape)
    return pl.BlockSpec(shape, lambda *_: (0,) * nd)


def _inproj_kernel(x_ref, *refs, t_rows):
    n = (len(refs) - 1) // 2
    xb = x_ref[...].astype(BF16)
    for w_ref, o_ref in zip(refs[:n], refs[n:2 * n]):
        res = jnp.dot(xb, w_ref[...], preferred_element_type=F32)
        if o_ref.shape[1] == w_ref.shape[1]:
            o_ref[...] = res.astype(o_ref.dtype)
        else:
            out = res.astype(o_ref.dtype)
            zeros = jnp.zeros((out.shape[0], HEAD_PAD - NSA_DH), o_ref.dtype)
            for hd in range(w_ref.shape[1] // NSA_DH):
                o_ref[:, hd * HEAD_PAD:hd * HEAD_PAD + NSA_DH] = out[:, hd * NSA_DH:(hd + 1) * NSA_DH]
                o_ref[:, hd * HEAD_PAD + NSA_DH:(hd + 1) * HEAD_PAD] = zeros
    t_ref = refs[2 * n]
    for j in range(t_ref.shape[0]):
        t_ref[j] = jnp.transpose(res[j * LANE:(j + 1) * LANE, :])[0:t_rows, :]


def _inproj(x2d, weights, out_dtypes, out_widths, t_rows):
    assert weights[-1].shape[1] == LANE and out_dtypes[-1] == F32
    m = x2d.shape[0]
    tm = ROW_TILE
    in_specs = [pl.BlockSpec((tm, D_MODEL), lambda i: (i, 0))]
    in_specs += [_full_spec(w.shape) for w in weights]
    out_specs = [pl.BlockSpec((tm, n), lambda i: (i, 0)) for n in out_widths]
    out_specs.append(pl.BlockSpec((tm // LANE, t_rows, LANE), lambda i: (i, 0, 0)))
    out_shape = [jax.ShapeDtypeStruct((m, n), dt) for n, dt in zip(out_widths, out_dtypes)]
    out_shape.append(jax.ShapeDtypeStruct((m // LANE, t_rows, LANE), F32))
    return pl.pallas_call(
        functools.partial(_inproj_kernel, t_rows=t_rows), grid=(m // tm,), in_specs=in_specs, out_specs=out_specs,
        out_shape=out_shape, compiler_params=_cparams(("parallel",)), name="inproj")(x2d, *weights)


def _gdn_kernel(q_ref, k_ref, v_ref, z_ref, cwq_ref, cwk_ref, cwv_ref, gcol_ref, grow_ref,
                alog_ref, dtb_ref, nw_ref, o_ref,
                xpad, qh, kh, vh, xp, qk_s, rhs, kd, qd, gl, sol, mq, ns, o0, st, *, seq):
    c = GDN_CHUNK
    pr = GDN_PAIR
    npair = seq // pr
    head = pl.program_id(1)
    pad = 8
    conv_rows = 256
    xpad[0:pad, :] = jnp.zeros((pad, 3 * LANE), F32)
    xpad[pad:, 0:LANE] = q_ref[...].astype(F32)
    xpad[pad:, LANE:2 * LANE] = k_ref[...].astype(F32)
    xpad[pad:, 2 * LANE:3 * LANE] = v_ref[...].astype(F32)
    cw = jnp.concatenate([cwq_ref[...], cwk_ref[...], cwv_ref[...]], axis=1)

    for t0 in range(0, seq, conv_rows):
        acc = jnp.zeros((conv_rows, 3 * LANE), F32)
        for j in range(GDN_CONV):
            xs = xpad[t0 + pad - (GDN_CONV - 1) + j:t0 + pad - (GDN_CONV - 1) + j + conv_rows, :]
            acc = acc + xs * cw[j:j + 1, :]
        y = acc * _sigmoid(acc)
        yq, yk, yv = y[:, 0:LANE], y[:, LANE:2 * LANE], y[:, 2 * LANE:3 * LANE]
        yq = yq * lax.rsqrt(jnp.sum(yq * yq, axis=-1, keepdims=True) + RMS_EPS) * (GDN_DK ** -0.5)
        yk = yk * lax.rsqrt(jnp.sum(yk * yk, axis=-1, keepdims=True) + RMS_EPS)
        qh[t0:t0 + conv_rows, :] = yq
        kh[t0:t0 + conv_rows, :] = yk
        vh[t0:t0 + conv_rows, :] = yv

    neg_a = -jnp.exp(alog_ref[0, :, 0:1])
    dtb = dtb_ref[0, :, 0:1]
    ii = lax.broadcasted_iota(jnp.int32, (pr, pr), 0)
    jj = lax.broadcasted_iota(jnp.int32, (pr, pr), 1)
    same = jnp.right_shift(ii, GDN_CHUNK_SHIFT) == jnp.right_shift(jj, GDN_CHUNK_SHIFT)
    incl = same & (jj <= ii)
    strict = same & (jj < ii)
    upper = same & (ii <= jj)
    eye = (ii == jj).astype(F32)
    sub8 = lax.broadcasted_iota(jnp.int32, (8, LANE), 0)
    lane_id = lax.broadcasted_iota(jnp.int32, (1, LANE), 1)

    def over_pairs(body):
        def run(gi, carry):
            for uu in range(GDN_UNROLL):
                body(gi * GDN_UNROLL + uu)
            return carry
        lax.fori_loop(0, npair // GDN_UNROLL, run, 0)

    def stage_gates(p):
        rows = pl.ds(pl.multiple_of(p * pr, pr), pr)
        q = qh[rows, :]
        k = kh[rows, :]
        v = vh[rows, :]
        gcol = gcol_ref[rows, :]
        b_logit = jnp.sum(jnp.where(lane_id == head, gcol, 0.0), axis=1, keepdims=True)
        a_logit = jnp.sum(jnp.where(lane_id == GDN_HEADS + head, gcol, 0.0), axis=1, keepdims=True)
        a_row = jnp.sum(jnp.where(sub8[:, 0:1] == GDN_HEADS + head, grow_ref[p], 0.0), axis=0, keepdims=True)
        beta = _sigmoid(b_logit)
        g_c = neg_a * _softplus(a_logit + dtb)
        g_r = neg_a * _softplus(a_row + dtb)
        gc_col = jnp.sum(jnp.where(incl, g_r, 0.0), axis=1, keepdims=True)
        gc_row = jnp.sum(jnp.where(upper, g_c, 0.0), axis=0, keepdims=True)
        gt_col = jnp.sum(jnp.where(same, g_r, 0.0), axis=1, keepdims=True)
        decay = jnp.where(incl, jnp.exp(jnp.where(incl, gc_col - gc_row, 0.0)), 0.0)
        kq_k = _dot_nt(jnp.concatenate([k, q], axis=0), k)
        a_mat = jnp.where(strict, kq_k[0:pr] * decay * beta, 0.0)
        xp[p, 0:pr, :] = eye - a_mat
        xp[p, pr:2 * pr, :] = _dot(a_mat, a_mat)
        qk_s[p] = (kq_k[pr:2 * pr] * decay).astype(qk_s.dtype)
        e_gc = jnp.exp(gc_col)
        rhs[rows, 0:GDN_DV] = (v * beta).astype(rhs.dtype)
        rhs[rows, GDN_DV:GDN_DV + GDN_DK] = (k * (beta * e_gc)).astype(rhs.dtype)
        kd[rows, :] = (k * jnp.exp(gt_col - gc_col)).astype(kd.dtype)
        qd[rows, :] = q * e_gc
        g_last = jnp.exp(gt_col)
        gl[p] = jnp.where(sub8 == 0, g_last[0:1, :], g_last[c:c + 1, :])

    def stage_double(p):
        y = _dot(xp[p], xp[p, pr:2 * pr, :])
        xp[p, 0:pr, :] = xp[p, 0:pr, :] + y[0:pr]
        xp[p, pr:2 * pr, :] = y[pr:2 * pr]

    def stage_solve(p):
        rows = pl.ds(pl.multiple_of(p * pr, pr), pr)
        x_inv = xp[p, 0:pr, :]
        x_inv = x_inv + _dot(x_inv, xp[p, pr:2 * pr, :])
        sol[rows, :] = _dot(x_inv, rhs[rows, :])

    def stage_affine(p):
        r0 = pl.multiple_of(p * pr, pr)
        rows = pl.ds(r0, pr)
        e = _dot(qk_s[p], sol[rows, :])
        qe = qd[rows, :] - e[:, GDN_DV:GDN_DV + GDN_DK]
        g_rows = gl[p]
        for cc in range(pr // c):
            rc = pl.ds(r0 + cc * c, c)
            mn = _dot_tn(kd[rc, :], sol[rc, :])
            idx = p * (pr // c) + cc
            mq[head, idx, 0:GDN_DK, :] = (eye * g_rows[cc:cc + 1, :] - mn[:, GDN_DV:GDN_DV + GDN_DK]).astype(mq.dtype)
            mq[head, idx, GDN_DK:GDN_DK + c, :] = qe[cc * c:(cc + 1) * c].astype(mq.dtype)
            ns[head, idx] = mn[:, 0:GDN_DV]
            o0[head, idx] = e[cc * c:(cc + 1) * c, 0:GDN_DV]

    over_pairs(stage_gates)
    for _ in range(int(math.log2(c)) - 2):
        over_pairs(stage_double)
    over_pairs(stage_solve)
    over_pairs(stage_affine)

    @pl.when(head == GDN_HEADS - 1)
    def _():
        st[...] = jnp.zeros_like(st)
        nw = nw_ref[...]

        def step(ci, carry):
            r0 = pl.multiple_of(ci * c, c)
            for h in range(GDN_HEADS):
                cols = slice(h * LANE, (h + 1) * LANE)
                ys = jnp.dot(mq[h, ci], st[h], preferred_element_type=F32)
                st[h] = (ys[0:GDN_DK] + ns[h, ci]).astype(st.dtype)
                o = ys[GDN_DK:GDN_DK + c] + o0[h, ci]
                o = o * lax.rsqrt(jnp.mean(o * o, axis=-1, keepdims=True) + RMS_EPS) * nw
                z = z_ref[pl.ds(r0, c), cols].astype(F32)
                o_ref[pl.ds(r0, c), cols] = (o * (z * _sigmoid(z))).astype(o_ref.dtype)
            return carry

        lax.fori_loop(0, seq // c, step, 0)


def _gdn(gqkvz, conv_w, gcol, grow, alog, dtb, nw, bsz, seq):
    assert GDN_DK == GDN_PAIR == GDN_DV == LANE
    h = GDN_HEADS
    npair = seq // GDN_PAIR
    nchunk = seq // GDN_CHUNK
    blk = lambda off: pl.BlockSpec((seq, LANE), lambda b, hh: (b, off + hh))
    cwb = lambda off: pl.BlockSpec((GDN_CONV, LANE), lambda b, hh: (0, off + hh))
    in_specs = [blk(0), blk(h), blk(2 * h),
                pl.BlockSpec((seq, GDN_VW), lambda b, hh: (b, 3)),
                cwb(0), cwb(h), cwb(2 * h),
                pl.BlockSpec((seq, LANE), lambda b, hh: (b, 0)),
                pl.BlockSpec((npair, 8, GDN_PAIR), lambda b, hh: (b, 0, 0)),
                pl.BlockSpec((1, 1, LANE), lambda b, hh: (hh, 0, 0)),
                pl.BlockSpec((1, 1, LANE), lambda b, hh: (hh, 0, 0)),
                pl.BlockSpec((1, LANE), lambda b, hh: (0, 0))]
    scratch = [pltpu.VMEM((seq + 8, 3 * LANE), F32),
               pltpu.VMEM((seq, LANE), F32), pltpu.VMEM((seq, LANE), F32), pltpu.VMEM((seq, LANE), F32),
               pltpu.VMEM((npair, 2 * GDN_PAIR, GDN_PAIR), F32),
               pltpu.VMEM((npair, GDN_PAIR, GDN_PAIR), BF16),
               pltpu.VMEM((seq, GDN_DV + GDN_DK), BF16),
               pltpu.VMEM((seq, GDN_DK), BF16),
               pltpu.VMEM((seq, GDN_DK), F32),
               pltpu.VMEM((npair, 8, LANE), F32),
               pltpu.VMEM((seq, GDN_DV + GDN_DK), F32),
               pltpu.VMEM((h, nchunk, GDN_DK + GDN_CHUNK, GDN_DK), BF16),
               pltpu.VMEM((h, nchunk, GDN_DK, GDN_DV), F32),
               pltpu.VMEM((h, nchunk, GDN_CHUNK, GDN_DV), F32),
               pltpu.VMEM((h, GDN_DK, GDN_DV), BF16)]
    return pl.pallas_call(
        functools.partial(_gdn_kernel, seq=seq), grid=(bsz, h), in_specs=in_specs,
        out_specs=pl.BlockSpec((seq, GDN_VW), lambda b, hh: (b, 0)),
        out_shape=jax.ShapeDtypeStruct((bsz * seq, GDN_VW), BF16),
        scratch_shapes=scratch,
        compiler_params=_cparams(("parallel", "arbitrary")), name="gdn",
    )(gqkvz, gqkvz, gqkvz, gqkvz, conv_w, conv_w, conv_w, gcol, grow, alog, dtb, nw)


def _gelu_tanh(x):
    return 0.5 * x * (1.0 + jnp.tanh(math.sqrt(2.0 / math.pi) * (x + 0.044715 * x * x * x)))


def _compress_kernel(k16_ref, v16_ref, wkc_ref, wvc_ref, w1k_ref, w1v_ref, pk_ref, pv_ref,
                     w2k_ref, w2v_ref, kc_ref, vc_ref):
    nblk = k16_ref.shape[1]
    for x_ref, wc_ref, w1_ref, p_ref, w2_ref, o_ref in (
            (k16_ref, wkc_ref, w1k_ref, pk_ref, w2k_ref, kc_ref),
            (v16_ref, wvc_ref, w1v_ref, pv_ref, w2v_ref, vc_ref)):
        y = jnp.dot(x_ref[0], wc_ref[...], preferred_element_type=F32)
        pos_term = jnp.dot(p_ref[...], w1_ref[...], preferred_element_type=F32)[0:1, :]
        for g in range(NSA_KV_HEADS):
            top = y[:, g * CMP_HIDDEN:(g + 1) * CMP_HIDDEN]
            bot = y[:, (NSA_KV_HEADS + g) * CMP_HIDDEN:(NSA_KV_HEADS + g + 1) * CMP_HIDDEN]
            hid = top + pltpu.roll(bot, nblk - 1, 0) + pos_term
            o_ref[0, g] = _dot(_gelu_tanh(hid), w2_ref[...]).astype(o_ref.dtype)


def _compress(k16, v16, wkc, wvc, w1k, w1v, pk, pv, w2k, w2v):
    bsz, nblk, width = k16.shape
    in_specs = [pl.BlockSpec((1, nblk, width), lambda b: (b, 0, 0))] * 2
    in_specs += [_full_spec(a.shape) for a in (wkc, wvc, w1k, w1v, pk, pv, w2k, w2v)]
    out_spec = pl.BlockSpec((1, NSA_KV_HEADS, nblk, HEAD_PAD), lambda b: (b, 0, 0, 0))
    out_shape = jax.ShapeDtypeStruct((bsz, NSA_KV_HEADS, nblk, HEAD_PAD), BF16)
    return pl.pallas_call(
        _compress_kernel, grid=(bsz,), in_specs=in_specs, out_specs=[out_spec, out_spec],
        out_shape=[out_shape, out_shape], compiler_params=_cparams(("parallel",)), name="nsa_compress",
    )(k16, v16, wkc, wvc, w1k, w1v, pk, pv, w2k, w2v)


def _nsa_kernel(q_ref, kv_ref, kc_ref, vc_ref, sm_ref, ovt_ref, o_ref, ksa, vst, vwa, *, seq):
    step = pl.program_id(1)
    n_cmp_pad = kc_ref.shape[2]
    n_sel_blk = seq // SEL_BLOCK
    kb_rows = SEL_KEY_BLOCK
    one_row = NSA_DH
    groups = range(NSA_KV_HEADS)
    units = [(sub, g) for sub in range(NSA_Q_SUB) for g in groups]

    @pl.when(step == 0)
    def _():
        lane = lax.broadcasted_iota(jnp.int32, (1, NSA_KV_HEADS * HEAD_PAD), 1) & (HEAD_PAD - 1)
        ones_lane = (lane == one_row).astype(vwa.dtype)
        row_t = lax.broadcasted_iota(jnp.int32, (HEAD_PAD, 1), 0)

        def fill(bi, carry):
            r0 = pl.multiple_of(bi * kb_rows, kb_rows)
            rows = pl.ds(r0, kb_rows)
            blk = jnp.right_shift(r0 + lax.broadcasted_iota(jnp.int32, (kb_rows, 1), 0), SEL_SHIFT)
            ksa[rows, :] = kv_ref[rows, 0:2 * HEAD_PAD] + (lane == NSA_DH + blk).astype(ksa.dtype)
            vwa[rows, :] = kv_ref[rows, 6 * HEAD_PAD:8 * HEAD_PAD] + ones_lane
            for g in range(NSA_KV_HEADS):
                v_t = jnp.transpose(kv_ref[rows, (2 + g) * HEAD_PAD:(3 + g) * HEAD_PAD].astype(F32))
                vst[bi, g] = jnp.where(row_t == one_row, 1.0, v_t).astype(vst.dtype)
            return carry

        lax.fori_loop(0, seq // kb_rows, fill, 0)

    tok = lax.broadcasted_iota(jnp.int32, (1, Q_BLOCK), 1)
    lane_q = lax.broadcasted_iota(jnp.int32, (1, HEAD_PAD), 1)
    pen_lanes = (lane_q >= NSA_DH) & (lane_q < NSA_DH + n_sel_blk)
    place = (lax.broadcasted_iota(jnp.int32, (n_sel_blk, HEAD_PAD), 1)
             == NSA_DH + lax.broadcasted_iota(jnp.int32, (n_sel_blk, HEAD_PAD), 0)).astype(BF16)
    n_sub = lax.broadcasted_iota(jnp.int32, (n_cmp_pad, 1), 0)

    def add_bias(s, bias):
        return jnp.concatenate([s[:, r * Q_BLOCK:(r + 1) * Q_BLOCK] + bias for r in range(s.shape[1] // Q_BLOCK)],
                               axis=1)

    q0_s, t_lane_s, gates_s, qp, o_cmp = [], [], [], {}, {}
    for sub in range(NSA_Q_SUB):
        q0 = (step * NSA_Q_SUB + sub) * Q_BLOCK
        tok_rows = slice(sub * Q_BLOCK, (sub + 1) * Q_BLOCK)
        t_lane = q0 + tok
        q0_s.append(q0)
        t_lane_s.append(t_lane)
        gates_s.append(jnp.transpose(_sigmoid(sm_ref[tok_rows, :])))

        cmp_bias = jnp.where((n_sub * CMP_STRIDE + (CMP_BLOCK - 1)) <= t_lane, 0.0, NEG_BIG)
        cmp_any = jnp.concatenate([(t_lane >= CMP_BLOCK - 1).astype(F32)] * NSA_REP, axis=1)

        for g in groups:
            qs = jnp.concatenate(
                [q_ref[tok_rows, (g * NSA_REP + r) * HEAD_PAD:(g * NSA_REP + r + 1) * HEAD_PAD]
                 for r in range(NSA_REP)], axis=0)

            s_c = add_bias(_dot_nt(kc_ref[0, g], qs), cmp_bias)
            e_c = jnp.exp2(s_c - jnp.max(s_c, axis=0, keepdims=True))
            p_c = e_c * (cmp_any / jnp.sum(e_c, axis=0, keepdims=True))
            o_cmp[sub, g] = _dot_tn(vc_ref[0, g], p_c)[0:NSA_DH]

            p_sum = p_c[:, 0:Q_BLOCK]
            for r in range(1, NSA_REP):
                p_sum = p_sum + p_c[:, r * Q_BLOCK:(r + 1) * Q_BLOCK]
            imp_t = _dot(ovt_ref[...], p_sum)

            def rank_select(imp_t=imp_t, t_lane=t_lane):
                j_sub = lax.broadcasted_iota(jnp.int32, (n_sel_blk, 1), 0)
                cur = jnp.right_shift(t_lane, SEL_SHIFT)
                forced = (j_sub == 0) | (j_sub == cur) | (j_sub == cur - 1)
                causal_blk = (j_sub * SEL_BLOCK) <= t_lane
                imp = jnp.where(forced, FORCE_SCORE, jnp.where(causal_blk, imp_t, -jnp.inf))
                rank = jnp.zeros((n_sel_blk, Q_BLOCK), F32)
                for jp in range(n_sel_blk):
                    row = imp[jp:jp + 1, :]
                    ahead = (row > imp) | ((row == imp) & (j_sub > jp))
                    rank = rank + ahead.astype(F32)
                return (rank < SEL_TOPK).astype(BF16)

            sel_t = lax.cond(q0 + Q_BLOCK <= SEL_TOPK * SEL_BLOCK,
                             lambda: jnp.ones((n_sel_blk, Q_BLOCK), BF16), rank_select)
            sel_lanes = _dot_tn(sel_t, place)
            pen = jnp.where(pen_lanes, (sel_lanes - 1.0) * (-NEG_BIG), 0.0).astype(BF16)
            qp[sub, g] = qs + jnp.concatenate([pen] * NSA_REP, axis=0)

    n_full = step

    def scores(kb, unit):
        k0 = pl.multiple_of(kb * kb_rows, kb_rows)
        g = unit[1]
        return _dot_nt(ksa[pl.ds(k0, kb_rows), g * HEAD_PAD:(g + 1) * HEAD_PAD], qp[unit])

    def absorb(state, s, kb, unit):
        m_i, acc = state
        m_new = jnp.maximum(m_i, jnp.max(s, axis=0, keepdims=True))
        return m_new, jnp.exp2(m_i - m_new) * acc + _dot(vst[kb, unit[1]], jnp.exp2(s - m_new))

    kpos_last = n_full * kb_rows + lax.broadcasted_iota(jnp.int32, (kb_rows, 1), 0)
    first = []
    for unit in units:
        s = add_bias(scores(n_full, unit), jnp.where(kpos_last <= t_lane_s[unit[0]], 0.0, NEG_BIG))
        m = jnp.max(s, axis=0, keepdims=True)
        first.append((m, _dot(vst[n_full, unit[1]], jnp.exp2(s - m))))

    def full_blocks(state):
        def loop_body(kb, carry):
            st, s_cur = carry
            s_next = tuple(scores(kb + 1, u) for u in units)
            return tuple(absorb(st[i], s_cur[i], kb, u) for i, u in enumerate(units)), s_next

        st, s_cur = lax.fori_loop(0, n_full - 1, loop_body, (state, tuple(scores(0, u) for u in units)))
        return tuple(absorb(st[i], s_cur[i], n_full - 1, u) for i, u in enumerate(units))

    sel_out = lax.cond(n_full > 0, full_blocks, lambda st: st, tuple(first))

    span = WINDOW + Q_BLOCK
    for i, (sub, g) in enumerate(units):
        q0, t_lane = q0_s[sub], t_lane_s[sub]
        tok_rows = slice(sub * Q_BLOCK, (sub + 1) * Q_BLOCK)
        w0 = pl.multiple_of(jnp.maximum(q0 - WINDOW, 0), Q_BLOCK)
        kpos_w = w0 + lax.broadcasted_iota(jnp.int32, (span, 1), 0)
        win_bias = jnp.where((kpos_w <= t_lane) & (kpos_w > t_lane - WINDOW), 0.0, NEG_BIG)

        acc_s = sel_out[i][1]
        o_slc = acc_s[0:NSA_DH] * (1.0 / acc_s[one_row:one_row + 1, :])

        k_w = kv_ref[pl.ds(w0, span), (4 + g) * HEAD_PAD:(5 + g) * HEAD_PAD]
        s_w = add_bias(_dot_nt(k_w, qp[sub, g]), win_bias)
        acc_w = _dot_tn(vwa[pl.ds(w0, span), g * HEAD_PAD:(g + 1) * HEAD_PAD],
                        jnp.exp2(s_w - jnp.max(s_w, axis=0, keepdims=True)))
        o_win = acc_w[0:NSA_DH] * (1.0 / acc_w[one_row:one_row + 1, :])

        gates_t = gates_s[sub]
        o_heads = []
        for r in range(NSA_REP):
            c0 = 8 + (g * NSA_REP + r) * 3
            sl = slice(r * Q_BLOCK, (r + 1) * Q_BLOCK)
            o_t = (gates_t[c0:c0 + 1, :] * o_cmp[sub, g][:, sl] + gates_t[c0 + 1:c0 + 2, :] * o_slc[:, sl]
                   + gates_t[c0 + 2:c0 + 3, :] * o_win[:, sl])
            o_heads.append(o_t)
        for pair in range(NSA_REP // 2):
            lanes = slice((g * NSA_REP + 2 * pair) * NSA_DH, (g * NSA_REP + 2 * pair + 2) * NSA_DH)
            o_ref[tok_rows, lanes] = jnp.transpose(jnp.concatenate(o_heads[2 * pair:2 * pair + 2], axis=0)
                                                   ).astype(o_ref.dtype)


def _nsa(nq, nkv, kc, vc, small, overlap, bsz, seq):
    assert NSA_Q_SUB * Q_BLOCK == SEL_KEY_BLOCK
    rows = NSA_Q_SUB * Q_BLOCK
    nstep = seq // rows
    in_specs = [pl.BlockSpec((rows, NSA_HEADS * HEAD_PAD), lambda b, i: (b * nstep + i, 0)),
                pl.BlockSpec((seq, nkv.shape[1]), lambda b, i: (b, 0)),
                pl.BlockSpec((1,) + kc.shape[1:], lambda b, i: (b, 0, 0, 0)),
                pl.BlockSpec((1,) + vc.shape[1:], lambda b, i: (b, 0, 0, 0)),
                pl.BlockSpec((rows, LANE), lambda b, i: (b * nstep + i, 0)),
                _full_spec(overlap.shape)]
    return pl.pallas_call(
        functools.partial(_nsa_kernel, seq=seq), grid=(bsz, nstep), in_specs=in_specs,
        out_specs=pl.BlockSpec((rows, NSA_QW), lambda b, i: (b * nstep + i, 0)),
        out_shape=jax.ShapeDtypeStruct((bsz * seq, NSA_QW), BF16),
        scratch_shapes=[pltpu.VMEM((seq, NSA_KV_HEADS * HEAD_PAD), BF16),
                        pltpu.VMEM((seq // SEL_KEY_BLOCK, NSA_KV_HEADS, HEAD_PAD, SEL_KEY_BLOCK), BF16),
                        pltpu.VMEM((seq, NSA_KV_HEADS * HEAD_PAD), BF16)],
        compiler_params=_cparams(("parallel", "arbitrary")), name="nsa_attention",
    )(nq, nkv, kc, vc, small, overlap)


def _mix_ffn_kernel(oa_ref, oa_halo, ob_ref, ob_halo, mix_ref, mix_halo, x_ref, x_halo,
                    wa_ref, wb_ref, wo_ref, g1_ref, b1_ref, wup_ref, cw_ref, wd_ref, g_ref, b_ref, o_ref,
                    hres, hb, ug, uv, act, *, seq, tm, halo):
    i = pl.program_id(0)
    first = lax.rem(i * tm, seq) == 0

    def merged(oa, ob, mix, x):
        ya = jnp.dot(oa, wa_ref[...], preferred_element_type=F32)
        yb = jnp.dot(ob, wb_ref[...], preferred_element_type=F32)
        mixf = mix.astype(F32)
        mixed = _sigmoid(mixf[:, 0:D_MODEL]) * ya + _sigmoid(mixf[:, D_MODEL:2 * D_MODEL]) * yb
        return _layer_norm(DN_ALPHA * x + _dot(mixed, wo_ref[...]), g1_ref[...], b1_ref[...])

    hres[...] = merged(oa_ref[...], ob_ref[...], mix_ref[...], x_ref[...])
    h_halo = merged(oa_halo[...], ob_halo[...], mix_halo[...], x_halo[...])
    hb[0:halo, :] = jnp.where(first, 0.0, h_halo).astype(BF16)
    hb[halo:, :] = hres[...].astype(BF16)

    def conv(u_ref, slot, cols):
        out = None
        for tap in range(FFN_CONV):
            term = cw_ref[tap:tap + 1, cols] * u_ref[slot, pl.ds(halo - (FFN_CONV - 1) + tap, tm), :]
            out = term if out is None else out + term
        return out

    for j in range(D_FF // FFN_CHUNK):
        slot = j % 2
        cols_g = slice(j * FFN_CHUNK, (j + 1) * FFN_CHUNK)
        cols_v = slice(D_FF + j * FFN_CHUNK, D_FF + (j + 1) * FFN_CHUNK)
        ug[slot] = jnp.dot(hb[...], wup_ref[:, cols_g], preferred_element_type=F32)
        uv[slot] = jnp.dot(hb[...], wup_ref[:, cols_v], preferred_element_type=F32)
        a = conv(ug, slot, cols_g)
        act[:, cols_g] = (a * _sigmoid(a) * conv(uv, slot, cols_v)).astype(act.dtype)

    f = jnp.dot(act[...], wd_ref[...], preferred_element_type=F32)
    o_ref[...] = _layer_norm(DN_ALPHA * hres[...] + f, g_ref[...], b_ref[...])


def _mix_ffn(oa, ob, mix, x2d, wa, wb, wo, g1, b1, w_up, conv_w, w_down, g2, b2, seq):
    m = x2d.shape[0]
    tm = ROW_TILE
    halo = 16
    resident = lambda a: pl.BlockSpec(a.shape, lambda i: (0, 0), pipeline_mode=pl.Buffered(1))
    rows = []
    for a in (oa, ob, mix, x2d):
        rows += [pl.BlockSpec((tm, a.shape[1]), lambda i: (i, 0)),
                 pl.BlockSpec((halo, a.shape[1]), lambda i: (jnp.maximum(i * (tm // halo) - 1, 0), 0))]
    weights = (wa, wb, wo, g1, b1, w_up, conv_w, w_down, g2, b2)
    return pl.pallas_call(
        functools.partial(_mix_ffn_kernel, seq=seq, tm=tm, halo=halo), grid=(m // tm,),
        in_specs=rows + [resident(a) for a in weights],
        out_specs=pl.BlockSpec((tm, D_MODEL), lambda i: (i, 0)),
        out_shape=jax.ShapeDtypeStruct((m, D_MODEL), F32),
        scratch_shapes=[pltpu.VMEM((tm, D_MODEL), F32), pltpu.VMEM((tm + halo, D_MODEL), BF16),
                        pltpu.VMEM((2, tm + halo, FFN_CHUNK), F32), pltpu.VMEM((2, tm + halo, FFN_CHUNK), F32),
                        pltpu.VMEM((tm, D_FF), BF16)],
        compiler_params=_cparams(("parallel",)), name="mix_ffn_ln",
    )(oa, oa, ob, ob, mix, mix, x2d, x2d, *weights)


def _compress_weights(w1):
    half = CMP_BLOCK // 2
    w1r = w1.reshape(2, half, NSA_DH, CMP_HIDDEN)
    eye = jnp.eye(NSA_KV_HEADS, dtype=w1.dtype)
    ex = jnp.einsum("aidh,gk->agikdh", w1r, eye)
    ex = ex.reshape(2 * NSA_KV_HEADS, half * NSA_KVW, CMP_HIDDEN)
    return jnp.concatenate([ex[n] for n in range(2 * NSA_KV_HEADS)], axis=1)


def _overlap_matrix(seq):
    n_cmp = (seq - CMP_BLOCK) // CMP_STRIDE + 1
    n_sel = seq // SEL_BLOCK
    starts = np.arange(n_cmp) * CMP_STRIDE
    jb = np.arange(n_sel) * SEL_BLOCK
    ov = ((starts[:, None] < jb[None] + SEL_BLOCK) & (starts[:, None] + CMP_BLOCK > jb[None])).astype(np.float32)
    out = np.zeros((n_sel, seq // CMP_STRIDE), np.float32)
    out[:, :n_cmp] = ov.T
    return jnp.asarray(out, BF16)


def kernel(x, w_in, gdn_conv_w, gdn_a_log, gdn_dt_bias, gdn_norm_w, cmp_pos_k, cmp_w1_k, cmp_w2_k,
           cmp_pos_v, cmp_w1_v, cmp_w2_v, w_branch_gdn, w_branch_nsa, w_out, ln1_g, ln1_b, w_up,
           ffn_conv_w, w_down, ln2_g, ln2_b):
    bsz, seq, _ = x.shape
    m = bsz * seq
    for i in range(DEPTH):
        x2d = x.reshape(m, D_MODEL)
        w = w_in[i]
        o_gdn = 2 * GDN_QK + 2 * GDN_VW
        o_nq = o_gdn + 2 * GDN_HEADS
        o_kv = o_nq + NSA_QW
        o_gate = o_kv + 6 * NSA_KVW
        o_mix = o_gate + 3 * NSA_HEADS
        kv = [w[:, o_kv + n * NSA_KVW:o_kv + (n + 1) * NSA_KVW] for n in range(6)]
        small_w = jnp.concatenate([w[:, o_gdn:o_nq], w[:, o_gate:o_mix]], axis=1)
        small_w = jnp.pad(small_w, ((0, 0), (0, LANE - small_w.shape[1])))
        weights = [w[:, :o_gdn],
                   w[:, o_nq:o_kv] * (NSA_DH ** -0.5 * LOG2E),
                   w[:, o_kv + 2 * NSA_KVW:o_gate],
                   kv[0], kv[1],
                   w[:, o_mix:],
                   small_w]
        weights = [t.astype(BF16) for t in weights]
        widths = [o_gdn, NSA_HEADS * HEAD_PAD, 4 * NSA_KV_HEADS * HEAD_PAD, NSA_KVW, NSA_KVW, 2 * D_MODEL, LANE]
        gqkvz, nq, nkv, kcp, vcp, mix, small, grow = _inproj(x2d, weights, [BF16] * 6 + [F32], widths,
                                                             2 * GDN_HEADS)

        alog = jnp.broadcast_to(gdn_a_log[i][:, None, None], (GDN_HEADS, 1, LANE))
        dtb = jnp.broadcast_to(gdn_dt_bias[i][:, None, None], (GDN_HEADS, 1, LANE))
        o_a = _gdn(gqkvz, gdn_conv_w[i], small, grow, alog, dtb, gdn_norm_w[i][None, :], bsz, seq)

        grp = seq // CMP_STRIDE
        k16 = kcp.reshape(bsz, grp, CMP_STRIDE * NSA_KVW)
        v16 = vcp.reshape(bsz, grp, CMP_STRIDE * NSA_KVW)
        pos_rows = lambda p: jnp.pad(p.reshape(1, CMP_BLOCK * NSA_DH), ((0, 7), (0, 0))).astype(BF16)
        w2_pad = lambda t: jnp.pad(t, ((0, 0), (0, HEAD_PAD - NSA_DH))).astype(BF16)
        kc, vc = _compress(k16, v16,
                           _compress_weights(cmp_w1_k[i]).astype(BF16), _compress_weights(cmp_w1_v[i]).astype(BF16),
                           cmp_w1_k[i].astype(BF16), cmp_w1_v[i].astype(BF16),
                           pos_rows(cmp_pos_k[i]), pos_rows(cmp_pos_v[i]),
                           w2_pad(cmp_w2_k[i]), w2_pad(cmp_w2_v[i]))
        o_b = _nsa(nq, nkv, kc, vc, small, _overlap_matrix(seq), bsz, seq)

        x = _mix_ffn(o_a, o_b, mix, x2d, w_branch_gdn[i].astype(BF16), w_branch_nsa[i].astype(BF16),
                     w_out[i].astype(BF16), ln1_g[i][None, :], ln1_b[i][None, :],
                     w_up[i].astype(BF16), ffn_conv_w[i], w_down[i].astype(BF16),
                     ln2_g[i][None, :], ln2_b[i][None, :], seq).reshape(bsz, seq, D_MODEL)
    return x
```

```python
import functools
import math

import numpy as np
import jax
import jax.numpy as jnp
from jax import lax
from jax.experimental import pallas as pl
from jax.experimental.pallas import tpu as pltpu

F32 = jnp.float32
BF16 = jnp.bfloat16

D_MODEL = 1024
GDN_HEADS = 4
GDN_DK = 128
GDN_DV = 128
GDN_CONV = 4
GDN_CHUNK = 64
NSA_HEADS = 8
NSA_KV_HEADS = 2
NSA_REP = NSA_HEADS // NSA_KV_HEADS
NSA_DH = 64
CMP_BLOCK = 32
CMP_STRIDE = 16
CMP_HIDDEN = 256
SEL_BLOCK = 64
SEL_SHIFT = 6
SEL_TOPK = 16
WINDOW = 512
FORCE_SCORE = 1e9
D_FF = 2816
FFN_CONV = 3
DEPTH = 1
DN_ALPHA = (2 * DEPTH) ** 0.25
LN_EPS = 1e-5
RMS_EPS = 1e-6
LOG2E = math.log2(math.e)

GDN_QK = GDN_HEADS * GDN_DK
GDN_VW = GDN_HEADS * GDN_DV
NSA_QW = NSA_HEADS * NSA_DH
NSA_KVW = NSA_KV_HEADS * NSA_DH

LANE = 128
HEAD_PAD = LANE
NEG_BIG = -1e30
VMEM_LIMIT = 56 * 1024 * 1024

Q_BLOCK = 128
NSA_Q_SUB = 2
SEL_KEY_BLOCK = 256
FFN_CHUNK = 256
ROW_TILE = 512
GDN_CHUNK_SHIFT = 6
GDN_PAIR = 2 * GDN_CHUNK
GDN_UNROLL = 16


def _cparams(sem):
    return pltpu.CompilerParams(dimension_semantics=sem, vmem_limit_bytes=VMEM_LIMIT)


def _dot(a, b):
    return jnp.dot(a.astype(BF16), b.astype(BF16), preferred_element_type=F32)


def _dot_nt(a, b):
    return lax.dot_general(a.astype(BF16), b.astype(BF16), (((1,), (1,)), ((), ())),
                           preferred_element_type=F32)


def _dot_tn(a, b):
    return lax.dot_general(a.astype(BF16), b.astype(BF16), (((0,), (0,)), ((), ())),
                           preferred_element_type=F32)


def _sigmoid(x):
    return jax.nn.sigmoid(x)


def _softplus(x):
    return jnp.maximum(x, 0.0) + jnp.log1p(jnp.exp(-jnp.abs(x)))


def _layer_norm(x, g, b):
    mu = jnp.mean(x, axis=-1, keepdims=True)
    xc = x - mu
    var = jnp.mean(xc * xc, axis=-1, keepdims=True)
    return xc * lax.rsqrt(var + LN_EPS) * g + b


def _full_spec(shape):
    nd = len(shape)
    return pl.BlockSpec(shape, lambda *_: (0,) * nd)


def _inproj_kernel(x_ref, *refs, t_rows, scales):
    n = (len(refs) - 1) // 2
    xb = x_ref[...].astype(BF16)
    for w_ref, o_ref, scale in zip(refs[:n], refs[n:2 * n], scales):
        res = jnp.dot(xb, w_ref[...], preferred_element_type=F32)
        if scale != 1.0:
            res = res * scale
        if o_ref.shape[1] == w_ref.shape[1]:
            o_ref[...] = res.astype(o_ref.dtype)
        else:
            out = res.astype(o_ref.dtype)
            zeros = jnp.zeros((out.shape[0], HEAD_PAD - NSA_DH), o_ref.dtype)
            for hd in range(w_ref.shape[1] // NSA_DH):
                o_ref[:, hd * HEAD_PAD:hd * HEAD_PAD + NSA_DH] = out[:, hd * NSA_DH:(hd + 1) * NSA_DH]
                o_ref[:, hd * HEAD_PAD + NSA_DH:(hd + 1) * HEAD_PAD] = zeros
    t_ref = refs[2 * n]
    for j in range(t_ref.shape[0]):
        t_ref[j] = jnp.transpose(res[j * LANE:(j + 1) * LANE, :])[0:t_rows, :]


def _inproj(x2d, weights, out_dtypes, out_widths, scales, t_rows):
    assert weights[-1].shape[1] == LANE and out_dtypes[-1] == F32
    m = x2d.shape[0]
    tm = ROW_TILE
    in_specs = [pl.BlockSpec((tm, D_MODEL), lambda i: (i, 0))]
    in_specs += [_full_spec(w.shape) for w in weights]
    out_specs = [pl.BlockSpec((tm, n), lambda i: (i, 0)) for n in out_widths]
    out_specs.append(pl.BlockSpec((tm // LANE, t_rows, LANE), lambda i: (i, 0, 0)))
    out_shape = [jax.ShapeDtypeStruct((m, n), dt) for n, dt in zip(out_widths, out_dtypes)]
    out_shape.append(jax.ShapeDtypeStruct((m // LANE, t_rows, LANE), F32))
    return pl.pallas_call(
        functools.partial(_inproj_kernel, t_rows=t_rows, scales=tuple(scales)), grid=(m // tm,),
        in_specs=in_specs, out_specs=out_specs,
        out_shape=out_shape, compiler_params=_cparams(("parallel",)), name="inproj")(x2d, *weights)


def _gdn_kernel(q_ref, k_ref, v_ref, z_ref, cwq_ref, cwk_ref, cwv_ref, gcol_ref, grow_ref,
                alog_ref, dtb_ref, nw_ref, o_ref,
                xpad, qh, kh, vh, xp, qk_s, rhs, kd, qd, gl, sol, mq, ns, o0, st, *, seq):
    c = GDN_CHUNK
    pr = GDN_PAIR
    npair = seq // pr
    head = pl.program_id(1)
    pad = 8
    conv_rows = 256
    xpad[0:pad, :] = jnp.zeros((pad, 3 * LANE), F32)
    xpad[pad:, 0:LANE] = q_ref[...].astype(F32)
    xpad[pad:, LANE:2 * LANE] = k_ref[...].astype(F32)
    xpad[pad:, 2 * LANE:3 * LANE] = v_ref[...].astype(F32)
    cw = jnp.concatenate([cwq_ref[...], cwk_ref[...], cwv_ref[...]], axis=1)

    for t0 in range(0, seq, conv_rows):
        acc = jnp.zeros((conv_rows, 3 * LANE), F32)
        for j in range(GDN_CONV):
            xs = xpad[t0 + pad - (GDN_CONV - 1) + j:t0 + pad - (GDN_CONV - 1) + j + conv_rows, :]
            acc = acc + xs * cw[j:j + 1, :]
        y = acc * _sigmoid(acc)
        yq, yk, yv = y[:, 0:LANE], y[:, LANE:2 * LANE], y[:, 2 * LANE:3 * LANE]
        yq = yq * lax.rsqrt(jnp.sum(yq * yq, axis=-1, keepdims=True) + RMS_EPS) * (GDN_DK ** -0.5)
        yk = yk * lax.rsqrt(jnp.sum(yk * yk, axis=-1, keepdims=True) + RMS_EPS)
        qh[t0:t0 + conv_rows, :] = yq
        kh[t0:t0 + conv_rows, :] = yk
        vh[t0:t0 + conv_rows, :] = yv

    neg_a = -jnp.exp(alog_ref[0, :, 0:1])
    dtb = dtb_ref[0, :, 0:1]
    ii = lax.broadcasted_iota(jnp.int32, (pr, pr), 0)
    jj = lax.broadcasted_iota(jnp.int32, (pr, pr), 1)
    same = jnp.right_shift(ii, GDN_CHUNK_SHIFT) == jnp.right_shift(jj, GDN_CHUNK_SHIFT)
    incl = same & (jj <= ii)
    strict = same & (jj < ii)
    upper = same & (ii <= jj)
    eye = (ii == jj).astype(F32)
    sub8 = lax.broadcasted_iota(jnp.int32, (8, LANE), 0)
    lane_id = lax.broadcasted_iota(jnp.int32, (1, LANE), 1)

    def over_pairs(body):
        def run(gi, carry):
            for uu in range(GDN_UNROLL):
                body(gi * GDN_UNROLL + uu)
            return carry
        lax.fori_loop(0, npair // GDN_UNROLL, run, 0)

    def stage_gates(p):
        rows = pl.ds(pl.multiple_of(p * pr, pr), pr)
        q = qh[rows, :]
        k = kh[rows, :]
        v = vh[rows, :]
        gcol = gcol_ref[rows, :]
        b_logit = jnp.sum(jnp.where(lane_id == head, gcol, 0.0), axis=1, keepdims=True)
        a_logit = jnp.sum(jnp.where(lane_id == GDN_HEADS + head, gcol, 0.0), axis=1, keepdims=True)
        a_row = jnp.sum(jnp.where(sub8[:, 0:1] == GDN_HEADS + head, grow_ref[p], 0.0), axis=0, keepdims=True)
        beta = _sigmoid(b_logit)
        g_c = neg_a * _softplus(a_logit + dtb)
        g_r = neg_a * _softplus(a_row + dtb)
        gc_col = jnp.sum(jnp.where(incl, g_r, 0.0), axis=1, keepdims=True)
        gc_row = jnp.sum(jnp.where(upper, g_c, 0.0), axis=0, keepdims=True)
        gt_col = jnp.sum(jnp.where(same, g_r, 0.0), axis=1, keepdims=True)
        decay = jnp.where(incl, jnp.exp(jnp.where(incl, gc_col - gc_row, 0.0)), 0.0)
        kq_k = _dot_nt(jnp.concatenate([k, q], axis=0), k)
        a_mat = jnp.where(strict, kq_k[0:pr] * decay * beta, 0.0)
        xp[p, 0:pr, :] = eye - a_mat
        xp[p, pr:2 * pr, :] = _dot(a_mat, a_mat)
        qk_s[p] = (kq_k[pr:2 * pr] * decay).astype(qk_s.dtype)
        e_gc = jnp.exp(gc_col)
        rhs[rows, 0:GDN_DV] = (v * beta).astype(rhs.dtype)
        rhs[rows, GDN_DV:GDN_DV + GDN_DK] = (k * (beta * e_gc)).astype(rhs.dtype)
        kd[rows, :] = (k * jnp.exp(gt_col - gc_col)).astype(kd.dtype)
        qd[rows, :] = q * e_gc
        g_last = jnp.exp(gt_col)
        gl[p] = jnp.where(sub8 == 0, g_last[0:1, :], g_last[c:c + 1, :])

    def stage_double(p):
        y = _dot(xp[p], xp[p, pr:2 * pr, :])
        xp[p, 0:pr, :] = xp[p, 0:pr, :] + y[0:pr]
        xp[p, pr:2 * pr, :] = y[pr:2 * pr]

    def stage_solve(p):
        rows = pl.ds(pl.multiple_of(p * pr, pr), pr)
        x_inv = xp[p, 0:pr, :]
        x_inv = x_inv + _dot(x_inv, xp[p, pr:2 * pr, :])
        sol[rows, :] = _dot(x_inv, rhs[rows, :])

    def stage_affine(p):
        r0 = pl.multiple_of(p * pr, pr)
        rows = pl.ds(r0, pr)
        e = _dot(qk_s[p], sol[rows, :])
        qe = qd[rows, :] - e[:, GDN_DV:GDN_DV + GDN_DK]
        g_rows = gl[p]
        for cc in range(pr // c):
            rc = pl.ds(r0 + cc * c, c)
            mn = _dot_tn(kd[rc, :], sol[rc, :])
            idx = p * (pr // c) + cc
            mq[head, idx, 0:GDN_DK, :] = (eye * g_rows[cc:cc + 1, :] - mn[:, GDN_DV:GDN_DV + GDN_DK]).astype(mq.dtype)
            mq[head, idx, GDN_DK:GDN_DK + c, :] = qe[cc * c:(cc + 1) * c].astype(mq.dtype)
            ns[head, idx] = mn[:, 0:GDN_DV]
            o0[head, idx] = e[cc * c:(cc + 1) * c, 0:GDN_DV]

    over_pairs(stage_gates)
    for _ in range(int(math.log2(c)) - 2):
        over_pairs(stage_double)
    over_pairs(stage_solve)
    over_pairs(stage_affine)

    @pl.when(head == GDN_HEADS - 1)
    def _():
        st[...] = jnp.zeros_like(st)
        nw = nw_ref[...]

        def step(ci, carry):
            r0 = pl.multiple_of(ci * c, c)
            for h in range(GDN_HEADS):
                cols = slice(h * LANE, (h + 1) * LANE)
                ys = jnp.dot(mq[h, ci], st[h], preferred_element_type=F32)
                st[h] = (ys[0:GDN_DK] + ns[h, ci]).astype(st.dtype)
                o = ys[GDN_DK:GDN_DK + c] + o0[h, ci]
                o = o * lax.rsqrt(jnp.mean(o * o, axis=-1, keepdims=True) + RMS_EPS) * nw
                z = z_ref[pl.ds(r0, c), cols].astype(F32)
                o_ref[pl.ds(r0, c), cols] = (o * (z * _sigmoid(z))).astype(o_ref.dtype)
            return carry

        lax.fori_loop(0, seq // c, step, 0)


def _gdn(gqkvz, conv_w, gcol, grow, alog, dtb, nw, bsz, seq):
    assert GDN_DK == GDN_PAIR == GDN_DV == LANE
    h = GDN_HEADS
    npair = seq // GDN_PAIR
    nchunk = seq // GDN_CHUNK
    blk = lambda off: pl.BlockSpec((seq, LANE), lambda b, hh: (b, off + hh))
    cwb = lambda off: pl.BlockSpec((GDN_CONV, LANE), lambda b, hh: (0, off + hh))
    in_specs = [blk(0), blk(h), blk(2 * h),
                pl.BlockSpec((seq, GDN_VW), lambda b, hh: (b, 3)),
                cwb(0), cwb(h), cwb(2 * h),
                pl.BlockSpec((seq, LANE), lambda b, hh: (b, 0)),
                pl.BlockSpec((npair, 8, GDN_PAIR), lambda b, hh: (b, 0, 0)),
                pl.BlockSpec((1, 1, LANE), lambda b, hh: (hh, 0, 0)),
                pl.BlockSpec((1, 1, LANE), lambda b, hh: (hh, 0, 0)),
                pl.BlockSpec((1, LANE), lambda b, hh: (0, 0))]
    scratch = [pltpu.VMEM((seq + 8, 3 * LANE), F32),
               pltpu.VMEM((seq, LANE), F32), pltpu.VMEM((seq, LANE), F32), pltpu.VMEM((seq, LANE), F32),
               pltpu.VMEM((npair, 2 * GDN_PAIR, GDN_PAIR), F32),
               pltpu.VMEM((npair, GDN_PAIR, GDN_PAIR), BF16),
               pltpu.VMEM((seq, GDN_DV + GDN_DK), BF16),
               pltpu.VMEM((seq, GDN_DK), BF16),
               pltpu.VMEM((seq, GDN_DK), F32),
               pltpu.VMEM((npair, 8, LANE), F32),
               pltpu.VMEM((seq, GDN_DV + GDN_DK), F32),
               pltpu.VMEM((h, nchunk, GDN_DK + GDN_CHUNK, GDN_DK), BF16),
               pltpu.VMEM((h, nchunk, GDN_DK, GDN_DV), F32),
               pltpu.VMEM((h, nchunk, GDN_CHUNK, GDN_DV), F32),
               pltpu.VMEM((h, GDN_DK, GDN_DV), BF16)]
    return pl.pallas_call(
        functools.partial(_gdn_kernel, seq=seq), grid=(bsz, h), in_specs=in_specs,
        out_specs=pl.BlockSpec((seq, GDN_VW), lambda b, hh: (b, 0)),
        out_shape=jax.ShapeDtypeStruct((bsz * seq, GDN_VW), BF16),
        scratch_shapes=scratch,
        compiler_params=_cparams(("parallel", "arbitrary")), name="gdn",
    )(gqkvz, gqkvz, gqkvz, gqkvz, conv_w, conv_w, conv_w, gcol, grow, alog, dtb, nw)


def _gelu_tanh(x):
    return 0.5 * x * (1.0 + jnp.tanh(math.sqrt(2.0 / math.pi) * (x + 0.044715 * x * x * x)))


def _compress_kernel(k16_ref, v16_ref, wkc_ref, wvc_ref, w1k_ref, w1v_ref, pk_ref, pv_ref,
                     w2k_ref, w2v_ref, kc_ref, vc_ref):
    nblk = k16_ref.shape[1]
    for x_ref, wc_ref, w1_ref, p_ref, w2_ref, o_ref in (
            (k16_ref, wkc_ref, w1k_ref, pk_ref, w2k_ref, kc_ref),
            (v16_ref, wvc_ref, w1v_ref, pv_ref, w2v_ref, vc_ref)):
        y = jnp.dot(x_ref[0], wc_ref[...], preferred_element_type=F32)
        pos_term = jnp.dot(p_ref[...], w1_ref[...], preferred_element_type=F32)[0:1, :]
        for g in range(NSA_KV_HEADS):
            top = y[:, g * CMP_HIDDEN:(g + 1) * CMP_HIDDEN]
            bot = y[:, (NSA_KV_HEADS + g) * CMP_HIDDEN:(NSA_KV_HEADS + g + 1) * CMP_HIDDEN]
            hid = top + pltpu.roll(bot, nblk - 1, 0) + pos_term
            o_ref[0, g] = _dot(_gelu_tanh(hid), w2_ref[...]).astype(o_ref.dtype)


def _compress(k16, v16, wkc, wvc, w1k, w1v, pk, pv, w2k, w2v):
    bsz, nblk, width = k16.shape
    in_specs = [pl.BlockSpec((1, nblk, width), lambda b: (b, 0, 0))] * 2
    in_specs += [_full_spec(a.shape) for a in (wkc, wvc, w1k, w1v, pk, pv, w2k, w2v)]
    out_spec = pl.BlockSpec((1, NSA_KV_HEADS, nblk, HEAD_PAD), lambda b: (b, 0, 0, 0))
    out_shape = jax.ShapeDtypeStruct((bsz, NSA_KV_HEADS, nblk, HEAD_PAD), BF16)
    return pl.pallas_call(
        _compress_kernel, grid=(bsz,), in_specs=in_specs, out_specs=[out_spec, out_spec],
        out_shape=[out_shape, out_shape], compiler_params=_cparams(("parallel",)), name="nsa_compress",
    )(k16, v16, wkc, wvc, w1k, w1v, pk, pv, w2k, w2v)


def _nsa_kernel(q_ref, kv_ref, kc_ref, vc_ref, sm_ref, ovt_ref, o_ref, ksa, vst, vwa, *, seq):
    step = pl.program_id(1)
    n_cmp_pad = kc_ref.shape[2]
    n_sel_blk = seq // SEL_BLOCK
    kb_rows = SEL_KEY_BLOCK
    one_row = NSA_DH
    groups = range(NSA_KV_HEADS)
    units = [(sub, g) for sub in range(NSA_Q_SUB) for g in groups]

    @pl.when(step == 0)
    def _():
        lane = lax.broadcasted_iota(jnp.int32, (1, NSA_KV_HEADS * HEAD_PAD), 1) & (HEAD_PAD - 1)
        ones_lane = (lane == one_row).astype(vwa.dtype)
        row_t = lax.broadcasted_iota(jnp.int32, (HEAD_PAD, 1), 0)

        def fill(bi, carry):
            r0 = pl.multiple_of(bi * kb_rows, kb_rows)
            rows = pl.ds(r0, kb_rows)
            blk = jnp.right_shift(r0 + lax.broadcasted_iota(jnp.int32, (kb_rows, 1), 0), SEL_SHIFT)
            ksa[rows, :] = kv_ref[rows, 0:2 * HEAD_PAD] + (lane == NSA_DH + blk).astype(ksa.dtype)
            vwa[rows, :] = kv_ref[rows, 6 * HEAD_PAD:8 * HEAD_PAD] + ones_lane
            for g in range(NSA_KV_HEADS):
                v_t = jnp.transpose(kv_ref[rows, (2 + g) * HEAD_PAD:(3 + g) * HEAD_PAD].astype(F32))
                vst[bi, g] = jnp.where(row_t == one_row, 1.0, v_t).astype(vst.dtype)
            return carry

        lax.fori_loop(0, seq // kb_rows, fill, 0)

    tok = lax.broadcasted_iota(jnp.int32, (1, Q_BLOCK), 1)
    lane_q = lax.broadcasted_iota(jnp.int32, (1, HEAD_PAD), 1)
    pen_lanes = (lane_q >= NSA_DH) & (lane_q < NSA_DH + n_sel_blk)
    place = (lax.broadcasted_iota(jnp.int32, (n_sel_blk, HEAD_PAD), 1)
             == NSA_DH + lax.broadcasted_iota(jnp.int32, (n_sel_blk, HEAD_PAD), 0)).astype(BF16)
    n_sub = lax.broadcasted_iota(jnp.int32, (n_cmp_pad, 1), 0)

    def add_bias(s, bias):
        return jnp.concatenate([s[:, r * Q_BLOCK:(r + 1) * Q_BLOCK] + bias for r in range(s.shape[1] // Q_BLOCK)],
                               axis=1)

    q0_s, t_lane_s, gates_s, qp, o_cmp = [], [], [], {}, {}
    for sub in range(NSA_Q_SUB):
        q0 = (step * NSA_Q_SUB + sub) * Q_BLOCK
        tok_rows = slice(sub * Q_BLOCK, (sub + 1) * Q_BLOCK)
        t_lane = q0 + tok
        q0_s.append(q0)
        t_lane_s.append(t_lane)
        gates_s.append(jnp.transpose(_sigmoid(sm_ref[tok_rows, :])))

        cmp_bias = jnp.where((n_sub * CMP_STRIDE + (CMP_BLOCK - 1)) <= t_lane, 0.0, NEG_BIG)
        cmp_any = jnp.concatenate([(t_lane >= CMP_BLOCK - 1).astype(F32)] * NSA_REP, axis=1)

        for g in groups:
            qs = jnp.concatenate(
                [q_ref[tok_rows, (g * NSA_REP + r) * HEAD_PAD:(g * NSA_REP + r + 1) * HEAD_PAD]
                 for r in range(NSA_REP)], axis=0)

            s_c = add_bias(_dot_nt(kc_ref[0, g], qs), cmp_bias)
            e_c = jnp.exp2(s_c - jnp.max(s_c, axis=0, keepdims=True))
            p_c = e_c * (cmp_any / jnp.sum(e_c, axis=0, keepdims=True))
            o_cmp[sub, g] = _dot_tn(vc_ref[0, g], p_c)[0:NSA_DH]

            p_sum = p_c[:, 0:Q_BLOCK]
            for r in range(1, NSA_REP):
                p_sum = p_sum + p_c[:, r * Q_BLOCK:(r + 1) * Q_BLOCK]
            imp_t = _dot(ovt_ref[...], p_sum)

            def rank_select(imp_t=imp_t, t_lane=t_lane):
                j_sub = lax.broadcasted_iota(jnp.int32, (n_sel_blk, 1), 0)
                cur = jnp.right_shift(t_lane, SEL_SHIFT)
                forced = (j_sub == 0) | (j_sub == cur) | (j_sub == cur - 1)
                causal_blk = (j_sub * SEL_BLOCK) <= t_lane
                imp = jnp.where(forced, FORCE_SCORE, jnp.where(causal_blk, imp_t, -jnp.inf))
                rank = jnp.zeros((n_sel_blk, Q_BLOCK), F32)
                for jp in range(n_sel_blk):
                    row = imp[jp:jp + 1, :]
                    ahead = (row > imp) | ((row == imp) & (j_sub > jp))
                    rank = rank + ahead.astype(F32)
                return (rank < SEL_TOPK).astype(BF16)

            sel_t = lax.cond(q0 + Q_BLOCK <= SEL_TOPK * SEL_BLOCK,
                             lambda: jnp.ones((n_sel_blk, Q_BLOCK), BF16), rank_select)
            sel_lanes = _dot_tn(sel_t, place)
            pen = jnp.where(pen_lanes, (sel_lanes - 1.0) * (-NEG_BIG), 0.0).astype(BF16)
            qp[sub, g] = qs + jnp.concatenate([pen] * NSA_REP, axis=0)

    n_full = step

    def scores(kb, unit):
        k0 = pl.multiple_of(kb * kb_rows, kb_rows)
        g = unit[1]
        return _dot_nt(ksa[pl.ds(k0, kb_rows), g * HEAD_PAD:(g + 1) * HEAD_PAD], qp[unit])

    def absorb(state, s, kb, unit):
        m_i, acc = state
        m_new = jnp.maximum(m_i, jnp.max(s, axis=0, keepdims=True))
        return m_new, jnp.exp2(m_i - m_new) * acc + _dot(vst[kb, unit[1]], jnp.exp2(s - m_new))

    kpos_last = n_full * kb_rows + lax.broadcasted_iota(jnp.int32, (kb_rows, 1), 0)
    first = []
    for unit in units:
        s = add_bias(scores(n_full, unit), jnp.where(kpos_last <= t_lane_s[unit[0]], 0.0, NEG_BIG))
        m = jnp.max(s, axis=0, keepdims=True)
        first.append((m, _dot(vst[n_full, unit[1]], jnp.exp2(s - m))))

    def full_blocks(state):
        def loop_body(kb, carry):
            st, s_cur = carry
            s_next = tuple(scores(kb + 1, u) for u in units)
            return tuple(absorb(st[i], s_cur[i], kb, u) for i, u in enumerate(units)), s_next

        st, s_cur = lax.fori_loop(0, n_full - 1, loop_body, (state, tuple(scores(0, u) for u in units)))
        return tuple(absorb(st[i], s_cur[i], n_full - 1, u) for i, u in enumerate(units))

    sel_out = lax.cond(n_full > 0, full_blocks, lambda st: st, tuple(first))

    span = WINDOW + Q_BLOCK
    for i, (sub, g) in enumerate(units):
        q0, t_lane = q0_s[sub], t_lane_s[sub]
        tok_rows = slice(sub * Q_BLOCK, (sub + 1) * Q_BLOCK)
        w0 = pl.multiple_of(jnp.maximum(q0 - WINDOW, 0), Q_BLOCK)
        kpos_w = w0 + lax.broadcasted_iota(jnp.int32, (span, 1), 0)
        win_bias = jnp.where((kpos_w <= t_lane) & (kpos_w > t_lane - WINDOW), 0.0, NEG_BIG)

        acc_s = sel_out[i][1]
        o_slc = acc_s[0:NSA_DH] * (1.0 / acc_s[one_row:one_row + 1, :])

        k_w = kv_ref[pl.ds(w0, span), (4 + g) * HEAD_PAD:(5 + g) * HEAD_PAD]
        s_w = add_bias(_dot_nt(k_w, qp[sub, g]), win_bias)
        acc_w = _dot_tn(vwa[pl.ds(w0, span), g * HEAD_PAD:(g + 1) * HEAD_PAD],
                        jnp.exp2(s_w - jnp.max(s_w, axis=0, keepdims=True)))
        o_win = acc_w[0:NSA_DH] * (1.0 / acc_w[one_row:one_row + 1, :])

        gates_t = gates_s[sub]
        o_heads = []
        for r in range(NSA_REP):
            c0 = 8 + (g * NSA_REP + r) * 3
            sl = slice(r * Q_BLOCK, (r + 1) * Q_BLOCK)
            o_t = (gates_t[c0:c0 + 1, :] * o_cmp[sub, g][:, sl] + gates_t[c0 + 1:c0 + 2, :] * o_slc[:, sl]
                   + gates_t[c0 + 2:c0 + 3, :] * o_win[:, sl])
            o_heads.append(o_t)
        for pair in range(NSA_REP // 2):
            lanes = slice((g * NSA_REP + 2 * pair) * NSA_DH, (g * NSA_REP + 2 * pair + 2) * NSA_DH)
            o_ref[tok_rows, lanes] = jnp.transpose(jnp.concatenate(o_heads[2 * pair:2 * pair + 2], axis=0)
                                                   ).astype(o_ref.dtype)


def _nsa(nq, nkv, kc, vc, small, overlap, bsz, seq):
    assert NSA_Q_SUB * Q_BLOCK == SEL_KEY_BLOCK
    rows = NSA_Q_SUB * Q_BLOCK
    nstep = seq // rows
    in_specs = [pl.BlockSpec((rows, NSA_HEADS * HEAD_PAD), lambda b, i: (b * nstep + i, 0)),
                pl.BlockSpec((seq, nkv.shape[1]), lambda b, i: (b, 0)),
                pl.BlockSpec((1,) + kc.shape[1:], lambda b, i: (b, 0, 0, 0)),
                pl.BlockSpec((1,) + vc.shape[1:], lambda b, i: (b, 0, 0, 0)),
                pl.BlockSpec((rows, LANE), lambda b, i: (b * nstep + i, 0)),
                _full_spec(overlap.shape)]
    return pl.pallas_call(
        functools.partial(_nsa_kernel, seq=seq), grid=(bsz, nstep), in_specs=in_specs,
        out_specs=pl.BlockSpec((rows, NSA_QW), lambda b, i: (b * nstep + i, 0)),
        out_shape=jax.ShapeDtypeStruct((bsz * seq, NSA_QW), BF16),
        scratch_shapes=[pltpu.VMEM((seq, NSA_KV_HEADS * HEAD_PAD), BF16),
                        pltpu.VMEM((seq // SEL_KEY_BLOCK, NSA_KV_HEADS, HEAD_PAD, SEL_KEY_BLOCK), BF16),
                        pltpu.VMEM((seq, NSA_KV_HEADS * HEAD_PAD), BF16)],
        compiler_params=_cparams(("parallel", "arbitrary")), name="nsa_attention",
    )(nq, nkv, kc, vc, small, overlap)


def _merge_kernel(oa_ref, ob_ref, mix_ref, x_ref, wa_ref, wb_ref, wo_ref, g_ref, b_ref, h_ref):
    ya = jnp.dot(oa_ref[...], wa_ref[...], preferred_element_type=F32)
    yb = jnp.dot(ob_ref[...], wb_ref[...], preferred_element_type=F32)
    mix = mix_ref[...].astype(F32)
    mixed = _sigmoid(mix[:, 0:D_MODEL]) * ya + _sigmoid(mix[:, D_MODEL:2 * D_MODEL]) * yb
    y = _dot(mixed, wo_ref[...])
    h_ref[...] = _layer_norm(DN_ALPHA * x_ref[...] + y, g_ref[...], b_ref[...])


def _merge(oa, ob, mix, x2d, wa, wb, wo, g, b):
    m = x2d.shape[0]
    tm = ROW_TILE
    row = lambda n: pl.BlockSpec((tm, n), lambda i: (i, 0))
    in_specs = [row(oa.shape[1]), row(ob.shape[1]), row(mix.shape[1]), row(D_MODEL),
                _full_spec(wa.shape), _full_spec(wb.shape), _full_spec(wo.shape),
                _full_spec(g.shape), _full_spec(b.shape)]
    return pl.pallas_call(
        _merge_kernel, grid=(m // tm,), in_specs=in_specs, out_specs=row(D_MODEL),
        out_shape=jax.ShapeDtypeStruct((m, D_MODEL), F32),
        compiler_params=_cparams(("parallel",)), name="merge_ln",
    )(oa, ob, mix, x2d, wa, wb, wo, g, b)


def _ffn_kernel(h_ref, halo_ref, wup_ref, cw_ref, wd_ref, g_ref, b_ref, o_ref,
                hb, ug, uv, act, *, seq, tm, halo):
    i = pl.program_id(0)
    first = lax.rem(i * tm, seq) == 0
    hb[0:halo, :] = jnp.where(first, 0.0, halo_ref[...]).astype(BF16)
    hb[halo:, :] = h_ref[...].astype(BF16)

    def conv(u_ref, slot, cols):
        out = None
        for tap in range(FFN_CONV):
            term = cw_ref[tap:tap + 1, cols] * u_ref[slot, pl.ds(halo - (FFN_CONV - 1) + tap, tm), :]
            out = term if out is None else out + term
        return out

    for j in range(D_FF // FFN_CHUNK):
        slot = j % 2
        cols_g = slice(j * FFN_CHUNK, (j + 1) * FFN_CHUNK)
        cols_v = slice(D_FF + j * FFN_CHUNK, D_FF + (j + 1) * FFN_CHUNK)
        ug[slot] = jnp.dot(hb[...], wup_ref[:, cols_g], preferred_element_type=F32)
        uv[slot] = jnp.dot(hb[...], wup_ref[:, cols_v], preferred_element_type=F32)
        a = conv(ug, slot, cols_g)
        act[:, cols_g] = (a * _sigmoid(a) * conv(uv, slot, cols_v)).astype(act.dtype)

    f = jnp.dot(act[...], wd_ref[...], preferred_element_type=F32)
    o_ref[...] = _layer_norm(DN_ALPHA * h_ref[...] + f, g_ref[...], b_ref[...])


def _ffn(h, w_up, conv_w, w_down, g, b, seq):
    m = h.shape[0]
    tm = ROW_TILE
    halo = 16
    resident = lambda shape: pl.BlockSpec(shape, lambda i: (0, 0), pipeline_mode=pl.Buffered(1))
    in_specs = [pl.BlockSpec((tm, D_MODEL), lambda i: (i, 0)),
                pl.BlockSpec((halo, D_MODEL), lambda i: (jnp.maximum(i * (tm // halo) - 1, 0), 0)),
                resident(w_up.shape), resident(conv_w.shape), resident(w_down.shape),
                resident(g.shape), resident(b.shape)]
    return pl.pallas_call(
        functools.partial(_ffn_kernel, seq=seq, tm=tm, halo=halo), grid=(m // tm,),
        in_specs=in_specs, out_specs=pl.BlockSpec((tm, D_MODEL), lambda i: (i, 0)),
        out_shape=jax.ShapeDtypeStruct((m, D_MODEL), F32),
        scratch_shapes=[pltpu.VMEM((tm + halo, D_MODEL), BF16),
                        pltpu.VMEM((2, tm + halo, FFN_CHUNK), F32), pltpu.VMEM((2, tm + halo, FFN_CHUNK), F32),
                        pltpu.VMEM((tm, D_FF), BF16)],
        compiler_params=_cparams(("parallel",)), name="conv_ffn_ln",
    )(h, h, w_up, conv_w, w_down, g, b)


def _compress_weights(w1):
    half = CMP_BLOCK // 2
    w1r = w1.reshape(2, half, NSA_DH, CMP_HIDDEN)
    eye = jnp.eye(NSA_KV_HEADS, dtype=w1.dtype)
    ex = jnp.einsum("aidh,gk->agikdh", w1r, eye)
    ex = ex.reshape(2 * NSA_KV_HEADS, half * NSA_KVW, CMP_HIDDEN)
    return jnp.concatenate([ex[n] for n in range(2 * NSA_KV_HEADS)], axis=1)


def _overlap_matrix(seq):
    n_cmp = (seq - CMP_BLOCK) // CMP_STRIDE + 1
    n_sel = seq // SEL_BLOCK
    starts = np.arange(n_cmp) * CMP_STRIDE
    jb = np.arange(n_sel) * SEL_BLOCK
    ov = ((starts[:, None] < jb[None] + SEL_BLOCK) & (starts[:, None] + CMP_BLOCK > jb[None])).astype(np.float32)
    out = np.zeros((n_sel, seq // CMP_STRIDE), np.float32)
    out[:, :n_cmp] = ov.T
    return jnp.asarray(out, BF16)


def kernel(x, w_in, gdn_conv_w, gdn_a_log, gdn_dt_bias, gdn_norm_w, cmp_pos_k, cmp_w1_k, cmp_w2_k,
           cmp_pos_v, cmp_w1_v, cmp_w2_v, w_branch_gdn, w_branch_nsa, w_out, ln1_g, ln1_b, w_up,
           ffn_conv_w, w_down, ln2_g, ln2_b):
    bsz, seq, _ = x.shape
    m = bsz * seq
    for i in range(DEPTH):
        x2d = x.reshape(m, D_MODEL)
        w = w_in[i]
        o_gdn = 2 * GDN_QK + 2 * GDN_VW
        o_nq = o_gdn + 2 * GDN_HEADS
        o_kv = o_nq + NSA_QW
        o_gate = o_kv + 6 * NSA_KVW
        o_mix = o_gate + 3 * NSA_HEADS
        kv = [w[:, o_kv + n * NSA_KVW:o_kv + (n + 1) * NSA_KVW] for n in range(6)]
        small_w = jnp.concatenate([w[:, o_gdn:o_nq], w[:, o_gate:o_mix]], axis=1)
        small_w = jnp.pad(small_w, ((0, 0), (0, LANE - small_w.shape[1])))
        weights = [w[:, :o_gdn],
                   w[:, o_nq:o_kv],
                   w[:, o_kv + 2 * NSA_KVW:o_gate],
                   kv[0], kv[1],
                   w[:, o_mix:],
                   small_w]
        weights = [t.astype(BF16) for t in weights]
        widths = [o_gdn, NSA_HEADS * HEAD_PAD, 4 * NSA_KV_HEADS * HEAD_PAD, NSA_KVW, NSA_KVW, 2 * D_MODEL, LANE]
        scales = [1.0, NSA_DH ** -0.5 * LOG2E, 1.0, 1.0, 1.0, 1.0, 1.0]
        gqkvz, nq, nkv, kcp, vcp, mix, small, grow = _inproj(x2d, weights, [BF16] * 6 + [F32], widths, scales,
                                                             2 * GDN_HEADS)

        alog = jnp.broadcast_to(gdn_a_log[i][:, None, None], (GDN_HEADS, 1, LANE))
        dtb = jnp.broadcast_to(gdn_dt_bias[i][:, None, None], (GDN_HEADS, 1, LANE))
        o_a = _gdn(gqkvz, gdn_conv_w[i], small, grow, alog, dtb, gdn_norm_w[i][None, :], bsz, seq)

        grp = seq // CMP_STRIDE
        k16 = kcp.reshape(bsz, grp, CMP_STRIDE * NSA_KVW)
        v16 = vcp.reshape(bsz, grp, CMP_STRIDE * NSA_KVW)
        pos_rows = lambda p: jnp.pad(p.reshape(1, CMP_BLOCK * NSA_DH), ((0, 7), (0, 0))).astype(BF16)
        w2_pad = lambda t: jnp.pad(t, ((0, 0), (0, HEAD_PAD - NSA_DH))).astype(BF16)
        kc, vc = _compress(k16, v16,
                           _compress_weights(cmp_w1_k[i]).astype(BF16), _compress_weights(cmp_w1_v[i]).astype(BF16),
                           cmp_w1_k[i].astype(BF16), cmp_w1_v[i].astype(BF16),
                           pos_rows(cmp_pos_k[i]), pos_rows(cmp_pos_v[i]),
                           w2_pad(cmp_w2_k[i]), w2_pad(cmp_w2_v[i]))
        o_b = _nsa(nq, nkv, kc, vc, small, _overlap_matrix(seq), bsz, seq)

        h = _merge(o_a, o_b, mix, x2d, w_branch_gdn[i].astype(BF16), w_branch_nsa[i].astype(BF16), w_out[i].astype(BF16),
                   ln1_g[i][None, :], ln1_b[i][None, :])

        x = _ffn(h, w_up[i].astype(BF16), ffn_conv_w[i], w_down[i].astype(BF16),
                 ln2_g[i][None, :], ln2_b[i][None, :], seq).reshape(bsz, seq, D_MODEL)
    return x
```

```python
import functools
import math

import numpy as np
import jax
import jax.numpy as jnp
from jax import lax
from jax.experimental import pallas as pl
from jax.experimental.pallas import tpu as pltpu

F32 = jnp.float32
BF16 = jnp.bfloat16

D_MODEL = 1024
GDN_HEADS = 4
GDN_DK = 128
GDN_DV = 128
GDN_CONV = 4
GDN_CHUNK = 64
NSA_HEADS = 8
NSA_KV_HEADS = 2
NSA_REP = NSA_HEADS // NSA_KV_HEADS
NSA_DH = 64
CMP_BLOCK = 32
CMP_STRIDE = 16
CMP_HIDDEN = 256
SEL_BLOCK = 64
SEL_SHIFT = 6
SEL_TOPK = 16
WINDOW = 512
FORCE_SCORE = 1e9
D_FF = 2816
FFN_CONV = 3
DEPTH = 1
DN_ALPHA = (2 * DEPTH) ** 0.25
LN_EPS = 1e-5
RMS_EPS = 1e-6
LOG2E = math.log2(math.e)

GDN_QK = GDN_HEADS * GDN_DK
GDN_VW = GDN_HEADS * GDN_DV
NSA_QW = NSA_HEADS * NSA_DH
NSA_KVW = NSA_KV_HEADS * NSA_DH

LANE = 128
HEAD_PAD = LANE
NEG_BIG = -1e30
VMEM_LIMIT = 56 * 1024 * 1024

Q_BLOCK = 128
NSA_Q_SUB = 2
SEL_KEY_BLOCK = 256
FFN_CHUNK = 256
ROW_TILE = 512
GDN_CHUNK_SHIFT = 6
GDN_PAIR = 2 * GDN_CHUNK
GDN_UNROLL = 16


def _cparams(sem):
    return pltpu.CompilerParams(dimension_semantics=sem, vmem_limit_bytes=VMEM_LIMIT)


def _dot(a, b):
    return jnp.dot(a.astype(BF16), b.astype(BF16), preferred_element_type=F32)


def _dot_nt(a, b):
    return lax.dot_general(a.astype(BF16), b.astype(BF16), (((1,), (1,)), ((), ())),
                           preferred_element_type=F32)


def _dot_tn(a, b):
    return lax.dot_general(a.astype(BF16), b.astype(BF16), (((0,), (0,)), ((), ())),
                           preferred_element_type=F32)


def _sigmoid(x):
    return jax.nn.sigmoid(x)


def _softplus(x):
    return jnp.maximum(x, 0.0) + jnp.log1p(jnp.exp(-jnp.abs(x)))


def _layer_norm(x, g, b):
    mu = jnp.mean(x, axis=-1, keepdims=True)
    xc = x - mu
    var = jnp.mean(xc * xc, axis=-1, keepdims=True)
    return xc * lax.rsqrt(var + LN_EPS) * g + b


def _full_spec(shape):
    nd = len(shape)
    return pl.BlockSpec(shape, lambda *_: (0,) * nd, pipeline_mode=pl.Buffered(1))


def _inproj_kernel(x_ref, *refs, t_rows, scales):
    n = (len(refs) - 1) // 2
    xb = x_ref[...].astype(BF16)
    for w_ref, o_ref, scale in zip(refs[:n], refs[n:2 * n], scales):
        res = jnp.dot(xb, w_ref[...], preferred_element_type=F32)
        if scale != 1.0:
            res = res * scale
        if o_ref.shape[1] == w_ref.shape[1]:
            o_ref[...] = res.astype(o_ref.dtype)
        else:
            out = res.astype(o_ref.dtype)
            zeros = jnp.zeros((out.shape[0], HEAD_PAD - NSA_DH), o_ref.dtype)
            for hd in range(w_ref.shape[1] // NSA_DH):
                o_ref[:, hd * HEAD_PAD:hd * HEAD_PAD + NSA_DH] = out[:, hd * NSA_DH:(hd + 1) * NSA_DH]
                o_ref[:, hd * HEAD_PAD + NSA_DH:(hd + 1) * HEAD_PAD] = zeros
    t_ref = refs[2 * n]
    for j in range(t_ref.shape[0]):
        t_ref[j] = jnp.transpose(res[j * LANE:(j + 1) * LANE, :])[0:t_rows, :]


def _inproj(x2d, weights, out_dtypes, out_widths, scales, t_rows):
    assert weights[-1].shape[1] == LANE and out_dtypes[-1] == F32
    m = x2d.shape[0]
    tm = 2 * ROW_TILE
    in_specs = [pl.BlockSpec((tm, D_MODEL), lambda i: (i, 0))]
    in_specs += [_full_spec(w.shape) for w in weights]
    out_specs = [pl.BlockSpec((tm, n), lambda i: (i, 0)) for n in out_widths]
    out_specs.append(pl.BlockSpec((tm // LANE, t_rows, LANE), lambda i: (i, 0, 0)))
    out_shape = [jax.ShapeDtypeStruct((m, n), dt) for n, dt in zip(out_widths, out_dtypes)]
    out_shape.append(jax.ShapeDtypeStruct((m // LANE, t_rows, LANE), F32))
    return pl.pallas_call(
        functools.partial(_inproj_kernel, t_rows=t_rows, scales=tuple(scales)), grid=(m // tm,),
        in_specs=in_specs, out_specs=out_specs,
        out_shape=out_shape, compiler_params=_cparams(("parallel",)), name="inproj")(x2d, *weights)


def _gdn_kernel(q_ref, k_ref, v_ref, z_ref, cwq_ref, cwk_ref, cwv_ref, gcol_ref, grow_ref,
                alog_ref, dtb_ref, nw_ref, o_ref,
                xpad, qh, kh, vh, xp, qk_s, rhs, kd, qd, gl, sol, mq, ns, o0, st, *, seq):
    c = GDN_CHUNK
    pr = GDN_PAIR
    npair = seq // pr
    head = pl.program_id(1)
    pad = 8
    conv_rows = 256
    xpad[0:pad, :] = jnp.zeros((pad, 3 * LANE), F32)
    xpad[pad:, 0:LANE] = q_ref[...].astype(F32)
    xpad[pad:, LANE:2 * LANE] = k_ref[...].astype(F32)
    xpad[pad:, 2 * LANE:3 * LANE] = v_ref[...].astype(F32)
    cw = jnp.concatenate([cwq_ref[...], cwk_ref[...], cwv_ref[...]], axis=1)

    for t0 in range(0, seq, conv_rows):
        acc = jnp.zeros((conv_rows, 3 * LANE), F32)
        for j in range(GDN_CONV):
            xs = xpad[t0 + pad - (GDN_CONV - 1) + j:t0 + pad - (GDN_CONV - 1) + j + conv_rows, :]
            acc = acc + xs * cw[j:j + 1, :]
        y = acc * _sigmoid(acc)
        yq, yk, yv = y[:, 0:LANE], y[:, LANE:2 * LANE], y[:, 2 * LANE:3 * LANE]
        yq = yq * lax.rsqrt(jnp.sum(yq * yq, axis=-1, keepdims=True) + RMS_EPS) * (GDN_DK ** -0.5)
        yk = yk * lax.rsqrt(jnp.sum(yk * yk, axis=-1, keepdims=True) + RMS_EPS)
        qh[t0:t0 + conv_rows, :] = yq
        kh[t0:t0 + conv_rows, :] = yk
        vh[t0:t0 + conv_rows, :] = yv

    neg_a = -jnp.exp(alog_ref[0, :, 0:1])
    dtb = dtb_ref[0, :, 0:1]
    ii = lax.broadcasted_iota(jnp.int32, (pr, pr), 0)
    jj = lax.broadcasted_iota(jnp.int32, (pr, pr), 1)
    same = jnp.right_shift(ii, GDN_CHUNK_SHIFT) == jnp.right_shift(jj, GDN_CHUNK_SHIFT)
    incl = same & (jj <= ii)
    strict = same & (jj < ii)
    upper = same & (ii <= jj)
    eye = (ii == jj).astype(F32)
    sub8 = lax.broadcasted_iota(jnp.int32, (8, LANE), 0)
    lane_id = lax.broadcasted_iota(jnp.int32, (1, LANE), 1)

    def over_pairs(body):
        def run(gi, carry):
            for uu in range(GDN_UNROLL):
                body(gi * GDN_UNROLL + uu)
            return carry
        lax.fori_loop(0, npair // GDN_UNROLL, run, 0)

    def stage_gates(p):
        rows = pl.ds(pl.multiple_of(p * pr, pr), pr)
        q = qh[rows, :]
        k = kh[rows, :]
        v = vh[rows, :]
        gcol = gcol_ref[rows, :]
        b_logit = jnp.sum(jnp.where(lane_id == head, gcol, 0.0), axis=1, keepdims=True)
        a_logit = jnp.sum(jnp.where(lane_id == GDN_HEADS + head, gcol, 0.0), axis=1, keepdims=True)
        a_row = jnp.sum(jnp.where(sub8[:, 0:1] == GDN_HEADS + head, grow_ref[p], 0.0), axis=0, keepdims=True)
        beta = _sigmoid(b_logit)
        g_c = neg_a * _softplus(a_logit + dtb)
        g_r = neg_a * _softplus(a_row + dtb)
        gc_col = jnp.sum(jnp.where(incl, g_r, 0.0), axis=1, keepdims=True)
        gc_row = jnp.sum(jnp.where(upper, g_c, 0.0), axis=0, keepdims=True)
        gt_col = jnp.sum(jnp.where(same, g_r, 0.0), axis=1, keepdims=True)
        decay = jnp.where(incl, jnp.exp(jnp.where(incl, gc_col - gc_row, 0.0)), 0.0)
        kq_k = _dot_nt(jnp.concatenate([k, q], axis=0), k)
        a_mat = jnp.where(strict, kq_k[0:pr] * decay * beta, 0.0)
        xp[p, 0:pr, :] = eye - a_mat
        xp[p, pr:2 * pr, :] = _dot(a_mat, a_mat)
        qk_s[p] = (kq_k[pr:2 * pr] * decay).astype(qk_s.dtype)
        e_gc = jnp.exp(gc_col)
        rhs[rows, 0:GDN_DV] = (v * beta).astype(rhs.dtype)
        rhs[rows, GDN_DV:GDN_DV + GDN_DK] = (k * (beta * e_gc)).astype(rhs.dtype)
        kd[rows, :] = (k * jnp.exp(gt_col - gc_col)).astype(kd.dtype)
        qd[rows, :] = q * e_gc
        g_last = jnp.exp(gt_col)
        gl[p] = jnp.where(sub8 == 0, g_last[0:1, :], g_last[c:c + 1, :])

    def stage_double(p):
        y = _dot(xp[p], xp[p, pr:2 * pr, :])
        xp[p, 0:pr, :] = xp[p, 0:pr, :] + y[0:pr]
        xp[p, pr:2 * pr, :] = y[pr:2 * pr]

    def stage_solve(p):
        rows = pl.ds(pl.multiple_of(p * pr, pr), pr)
        x_inv = xp[p, 0:pr, :]
        x_inv = x_inv + _dot(x_inv, xp[p, pr:2 * pr, :])
        sol[rows, :] = _dot(x_inv, rhs[rows, :])

    def stage_affine(p):
        r0 = pl.multiple_of(p * pr, pr)
        rows = pl.ds(r0, pr)
        e = _dot(qk_s[p], sol[rows, :])
        qe = qd[rows, :] - e[:, GDN_DV:GDN_DV + GDN_DK]
        g_rows = gl[p]
        for cc in range(pr // c):
            rc = pl.ds(r0 + cc * c, c)
            mn = _dot_tn(kd[rc, :], sol[rc, :])
            idx = p * (pr // c) + cc
            mq[head, idx, 0:GDN_DK, :] = (eye * g_rows[cc:cc + 1, :] - mn[:, GDN_DV:GDN_DV + GDN_DK]).astype(mq.dtype)
            mq[head, idx, GDN_DK:GDN_DK + c, :] = qe[cc * c:(cc + 1) * c].astype(mq.dtype)
            ns[head, idx] = mn[:, 0:GDN_DV]
            o0[head, idx] = e[cc * c:(cc + 1) * c, 0:GDN_DV]

    over_pairs(stage_gates)
    for _ in range(int(math.log2(c)) - 2):
        over_pairs(stage_double)
    over_pairs(stage_solve)
    over_pairs(stage_affine)

    @pl.when(head == GDN_HEADS - 1)
    def _():
        st[...] = jnp.zeros_like(st)
        nw = nw_ref[...]

        def step(ci, carry):
            r0 = pl.multiple_of(ci * c, c)
            for h in range(GDN_HEADS):
                cols = slice(h * LANE, (h + 1) * LANE)
                ys = jnp.dot(mq[h, ci], st[h], preferred_element_type=F32)
                st[h] = (ys[0:GDN_DK] + ns[h, ci]).astype(st.dtype)
                o = ys[GDN_DK:GDN_DK + c] + o0[h, ci]
                o = o * lax.rsqrt(jnp.mean(o * o, axis=-1, keepdims=True) + RMS_EPS) * nw
                z = z_ref[pl.ds(r0, c), cols].astype(F32)
                o_ref[pl.ds(r0, c), cols] = (o * (z * _sigmoid(z))).astype(o_ref.dtype)
            return carry

        lax.fori_loop(0, seq // c, step, 0)


def _gdn(gqkvz, conv_w, gcol, grow, alog, dtb, nw, bsz, seq):
    assert GDN_DK == GDN_PAIR == GDN_DV == LANE
    h = GDN_HEADS
    npair = seq // GDN_PAIR
    nchunk = seq // GDN_CHUNK
    blk = lambda off: pl.BlockSpec((seq, LANE), lambda b, hh: (b, off + hh))
    cwb = lambda off: pl.BlockSpec((GDN_CONV, LANE), lambda b, hh: (0, off + hh))
    in_specs = [blk(0), blk(h), blk(2 * h),
                pl.BlockSpec((seq, GDN_VW), lambda b, hh: (b, 3)),
                cwb(0), cwb(h), cwb(2 * h),
                pl.BlockSpec((seq, LANE), lambda b, hh: (b, 0)),
                pl.BlockSpec((npair, 8, GDN_PAIR), lambda b, hh: (b, 0, 0)),
                pl.BlockSpec((1, 1, LANE), lambda b, hh: (hh, 0, 0)),
                pl.BlockSpec((1, 1, LANE), lambda b, hh: (hh, 0, 0)),
                pl.BlockSpec((1, LANE), lambda b, hh: (0, 0))]
    scratch = [pltpu.VMEM((seq + 8, 3 * LANE), F32),
               pltpu.VMEM((seq, LANE), F32), pltpu.VMEM((seq, LANE), F32), pltpu.VMEM((seq, LANE), F32),
               pltpu.VMEM((npair, 2 * GDN_PAIR, GDN_PAIR), F32),
               pltpu.VMEM((npair, GDN_PAIR, GDN_PAIR), BF16),
               pltpu.VMEM((seq, GDN_DV + GDN_DK), BF16),
               pltpu.VMEM((seq, GDN_DK), BF16),
               pltpu.VMEM((seq, GDN_DK), F32),
               pltpu.VMEM((npair, 8, LANE), F32),
               pltpu.VMEM((seq, GDN_DV + GDN_DK), F32),
               pltpu.VMEM((h, nchunk, GDN_DK + GDN_CHUNK, GDN_DK), BF16),
               pltpu.VMEM((h, nchunk, GDN_DK, GDN_DV), F32),
               pltpu.VMEM((h, nchunk, GDN_CHUNK, GDN_DV), F32),
               pltpu.VMEM((h, GDN_DK, GDN_DV), BF16)]
    return pl.pallas_call(
        functools.partial(_gdn_kernel, seq=seq), grid=(bsz, h), in_specs=in_specs,
        out_specs=pl.BlockSpec((seq, GDN_VW), lambda b, hh: (b, 0)),
        out_shape=jax.ShapeDtypeStruct((bsz * seq, GDN_VW), BF16),
        scratch_shapes=scratch,
        compiler_params=_cparams(("parallel", "arbitrary")), name="gdn",
    )(gqkvz, gqkvz, gqkvz, gqkvz, conv_w, conv_w, conv_w, gcol, grow, alog, dtb, nw)


def _gelu_tanh(x):
    return 0.5 * x * (1.0 + jnp.tanh(math.sqrt(2.0 / math.pi) * (x + 0.044715 * x * x * x)))


def _compress_kernel(k16_ref, v16_ref, wkc_ref, wvc_ref, w1k_ref, w1v_ref, pk_ref, pv_ref,
                     w2k_ref, w2v_ref, kc_ref, vc_ref):
    nblk = k16_ref.shape[1]
    for x_ref, wc_ref, w1_ref, p_ref, w2_ref, o_ref in (
            (k16_ref, wkc_ref, w1k_ref, pk_ref, w2k_ref, kc_ref),
            (v16_ref, wvc_ref, w1v_ref, pv_ref, w2v_ref, vc_ref)):
        y = jnp.dot(x_ref[0], wc_ref[...], preferred_element_type=F32)
        pos_term = jnp.dot(p_ref[...], w1_ref[...], preferred_element_type=F32)[0:1, :]
        for g in range(NSA_KV_HEADS):
            top = y[:, g * CMP_HIDDEN:(g + 1) * CMP_HIDDEN]
            bot = y[:, (NSA_KV_HEADS + g) * CMP_HIDDEN:(NSA_KV_HEADS + g + 1) * CMP_HIDDEN]
            hid = top + pltpu.roll(bot, nblk - 1, 0) + pos_term
            o_ref[0, g] = _dot(_gelu_tanh(hid), w2_ref[...]).astype(o_ref.dtype)


def _compress(k16, v16, wkc, wvc, w1k, w1v, pk, pv, w2k, w2v):
    bsz, nblk, width = k16.shape
    in_specs = [pl.BlockSpec((1, nblk, width), lambda b: (b, 0, 0))] * 2
    in_specs += [_full_spec(a.shape) for a in (wkc, wvc, w1k, w1v, pk, pv, w2k, w2v)]
    out_spec = pl.BlockSpec((1, NSA_KV_HEADS, nblk, HEAD_PAD), lambda b: (b, 0, 0, 0))
    out_shape = jax.ShapeDtypeStruct((bsz, NSA_KV_HEADS, nblk, HEAD_PAD), BF16)
    return pl.pallas_call(
        _compress_kernel, grid=(bsz,), in_specs=in_specs, out_specs=[out_spec, out_spec],
        out_shape=[out_shape, out_shape], compiler_params=_cparams(("parallel",)), name="nsa_compress",
    )(k16, v16, wkc, wvc, w1k, w1v, pk, pv, w2k, w2v)


def _nsa_kernel(q_ref, kv_ref, kc_ref, vc_ref, sm_ref, ovt_ref, o_ref, ksa, vst, vwa, *, seq):
    step = pl.program_id(1)
    n_cmp_pad = kc_ref.shape[2]
    n_sel_blk = seq // SEL_BLOCK
    kb_rows = SEL_KEY_BLOCK
    one_row = NSA_DH
    groups = range(NSA_KV_HEADS)
    units = [(sub, g) for sub in range(NSA_Q_SUB) for g in groups]

    @pl.when(step == 0)
    def _():
        lane = lax.broadcasted_iota(jnp.int32, (1, NSA_KV_HEADS * HEAD_PAD), 1) & (HEAD_PAD - 1)
        ones_lane = (lane == one_row).astype(vwa.dtype)
        row_t = lax.broadcasted_iota(jnp.int32, (HEAD_PAD, 1), 0)

        def fill(bi, carry):
            r0 = pl.multiple_of(bi * kb_rows, kb_rows)
            rows = pl.ds(r0, kb_rows)
            blk = jnp.right_shift(r0 + lax.broadcasted_iota(jnp.int32, (kb_rows, 1), 0), SEL_SHIFT)
            ksa[rows, :] = kv_ref[rows, 0:2 * HEAD_PAD] + (lane == NSA_DH + blk).astype(ksa.dtype)
            vwa[rows, :] = kv_ref[rows, 6 * HEAD_PAD:8 * HEAD_PAD] + ones_lane
            for g in range(NSA_KV_HEADS):
                v_t = jnp.transpose(kv_ref[rows, (2 + g) * HEAD_PAD:(3 + g) * HEAD_PAD].astype(F32))
                vst[bi, g] = jnp.where(row_t == one_row, 1.0, v_t).astype(vst.dtype)
            return carry

        lax.fori_loop(0, seq // kb_rows, fill, 0)

    tok = lax.broadcasted_iota(jnp.int32, (1, Q_BLOCK), 1)
    lane_q = lax.broadcasted_iota(jnp.int32, (1, HEAD_PAD), 1)
    pen_lanes = (lane_q >= NSA_DH) & (lane_q < NSA_DH + n_sel_blk)
    place = (lax.broadcasted_iota(jnp.int32, (n_sel_blk, HEAD_PAD), 1)
             == NSA_DH + lax.broadcasted_iota(jnp.int32, (n_sel_blk, HEAD_PAD), 0)).astype(BF16)
    n_sub = lax.broadcasted_iota(jnp.int32, (n_cmp_pad, 1), 0)

    def add_bias(s, bias):
        return jnp.concatenate([s[:, r * Q_BLOCK:(r + 1) * Q_BLOCK] + bias for r in range(s.shape[1] // Q_BLOCK)],
                               axis=1)

    q0_s, t_lane_s, gates_s, qp, o_cmp = [], [], [], {}, {}
    for sub in range(NSA_Q_SUB):
        q0 = (step * NSA_Q_SUB + sub) * Q_BLOCK
        tok_rows = slice(sub * Q_BLOCK, (sub + 1) * Q_BLOCK)
        t_lane = q0 + tok
        q0_s.append(q0)
        t_lane_s.append(t_lane)
        gates_s.append(jnp.transpose(_sigmoid(sm_ref[tok_rows, :])))

        cmp_bias = jnp.where((n_sub * CMP_STRIDE + (CMP_BLOCK - 1)) <= t_lane, 0.0, NEG_BIG)
        cmp_any = jnp.concatenate([(t_lane >= CMP_BLOCK - 1).astype(F32)] * NSA_REP, axis=1)

        for g in groups:
            qs = jnp.concatenate(
                [q_ref[tok_rows, (g * NSA_REP + r) * HEAD_PAD:(g * NSA_REP + r + 1) * HEAD_PAD]
                 for r in range(NSA_REP)], axis=0)

            s_c = add_bias(_dot_nt(kc_ref[0, g], qs), cmp_bias)
            e_c = jnp.exp2(s_c - jnp.max(s_c, axis=0, keepdims=True))
            p_c = e_c * (cmp_any / jnp.sum(e_c, axis=0, keepdims=True))
            o_cmp[sub, g] = _dot_tn(vc_ref[0, g], p_c)[0:NSA_DH]

            p_sum = p_c[:, 0:Q_BLOCK]
            for r in range(1, NSA_REP):
                p_sum = p_sum + p_c[:, r * Q_BLOCK:(r + 1) * Q_BLOCK]
            imp_t = _dot(ovt_ref[...], p_sum)

            def rank_select(imp_t=imp_t, t_lane=t_lane):
                j_sub = lax.broadcasted_iota(jnp.int32, (n_sel_blk, 1), 0)
                cur = jnp.right_shift(t_lane, SEL_SHIFT)
                forced = (j_sub == 0) | (j_sub == cur) | (j_sub == cur - 1)
                causal_blk = (j_sub * SEL_BLOCK) <= t_lane
                imp = jnp.where(forced, FORCE_SCORE, jnp.where(causal_blk, imp_t, -jnp.inf))
                rank = jnp.zeros((n_sel_blk, Q_BLOCK), F32)
                for jp in range(n_sel_blk):
                    row = imp[jp:jp + 1, :]
                    ahead = (row > imp) | ((row == imp) & (j_sub > jp))
                    rank = rank + ahead.astype(F32)
                return (rank < SEL_TOPK).astype(BF16)

            sel_t = lax.cond(q0 + Q_BLOCK <= SEL_TOPK * SEL_BLOCK,
                             lambda: jnp.ones((n_sel_blk, Q_BLOCK), BF16), rank_select)
            sel_lanes = _dot_tn(sel_t, place)
            pen = jnp.where(pen_lanes, (sel_lanes - 1.0) * (-NEG_BIG), 0.0).astype(BF16)
            qp[sub, g] = qs + jnp.concatenate([pen] * NSA_REP, axis=0)

    n_full = step

    def scores(kb, unit):
        k0 = pl.multiple_of(kb * kb_rows, kb_rows)
        g = unit[1]
        return _dot_nt(ksa[pl.ds(k0, kb_rows), g * HEAD_PAD:(g + 1) * HEAD_PAD], qp[unit])

    def absorb(state, s, kb, unit):
        m_i, acc = state
        m_new = jnp.maximum(m_i, jnp.max(s, axis=0, keepdims=True))
        return m_new, jnp.exp2(m_i - m_new) * acc + _dot(vst[kb, unit[1]], jnp.exp2(s - m_new))

    kpos_last = n_full * kb_rows + lax.broadcasted_iota(jnp.int32, (kb_rows, 1), 0)
    first = []
    for unit in units:
        s = add_bias(scores(n_full, unit), jnp.where(kpos_last <= t_lane_s[unit[0]], 0.0, NEG_BIG))
        m = jnp.max(s, axis=0, keepdims=True)
        first.append((m, _dot(vst[n_full, unit[1]], jnp.exp2(s - m))))

    def full_blocks(state):
        def loop_body(kb, carry):
            st, s_cur = carry
            s_next = tuple(scores(kb + 1, u) for u in units)
            return tuple(absorb(st[i], s_cur[i], kb, u) for i, u in enumerate(units)), s_next

        st, s_cur = lax.fori_loop(0, n_full - 1, loop_body, (state, tuple(scores(0, u) for u in units)))
        return tuple(absorb(st[i], s_cur[i], n_full - 1, u) for i, u in enumerate(units))

    sel_out = lax.cond(n_full > 0, full_blocks, lambda st: st, tuple(first))

    span = WINDOW + Q_BLOCK
    for i, (sub, g) in enumerate(units):
        q0, t_lane = q0_s[sub], t_lane_s[sub]
        tok_rows = slice(sub * Q_BLOCK, (sub + 1) * Q_BLOCK)
        w0 = pl.multiple_of(jnp.maximum(q0 - WINDOW, 0), Q_BLOCK)
        kpos_w = w0 + lax.broadcasted_iota(jnp.int32, (span, 1), 0)
        win_bias = jnp.where((kpos_w <= t_lane) & (kpos_w > t_lane - WINDOW), 0.0, NEG_BIG)

        acc_s = sel_out[i][1]
        o_slc = acc_s[0:NSA_DH] * (1.0 / acc_s[one_row:one_row + 1, :])

        k_w = kv_ref[pl.ds(w0, span), (4 + g) * HEAD_PAD:(5 + g) * HEAD_PAD]
        s_w = add_bias(_dot_nt(k_w, qp[sub, g]), win_bias)
        acc_w = _dot_tn(vwa[pl.ds(w0, span), g * HEAD_PAD:(g + 1) * HEAD_PAD],
                        jnp.exp2(s_w - jnp.max(s_w, axis=0, keepdims=True)))
        o_win = acc_w[0:NSA_DH] * (1.0 / acc_w[one_row:one_row + 1, :])

        gates_t = gates_s[sub]
        o_heads = []
        for r in range(NSA_REP):
            c0 = 8 + (g * NSA_REP + r) * 3
            sl = slice(r * Q_BLOCK, (r + 1) * Q_BLOCK)
            o_t = (gates_t[c0:c0 + 1, :] * o_cmp[sub, g][:, sl] + gates_t[c0 + 1:c0 + 2, :] * o_slc[:, sl]
                   + gates_t[c0 + 2:c0 + 3, :] * o_win[:, sl])
            o_heads.append(o_t)
        for pair in range(NSA_REP // 2):
            lanes = slice((g * NSA_REP + 2 * pair) * NSA_DH, (g * NSA_REP + 2 * pair + 2) * NSA_DH)
            o_ref[tok_rows, lanes] = jnp.transpose(jnp.concatenate(o_heads[2 * pair:2 * pair + 2], axis=0)
                                                   ).astype(o_ref.dtype)


def _nsa(nq, nkv, kc, vc, small, overlap, bsz, seq):
    assert NSA_Q_SUB * Q_BLOCK == SEL_KEY_BLOCK
    rows = NSA_Q_SUB * Q_BLOCK
    nstep = seq // rows
    in_specs = [pl.BlockSpec((rows, NSA_HEADS * HEAD_PAD), lambda b, i: (b * nstep + i, 0)),
                pl.BlockSpec((seq, nkv.shape[1]), lambda b, i: (b, 0)),
                pl.BlockSpec((1,) + kc.shape[1:], lambda b, i: (b, 0, 0, 0)),
                pl.BlockSpec((1,) + vc.shape[1:], lambda b, i: (b, 0, 0, 0)),
                pl.BlockSpec((rows, LANE), lambda b, i: (b * nstep + i, 0)),
                _full_spec(overlap.shape)]
    return pl.pallas_call(
        functools.partial(_nsa_kernel, seq=seq), grid=(bsz, nstep), in_specs=in_specs,
        out_specs=pl.BlockSpec((rows, NSA_QW), lambda b, i: (b * nstep + i, 0)),
        out_shape=jax.ShapeDtypeStruct((bsz * seq, NSA_QW), BF16),
        scratch_shapes=[pltpu.VMEM((seq, NSA_KV_HEADS * HEAD_PAD), BF16),
                        pltpu.VMEM((seq // SEL_KEY_BLOCK, NSA_KV_HEADS, HEAD_PAD, SEL_KEY_BLOCK), BF16),
                        pltpu.VMEM((seq, NSA_KV_HEADS * HEAD_PAD), BF16)],
        compiler_params=_cparams(("parallel", "arbitrary")), name="nsa_attention",
    )(nq, nkv, kc, vc, small, overlap)


def _merge_kernel(oa_ref, ob_ref, mix_ref, x_ref, wa_ref, wb_ref, wo_ref, g_ref, b_ref, h_ref):
    ya = jnp.dot(oa_ref[...], wa_ref[...], preferred_element_type=F32)
    yb = jnp.dot(ob_ref[...], wb_ref[...], preferred_element_type=F32)
    mix = mix_ref[...].astype(F32)
    mixed = _sigmoid(mix[:, 0:D_MODEL]) * ya + _sigmoid(mix[:, D_MODEL:2 * D_MODEL]) * yb
    y = _dot(mixed, wo_ref[...])
    h_ref[...] = _layer_norm(DN_ALPHA * x_ref[...] + y, g_ref[...], b_ref[...])


def _merge(oa, ob, mix, x2d, wa, wb, wo, g, b):
    m = x2d.shape[0]
    tm = 2 * ROW_TILE
    row = lambda n: pl.BlockSpec((tm, n), lambda i: (i, 0))
    in_specs = [row(oa.shape[1]), row(ob.shape[1]), row(mix.shape[1]), row(D_MODEL),
                _full_spec(wa.shape), _full_spec(wb.shape), _full_spec(wo.shape),
                _full_spec(g.shape), _full_spec(b.shape)]
    return pl.pallas_call(
        _merge_kernel, grid=(m // tm,), in_specs=in_specs, out_specs=row(D_MODEL),
        out_shape=jax.ShapeDtypeStruct((m, D_MODEL), F32),
        compiler_params=_cparams(("parallel",)), name="merge_ln",
    )(oa, ob, mix, x2d, wa, wb, wo, g, b)


def _ffn_kernel(h_ref, halo_ref, wup_ref, cw_ref, wd_ref, g_ref, b_ref, o_ref,
                hb, ug, uv, act, *, seq, tm, halo):
    i = pl.program_id(0)
    first = lax.rem(i * tm, seq) == 0
    hb[0:halo, :] = jnp.where(first, 0.0, halo_ref[...]).astype(BF16)
    hb[halo:, :] = h_ref[...].astype(BF16)

    def conv(u_ref, slot, cols):
        out = None
        for tap in range(FFN_CONV):
            term = cw_ref[tap:tap + 1, cols] * u_ref[slot, pl.ds(halo - (FFN_CONV - 1) + tap, tm), :]
            out = term if out is None else out + term
        return out

    for j in range(D_FF // FFN_CHUNK):
        slot = j % 2
        cols_g = slice(j * FFN_CHUNK, (j + 1) * FFN_CHUNK)
        cols_v = slice(D_FF + j * FFN_CHUNK, D_FF + (j + 1) * FFN_CHUNK)
        ug[slot] = jnp.dot(hb[...], wup_ref[:, cols_g], preferred_element_type=F32)
        uv[slot] = jnp.dot(hb[...], wup_ref[:, cols_v], preferred_element_type=F32)
        a = conv(ug, slot, cols_g)
        act[:, cols_g] = (a * _sigmoid(a) * conv(uv, slot, cols_v)).astype(act.dtype)

    f = jnp.dot(act[...], wd_ref[...], preferred_element_type=F32)
    o_ref[...] = _layer_norm(DN_ALPHA * h_ref[...] + f, g_ref[...], b_ref[...])


def _ffn(h, w_up, conv_w, w_down, g, b, seq):
    m = h.shape[0]
    tm = ROW_TILE
    halo = 16
    resident = lambda shape: pl.BlockSpec(shape, lambda i: (0, 0), pipeline_mode=pl.Buffered(1))
    in_specs = [pl.BlockSpec((tm, D_MODEL), lambda i: (i, 0)),
                pl.BlockSpec((halo, D_MODEL), lambda i: (jnp.maximum(i * (tm // halo) - 1, 0), 0)),
                resident(w_up.shape), resident(conv_w.shape), resident(w_down.shape),
                resident(g.shape), resident(b.shape)]
    return pl.pallas_call(
        functools.partial(_ffn_kernel, seq=seq, tm=tm, halo=halo), grid=(m // tm,),
        in_specs=in_specs, out_specs=pl.BlockSpec((tm, D_MODEL), lambda i: (i, 0)),
        out_shape=jax.ShapeDtypeStruct((m, D_MODEL), F32),
        scratch_shapes=[pltpu.VMEM((tm + halo, D_MODEL), BF16),
                        pltpu.VMEM((2, tm + halo, FFN_CHUNK), F32), pltpu.VMEM((2, tm + halo, FFN_CHUNK), F32),
                        pltpu.VMEM((tm, D_FF), BF16)],
        compiler_params=_cparams(("parallel",)), name="conv_ffn_ln",
    )(h, h, w_up, conv_w, w_down, g, b)


def _compress_weights(w1):
    half = CMP_BLOCK // 2
    w1r = w1.reshape(2, half, NSA_DH, CMP_HIDDEN)
    eye = jnp.eye(NSA_KV_HEADS, dtype=w1.dtype)
    ex = jnp.einsum("aidh,gk->agikdh", w1r, eye)
    ex = ex.reshape(2 * NSA_KV_HEADS, half * NSA_KVW, CMP_HIDDEN)
    return jnp.concatenate([ex[n] for n in range(2 * NSA_KV_HEADS)], axis=1)


def _overlap_matrix(seq):
    n_cmp = (seq - CMP_BLOCK) // CMP_STRIDE + 1
    n_sel = seq // SEL_BLOCK
    starts = np.arange(n_cmp) * CMP_STRIDE
    jb = np.arange(n_sel) * SEL_BLOCK
    ov = ((starts[:, None] < jb[None] + SEL_BLOCK) & (starts[:, None] + CMP_BLOCK > jb[None])).astype(np.float32)
    out = np.zeros((n_sel, seq // CMP_STRIDE), np.float32)
    out[:, :n_cmp] = ov.T
    return jnp.asarray(out, BF16)


def kernel(x, w_in, gdn_conv_w, gdn_a_log, gdn_dt_bias, gdn_norm_w, cmp_pos_k, cmp_w1_k, cmp_w2_k,
           cmp_pos_v, cmp_w1_v, cmp_w2_v, w_branch_gdn, w_branch_nsa, w_out, ln1_g, ln1_b, w_up,
           ffn_conv_w, w_down, ln2_g, ln2_b):
    bsz, seq, _ = x.shape
    m = bsz * seq
    for i in range(DEPTH):
        x2d = x.reshape(m, D_MODEL)
        w = w_in[i]
        o_gdn = 2 * GDN_QK + 2 * GDN_VW
        o_nq = o_gdn + 2 * GDN_HEADS
        o_kv = o_nq + NSA_QW
        o_gate = o_kv + 6 * NSA_KVW
        o_mix = o_gate + 3 * NSA_HEADS
        kv = [w[:, o_kv + n * NSA_KVW:o_kv + (n + 1) * NSA_KVW] for n in range(6)]
        small_w = jnp.concatenate([w[:, o_gdn:o_nq], w[:, o_gate:o_mix]], axis=1)
        small_w = jnp.pad(small_w, ((0, 0), (0, LANE - small_w.shape[1])))
        weights = [w[:, :o_gdn],
                   w[:, o_nq:o_kv],
                   w[:, o_kv + 2 * NSA_KVW:o_gate],
                   kv[0], kv[1],
                   w[:, o_mix:],
                   small_w]
        weights = [t.astype(BF16) for t in weights]
        widths = [o_gdn, NSA_HEADS * HEAD_PAD, 4 * NSA_KV_HEADS * HEAD_PAD, NSA_KVW, NSA_KVW, 2 * D_MODEL, LANE]
        scales = [1.0, NSA_DH ** -0.5 * LOG2E, 1.0, 1.0, 1.0, 1.0, 1.0]
        gqkvz, nq, nkv, kcp, vcp, mix, small, grow = _inproj(x2d, weights, [BF16] * 6 + [F32], widths, scales,
                                                             2 * GDN_HEADS)

        alog = jnp.broadcast_to(gdn_a_log[i][:, None, None], (GDN_HEADS, 1, LANE))
        dtb = jnp.broadcast_to(gdn_dt_bias[i][:, None, None], (GDN_HEADS, 1, LANE))
        o_a = _gdn(gqkvz, gdn_conv_w[i], small, grow, alog, dtb, gdn_norm_w[i][None, :], bsz, seq)

        grp = seq // CMP_STRIDE
        k16 = kcp.reshape(bsz, grp, CMP_STRIDE * NSA_KVW)
        v16 = vcp.reshape(bsz, grp, CMP_STRIDE * NSA_KVW)
        pos_rows = lambda p: jnp.pad(p.reshape(1, CMP_BLOCK * NSA_DH), ((0, 7), (0, 0))).astype(BF16)
        w2_pad = lambda t: jnp.pad(t, ((0, 0), (0, HEAD_PAD - NSA_DH))).astype(BF16)
        kc, vc = _compress(k16, v16,
                           _compress_weights(cmp_w1_k[i]).astype(BF16), _compress_weights(cmp_w1_v[i]).astype(BF16),
                           cmp_w1_k[i].astype(BF16), cmp_w1_v[i].astype(BF16),
                           pos_rows(cmp_pos_k[i]), pos_rows(cmp_pos_v[i]),
                           w2_pad(cmp_w2_k[i]), w2_pad(cmp_w2_v[i]))
        o_b = _nsa(nq, nkv, kc, vc, small, _overlap_matrix(seq), bsz, seq)

        h = _merge(o_a, o_b, mix, x2d, w_branch_gdn[i].astype(BF16), w_branch_nsa[i].astype(BF16), w_out[i].astype(BF16),
                   ln1_g[i][None, :], ln1_b[i][None, :])

        x = _ffn(h, w_up[i].astype(BF16), ffn_conv_w[i], w_down[i].astype(BF16),
                 ln2_g[i][None, :], ln2_b[i][None, :], seq).reshape(bsz, seq, D_MODEL)
    return x
```

```python
import functools
import math

import numpy as np
import jax
import jax.numpy as jnp
from jax import lax
from jax.experimental import pallas as pl
from jax.experimental.pallas import tpu as pltpu

F32 = jnp.float32
BF16 = jnp.bfloat16

D_MODEL = 1024
GDN_HEADS = 4
GDN_DK = 128
GDN_DV = 128
GDN_CONV = 4
GDN_CHUNK = 64
NSA_HEADS = 8
NSA_KV_HEADS = 2
NSA_REP = NSA_HEADS // NSA_KV_HEADS
NSA_DH = 64
CMP_BLOCK = 32
CMP_STRIDE = 16
CMP_HIDDEN = 256
SEL_BLOCK = 64
SEL_SHIFT = 6
SEL_TOPK = 16
WINDOW = 512
FORCE_SCORE = 1e9
D_FF = 2816
FFN_CONV = 3
DEPTH = 1
DN_ALPHA = (2 * DEPTH) ** 0.25
LN_EPS = 1e-5
RMS_EPS = 1e-6
LOG2E = math.log2(math.e)

GDN_QK = GDN_HEADS * GDN_DK
GDN_VW = GDN_HEADS * GDN_DV
NSA_QW = NSA_HEADS * NSA_DH
NSA_KVW = NSA_KV_HEADS * NSA_DH

LANE = 128
HEAD_PAD = LANE
NEG_BIG = -1e30
VMEM_LIMIT = 56 * 1024 * 1024

Q_BLOCK = 128
NSA_Q_SUB = 2
SEL_KEY_BLOCK = 256
FFN_CHUNK = 256
FFN_SLOTS = 2
ROW_TILE = 512
GDN_CHUNK_SHIFT = 6
GDN_PAIR = 2 * GDN_CHUNK
GDN_UNROLL = 16


def _cparams(sem):
    return pltpu.CompilerParams(dimension_semantics=sem, vmem_limit_bytes=VMEM_LIMIT)


def _dot(a, b):
    return jnp.dot(a.astype(BF16), b.astype(BF16), preferred_element_type=F32)


def _dot_nt(a, b):
    return lax.dot_general(a.astype(BF16), b.astype(BF16), (((1,), (1,)), ((), ())),
                           preferred_element_type=F32)


def _dot_tn(a, b):
    return lax.dot_general(a.astype(BF16), b.astype(BF16), (((0,), (0,)), ((), ())),
                           preferred_element_type=F32)


def _sigmoid(x):
    return jax.nn.sigmoid(x)


def _softplus(x):
    return jnp.maximum(x, 0.0) + jnp.log1p(jnp.exp(-jnp.abs(x)))


def _layer_norm(x, g, b):
    mu = jnp.mean(x, axis=-1, keepdims=True)
    xc = x - mu
    var = jnp.mean(xc * xc, axis=-1, keepdims=True)
    return xc * lax.rsqrt(var + LN_EPS) * g + b


def _full_spec(shape):
    nd = len(shape)
    return pl.BlockSpec(shape, lambda *_: (0,) * nd, pipeline_mode=pl.Buffered(1))


def _inproj_kernel(x_ref, *refs, t_rows, scales):
    n = (len(refs) - 1) // 2
    xb = x_ref[...].astype(BF16)
    for w_ref, o_ref, scale in zip(refs[:n], refs[n:2 * n], scales):
        res = jnp.dot(xb, w_ref[...], preferred_element_type=F32)
        if scale != 1.0:
            res = res * scale
        if o_ref.shape[1] == w_ref.shape[1]:
            o_ref[...] = res.astype(o_ref.dtype)
        else:
            out = res.astype(o_ref.dtype)
            zeros = jnp.zeros((out.shape[0], HEAD_PAD - NSA_DH), o_ref.dtype)
            for hd in range(w_ref.shape[1] // NSA_DH):
                o_ref[:, hd * HEAD_PAD:hd * HEAD_PAD + NSA_DH] = out[:, hd * NSA_DH:(hd + 1) * NSA_DH]
                o_ref[:, hd * HEAD_PAD + NSA_DH:(hd + 1) * HEAD_PAD] = zeros
    t_ref = refs[2 * n]
    for j in range(t_ref.shape[0]):
        t_ref[j] = jnp.transpose(res[j * LANE:(j + 1) * LANE, :])[0:t_rows, :]


def _inproj(x2d, weights, out_dtypes, out_widths, scales, t_rows):
    assert weights[-1].shape[1] == LANE and out_dtypes[-1] == F32
    m = x2d.shape[0]
    tm = 2 * ROW_TILE
    in_specs = [pl.BlockSpec((tm, D_MODEL), lambda i: (i, 0))]
    in_specs += [_full_spec(w.shape) for w in weights]
    out_specs = [pl.BlockSpec((tm, n), lambda i: (i, 0)) for n in out_widths]
    out_specs.append(pl.BlockSpec((tm // LANE, t_rows, LANE), lambda i: (i, 0, 0)))
    out_shape = [jax.ShapeDtypeStruct((m, n), dt) for n, dt in zip(out_widths, out_dtypes)]
    out_shape.append(jax.ShapeDtypeStruct((m // LANE, t_rows, LANE), F32))
    return pl.pallas_call(
        functools.partial(_inproj_kernel, t_rows=t_rows, scales=tuple(scales)), grid=(m // tm,),
        in_specs=in_specs, out_specs=out_specs,
        out_shape=out_shape, compiler_params=_cparams(("parallel",)), name="inproj")(x2d, *weights)


def _gdn_kernel(q_ref, k_ref, v_ref, z_ref, cwq_ref, cwk_ref, cwv_ref, gcol_ref, grow_ref,
                alog_ref, dtb_ref, nw_ref, o_ref,
                xpad, qh, kh, vh, xp, qk_s, rhs, kd, qd, gl, sol, mq, ns, o0, st, *, seq):
    c = GDN_CHUNK
    pr = GDN_PAIR
    npair = seq // pr
    head = pl.program_id(1)
    pad = 8
    conv_rows = 256
    xpad[0:pad, :] = jnp.zeros((pad, 3 * LANE), F32)
    xpad[pad:, 0:LANE] = q_ref[...].astype(F32)
    xpad[pad:, LANE:2 * LANE] = k_ref[...].astype(F32)
    xpad[pad:, 2 * LANE:3 * LANE] = v_ref[...].astype(F32)
    cw = jnp.concatenate([cwq_ref[...], cwk_ref[...], cwv_ref[...]], axis=1)

    for t0 in range(0, seq, conv_rows):
        xe = xpad[t0:t0 + pad + conv_rows, :]
        acc = xe * cw[0:1, :]
        for j in range(1, GDN_CONV):
            acc = xe * cw[j:j + 1, :] + pltpu.roll(acc, 1, 0)
        acc = acc[pad:, :]
        y = acc * _sigmoid(acc)
        yq, yk, yv = y[:, 0:LANE], y[:, LANE:2 * LANE], y[:, 2 * LANE:3 * LANE]
        yq = yq * lax.rsqrt(jnp.sum(yq * yq, axis=-1, keepdims=True) + RMS_EPS) * (GDN_DK ** -0.5)
        yk = yk * lax.rsqrt(jnp.sum(yk * yk, axis=-1, keepdims=True) + RMS_EPS)
        qh[t0:t0 + conv_rows, :] = yq
        kh[t0:t0 + conv_rows, :] = yk
        vh[t0:t0 + conv_rows, :] = yv

    neg_a = -jnp.exp(alog_ref[0, :, 0:1])
    dtb = dtb_ref[0, :, 0:1]
    ii = lax.broadcasted_iota(jnp.int32, (pr, pr), 0)
    jj = lax.broadcasted_iota(jnp.int32, (pr, pr), 1)
    same = jnp.right_shift(ii, GDN_CHUNK_SHIFT) == jnp.right_shift(jj, GDN_CHUNK_SHIFT)
    incl = same & (jj <= ii)
    strict = same & (jj < ii)
    upper = same & (ii <= jj)
    eye = (ii == jj).astype(F32)
    sub8 = lax.broadcasted_iota(jnp.int32, (8, LANE), 0)
    lane_id = lax.broadcasted_iota(jnp.int32, (1, LANE), 1)

    def over_pairs(body):
        def run(gi, carry):
            for uu in range(GDN_UNROLL):
                body(gi * GDN_UNROLL + uu)
            return carry
        lax.fori_loop(0, npair // GDN_UNROLL, run, 0)

    def stage_gates(p):
        rows = pl.ds(pl.multiple_of(p * pr, pr), pr)
        q = qh[rows, :]
        k = kh[rows, :]
        v = vh[rows, :]
        gcol = gcol_ref[rows, :]
        b_logit = jnp.sum(jnp.where(lane_id == head, gcol, 0.0), axis=1, keepdims=True)
        a_logit = jnp.sum(jnp.where(lane_id == GDN_HEADS + head, gcol, 0.0), axis=1, keepdims=True)
        a_row = jnp.sum(jnp.where(sub8[:, 0:1] == GDN_HEADS + head, grow_ref[p], 0.0), axis=0, keepdims=True)
        beta = _sigmoid(b_logit)
        g_c = neg_a * _softplus(a_logit + dtb)
        g_r = neg_a * _softplus(a_row + dtb)
        gc_col = jnp.sum(jnp.where(incl, g_r, 0.0), axis=1, keepdims=True)
        gc_row = jnp.sum(jnp.where(upper, g_c, 0.0), axis=0, keepdims=True)
        gt_col = jnp.sum(jnp.where(same, g_r, 0.0), axis=1, keepdims=True)
        decay = jnp.where(incl, jnp.exp(jnp.where(incl, gc_col - gc_row, 0.0)), 0.0)
        kq_k = _dot_nt(jnp.concatenate([k, q], axis=0), k)
        a_mat = jnp.where(strict, kq_k[0:pr] * decay * beta, 0.0)
        xp[p, 0:pr, :] = eye - a_mat
        xp[p, pr:2 * pr, :] = _dot(a_mat, a_mat)
        qk_s[p] = (kq_k[pr:2 * pr] * decay).astype(qk_s.dtype)
        e_gc = jnp.exp(gc_col)
        rhs[rows, 0:GDN_DV] = (v * beta).astype(rhs.dtype)
        rhs[rows, GDN_DV:GDN_DV + GDN_DK] = (k * (beta * e_gc)).astype(rhs.dtype)
        kd[rows, :] = (k * jnp.exp(gt_col - gc_col)).astype(kd.dtype)
        qd[rows, :] = q * e_gc
        g_last = jnp.exp(gt_col)
        gl[p] = jnp.where(sub8 == 0, g_last[0:1, :], g_last[c:c + 1, :])

    def stage_double(p):
        y = _dot(xp[p], xp[p, pr:2 * pr, :])
        xp[p, 0:pr, :] = xp[p, 0:pr, :] + y[0:pr]
        xp[p, pr:2 * pr, :] = y[pr:2 * pr]

    def stage_solve(p):
        rows = pl.ds(pl.multiple_of(p * pr, pr), pr)
        x_inv = xp[p, 0:pr, :]
        x_inv = x_inv + _dot(x_inv, xp[p, pr:2 * pr, :])
        sol[rows, :] = _dot(x_inv, rhs[rows, :])

    def stage_affine(p):
        r0 = pl.multiple_of(p * pr, pr)
        rows = pl.ds(r0, pr)
        e = _dot(qk_s[p], sol[rows, :])
        qe = qd[rows, :] - e[:, GDN_DV:GDN_DV + GDN_DK]
        g_rows = gl[p]
        for cc in range(pr // c):
            rc = pl.ds(r0 + cc * c, c)
            mn = _dot_tn(kd[rc, :], sol[rc, :])
            idx = p * (pr // c) + cc
            mq[head, idx, 0:GDN_DK, :] = (eye * g_rows[cc:cc + 1, :] - mn[:, GDN_DV:GDN_DV + GDN_DK]).astype(mq.dtype)
            mq[head, idx, GDN_DK:GDN_DK + c, :] = qe[cc * c:(cc + 1) * c].astype(mq.dtype)
            ns[head, idx] = mn[:, 0:GDN_DV]
            o0[head, idx] = e[cc * c:(cc + 1) * c, 0:GDN_DV]

    over_pairs(stage_gates)
    for _ in range(int(math.log2(c)) - 2):
        over_pairs(stage_double)
    over_pairs(stage_solve)
    over_pairs(stage_affine)

    @pl.when(head == GDN_HEADS - 1)
    def _():
        st[...] = jnp.zeros_like(st)
        nw = nw_ref[...]

        def step(ci, carry):
            r0 = pl.multiple_of(ci * c, c)
            for h in range(GDN_HEADS):
                cols = slice(h * LANE, (h + 1) * LANE)
                ys = jnp.dot(mq[h, ci], st[h], preferred_element_type=F32)
                st[h] = (ys[0:GDN_DK] + ns[h, ci]).astype(st.dtype)
                o = ys[GDN_DK:GDN_DK + c] + o0[h, ci]
                o = o * lax.rsqrt(jnp.mean(o * o, axis=-1, keepdims=True) + RMS_EPS) * nw
                z = z_ref[pl.ds(r0, c), cols].astype(F32)
                o_ref[pl.ds(r0, c), cols] = (o * (z * _sigmoid(z))).astype(o_ref.dtype)
            return carry

        lax.fori_loop(0, seq // c, step, 0)


def _gdn(gqkvz, conv_w, gcol, grow, alog, dtb, nw, bsz, seq):
    assert GDN_DK == GDN_PAIR == GDN_DV == LANE
    h = GDN_HEADS
    npair = seq // GDN_PAIR
    nchunk = seq // GDN_CHUNK
    blk = lambda off: pl.BlockSpec((seq, LANE), lambda b, hh: (b, off + hh))
    cwb = lambda off: pl.BlockSpec((GDN_CONV, LANE), lambda b, hh: (0, off + hh))
    in_specs = [blk(0), blk(h), blk(2 * h),
                pl.BlockSpec((seq, GDN_VW), lambda b, hh: (b, 3)),
                cwb(0), cwb(h), cwb(2 * h),
                pl.BlockSpec((seq, LANE), lambda b, hh: (b, 0)),
                pl.BlockSpec((npair, 8, GDN_PAIR), lambda b, hh: (b, 0, 0)),
                pl.BlockSpec((1, 1, LANE), lambda b, hh: (hh, 0, 0)),
                pl.BlockSpec((1, 1, LANE), lambda b, hh: (hh, 0, 0)),
                pl.BlockSpec((1, LANE), lambda b, hh: (0, 0))]
    scratch = [pltpu.VMEM((seq + 8, 3 * LANE), F32),
               pltpu.VMEM((seq, LANE), F32), pltpu.VMEM((seq, LANE), F32), pltpu.VMEM((seq, LANE), F32),
               pltpu.VMEM((npair, 2 * GDN_PAIR, GDN_PAIR), F32),
               pltpu.VMEM((npair, GDN_PAIR, GDN_PAIR), BF16),
               pltpu.VMEM((seq, GDN_DV + GDN_DK), BF16),
               pltpu.VMEM((seq, GDN_DK), BF16),
               pltpu.VMEM((seq, GDN_DK), F32),
               pltpu.VMEM((npair, 8, LANE), F32),
               pltpu.VMEM((seq, GDN_DV + GDN_DK), F32),
               pltpu.VMEM((h, nchunk, GDN_DK + GDN_CHUNK, GDN_DK), BF16),
               pltpu.VMEM((h, nchunk, GDN_DK, GDN_DV), F32),
               pltpu.VMEM((h, nchunk, GDN_CHUNK, GDN_DV), F32),
               pltpu.VMEM((h, GDN_DK, GDN_DV), BF16)]
    return pl.pallas_call(
        functools.partial(_gdn_kernel, seq=seq), grid=(bsz, h), in_specs=in_specs,
        out_specs=pl.BlockSpec((seq, GDN_VW), lambda b, hh: (b, 0)),
        out_shape=jax.ShapeDtypeStruct((bsz * seq, GDN_VW), BF16),
        scratch_shapes=scratch,
        compiler_params=_cparams(("parallel", "arbitrary")), name="gdn",
    )(gqkvz, gqkvz, gqkvz, gqkvz, conv_w, conv_w, conv_w, gcol, grow, alog, dtb, nw)


def _gelu_tanh(x):
    return 0.5 * x * (1.0 + jnp.tanh(math.sqrt(2.0 / math.pi) * (x + 0.044715 * x * x * x)))


def _compress_kernel(k16_ref, v16_ref, wkc_ref, wvc_ref, w1k_ref, w1v_ref, pk_ref, pv_ref,
                     w2k_ref, w2v_ref, kc_ref, vc_ref):
    nblk = k16_ref.shape[1]
    for x_ref, wc_ref, w1_ref, p_ref, w2_ref, o_ref in (
            (k16_ref, wkc_ref, w1k_ref, pk_ref, w2k_ref, kc_ref),
            (v16_ref, wvc_ref, w1v_ref, pv_ref, w2v_ref, vc_ref)):
        y = jnp.dot(x_ref[0], wc_ref[...], preferred_element_type=F32)
        pos_term = jnp.dot(p_ref[...], w1_ref[...], preferred_element_type=F32)[0:1, :]
        for g in range(NSA_KV_HEADS):
            top = y[:, g * CMP_HIDDEN:(g + 1) * CMP_HIDDEN]
            bot = y[:, (NSA_KV_HEADS + g) * CMP_HIDDEN:(NSA_KV_HEADS + g + 1) * CMP_HIDDEN]
            hid = top + pltpu.roll(bot, nblk - 1, 0) + pos_term
            out = _dot(_gelu_tanh(hid), w2_ref[...])
            o_ref[0, g] = (jnp.transpose(out) if o_ref is vc_ref else out).astype(o_ref.dtype)


def _compress(k16, v16, wkc, wvc, w1k, w1v, pk, pv, w2k, w2v):
    bsz, nblk, width = k16.shape
    in_specs = [pl.BlockSpec((1, nblk, width), lambda b: (b, 0, 0))] * 2
    in_specs += [_full_spec(a.shape) for a in (wkc, wvc, w1k, w1v, pk, pv, w2k, w2v)]
    out_spec = pl.BlockSpec((1, NSA_KV_HEADS, nblk, HEAD_PAD), lambda b: (b, 0, 0, 0))
    out_shape = jax.ShapeDtypeStruct((bsz, NSA_KV_HEADS, nblk, HEAD_PAD), BF16)
    return pl.pallas_call(
        _compress_kernel, grid=(bsz,), in_specs=in_specs, out_specs=[out_spec, out_spec],
        out_shape=[out_shape, out_shape], compiler_params=_cparams(("parallel",)), name="nsa_compress",
    )(k16, v16, wkc, wvc, w1k, w1v, pk, pv, w2k, w2v)


def _nsa_kernel(q_ref, kv_ref, kc_ref, vct_ref, sm_ref, ovt_ref, o_ref, ksa, vst, vwt, *, seq):
    step = pl.program_id(1)
    n_cmp_pad = kc_ref.shape[2]
    n_sel_blk = seq // SEL_BLOCK
    kb_rows = SEL_KEY_BLOCK
    one_row = NSA_DH
    groups = range(NSA_KV_HEADS)
    units = [(sub, g) for sub in range(NSA_Q_SUB) for g in groups]

    @pl.when(step == 0)
    def _():
        lane = lax.broadcasted_iota(jnp.int32, (1, NSA_KV_HEADS * HEAD_PAD), 1) & (HEAD_PAD - 1)
        row_t = lax.broadcasted_iota(jnp.int32, (HEAD_PAD, 1), 0)
        tiles = kb_rows // Q_BLOCK

        def fill(bi, carry):
            r0 = pl.multiple_of(bi * kb_rows, kb_rows)
            rows = pl.ds(r0, kb_rows)
            blk = jnp.right_shift(r0 + lax.broadcasted_iota(jnp.int32, (kb_rows, 1), 0), SEL_SHIFT)
            ksa[rows, :] = kv_ref[rows, 0:2 * HEAD_PAD] + (lane == NSA_DH + blk).astype(ksa.dtype)
            for g in range(NSA_KV_HEADS):
                v_t = jnp.transpose(kv_ref[rows, (2 + g) * HEAD_PAD:(3 + g) * HEAD_PAD].astype(F32))
                vst[bi, g] = jnp.where(row_t == one_row, 1.0, v_t).astype(vst.dtype)
                w_t = jnp.transpose(kv_ref[rows, (6 + g) * HEAD_PAD:(7 + g) * HEAD_PAD].astype(F32))
                w_t = jnp.where(row_t == one_row, 1.0, w_t).astype(vwt.dtype)
                for part in range(tiles):
                    vwt[bi * tiles + part, g] = w_t[:, part * Q_BLOCK:(part + 1) * Q_BLOCK]
            return carry

        lax.fori_loop(0, seq // kb_rows, fill, 0)

    tok = lax.broadcasted_iota(jnp.int32, (1, Q_BLOCK), 1)
    lane_q = lax.broadcasted_iota(jnp.int32, (1, HEAD_PAD), 1)
    pen_lanes = (lane_q >= NSA_DH) & (lane_q < NSA_DH + n_sel_blk)
    place = (lax.broadcasted_iota(jnp.int32, (n_sel_blk, HEAD_PAD), 1)
             == NSA_DH + lax.broadcasted_iota(jnp.int32, (n_sel_blk, HEAD_PAD), 0)).astype(BF16)
    n_sub = lax.broadcasted_iota(jnp.int32, (n_cmp_pad, 1), 0)

    def add_bias(s, bias):
        return jnp.concatenate([s[:, r * Q_BLOCK:(r + 1) * Q_BLOCK] + bias for r in range(s.shape[1] // Q_BLOCK)],
                               axis=1)

    q0_s, t_lane_s, gates_s, qp, o_cmp = [], [], [], {}, {}
    for sub in range(NSA_Q_SUB):
        q0 = (step * NSA_Q_SUB + sub) * Q_BLOCK
        tok_rows = slice(sub * Q_BLOCK, (sub + 1) * Q_BLOCK)
        t_lane = q0 + tok
        q0_s.append(q0)
        t_lane_s.append(t_lane)
        gates_s.append(jnp.transpose(_sigmoid(sm_ref[tok_rows, :])))

        cmp_bias = jnp.where((n_sub * CMP_STRIDE + (CMP_BLOCK - 1)) <= t_lane, 0.0, NEG_BIG)
        cmp_any = jnp.concatenate([(t_lane >= CMP_BLOCK - 1).astype(F32)] * NSA_REP, axis=1)

        for g in groups:
            qs = jnp.concatenate(
                [q_ref[tok_rows, (g * NSA_REP + r) * HEAD_PAD:(g * NSA_REP + r + 1) * HEAD_PAD]
                 for r in range(NSA_REP)], axis=0)

            s_c = add_bias(_dot_nt(kc_ref[0, g], qs), cmp_bias)
            e_c = jnp.exp2(s_c - jnp.max(s_c, axis=0, keepdims=True))
            p_c = e_c * (cmp_any / jnp.sum(e_c, axis=0, keepdims=True))
            o_cmp[sub, g] = _dot(vct_ref[0, g], p_c)[0:NSA_DH]

            p_sum = p_c[:, 0:Q_BLOCK]
            for r in range(1, NSA_REP):
                p_sum = p_sum + p_c[:, r * Q_BLOCK:(r + 1) * Q_BLOCK]
            imp_t = _dot(ovt_ref[...], p_sum)

            def rank_select(imp_t=imp_t, t_lane=t_lane):
                j_sub = lax.broadcasted_iota(jnp.int32, (n_sel_blk, 1), 0)
                cur = jnp.right_shift(t_lane, SEL_SHIFT)
                forced = (j_sub == 0) | (j_sub == cur) | (j_sub == cur - 1)
                causal_blk = (j_sub * SEL_BLOCK) <= t_lane
                imp = jnp.where(forced, FORCE_SCORE, jnp.where(causal_blk, imp_t, -jnp.inf))
                rank = jnp.zeros((n_sel_blk, Q_BLOCK), F32)
                for jp in range(n_sel_blk):
                    row = imp[jp:jp + 1, :]
                    ahead = (row > imp) | ((row == imp) & (j_sub > jp))
                    rank = rank + ahead.astype(F32)
                return (rank < SEL_TOPK).astype(BF16)

            sel_t = lax.cond(q0 + Q_BLOCK <= SEL_TOPK * SEL_BLOCK,
                             lambda: jnp.ones((n_sel_blk, Q_BLOCK), BF16), rank_select)
            sel_lanes = _dot_tn(sel_t, place)
            pen = jnp.where(pen_lanes, (sel_lanes - 1.0) * (-NEG_BIG), 0.0).astype(BF16)
            qp[sub, g] = qs + jnp.concatenate([pen] * NSA_REP, axis=0)

    n_full = step

    def scores(kb, unit):
        k0 = pl.multiple_of(kb * kb_rows, kb_rows)
        g = unit[1]
        return _dot_nt(ksa[pl.ds(k0, kb_rows), g * HEAD_PAD:(g + 1) * HEAD_PAD], qp[unit])

    def absorb(state, s, kb, unit):
        m_i, acc = state
        m_new = jnp.maximum(m_i, jnp.max(s, axis=0, keepdims=True))
        return m_new, jnp.exp2(m_i - m_new) * acc + _dot(vst[kb, unit[1]], jnp.exp2(s - m_new))

    kpos_last = n_full * kb_rows + lax.broadcasted_iota(jnp.int32, (kb_rows, 1), 0)
    first = []
    for unit in units:
        s = add_bias(scores(n_full, unit), jnp.where(kpos_last <= t_lane_s[unit[0]], 0.0, NEG_BIG))
        m = jnp.max(s, axis=0, keepdims=True)
        first.append((m, _dot(vst[n_full, unit[1]], jnp.exp2(s - m))))

    def full_blocks(state):
        def loop_body(kb, carry):
            st, s_cur = carry
            s_next = tuple(scores(kb + 1, u) for u in units)
            return tuple(absorb(st[i], s_cur[i], kb, u) for i, u in enumerate(units)), s_next

        st, s_cur = lax.fori_loop(0, n_full - 1, loop_body, (state, tuple(scores(0, u) for u in units)))
        return tuple(absorb(st[i], s_cur[i], n_full - 1, u) for i, u in enumerate(units))

    sel_out = lax.cond(n_full > 0, full_blocks, lambda st: st, tuple(first))

    span = WINDOW + Q_BLOCK
    for i, (sub, g) in enumerate(units):
        q0, t_lane = q0_s[sub], t_lane_s[sub]
        tok_rows = slice(sub * Q_BLOCK, (sub + 1) * Q_BLOCK)
        w0 = pl.multiple_of(jnp.maximum(q0 - WINDOW, 0), Q_BLOCK)
        kpos_w = w0 + lax.broadcasted_iota(jnp.int32, (span, 1), 0)
        win_bias = jnp.where((kpos_w <= t_lane) & (kpos_w > t_lane - WINDOW), 0.0, NEG_BIG)

        acc_s = sel_out[i][1]
        o_slc = acc_s[0:NSA_DH] * (1.0 / acc_s[one_row:one_row + 1, :])

        k_w = kv_ref[pl.ds(w0, span), (4 + g) * HEAD_PAD:(5 + g) * HEAD_PAD]
        s_w = add_bias(_dot_nt(k_w, qp[sub, g]), win_bias)
        tile0 = lax.div(w0, Q_BLOCK)
        v_wt = jnp.concatenate([vwt[tile0 + t, g] for t in range(span // Q_BLOCK)], axis=1)
        acc_w = _dot(v_wt, jnp.exp2(s_w - jnp.max(s_w, axis=0, keepdims=True)))
        o_win = acc_w[0:NSA_DH] * (1.0 / acc_w[one_row:one_row + 1, :])

        gates_t = gates_s[sub]
        o_heads = []
        for r in range(NSA_REP):
            c0 = 8 + (g * NSA_REP + r) * 3
            sl = slice(r * Q_BLOCK, (r + 1) * Q_BLOCK)
            o_t = (gates_t[c0:c0 + 1, :] * o_cmp[sub, g][:, sl] + gates_t[c0 + 1:c0 + 2, :] * o_slc[:, sl]
                   + gates_t[c0 + 2:c0 + 3, :] * o_win[:, sl])
            o_heads.append(o_t)
        for pair in range(NSA_REP // 2):
            lanes = slice((g * NSA_REP + 2 * pair) * NSA_DH, (g * NSA_REP + 2 * pair + 2) * NSA_DH)
            o_ref[tok_rows, lanes] = jnp.transpose(jnp.concatenate(o_heads[2 * pair:2 * pair + 2], axis=0)
                                                   ).astype(o_ref.dtype)


def _nsa(nq, nkv, kc, vc, small, overlap, bsz, seq):
    assert NSA_Q_SUB * Q_BLOCK == SEL_KEY_BLOCK
    rows = NSA_Q_SUB * Q_BLOCK
    nstep = seq // rows
    in_specs = [pl.BlockSpec((rows, NSA_HEADS * HEAD_PAD), lambda b, i: (b * nstep + i, 0)),
                pl.BlockSpec((seq, nkv.shape[1]), lambda b, i: (b, 0)),
                pl.BlockSpec((1,) + kc.shape[1:], lambda b, i: (b, 0, 0, 0)),
                pl.BlockSpec((1,) + vc.shape[1:], lambda b, i: (b, 0, 0, 0)),
                pl.BlockSpec((rows, LANE), lambda b, i: (b * nstep + i, 0)),
                _full_spec(overlap.shape)]
    return pl.pallas_call(
        functools.partial(_nsa_kernel, seq=seq), grid=(bsz, nstep), in_specs=in_specs,
        out_specs=pl.BlockSpec((rows, NSA_QW), lambda b, i: (b * nstep + i, 0)),
        out_shape=jax.ShapeDtypeStruct((bsz * seq, NSA_QW), BF16),
        scratch_shapes=[pltpu.VMEM((seq, NSA_KV_HEADS * HEAD_PAD), BF16),
                        pltpu.VMEM((seq // SEL_KEY_BLOCK, NSA_KV_HEADS, HEAD_PAD, SEL_KEY_BLOCK), BF16),
                        pltpu.VMEM((seq // Q_BLOCK, NSA_KV_HEADS, HEAD_PAD, Q_BLOCK), BF16)],
        compiler_params=_cparams(("parallel", "arbitrary")), name="nsa_attention",
    )(nq, nkv, kc, vc, small, overlap)


def _merge_kernel(oa_ref, ob_ref, mix_ref, x_ref, wa_ref, wb_ref, wo_ref, g_ref, b_ref, h_ref):
    ya = jnp.dot(oa_ref[...], wa_ref[...], preferred_element_type=F32)
    yb = jnp.dot(ob_ref[...], wb_ref[...], preferred_element_type=F32)
    mix = mix_ref[...].astype(F32)
    mixed = _sigmoid(mix[:, 0:D_MODEL]) * ya + _sigmoid(mix[:, D_MODEL:2 * D_MODEL]) * yb
    y = _dot(mixed, wo_ref[...])
    h_ref[...] = _layer_norm(DN_ALPHA * x_ref[...] + y, g_ref[...], b_ref[...])


def _merge(oa, ob, mix, x2d, wa, wb, wo, g, b):
    m = x2d.shape[0]
    tm = 2 * ROW_TILE
    row = lambda n: pl.BlockSpec((tm, n), lambda i: (i, 0))
    in_specs = [row(oa.shape[1]), row(ob.shape[1]), row(mix.shape[1]), row(D_MODEL),
                _full_spec(wa.shape), _full_spec(wb.shape), _full_spec(wo.shape),
                _full_spec(g.shape), _full_spec(b.shape)]
    return pl.pallas_call(
        _merge_kernel, grid=(m // tm,), in_specs=in_specs, out_specs=row(D_MODEL),
        out_shape=jax.ShapeDtypeStruct((m, D_MODEL), F32),
        compiler_params=_cparams(("parallel",)), name="merge_ln",
    )(oa, ob, mix, x2d, wa, wb, wo, g, b)


def _ffn_kernel(h_ref, halo_ref, wup_ref, cw_ref, wd_ref, g_ref, b_ref, o_ref,
                hb, ug, uv, act, *, seq, tm, halo):
    i = pl.program_id(0)
    first = lax.rem(i * tm, seq) == 0
    hb[0:halo, :] = jnp.where(first, 0.0, halo_ref[...]).astype(BF16)
    hb[halo:, :] = h_ref[...].astype(BF16)

    def conv(u_ref, slot, cols):
        out = None
        for tap in range(FFN_CONV):
            term = cw_ref[tap:tap + 1, cols] * u_ref[slot, pl.ds(halo - (FFN_CONV - 1) + tap, tm), :]
            out = term if out is None else out + term
        return out

    for j in range(D_FF // FFN_CHUNK):
        slot = j % FFN_SLOTS
        cols_g = slice(j * FFN_CHUNK, (j + 1) * FFN_CHUNK)
        cols_v = slice(D_FF + j * FFN_CHUNK, D_FF + (j + 1) * FFN_CHUNK)
        ug[slot] = jnp.dot(hb[...], wup_ref[:, cols_g], preferred_element_type=F32)
        uv[slot] = jnp.dot(hb[...], wup_ref[:, cols_v], preferred_element_type=F32)
        a = conv(ug, slot, cols_g)
        act[:, cols_g] = (a * _sigmoid(a) * conv(uv, slot, cols_v)).astype(act.dtype)

    f = jnp.dot(act[...], wd_ref[...], preferred_element_type=F32)
    o_ref[...] = _layer_norm(DN_ALPHA * h_ref[...] + f, g_ref[...], b_ref[...])


def _ffn(h, w_up, conv_w, w_down, g, b, seq):
    m = h.shape[0]
    tm = ROW_TILE
    halo = 16
    resident = lambda shape: pl.BlockSpec(shape, lambda i: (0, 0), pipeline_mode=pl.Buffered(1))
    in_specs = [pl.BlockSpec((tm, D_MODEL), lambda i: (i, 0)),
                pl.BlockSpec((halo, D_MODEL), lambda i: (jnp.maximum(i * (tm // halo) - 1, 0), 0)),
                resident(w_up.shape), resident(conv_w.shape), resident(w_down.shape),
                resident(g.shape), resident(b.shape)]
    return pl.pallas_call(
        functools.partial(_ffn_kernel, seq=seq, tm=tm, halo=halo), grid=(m // tm,),
        in_specs=in_specs, out_specs=pl.BlockSpec((tm, D_MODEL), lambda i: (i, 0)),
        out_shape=jax.ShapeDtypeStruct((m, D_MODEL), F32),
        scratch_shapes=[pltpu.VMEM((tm + halo, D_MODEL), BF16),
                        pltpu.VMEM((FFN_SLOTS, tm + halo, FFN_CHUNK), F32),
                        pltpu.VMEM((FFN_SLOTS, tm + halo, FFN_CHUNK), F32),
                        pltpu.VMEM((tm, D_FF), BF16)],
        compiler_params=_cparams(("parallel",)), name="conv_ffn_ln",
    )(h, h, w_up, conv_w, w_down, g, b)


def _compress_weights(w1):
    half = CMP_BLOCK // 2
    w1r = w1.reshape(2, half, NSA_DH, CMP_HIDDEN)
    eye = jnp.eye(NSA_KV_HEADS, dtype=w1.dtype)
    ex = jnp.einsum("aidh,gk->agikdh", w1r, eye)
    ex = ex.reshape(2 * NSA_KV_HEADS, half * NSA_KVW, CMP_HIDDEN)
    return jnp.concatenate([ex[n] for n in range(2 * NSA_KV_HEADS)], axis=1)


def _overlap_matrix(seq):
    n_cmp = (seq - CMP_BLOCK) // CMP_STRIDE + 1
    n_sel = seq // SEL_BLOCK
    starts = np.arange(n_cmp) * CMP_STRIDE
    jb = np.arange(n_sel) * SEL_BLOCK
    ov = ((starts[:, None] < jb[None] + SEL_BLOCK) & (starts[:, None] + CMP_BLOCK > jb[None])).astype(np.float32)
    out = np.zeros((n_sel, seq // CMP_STRIDE), np.float32)
    out[:, :n_cmp] = ov.T
    return jnp.asarray(out, BF16)


def kernel(x, w_in, gdn_conv_w, gdn_a_log, gdn_dt_bias, gdn_norm_w, cmp_pos_k, cmp_w1_k, cmp_w2_k,
           cmp_pos_v, cmp_w1_v, cmp_w2_v, w_branch_gdn, w_branch_nsa, w_out, ln1_g, ln1_b, w_up,
           ffn_conv_w, w_down, ln2_g, ln2_b):
    bsz, seq, _ = x.shape
    m = bsz * seq
    for i in range(DEPTH):
        x2d = x.reshape(m, D_MODEL)
        w = w_in[i]
        o_gdn = 2 * GDN_QK + 2 * GDN_VW
        o_nq = o_gdn + 2 * GDN_HEADS
        o_kv = o_nq + NSA_QW
        o_gate = o_kv + 6 * NSA_KVW
        o_mix = o_gate + 3 * NSA_HEADS
        kv = [w[:, o_kv + n * NSA_KVW:o_kv + (n + 1) * NSA_KVW] for n in range(6)]
        small_w = jnp.concatenate([w[:, o_gdn:o_nq], w[:, o_gate:o_mix]], axis=1)
        small_w = jnp.pad(small_w, ((0, 0), (0, LANE - small_w.shape[1])))
        weights = [w[:, :o_gdn],
                   w[:, o_nq:o_kv],
                   w[:, o_kv + 2 * NSA_KVW:o_gate],
                   kv[0], kv[1],
                   w[:, o_mix:],
                   small_w]
        weights = [t.astype(BF16) for t in weights]
        widths = [o_gdn, NSA_HEADS * HEAD_PAD, 4 * NSA_KV_HEADS * HEAD_PAD, NSA_KVW, NSA_KVW, 2 * D_MODEL, LANE]
        scales = [1.0, NSA_DH ** -0.5 * LOG2E, 1.0, 1.0, 1.0, 1.0, 1.0]
        gqkvz, nq, nkv, kcp, vcp, mix, small, grow = _inproj(x2d, weights, [BF16] * 6 + [F32], widths, scales,
                                                             2 * GDN_HEADS)

        alog = jnp.broadcast_to(gdn_a_log[i][:, None, None], (GDN_HEADS, 1, LANE))
        dtb = jnp.broadcast_to(gdn_dt_bias[i][:, None, None], (GDN_HEADS, 1, LANE))
        o_a = _gdn(gqkvz, gdn_conv_w[i], small, grow, alog, dtb, gdn_norm_w[i][None, :], bsz, seq)

        grp = seq // CMP_STRIDE
        k16 = kcp.reshape(bsz, grp, CMP_STRIDE * NSA_KVW)
        v16 = vcp.reshape(bsz, grp, CMP_STRIDE * NSA_KVW)
        pos_rows = lambda p: jnp.pad(p.reshape(1, CMP_BLOCK * NSA_DH), ((0, 7), (0, 0))).astype(BF16)
        w2_pad = lambda t: jnp.pad(t, ((0, 0), (0, HEAD_PAD - NSA_DH))).astype(BF16)
        kc, vc = _compress(k16, v16,
                           _compress_weights(cmp_w1_k[i]).astype(BF16), _compress_weights(cmp_w1_v[i]).astype(BF16),
                           cmp_w1_k[i].astype(BF16), cmp_w1_v[i].astype(BF16),
                           pos_rows(cmp_pos_k[i]), pos_rows(cmp_pos_v[i]),
                           w2_pad(cmp_w2_k[i]), w2_pad(cmp_w2_v[i]))
        o_b = _nsa(nq, nkv, kc, vc, small, _overlap_matrix(seq), bsz, seq)

        h = _merge(o_a, o_b, mix, x2d, w_branch_gdn[i].astype(BF16), w_branch_nsa[i].astype(BF16), w_out[i].astype(BF16),
                   ln1_g[i][None, :], ln1_b[i][None, :])

        x = _ffn(h, w_up[i].astype(BF16), ffn_conv_w[i], w_down[i].astype(BF16),
                 ln2_g[i][None, :], ln2_b[i][None, :], seq).reshape(bsz, seq, D_MODEL)
    return x
```

```python
import functools
import math

import numpy as np
import jax
import jax.numpy as jnp
from jax import lax
from jax.experimental import pallas as pl
from jax.experimental.pallas import tpu as pltpu

F32 = jnp.float32
BF16 = jnp.bfloat16

D_MODEL = 1024
GDN_HEADS = 4
GDN_DK = 128
GDN_DV = 128
GDN_CONV = 4
GDN_CHUNK = 64
NSA_HEADS = 8
NSA_KV_HEADS = 2
NSA_REP = NSA_HEADS // NSA_KV_HEADS
NSA_DH = 64
CMP_BLOCK = 32
CMP_STRIDE = 16
CMP_HIDDEN = 256
SEL_BLOCK = 64
SEL_SHIFT = 6
SEL_TOPK = 16
WINDOW = 512
FORCE_SCORE = 1e9
D_FF = 2816
FFN_CONV = 3
DEPTH = 1
DN_ALPHA = (2 * DEPTH) ** 0.25
LN_EPS = 1e-5
RMS_EPS = 1e-6
LOG2E = math.log2(math.e)

GDN_QK = GDN_HEADS * GDN_DK
GDN_VW = GDN_HEADS * GDN_DV
NSA_QW = NSA_HEADS * NSA_DH
NSA_KVW = NSA_KV_HEADS * NSA_DH

LANE = 128
HEAD_PAD = LANE
NEG_BIG = -1e30
VMEM_LIMIT = 56 * 1024 * 1024

Q_BLOCK = 128
NSA_Q_SUB = 2
SEL_KEY_BLOCK = 256
FFN_CHUNK = 256
FFN_SLOTS = 2
ROW_TILE = 512
GDN_CHUNK_SHIFT = 6
GDN_PAIR = 2 * GDN_CHUNK
GDN_UNROLL = 16


def _cparams(sem):
    return pltpu.CompilerParams(dimension_semantics=sem, vmem_limit_bytes=VMEM_LIMIT)


def _dot(a, b):
    return jnp.dot(a.astype(BF16), b.astype(BF16), preferred_element_type=F32)


def _dot_nt(a, b):
    return lax.dot_general(a.astype(BF16), b.astype(BF16), (((1,), (1,)), ((), ())),
                           preferred_element_type=F32)


def _dot_tn(a, b):
    return lax.dot_general(a.astype(BF16), b.astype(BF16), (((0,), (0,)), ((), ())),
                           preferred_element_type=F32)


def _sigmoid(x):
    return jax.nn.sigmoid(x)


def _softplus(x):
    return jnp.maximum(x, 0.0) + jnp.log1p(jnp.exp(-jnp.abs(x)))


def _layer_norm(x, g, b):
    mu = jnp.mean(x, axis=-1, keepdims=True)
    xc = x - mu
    var = jnp.mean(xc * xc, axis=-1, keepdims=True)
    return xc * lax.rsqrt(var + LN_EPS) * g + b


def _full_spec(shape):
    nd = len(shape)
    return pl.BlockSpec(shape, lambda *_: (0,) * nd, pipeline_mode=pl.Buffered(1))


def _inproj_kernel(x_ref, *refs, t_rows, scales):
    n = (len(refs) - 1) // 2
    xb = x_ref[...].astype(BF16)
    for w_ref, o_ref, scale in zip(refs[:n], refs[n:2 * n], scales):
        res = jnp.dot(xb, w_ref[...], preferred_element_type=F32)
        if scale != 1.0:
            res = res * scale
        if o_ref.shape[1] == w_ref.shape[1]:
            o_ref[...] = res.astype(o_ref.dtype)
        else:
            out = res.astype(o_ref.dtype)
            zeros = jnp.zeros((out.shape[0], HEAD_PAD - NSA_DH), o_ref.dtype)
            for hd in range(w_ref.shape[1] // NSA_DH):
                o_ref[:, hd * HEAD_PAD:hd * HEAD_PAD + NSA_DH] = out[:, hd * NSA_DH:(hd + 1) * NSA_DH]
                o_ref[:, hd * HEAD_PAD + NSA_DH:(hd + 1) * HEAD_PAD] = zeros
    t_ref = refs[2 * n]
    for j in range(t_ref.shape[0]):
        t_ref[j] = jnp.transpose(res[j * LANE:(j + 1) * LANE, :])[0:t_rows, :]


def _inproj(x2d, weights, out_dtypes, out_widths, scales, t_rows):
    assert weights[-1].shape[1] == LANE and out_dtypes[-1] == F32
    m = x2d.shape[0]
    tm = 2 * ROW_TILE
    in_specs = [pl.BlockSpec((tm, D_MODEL), lambda i: (i, 0))]
    in_specs += [_full_spec(w.shape) for w in weights]
    out_specs = [pl.BlockSpec((tm, n), lambda i: (i, 0)) for n in out_widths]
    out_specs.append(pl.BlockSpec((tm // LANE, t_rows, LANE), lambda i: (i, 0, 0)))
    out_shape = [jax.ShapeDtypeStruct((m, n), dt) for n, dt in zip(out_widths, out_dtypes)]
    out_shape.append(jax.ShapeDtypeStruct((m // LANE, t_rows, LANE), F32))
    return pl.pallas_call(
        functools.partial(_inproj_kernel, t_rows=t_rows, scales=tuple(scales)), grid=(m // tm,),
        in_specs=in_specs, out_specs=out_specs,
        out_shape=out_shape, compiler_params=_cparams(("parallel",)), name="inproj")(x2d, *weights)


def _gdn_kernel(q_ref, k_ref, v_ref, z_ref, cwq_ref, cwk_ref, cwv_ref, gcol_ref, grow_ref,
                alog_ref, dtb_ref, nw_ref, o_ref,
                xpad, qh, kh, vh, xp, qk_s, rhs, kd, qd, gl, sol, mq, ns, o0, st, *, seq):
    c = GDN_CHUNK
    pr = GDN_PAIR
    npair = seq // pr
    head = pl.program_id(1)
    pad = 8
    conv_rows = 256
    xpad[0:pad, :] = jnp.zeros((pad, 3 * LANE), F32)
    xpad[pad:, 0:LANE] = q_ref[...].astype(F32)
    xpad[pad:, LANE:2 * LANE] = k_ref[...].astype(F32)
    xpad[pad:, 2 * LANE:3 * LANE] = v_ref[...].astype(F32)
    cw = jnp.concatenate([cwq_ref[...], cwk_ref[...], cwv_ref[...]], axis=1)

    for t0 in range(0, seq, conv_rows):
        xe = xpad[t0:t0 + pad + conv_rows, :]
        acc = xe * cw[0:1, :]
        for j in range(1, GDN_CONV):
            acc = xe * cw[j:j + 1, :] + pltpu.roll(acc, 1, 0)
        acc = acc[pad:, :]
        y = acc * _sigmoid(acc)
        yq, yk, yv = y[:, 0:LANE], y[:, LANE:2 * LANE], y[:, 2 * LANE:3 * LANE]
        yq = yq * lax.rsqrt(jnp.sum(yq * yq, axis=-1, keepdims=True) + RMS_EPS) * (GDN_DK ** -0.5)
        yk = yk * lax.rsqrt(jnp.sum(yk * yk, axis=-1, keepdims=True) + RMS_EPS)
        qh[t0:t0 + conv_rows, :] = yq
        kh[t0:t0 + conv_rows, :] = yk
        vh[t0:t0 + conv_rows, :] = yv

    neg_a = -jnp.exp(alog_ref[0, :, 0:1])
    dtb = dtb_ref[0, :, 0:1]
    ii = lax.broadcasted_iota(jnp.int32, (pr, pr), 0)
    jj = lax.broadcasted_iota(jnp.int32, (pr, pr), 1)
    same = jnp.right_shift(ii, GDN_CHUNK_SHIFT) == jnp.right_shift(jj, GDN_CHUNK_SHIFT)
    incl = same & (jj <= ii)
    strict = same & (jj < ii)
    upper = same & (ii <= jj)
    eye = (ii == jj).astype(F32)
    sub8 = lax.broadcasted_iota(jnp.int32, (8, LANE), 0)
    lane_id = lax.broadcasted_iota(jnp.int32, (1, LANE), 1)

    def over_pairs(body):
        def run(gi, carry):
            for uu in range(GDN_UNROLL):
                body(gi * GDN_UNROLL + uu)
            return carry
        lax.fori_loop(0, npair // GDN_UNROLL, run, 0)

    def stage_gates(p):
        rows = pl.ds(pl.multiple_of(p * pr, pr), pr)
        q = qh[rows, :]
        k = kh[rows, :]
        v = vh[rows, :]
        gcol = gcol_ref[rows, :]
        b_logit = jnp.sum(jnp.where(lane_id == head, gcol, 0.0), axis=1, keepdims=True)
        a_logit = jnp.sum(jnp.where(lane_id == GDN_HEADS + head, gcol, 0.0), axis=1, keepdims=True)
        a_row = jnp.sum(jnp.where(sub8[:, 0:1] == GDN_HEADS + head, grow_ref[p], 0.0), axis=0, keepdims=True)
        beta = _sigmoid(b_logit)
        g_c = neg_a * _softplus(a_logit + dtb)
        g_r = neg_a * _softplus(a_row + dtb)
        gc_col = jnp.sum(jnp.where(incl, g_r, 0.0), axis=1, keepdims=True)
        gc_row = jnp.sum(jnp.where(upper, g_c, 0.0), axis=0, keepdims=True)
        gt_col = jnp.sum(jnp.where(same, g_r, 0.0), axis=1, keepdims=True)
        decay = jnp.where(incl, jnp.exp(jnp.where(incl, gc_col - gc_row, 0.0)), 0.0)
        kq_k = _dot_nt(jnp.concatenate([k, q], axis=0), k)
        a_mat = jnp.where(strict, kq_k[0:pr] * decay * beta, 0.0)
        xp[p, 0:pr, :] = eye - a_mat
        xp[p, pr:2 * pr, :] = _dot(a_mat, a_mat)
        qk_s[p] = (kq_k[pr:2 * pr] * decay).astype(qk_s.dtype)
        e_gc = jnp.exp(gc_col)
        rhs[rows, 0:GDN_DV] = (v * beta).astype(rhs.dtype)
        rhs[rows, GDN_DV:GDN_DV + GDN_DK] = (k * (beta * e_gc)).astype(rhs.dtype)
        kd[rows, :] = (k * jnp.exp(gt_col - gc_col)).astype(kd.dtype)
        qd[rows, :] = q * e_gc
        g_last = jnp.exp(gt_col)
        gl[p] = jnp.where(sub8 == 0, g_last[0:1, :], g_last[c:c + 1, :])

    def stage_double(p):
        y = _dot(xp[p], xp[p, pr:2 * pr, :])
        xp[p, 0:pr, :] = xp[p, 0:pr, :] + y[0:pr]
        xp[p, pr:2 * pr, :] = y[pr:2 * pr]

    def stage_solve(p):
        rows = pl.ds(pl.multiple_of(p * pr, pr), pr)
        x_inv = xp[p, 0:pr, :]
        x_inv = x_inv + _dot(x_inv, xp[p, pr:2 * pr, :])
        sol[rows, :] = _dot(x_inv, rhs[rows, :])

    def stage_affine(p):
        r0 = pl.multiple_of(p * pr, pr)
        rows = pl.ds(r0, pr)
        e = _dot(qk_s[p], sol[rows, :])
        qe = qd[rows, :] - e[:, GDN_DV:GDN_DV + GDN_DK]
        g_rows = gl[p]
        for cc in range(pr // c):
            rc = pl.ds(r0 + cc * c, c)
            mn = _dot_tn(kd[rc, :], sol[rc, :])
            idx = p * (pr // c) + cc
            mq[head, idx, 0:GDN_DK, :] = (eye * g_rows[cc:cc + 1, :] - mn[:, GDN_DV:GDN_DV + GDN_DK]).astype(mq.dtype)
            mq[head, idx, GDN_DK:GDN_DK + c, :] = qe[cc * c:(cc + 1) * c].astype(mq.dtype)
            ns[head, idx] = mn[:, 0:GDN_DV]
            o0[head, idx] = e[cc * c:(cc + 1) * c, 0:GDN_DV]

    over_pairs(stage_gates)
    for _ in range(int(math.log2(c)) - 2):
        over_pairs(stage_double)
    over_pairs(stage_solve)
    over_pairs(stage_affine)

    @pl.when(head == GDN_HEADS - 1)
    def _():
        st[...] = jnp.zeros_like(st)
        nw = nw_ref[...]

        def step(ci, carry):
            r0 = pl.multiple_of(ci * c, c)
            for h in range(GDN_HEADS):
                cols = slice(h * LANE, (h + 1) * LANE)
                ys = jnp.dot(mq[h, ci], st[h], preferred_element_type=F32)
                st[h] = (ys[0:GDN_DK] + ns[h, ci]).astype(st.dtype)
                o = ys[GDN_DK:GDN_DK + c] + o0[h, ci]
                o = o * lax.rsqrt(jnp.mean(o * o, axis=-1, keepdims=True) + RMS_EPS) * nw
                z = z_ref[pl.ds(r0, c), cols].astype(F32)
                o_ref[pl.ds(r0, c), cols] = (o * (z * _sigmoid(z))).astype(o_ref.dtype)
            return carry

        lax.fori_loop(0, seq // c, step, 0, unroll=2)


def _gdn(gqkvz, conv_w, gcol, grow, alog, dtb, nw, bsz, seq):
    assert GDN_DK == GDN_PAIR == GDN_DV == LANE
    h = GDN_HEADS
    npair = seq // GDN_PAIR
    nchunk = seq // GDN_CHUNK
    blk = lambda off: pl.BlockSpec((seq, LANE), lambda b, hh: (b, off + hh))
    cwb = lambda off: pl.BlockSpec((GDN_CONV, LANE), lambda b, hh: (0, off + hh))
    in_specs = [blk(0), blk(h), blk(2 * h),
                pl.BlockSpec((seq, GDN_VW), lambda b, hh: (b, 3)),
                cwb(0), cwb(h), cwb(2 * h),
                pl.BlockSpec((seq, LANE), lambda b, hh: (b, 0)),
                pl.BlockSpec((npair, 8, GDN_PAIR), lambda b, hh: (b, 0, 0)),
                pl.BlockSpec((1, 1, LANE), lambda b, hh: (hh, 0, 0)),
                pl.BlockSpec((1, 1, LANE), lambda b, hh: (hh, 0, 0)),
                pl.BlockSpec((1, LANE), lambda b, hh: (0, 0))]
    scratch = [pltpu.VMEM((seq + 8, 3 * LANE), F32),
               pltpu.VMEM((seq, LANE), F32), pltpu.VMEM((seq, LANE), F32), pltpu.VMEM((seq, LANE), F32),
               pltpu.VMEM((npair, 2 * GDN_PAIR, GDN_PAIR), F32),
               pltpu.VMEM((npair, GDN_PAIR, GDN_PAIR), BF16),
               pltpu.VMEM((seq, GDN_DV + GDN_DK), BF16),
               pltpu.VMEM((seq, GDN_DK), BF16),
               pltpu.VMEM((seq, GDN_DK), F32),
               pltpu.VMEM((npair, 8, LANE), F32),
               pltpu.VMEM((seq, GDN_DV + GDN_DK), F32),
               pltpu.VMEM((h, nchunk, GDN_DK + GDN_CHUNK, GDN_DK), BF16),
               pltpu.VMEM((h, nchunk, GDN_DK, GDN_DV), F32),
               pltpu.VMEM((h, nchunk, GDN_CHUNK, GDN_DV), F32),
               pltpu.VMEM((h, GDN_DK, GDN_DV), BF16)]
    return pl.pallas_call(
        functools.partial(_gdn_kernel, seq=seq), grid=(bsz, h), in_specs=in_specs,
        out_specs=pl.BlockSpec((seq, GDN_VW), lambda b, hh: (b, 0)),
        out_shape=jax.ShapeDtypeStruct((bsz * seq, GDN_VW), BF16),
        scratch_shapes=scratch,
        compiler_params=_cparams(("parallel", "arbitrary")), name="gdn",
    )(gqkvz, gqkvz, gqkvz, gqkvz, conv_w, conv_w, conv_w, gcol, grow, alog, dtb, nw)


def _gelu_tanh(x):
    return 0.5 * x * (1.0 + jnp.tanh(math.sqrt(2.0 / math.pi) * (x + 0.044715 * x * x * x)))


def _compress_kernel(k16_ref, v16_ref, wkc_ref, wvc_ref, w1k_ref, w1v_ref, pk_ref, pv_ref,
                     w2k_ref, w2v_ref, kc_ref, vc_ref):
    nblk = k16_ref.shape[1]
    for x_ref, wc_ref, w1_ref, p_ref, w2_ref, o_ref in (
            (k16_ref, wkc_ref, w1k_ref, pk_ref, w2k_ref, kc_ref),
            (v16_ref, wvc_ref, w1v_ref, pv_ref, w2v_ref, vc_ref)):
        y = jnp.dot(x_ref[0], wc_ref[...], preferred_element_type=F32)
        pos_term = jnp.dot(p_ref[...], w1_ref[...], preferred_element_type=F32)[0:1, :]
        for g in range(NSA_KV_HEADS):
            top = y[:, g * CMP_HIDDEN:(g + 1) * CMP_HIDDEN]
            bot = y[:, (NSA_KV_HEADS + g) * CMP_HIDDEN:(NSA_KV_HEADS + g + 1) * CMP_HIDDEN]
            hid = top + pltpu.roll(bot, nblk - 1, 0) + pos_term
            out = _dot(_gelu_tanh(hid), w2_ref[...])
            o_ref[0, g] = (jnp.transpose(out) if o_ref is vc_ref else out).astype(o_ref.dtype)


def _compress(k16, v16, wkc, wvc, w1k, w1v, pk, pv, w2k, w2v):
    bsz, nblk, width = k16.shape
    in_specs = [pl.BlockSpec((1, nblk, width), lambda b: (b, 0, 0))] * 2
    in_specs += [_full_spec(a.shape) for a in (wkc, wvc, w1k, w1v, pk, pv, w2k, w2v)]
    out_spec = pl.BlockSpec((1, NSA_KV_HEADS, nblk, HEAD_PAD), lambda b: (b, 0, 0, 0))
    out_shape = jax.ShapeDtypeStruct((bsz, NSA_KV_HEADS, nblk, HEAD_PAD), BF16)
    return pl.pallas_call(
        _compress_kernel, grid=(bsz,), in_specs=in_specs, out_specs=[out_spec, out_spec],
        out_shape=[out_shape, out_shape], compiler_params=_cparams(("parallel",)), name="nsa_compress",
    )(k16, v16, wkc, wvc, w1k, w1v, pk, pv, w2k, w2v)


def _nsa_kernel(q_ref, kv_ref, kc_ref, vct_ref, sm_ref, ovt_ref, o_ref, ksa, vst, vwt, *, seq):
    step = pl.program_id(1)
    n_cmp_pad = kc_ref.shape[2]
    n_sel_blk = seq // SEL_BLOCK
    kb_rows = SEL_KEY_BLOCK
    one_row = NSA_DH
    groups = range(NSA_KV_HEADS)
    units = [(sub, g) for sub in range(NSA_Q_SUB) for g in groups]

    @pl.when(step == 0)
    def _():
        lane = lax.broadcasted_iota(jnp.int32, (1, NSA_KV_HEADS * HEAD_PAD), 1) & (HEAD_PAD - 1)
        row_t = lax.broadcasted_iota(jnp.int32, (HEAD_PAD, 1), 0)
        tiles = kb_rows // Q_BLOCK

        def fill(bi, carry):
            r0 = pl.multiple_of(bi * kb_rows, kb_rows)
            rows = pl.ds(r0, kb_rows)
            blk = jnp.right_shift(r0 + lax.broadcasted_iota(jnp.int32, (kb_rows, 1), 0), SEL_SHIFT)
            ksa[rows, :] = kv_ref[rows, 0:2 * HEAD_PAD] + (lane == NSA_DH + blk).astype(ksa.dtype)
            for g in range(NSA_KV_HEADS):
                v_t = jnp.transpose(kv_ref[rows, (2 + g) * HEAD_PAD:(3 + g) * HEAD_PAD].astype(F32))
                vst[bi, g] = jnp.where(row_t == one_row, 1.0, v_t).astype(vst.dtype)
                w_t = jnp.transpose(kv_ref[rows, (6 + g) * HEAD_PAD:(7 + g) * HEAD_PAD].astype(F32))
                w_t = jnp.where(row_t == one_row, 1.0, w_t).astype(vwt.dtype)
                for part in range(tiles):
                    vwt[bi * tiles + part, g] = w_t[:, part * Q_BLOCK:(part + 1) * Q_BLOCK]
            return carry

        lax.fori_loop(0, seq // kb_rows, fill, 0)

    tok = lax.broadcasted_iota(jnp.int32, (1, Q_BLOCK), 1)
    lane_q = lax.broadcasted_iota(jnp.int32, (1, HEAD_PAD), 1)
    pen_lanes = (lane_q >= NSA_DH) & (lane_q < NSA_DH + n_sel_blk)
    place = (lax.broadcasted_iota(jnp.int32, (n_sel_blk, HEAD_PAD), 1)
             == NSA_DH + lax.broadcasted_iota(jnp.int32, (n_sel_blk, HEAD_PAD), 0)).astype(BF16)
    n_sub = lax.broadcasted_iota(jnp.int32, (n_cmp_pad, 1), 0)

    def add_bias(s, bias):
        return jnp.concatenate([s[:, r * Q_BLOCK:(r + 1) * Q_BLOCK] + bias for r in range(s.shape[1] // Q_BLOCK)],
                               axis=1)

    q0_s, t_lane_s, gates_s, qp, o_cmp = [], [], [], {}, {}
    for sub in range(NSA_Q_SUB):
        q0 = (step * NSA_Q_SUB + sub) * Q_BLOCK
        tok_rows = slice(sub * Q_BLOCK, (sub + 1) * Q_BLOCK)
        t_lane = q0 + tok
        q0_s.append(q0)
        t_lane_s.append(t_lane)
        gates_s.append(jnp.transpose(_sigmoid(sm_ref[tok_rows, :])))

        cmp_bias = jnp.where((n_sub * CMP_STRIDE + (CMP_BLOCK - 1)) <= t_lane, 0.0, NEG_BIG)
        cmp_any = jnp.concatenate([(t_lane >= CMP_BLOCK - 1).astype(F32)] * NSA_REP, axis=1)

        for g in groups:
            qs = jnp.concatenate(
                [q_ref[tok_rows, (g * NSA_REP + r) * HEAD_PAD:(g * NSA_REP + r + 1) * HEAD_PAD]
                 for r in range(NSA_REP)], axis=0)

            s_c = add_bias(_dot_nt(kc_ref[0, g], qs), cmp_bias)
            e_c = jnp.exp2(s_c - jnp.max(s_c, axis=0, keepdims=True))
            p_c = e_c * (cmp_any / jnp.sum(e_c, axis=0, keepdims=True))
            o_cmp[sub, g] = _dot(vct_ref[0, g], p_c)[0:NSA_DH]

            p_sum = p_c[:, 0:Q_BLOCK]
            for r in range(1, NSA_REP):
                p_sum = p_sum + p_c[:, r * Q_BLOCK:(r + 1) * Q_BLOCK]
            imp_t = _dot(ovt_ref[...], p_sum)

            def rank_select(imp_t=imp_t, t_lane=t_lane):
                j_sub = lax.broadcasted_iota(jnp.int32, (n_sel_blk, 1), 0)
                cur = jnp.right_shift(t_lane, SEL_SHIFT)
                forced = (j_sub == 0) | (j_sub == cur) | (j_sub == cur - 1)
                causal_blk = (j_sub * SEL_BLOCK) <= t_lane
                imp = jnp.where(forced, FORCE_SCORE, jnp.where(causal_blk, imp_t, -jnp.inf))
                rank = jnp.zeros((n_sel_blk, Q_BLOCK), F32)
                for jp in range(n_sel_blk):
                    row = imp[jp:jp + 1, :]
                    ahead = (row > imp) | ((row == imp) & (j_sub > jp))
                    rank = rank + ahead.astype(F32)
                return (rank < SEL_TOPK).astype(BF16)

            sel_t = lax.cond(q0 + Q_BLOCK <= SEL_TOPK * SEL_BLOCK,
                             lambda: jnp.ones((n_sel_blk, Q_BLOCK), BF16), rank_select)
            sel_lanes = _dot_tn(sel_t, place)
            pen = jnp.where(pen_lanes, (sel_lanes - 1.0) * (-NEG_BIG), 0.0).astype(BF16)
            qp[sub, g] = qs + jnp.concatenate([pen] * NSA_REP, axis=0)

    n_full = step

    def scores(kb, unit):
        k0 = pl.multiple_of(kb * kb_rows, kb_rows)
        g = unit[1]
        return _dot_nt(ksa[pl.ds(k0, kb_rows), g * HEAD_PAD:(g + 1) * HEAD_PAD], qp[unit])

    def absorb(state, s, kb, unit):
        m_i, acc = state
        m_new = jnp.maximum(m_i, jnp.max(s, axis=0, keepdims=True))
        return m_new, jnp.exp2(m_i - m_new) * acc + _dot(vst[kb, unit[1]], jnp.exp2(s - m_new))

    kpos_last = n_full * kb_rows + lax.broadcasted_iota(jnp.int32, (kb_rows, 1), 0)
    first = []
    for unit in units:
        s = add_bias(scores(n_full, unit), jnp.where(kpos_last <= t_lane_s[unit[0]], 0.0, NEG_BIG))
        m = jnp.max(s, axis=0, keepdims=True)
        first.append((m, _dot(vst[n_full, unit[1]], jnp.exp2(s - m))))

    def full_blocks(state):
        def loop_body(kb, carry):
            st, s_cur = carry
            s_next = tuple(scores(kb + 1, u) for u in units)
            return tuple(absorb(st[i], s_cur[i], kb, u) for i, u in enumerate(units)), s_next

        st, s_cur = lax.fori_loop(0, n_full - 1, loop_body, (state, tuple(scores(0, u) for u in units)))
        return tuple(absorb(st[i], s_cur[i], n_full - 1, u) for i, u in enumerate(units))

    sel_out = lax.cond(n_full > 0, full_blocks, lambda st: st, tuple(first))

    span = WINDOW + Q_BLOCK
    for i, (sub, g) in enumerate(units):
        q0, t_lane = q0_s[sub], t_lane_s[sub]
        tok_rows = slice(sub * Q_BLOCK, (sub + 1) * Q_BLOCK)
        w0 = pl.multiple_of(jnp.maximum(q0 - WINDOW, 0), Q_BLOCK)
        kpos_w = w0 + lax.broadcasted_iota(jnp.int32, (span, 1), 0)
        win_bias = jnp.where((kpos_w <= t_lane) & (kpos_w > t_lane - WINDOW), 0.0, NEG_BIG)

        acc_s = sel_out[i][1]
        o_slc = acc_s[0:NSA_DH] * (1.0 / acc_s[one_row:one_row + 1, :])

        k_w = kv_ref[pl.ds(w0, span), (4 + g) * HEAD_PAD:(5 + g) * HEAD_PAD]
        s_w = add_bias(_dot_nt(k_w, qp[sub, g]), win_bias)
        tile0 = lax.div(w0, Q_BLOCK)
        v_wt = jnp.concatenate([vwt[tile0 + t, g] for t in range(span // Q_BLOCK)], axis=1)
        acc_w = _dot(v_wt, jnp.exp2(s_w - jnp.max(s_w, axis=0, keepdims=True)))
        o_win = acc_w[0:NSA_DH] * (1.0 / acc_w[one_row:one_row + 1, :])

        gates_t = gates_s[sub]
        o_heads = []
        for r in range(NSA_REP):
            c0 = 8 + (g * NSA_REP + r) * 3
            sl = slice(r * Q_BLOCK, (r + 1) * Q_BLOCK)
            o_t = (gates_t[c0:c0 + 1, :] * o_cmp[sub, g][:, sl] + gates_t[c0 + 1:c0 + 2, :] * o_slc[:, sl]
                   + gates_t[c0 + 2:c0 + 3, :] * o_win[:, sl])
            o_heads.append(o_t)
        for pair in range(NSA_REP // 2):
            lanes = slice((g * NSA_REP + 2 * pair) * NSA_DH, (g * NSA_REP + 2 * pair + 2) * NSA_DH)
            o_ref[tok_rows, lanes] = jnp.transpose(jnp.concatenate(o_heads[2 * pair:2 * pair + 2], axis=0)
                                                   ).astype(o_ref.dtype)


def _nsa(nq, nkv, kc, vc, small, overlap, bsz, seq):
    assert NSA_Q_SUB * Q_BLOCK == SEL_KEY_BLOCK
    rows = NSA_Q_SUB * Q_BLOCK
    nstep = seq // rows
    in_specs = [pl.BlockSpec((rows, NSA_HEADS * HEAD_PAD), lambda b, i: (b * nstep + i, 0)),
                pl.BlockSpec((seq, nkv.shape[1]), lambda b, i: (b, 0)),
                pl.BlockSpec((1,) + kc.shape[1:], lambda b, i: (b, 0, 0, 0)),
                pl.BlockSpec((1,) + vc.shape[1:], lambda b, i: (b, 0, 0, 0)),
                pl.BlockSpec((rows, LANE), lambda b, i: (b * nstep + i, 0)),
                _full_spec(overlap.shape)]
    return pl.pallas_call(
        functools.partial(_nsa_kernel, seq=seq), grid=(bsz, nstep), in_specs=in_specs,
        out_specs=pl.BlockSpec((rows, NSA_QW), lambda b, i: (b * nstep + i, 0)),
        out_shape=jax.ShapeDtypeStruct((bsz * seq, NSA_QW), BF16),
        scratch_shapes=[pltpu.VMEM((seq, NSA_KV_HEADS * HEAD_PAD), BF16),
                        pltpu.VMEM((seq // SEL_KEY_BLOCK, NSA_KV_HEADS, HEAD_PAD, SEL_KEY_BLOCK), BF16),
                        pltpu.VMEM((seq // Q_BLOCK, NSA_KV_HEADS, HEAD_PAD, Q_BLOCK), BF16)],
        compiler_params=_cparams(("parallel", "arbitrary")), name="nsa_attention",
    )(nq, nkv, kc, vc, small, overlap)


def _merge_kernel(oa_ref, ob_ref, mix_ref, x_ref, wa_ref, wb_ref, wo_ref, g_ref, b_ref, h_ref):
    ya = jnp.dot(oa_ref[...], wa_ref[...], preferred_element_type=F32)
    yb = jnp.dot(ob_ref[...], wb_ref[...], preferred_element_type=F32)
    mix = mix_ref[...].astype(F32)
    mixed = _sigmoid(mix[:, 0:D_MODEL]) * ya + _sigmoid(mix[:, D_MODEL:2 * D_MODEL]) * yb
    y = _dot(mixed, wo_ref[...])
    h_ref[...] = _layer_norm(DN_ALPHA * x_ref[...] + y, g_ref[...], b_ref[...])


def _merge(oa, ob, mix, x2d, wa, wb, wo, g, b):
    m = x2d.shape[0]
    tm = 2 * ROW_TILE
    row = lambda n: pl.BlockSpec((tm, n), lambda i: (i, 0))
    in_specs = [row(oa.shape[1]), row(ob.shape[1]), row(mix.shape[1]), row(D_MODEL),
                _full_spec(wa.shape), _full_spec(wb.shape), _full_spec(wo.shape),
                _full_spec(g.shape), _full_spec(b.shape)]
    return pl.pallas_call(
        _merge_kernel, grid=(m // tm,), in_specs=in_specs, out_specs=row(D_MODEL),
        out_shape=jax.ShapeDtypeStruct((m, D_MODEL), F32),
        compiler_params=_cparams(("parallel",)), name="merge_ln",
    )(oa, ob, mix, x2d, wa, wb, wo, g, b)


def _ffn_kernel(h_ref, halo_ref, wup_ref, cw_ref, wd_ref, g_ref, b_ref, o_ref,
                hb, ug, uv, act, *, seq, tm, halo):
    i = pl.program_id(0)
    first = lax.rem(i * tm, seq) == 0
    hb[0:halo, :] = jnp.where(first, 0.0, halo_ref[...]).astype(BF16)
    hb[halo:, :] = h_ref[...].astype(BF16)

    def conv(u_ref, slot, cols):
        out = None
        for tap in range(FFN_CONV):
            term = cw_ref[tap:tap + 1, cols] * u_ref[slot, pl.ds(halo - (FFN_CONV - 1) + tap, tm), :]
            out = term if out is None else out + term
        return out

    for j in range(D_FF // FFN_CHUNK):
        slot = j % FFN_SLOTS
        cols_g = slice(j * FFN_CHUNK, (j + 1) * FFN_CHUNK)
        cols_v = slice(D_FF + j * FFN_CHUNK, D_FF + (j + 1) * FFN_CHUNK)
        ug[slot] = jnp.dot(hb[...], wup_ref[:, cols_g], preferred_element_type=F32)
        uv[slot] = jnp.dot(hb[...], wup_ref[:, cols_v], preferred_element_type=F32)
        a = conv(ug, slot, cols_g)
        act[:, cols_g] = (a * _sigmoid(a) * conv(uv, slot, cols_v)).astype(act.dtype)

    f = jnp.dot(act[...], wd_ref[...], preferred_element_type=F32)
    o_ref[...] = _layer_norm(DN_ALPHA * h_ref[...] + f, g_ref[...], b_ref[...])


def _ffn(h, w_up, conv_w, w_down, g, b, seq):
    m = h.shape[0]
    tm = ROW_TILE
    halo = 16
    resident = lambda shape: pl.BlockSpec(shape, lambda i: (0, 0), pipeline_mode=pl.Buffered(1))
    in_specs = [pl.BlockSpec((tm, D_MODEL), lambda i: (i, 0)),
                pl.BlockSpec((halo, D_MODEL), lambda i: (jnp.maximum(i * (tm // halo) - 1, 0), 0)),
                resident(w_up.shape), resident(conv_w.shape), resident(w_down.shape),
                resident(g.shape), resident(b.shape)]
    return pl.pallas_call(
        functools.partial(_ffn_kernel, seq=seq, tm=tm, halo=halo), grid=(m // tm,),
        in_specs=in_specs, out_specs=pl.BlockSpec((tm, D_MODEL), lambda i: (i, 0)),
        out_shape=jax.ShapeDtypeStruct((m, D_MODEL), F32),
        scratch_shapes=[pltpu.VMEM((tm + halo, D_MODEL), BF16),
                        pltpu.VMEM((FFN_SLOTS, tm + halo, FFN_CHUNK), F32),
                        pltpu.VMEM((FFN_SLOTS, tm + halo, FFN_CHUNK), F32),
                        pltpu.VMEM((tm, D_FF), BF16)],
        compiler_params=_cparams(("parallel",)), name="conv_ffn_ln",
    )(h, h, w_up, conv_w, w_down, g, b)


def _compress_weights(w1):
    half = CMP_BLOCK // 2
    w1r = w1.reshape(2, half, NSA_DH, CMP_HIDDEN)
    eye = jnp.eye(NSA_KV_HEADS, dtype=w1.dtype)
    ex = jnp.einsum("aidh,gk->agikdh", w1r, eye)
    ex = ex.reshape(2 * NSA_KV_HEADS, half * NSA_KVW, CMP_HIDDEN)
    return jnp.concatenate([ex[n] for n in range(2 * NSA_KV_HEADS)], axis=1)


def _overlap_matrix(seq):
    n_cmp = (seq - CMP_BLOCK) // CMP_STRIDE + 1
    n_sel = seq // SEL_BLOCK
    starts = np.arange(n_cmp) * CMP_STRIDE
    jb = np.arange(n_sel) * SEL_BLOCK
    ov = ((starts[:, None] < jb[None] + SEL_BLOCK) & (starts[:, None] + CMP_BLOCK > jb[None])).astype(np.float32)
    out = np.zeros((n_sel, seq // CMP_STRIDE), np.float32)
    out[:, :n_cmp] = ov.T
    return jnp.asarray(out, BF16)


def kernel(x, w_in, gdn_conv_w, gdn_a_log, gdn_dt_bias, gdn_norm_w, cmp_pos_k, cmp_w1_k, cmp_w2_k,
           cmp_pos_v, cmp_w1_v, cmp_w2_v, w_branch_gdn, w_branch_nsa, w_out, ln1_g, ln1_b, w_up,
           ffn_conv_w, w_down, ln2_g, ln2_b):
    bsz, seq, _ = x.shape
    m = bsz * seq
    for i in range(DEPTH):
        x2d = x.reshape(m, D_MODEL)
        w = w_in[i]
        o_gdn = 2 * GDN_QK + 2 * GDN_VW
        o_nq = o_gdn + 2 * GDN_HEADS
        o_kv = o_nq + NSA_QW
        o_gate = o_kv + 6 * NSA_KVW
        o_mix = o_gate + 3 * NSA_HEADS
        kv = [w[:, o_kv + n * NSA_KVW:o_kv + (n + 1) * NSA_KVW] for n in range(6)]
        small_w = jnp.concatenate([w[:, o_gdn:o_nq], w[:, o_gate:o_mix]], axis=1)
        small_w = jnp.pad(small_w, ((0, 0), (0, LANE - small_w.shape[1])))
        weights = [w[:, :o_gdn],
                   w[:, o_nq:o_kv],
                   w[:, o_kv + 2 * NSA_KVW:o_gate],
                   kv[0], kv[1],
                   w[:, o_mix:],
                   small_w]
        weights = [t.astype(BF16) for t in weights]
        widths = [o_gdn, NSA_HEADS * HEAD_PAD, 4 * NSA_KV_HEADS * HEAD_PAD, NSA_KVW, NSA_KVW, 2 * D_MODEL, LANE]
        scales = [1.0, NSA_DH ** -0.5 * LOG2E, 1.0, 1.0, 1.0, 1.0, 1.0]
        gqkvz, nq, nkv, kcp, vcp, mix, small, grow = _inproj(x2d, weights, [BF16] * 6 + [F32], widths, scales,
                                                             2 * GDN_HEADS)

        alog = jnp.broadcast_to(gdn_a_log[i][:, None, None], (GDN_HEADS, 1, LANE))
        dtb = jnp.broadcast_to(gdn_dt_bias[i][:, None, None], (GDN_HEADS, 1, LANE))
        o_a = _gdn(gqkvz, gdn_conv_w[i], small, grow, alog, dtb, gdn_norm_w[i][None, :], bsz, seq)

        grp = seq // CMP_STRIDE
        k16 = kcp.reshape(bsz, grp, CMP_STRIDE * NSA_KVW)
        v16 = vcp.reshape(bsz, grp, CMP_STRIDE * NSA_KVW)
        pos_rows = lambda p: jnp.pad(p.reshape(1, CMP_BLOCK * NSA_DH), ((0, 7), (0, 0))).astype(BF16)
        w2_pad = lambda t: jnp.pad(t, ((0, 0), (0, HEAD_PAD - NSA_DH))).astype(BF16)
        kc, vc = _compress(k16, v16,
                           _compress_weights(cmp_w1_k[i]).astype(BF16), _compress_weights(cmp_w1_v[i]).astype(BF16),
                           cmp_w1_k[i].astype(BF16), cmp_w1_v[i].astype(BF16),
                           pos_rows(cmp_pos_k[i]), pos_rows(cmp_pos_v[i]),
                           w2_pad(cmp_w2_k[i]), w2_pad(cmp_w2_v[i]))
        o_b = _nsa(nq, nkv, kc, vc, small, _overlap_matrix(seq), bsz, seq)

        h = _merge(o_a, o_b, mix, x2d, w_branch_gdn[i].astype(BF16), w_branch_nsa[i].astype(BF16), w_out[i].astype(BF16),
                   ln1_g[i][None, :], ln1_b[i][None, :])

        x = _ffn(h, w_up[i].astype(BF16), ffn_conv_w[i], w_down[i].astype(BF16),
                 ln2_g[i][None, :], ln2_b[i][None, :], seq).reshape(bsz, seq, D_MODEL)
    return x
```

```python
import functools
import math

import numpy as np
import jax
import jax.numpy as jnp
from jax import lax
from jax.experimental import pallas as pl
from jax.experimental.pallas import tpu as pltpu

F32 = jnp.float32
BF16 = jnp.bfloat16

D_MODEL = 1024
GDN_HEADS = 4
GDN_DK = 128
GDN_DV = 128
GDN_CONV = 4
GDN_CHUNK = 64
NSA_HEADS = 8
NSA_KV_HEADS = 2
NSA_REP = NSA_HEADS // NSA_KV_HEADS
NSA_DH = 64
CMP_BLOCK = 32
CMP_STRIDE = 16
CMP_HIDDEN = 256
SEL_BLOCK = 64
SEL_SHIFT = 6
SEL_TOPK = 16
WINDOW = 512
FORCE_SCORE = 1e9
D_FF = 2816
FFN_CONV = 3
DEPTH = 1
DN_ALPHA = (2 * DEPTH) ** 0.25
LN_EPS = 1e-5
RMS_EPS = 1e-6
LOG2E = math.log2(math.e)

GDN_QK = GDN_HEADS * GDN_DK
GDN_VW = GDN_HEADS * GDN_DV
NSA_QW = NSA_HEADS * NSA_DH
NSA_KVW = NSA_KV_HEADS * NSA_DH

LANE = 128
HEAD_PAD = LANE
NEG_BIG = -1e30
VMEM_LIMIT = 56 * 1024 * 1024

Q_BLOCK = 128
NSA_Q_SUB = 2
SEL_KEY_BLOCK = 256
FFN_CHUNK = 256
FFN_SLOTS = 2
ROW_TILE = 512
GDN_CHUNK_SHIFT = 6
GDN_PAIR = 2 * GDN_CHUNK
GDN_UNROLL = 16


def _cparams(sem):
    return pltpu.CompilerParams(dimension_semantics=sem, vmem_limit_bytes=VMEM_LIMIT)


def _dot(a, b):
    return jnp.dot(a.astype(BF16), b.astype(BF16), preferred_element_type=F32)


def _dot_nt(a, b):
    return lax.dot_general(a.astype(BF16), b.astype(BF16), (((1,), (1,)), ((), ())),
                           preferred_element_type=F32)


def _dot_tn(a, b):
    return lax.dot_general(a.astype(BF16), b.astype(BF16), (((0,), (0,)), ((), ())),
                           preferred_element_type=F32)


def _sigmoid(x):
    return jax.nn.sigmoid(x)


def _softplus(x):
    return jnp.maximum(x, 0.0) + jnp.log1p(jnp.exp(-jnp.abs(x)))


def _layer_norm(x, g, b):
    mu = jnp.mean(x, axis=-1, keepdims=True)
    xc = x - mu
    var = jnp.mean(xc * xc, axis=-1, keepdims=True)
    return xc * lax.rsqrt(var + LN_EPS) * g + b


def _full_spec(shape):
    nd = len(shape)
    return pl.BlockSpec(shape, lambda *_: (0,) * nd, pipeline_mode=pl.Buffered(1))


def _inproj_kernel(x_ref, *refs, t_rows, scales):
    n = (len(refs) - 1) // 2
    xb = x_ref[...].astype(BF16)
    for w_ref, o_ref, scale in zip(refs[:n], refs[n:2 * n], scales):
        res = jnp.dot(xb, w_ref[...], preferred_element_type=F32)
        if scale != 1.0:
            res = res * scale
        if o_ref.shape[1] == w_ref.shape[1]:
            o_ref[...] = res.astype(o_ref.dtype)
        else:
            out = res.astype(o_ref.dtype)
            zeros = jnp.zeros((out.shape[0], HEAD_PAD - NSA_DH), o_ref.dtype)
            for hd in range(w_ref.shape[1] // NSA_DH):
                o_ref[:, hd * HEAD_PAD:hd * HEAD_PAD + NSA_DH] = out[:, hd * NSA_DH:(hd + 1) * NSA_DH]
                o_ref[:, hd * HEAD_PAD + NSA_DH:(hd + 1) * HEAD_PAD] = zeros
    t_ref = refs[2 * n]
    for j in range(t_ref.shape[0]):
        t_ref[j] = jnp.transpose(res[j * LANE:(j + 1) * LANE, :])[0:t_rows, :]


def _inproj(x2d, weights, out_dtypes, out_widths, scales, t_rows):
    assert weights[-1].shape[1] == LANE and out_dtypes[-1] == F32
    m = x2d.shape[0]
    tm = 2 * ROW_TILE
    in_specs = [pl.BlockSpec((tm, D_MODEL), lambda i: (i, 0))]
    in_specs += [_full_spec(w.shape) for w in weights]
    out_specs = [pl.BlockSpec((tm, n), lambda i: (i, 0)) for n in out_widths]
    out_specs.append(pl.BlockSpec((tm // LANE, t_rows, LANE), lambda i: (i, 0, 0)))
    out_shape = [jax.ShapeDtypeStruct((m, n), dt) for n, dt in zip(out_widths, out_dtypes)]
    out_shape.append(jax.ShapeDtypeStruct((m // LANE, t_rows, LANE), F32))
    return pl.pallas_call(
        functools.partial(_inproj_kernel, t_rows=t_rows, scales=tuple(scales)), grid=(m // tm,),
        in_specs=in_specs, out_specs=out_specs,
        out_shape=out_shape, compiler_params=_cparams(("parallel",)), name="inproj")(x2d, *weights)


def _gdn_kernel(q_ref, k_ref, v_ref, z_ref, cwq_ref, cwk_ref, cwv_ref, gcol_ref, grow_ref,
                alog_ref, dtb_ref, nw_ref, o_ref,
                xpad, qh, kh, vh, xp, qk_s, rhs, kd, qd, gl, sol, mq, ns, o0, st, *, seq):
    c = GDN_CHUNK
    pr = GDN_PAIR
    npair = seq // pr
    head = pl.program_id(1)
    pad = 8
    conv_rows = 256
    xpad[0:pad, :] = jnp.zeros((pad, 3 * LANE), F32)
    xpad[pad:, 0:LANE] = q_ref[...].astype(F32)
    xpad[pad:, LANE:2 * LANE] = k_ref[...].astype(F32)
    xpad[pad:, 2 * LANE:3 * LANE] = v_ref[...].astype(F32)
    cw = jnp.concatenate([cwq_ref[...], cwk_ref[...], cwv_ref[...]], axis=1)

    for t0 in range(0, seq, conv_rows):
        xe = xpad[t0:t0 + pad + conv_rows, :]
        acc = xe * cw[0:1, :]
        for j in range(1, GDN_CONV):
            acc = xe * cw[j:j + 1, :] + pltpu.roll(acc, 1, 0)
        acc = acc[pad:, :]
        y = acc * _sigmoid(acc)
        yq, yk, yv = y[:, 0:LANE], y[:, LANE:2 * LANE], y[:, 2 * LANE:3 * LANE]
        yq = yq * lax.rsqrt(jnp.sum(yq * yq, axis=-1, keepdims=True) + RMS_EPS) * (GDN_DK ** -0.5)
        yk = yk * lax.rsqrt(jnp.sum(yk * yk, axis=-1, keepdims=True) + RMS_EPS)
        qh[t0:t0 + conv_rows, :] = yq
        kh[t0:t0 + conv_rows, :] = yk
        vh[t0:t0 + conv_rows, :] = yv

    neg_a = -jnp.exp(alog_ref[0, :, 0:1])
    dtb = dtb_ref[0, :, 0:1]
    ii = lax.broadcasted_iota(jnp.int32, (pr, pr), 0)
    jj = lax.broadcasted_iota(jnp.int32, (pr, pr), 1)
    same = jnp.right_shift(ii, GDN_CHUNK_SHIFT) == jnp.right_shift(jj, GDN_CHUNK_SHIFT)
    incl = same & (jj <= ii)
    strict = same & (jj < ii)
    upper = same & (ii <= jj)
    eye = (ii == jj).astype(F32)
    sub8 = lax.broadcasted_iota(jnp.int32, (8, LANE), 0)
    lane_id = lax.broadcasted_iota(jnp.int32, (1, LANE), 1)

    def over_pairs(body):
        def run(gi, carry):
            for uu in range(GDN_UNROLL):
                body(gi * GDN_UNROLL + uu)
            return carry
        lax.fori_loop(0, npair // GDN_UNROLL, run, 0)

    def stage_gates(p):
        rows = pl.ds(pl.multiple_of(p * pr, pr), pr)
        q = qh[rows, :]
        k = kh[rows, :]
        v = vh[rows, :]
        gcol = gcol_ref[rows, :]
        b_logit = jnp.sum(jnp.where(lane_id == head, gcol, 0.0), axis=1, keepdims=True)
        a_logit = jnp.sum(jnp.where(lane_id == GDN_HEADS + head, gcol, 0.0), axis=1, keepdims=True)
        a_row = jnp.sum(jnp.where(sub8[:, 0:1] == GDN_HEADS + head, grow_ref[p], 0.0), axis=0, keepdims=True)
        beta = _sigmoid(b_logit)
        g_c = neg_a * _softplus(a_logit + dtb)
        g_r = neg_a * _softplus(a_row + dtb)
        gc_col = jnp.sum(jnp.where(incl, g_r, 0.0), axis=1, keepdims=True)
        gc_row = jnp.sum(jnp.where(upper, g_c, 0.0), axis=0, keepdims=True)
        gt_col = jnp.sum(jnp.where(same, g_r, 0.0), axis=1, keepdims=True)
        decay = jnp.where(incl, jnp.exp(jnp.where(incl, gc_col - gc_row, 0.0)), 0.0)
        kq_k = _dot_nt(jnp.concatenate([k, q], axis=0), k)
        a_mat = jnp.where(strict, kq_k[0:pr] * decay * beta, 0.0)
        xp[p, 0:pr, :] = eye - a_mat
        xp[p, pr:2 * pr, :] = _dot(a_mat, a_mat)
        qk_s[p] = (kq_k[pr:2 * pr] * decay).astype(qk_s.dtype)
        e_gc = jnp.exp(gc_col)
        rhs[rows, 0:GDN_DV] = (v * beta).astype(rhs.dtype)
        rhs[rows, GDN_DV:GDN_DV + GDN_DK] = (k * (beta * e_gc)).astype(rhs.dtype)
        kd[rows, :] = (k * jnp.exp(gt_col - gc_col)).astype(kd.dtype)
        qd[rows, :] = q * e_gc
        g_last = jnp.exp(gt_col)
        gl[p] = jnp.where(sub8 == 0, g_last[0:1, :], g_last[c:c + 1, :])

    def stage_double(p):
        y = _dot(xp[p], xp[p, pr:2 * pr, :])
        xp[p, 0:pr, :] = xp[p, 0:pr, :] + y[0:pr]
        xp[p, pr:2 * pr, :] = y[pr:2 * pr]

    def stage_solve(p):
        rows = pl.ds(pl.multiple_of(p * pr, pr), pr)
        x_inv = xp[p, 0:pr, :]
        x_inv = x_inv + _dot(x_inv, xp[p, pr:2 * pr, :])
        sol[rows, :] = _dot(x_inv, rhs[rows, :])

    def stage_affine(p):
        r0 = pl.multiple_of(p * pr, pr)
        rows = pl.ds(r0, pr)
        e = _dot(qk_s[p], sol[rows, :])
        qe = qd[rows, :] - e[:, GDN_DV:GDN_DV + GDN_DK]
        g_rows = gl[p]
        for cc in range(pr // c):
            rc = pl.ds(r0 + cc * c, c)
            mn = _dot_tn(kd[rc, :], sol[rc, :])
            idx = p * (pr // c) + cc
            mq[head, idx, 0:GDN_DK, :] = (eye * g_rows[cc:cc + 1, :] - mn[:, GDN_DV:GDN_DV + GDN_DK]).astype(mq.dtype)
            mq[head, idx, GDN_DK:GDN_DK + c, :] = qe[cc * c:(cc + 1) * c].astype(mq.dtype)
            ns[head, idx] = mn[:, 0:GDN_DV]
            o0[head, idx] = e[cc * c:(cc + 1) * c, 0:GDN_DV]

    over_pairs(stage_gates)
    for _ in range(int(math.log2(c)) - 2):
        over_pairs(stage_double)
    over_pairs(stage_solve)
    over_pairs(stage_affine)

    @pl.when(head == GDN_HEADS - 1)
    def _():
        st[...] = jnp.zeros_like(st)
        nw = nw_ref[...]

        def step(ci, carry):
            r0 = pl.multiple_of(ci * c, c)
            for h in range(GDN_HEADS):
                cols = slice(h * LANE, (h + 1) * LANE)
                ys = jnp.dot(mq[h, ci], st[h], preferred_element_type=F32)
                st[h] = (ys[0:GDN_DK] + ns[h, ci]).astype(st.dtype)
                o = ys[GDN_DK:GDN_DK + c] + o0[h, ci]
                o = o * lax.rsqrt(jnp.mean(o * o, axis=-1, keepdims=True) + RMS_EPS) * nw
                z = z_ref[pl.ds(r0, c), cols].astype(F32)
                o_ref[pl.ds(r0, c), cols] = (o * (z * _sigmoid(z))).astype(o_ref.dtype)
            return carry

        lax.fori_loop(0, seq // c, step, 0, unroll=8)


def _gdn(gqkvz, conv_w, gcol, grow, alog, dtb, nw, bsz, seq):
    assert GDN_DK == GDN_PAIR == GDN_DV == LANE
    h = GDN_HEADS
    npair = seq // GDN_PAIR
    nchunk = seq // GDN_CHUNK
    blk = lambda off: pl.BlockSpec((seq, LANE), lambda b, hh: (b, off + hh))
    cwb = lambda off: pl.BlockSpec((GDN_CONV, LANE), lambda b, hh: (0, off + hh))
    in_specs = [blk(0), blk(h), blk(2 * h),
                pl.BlockSpec((seq, GDN_VW), lambda b, hh: (b, 3)),
                cwb(0), cwb(h), cwb(2 * h),
                pl.BlockSpec((seq, LANE), lambda b, hh: (b, 0)),
                pl.BlockSpec((npair, 8, GDN_PAIR), lambda b, hh: (b, 0, 0)),
                pl.BlockSpec((1, 1, LANE), lambda b, hh: (hh, 0, 0)),
                pl.BlockSpec((1, 1, LANE), lambda b, hh: (hh, 0, 0)),
                pl.BlockSpec((1, LANE), lambda b, hh: (0, 0))]
    scratch = [pltpu.VMEM((seq + 8, 3 * LANE), F32),
               pltpu.VMEM((seq, LANE), F32), pltpu.VMEM((seq, LANE), F32), pltpu.VMEM((seq, LANE), F32),
               pltpu.VMEM((npair, 2 * GDN_PAIR, GDN_PAIR), F32),
               pltpu.VMEM((npair, GDN_PAIR, GDN_PAIR), BF16),
               pltpu.VMEM((seq, GDN_DV + GDN_DK), BF16),
               pltpu.VMEM((seq, GDN_DK), BF16),
               pltpu.VMEM((seq, GDN_DK), F32),
               pltpu.VMEM((npair, 8, LANE), F32),
               pltpu.VMEM((seq, GDN_DV + GDN_DK), F32),
               pltpu.VMEM((h, nchunk, GDN_DK + GDN_CHUNK, GDN_DK), BF16),
               pltpu.VMEM((h, nchunk, GDN_DK, GDN_DV), F32),
               pltpu.VMEM((h, nchunk, GDN_CHUNK, GDN_DV), F32),
               pltpu.VMEM((h, GDN_DK, GDN_DV), BF16)]
    return pl.pallas_call(
        functools.partial(_gdn_kernel, seq=seq), grid=(bsz, h), in_specs=in_specs,
        out_specs=pl.BlockSpec((seq, GDN_VW), lambda b, hh: (b, 0)),
        out_shape=jax.ShapeDtypeStruct((bsz * seq, GDN_VW), BF16),
        scratch_shapes=scratch,
        compiler_params=_cparams(("parallel", "arbitrary")), name="gdn",
    )(gqkvz, gqkvz, gqkvz, gqkvz, conv_w, conv_w, conv_w, gcol, grow, alog, dtb, nw)


def _gelu_tanh(x):
    return 0.5 * x * (1.0 + jnp.tanh(math.sqrt(2.0 / math.pi) * (x + 0.044715 * x * x * x)))


def _compress_kernel(k16_ref, v16_ref, wkc_ref, wvc_ref, w1k_ref, w1v_ref, pk_ref, pv_ref,
                     w2k_ref, w2v_ref, kc_ref, vc_ref):
    nblk = k16_ref.shape[1]
    for x_ref, wc_ref, w1_ref, p_ref, w2_ref, o_ref in (
            (k16_ref, wkc_ref, w1k_ref, pk_ref, w2k_ref, kc_ref),
            (v16_ref, wvc_ref, w1v_ref, pv_ref, w2v_ref, vc_ref)):
        y = jnp.dot(x_ref[0], wc_ref[...], preferred_element_type=F32)
        pos_term = jnp.dot(p_ref[...], w1_ref[...], preferred_element_type=F32)[0:1, :]
        for g in range(NSA_KV_HEADS):
            top = y[:, g * CMP_HIDDEN:(g + 1) * CMP_HIDDEN]
            bot = y[:, (NSA_KV_HEADS + g) * CMP_HIDDEN:(NSA_KV_HEADS + g + 1) * CMP_HIDDEN]
            hid = top + pltpu.roll(bot, nblk - 1, 0) + pos_term
            out = _dot(_gelu_tanh(hid), w2_ref[...])
            o_ref[0, g] = (jnp.transpose(out) if o_ref is vc_ref else out).astype(o_ref.dtype)


def _compress(k16, v16, wkc, wvc, w1k, w1v, pk, pv, w2k, w2v):
    bsz, nblk, width = k16.shape
    in_specs = [pl.BlockSpec((1, nblk, width), lambda b: (b, 0, 0))] * 2
    in_specs += [_full_spec(a.shape) for a in (wkc, wvc, w1k, w1v, pk, pv, w2k, w2v)]
    out_spec = pl.BlockSpec((1, NSA_KV_HEADS, nblk, HEAD_PAD), lambda b: (b, 0, 0, 0))
    out_shape = jax.ShapeDtypeStruct((bsz, NSA_KV_HEADS, nblk, HEAD_PAD), BF16)
    return pl.pallas_call(
        _compress_kernel, grid=(bsz,), in_specs=in_specs, out_specs=[out_spec, out_spec],
        out_shape=[out_shape, out_shape], compiler_params=_cparams(("parallel",)), name="nsa_compress",
    )(k16, v16, wkc, wvc, w1k, w1v, pk, pv, w2k, w2v)


def _nsa_kernel(q_ref, kv_ref, kc_ref, vct_ref, sm_ref, ovt_ref, o_ref, ksa, vst, vwt, *, seq):
    step = pl.program_id(1)
    n_cmp_pad = kc_ref.shape[2]
    n_sel_blk = seq // SEL_BLOCK
    kb_rows = SEL_KEY_BLOCK
    one_row = NSA_DH
    groups = range(NSA_KV_HEADS)
    units = [(sub, g) for sub in range(NSA_Q_SUB) for g in groups]

    @pl.when(step == 0)
    def _():
        lane = lax.broadcasted_iota(jnp.int32, (1, NSA_KV_HEADS * HEAD_PAD), 1) & (HEAD_PAD - 1)
        row_t = lax.broadcasted_iota(jnp.int32, (HEAD_PAD, 1), 0)
        tiles = kb_rows // Q_BLOCK

        def fill(bi, carry):
            r0 = pl.multiple_of(bi * kb_rows, kb_rows)
            rows = pl.ds(r0, kb_rows)
            blk = jnp.right_shift(r0 + lax.broadcasted_iota(jnp.int32, (kb_rows, 1), 0), SEL_SHIFT)
            ksa[rows, :] = kv_ref[rows, 0:2 * HEAD_PAD] + (lane == NSA_DH + blk).astype(ksa.dtype)
            for g in range(NSA_KV_HEADS):
                v_t = jnp.transpose(kv_ref[rows, (2 + g) * HEAD_PAD:(3 + g) * HEAD_PAD].astype(F32))
                vst[bi, g] = jnp.where(row_t == one_row, 1.0, v_t).astype(vst.dtype)
                w_t = jnp.transpose(kv_ref[rows, (6 + g) * HEAD_PAD:(7 + g) * HEAD_PAD].astype(F32))
                w_t = jnp.where(row_t == one_row, 1.0, w_t).astype(vwt.dtype)
                for part in range(tiles):
                    vwt[bi * tiles + part, g] = w_t[:, part * Q_BLOCK:(part + 1) * Q_BLOCK]
            return carry

        lax.fori_loop(0, seq // kb_rows, fill, 0)

    tok = lax.broadcasted_iota(jnp.int32, (1, Q_BLOCK), 1)
    lane_q = lax.broadcasted_iota(jnp.int32, (1, HEAD_PAD), 1)
    pen_lanes = (lane_q >= NSA_DH) & (lane_q < NSA_DH + n_sel_blk)
    place = (lax.broadcasted_iota(jnp.int32, (n_sel_blk, HEAD_PAD), 1)
             == NSA_DH + lax.broadcasted_iota(jnp.int32, (n_sel_blk, HEAD_PAD), 0)).astype(BF16)
    n_sub = lax.broadcasted_iota(jnp.int32, (n_cmp_pad, 1), 0)

    def add_bias(s, bias):
        return jnp.concatenate([s[:, r * Q_BLOCK:(r + 1) * Q_BLOCK] + bias for r in range(s.shape[1] // Q_BLOCK)],
                               axis=1)

    q0_s, t_lane_s, gates_s, qp, o_cmp = [], [], [], {}, {}
    for sub in range(NSA_Q_SUB):
        q0 = (step * NSA_Q_SUB + sub) * Q_BLOCK
        tok_rows = slice(sub * Q_BLOCK, (sub + 1) * Q_BLOCK)
        t_lane = q0 + tok
        q0_s.append(q0)
        t_lane_s.append(t_lane)
        gates_s.append(jnp.transpose(_sigmoid(sm_ref[tok_rows, :])))

        cmp_bias = jnp.where((n_sub * CMP_STRIDE + (CMP_BLOCK - 1)) <= t_lane, 0.0, NEG_BIG)
        cmp_any = jnp.concatenate([(t_lane >= CMP_BLOCK - 1).astype(F32)] * NSA_REP, axis=1)

        for g in groups:
            qs = jnp.concatenate(
                [q_ref[tok_rows, (g * NSA_REP + r) * HEAD_PAD:(g * NSA_REP + r + 1) * HEAD_PAD]
                 for r in range(NSA_REP)], axis=0)

            s_c = add_bias(_dot_nt(kc_ref[0, g], qs), cmp_bias)
            e_c = jnp.exp2(s_c - jnp.max(s_c, axis=0, keepdims=True))
            p_c = e_c * (cmp_any / jnp.sum(e_c, axis=0, keepdims=True))
            o_cmp[sub, g] = _dot(vct_ref[0, g], p_c)[0:NSA_DH]

            p_sum = p_c[:, 0:Q_BLOCK]
            for r in range(1, NSA_REP):
                p_sum = p_sum + p_c[:, r * Q_BLOCK:(r + 1) * Q_BLOCK]
            imp_t = _dot(ovt_ref[...], p_sum)

            def rank_select(imp_t=imp_t, t_lane=t_lane):
                j_sub = lax.broadcasted_iota(jnp.int32, (n_sel_blk, 1), 0)
                cur = jnp.right_shift(t_lane, SEL_SHIFT)
                forced = (j_sub == 0) | (j_sub == cur) | (j_sub == cur - 1)
                causal_blk = (j_sub * SEL_BLOCK) <= t_lane
                imp = jnp.where(forced, FORCE_SCORE, jnp.where(causal_blk, imp_t, -jnp.inf))
                rank = jnp.zeros((n_sel_blk, Q_BLOCK), F32)
                for jp in range(n_sel_blk):
                    row = imp[jp:jp + 1, :]
                    ahead = (row > imp) | ((row == imp) & (j_sub > jp))
                    rank = rank + ahead.astype(F32)
                return (rank < SEL_TOPK).astype(BF16)

            sel_t = lax.cond(q0 + Q_BLOCK <= SEL_TOPK * SEL_BLOCK,
                             lambda: jnp.ones((n_sel_blk, Q_BLOCK), BF16), rank_select)
            sel_lanes = _dot_tn(sel_t, place)
            pen = jnp.where(pen_lanes, (sel_lanes - 1.0) * (-NEG_BIG), 0.0).astype(BF16)
            qp[sub, g] = qs + jnp.concatenate([pen] * NSA_REP, axis=0)

    n_full = step

    def scores(kb, unit):
        k0 = pl.multiple_of(kb * kb_rows, kb_rows)
        g = unit[1]
        return _dot_nt(ksa[pl.ds(k0, kb_rows), g * HEAD_PAD:(g + 1) * HEAD_PAD], qp[unit])

    def absorb(state, s, kb, unit):
        m_i, acc = state
        m_new = jnp.maximum(m_i, jnp.max(s, axis=0, keepdims=True))
        return m_new, jnp.exp2(m_i - m_new) * acc + _dot(vst[kb, unit[1]], jnp.exp2(s - m_new))

    kpos_last = n_full * kb_rows + lax.broadcasted_iota(jnp.int32, (kb_rows, 1), 0)
    first = []
    for unit in units:
        s = add_bias(scores(n_full, unit), jnp.where(kpos_last <= t_lane_s[unit[0]], 0.0, NEG_BIG))
        m = jnp.max(s, axis=0, keepdims=True)
        first.append((m, _dot(vst[n_full, unit[1]], jnp.exp2(s - m))))

    def full_blocks(state):
        def loop_body(kb, carry):
            st, s_cur = carry
            s_next = tuple(scores(kb + 1, u) for u in units)
            return tuple(absorb(st[i], s_cur[i], kb, u) for i, u in enumerate(units)), s_next

        st, s_cur = lax.fori_loop(0, n_full - 1, loop_body, (state, tuple(scores(0, u) for u in units)))
        return tuple(absorb(st[i], s_cur[i], n_full - 1, u) for i, u in enumerate(units))

    sel_out = lax.cond(n_full > 0, full_blocks, lambda st: st, tuple(first))

    span = WINDOW + Q_BLOCK
    for i, (sub, g) in enumerate(units):
        q0, t_lane = q0_s[sub], t_lane_s[sub]
        tok_rows = slice(sub * Q_BLOCK, (sub + 1) * Q_BLOCK)
        w0 = pl.multiple_of(jnp.maximum(q0 - WINDOW, 0), Q_BLOCK)
        kpos_w = w0 + lax.broadcasted_iota(jnp.int32, (span, 1), 0)
        win_bias = jnp.where((kpos_w <= t_lane) & (kpos_w > t_lane - WINDOW), 0.0, NEG_BIG)

        acc_s = sel_out[i][1]
        o_slc = acc_s[0:NSA_DH] * (1.0 / acc_s[one_row:one_row + 1, :])

        k_w = kv_ref[pl.ds(w0, span), (4 + g) * HEAD_PAD:(5 + g) * HEAD_PAD]
        s_w = add_bias(_dot_nt(k_w, qp[sub, g]), win_bias)
        tile0 = lax.div(w0, Q_BLOCK)
        v_wt = jnp.concatenate([vwt[tile0 + t, g] for t in range(span // Q_BLOCK)], axis=1)
        acc_w = _dot(v_wt, jnp.exp2(s_w - jnp.max(s_w, axis=0, keepdims=True)))
        o_win = acc_w[0:NSA_DH] * (1.0 / acc_w[one_row:one_row + 1, :])

        gates_t = gates_s[sub]
        o_heads = []
        for r in range(NSA_REP):
            c0 = 8 + (g * NSA_REP + r) * 3
            sl = slice(r * Q_BLOCK, (r + 1) * Q_BLOCK)
            o_t = (gates_t[c0:c0 + 1, :] * o_cmp[sub, g][:, sl] + gates_t[c0 + 1:c0 + 2, :] * o_slc[:, sl]
                   + gates_t[c0 + 2:c0 + 3, :] * o_win[:, sl])
            o_heads.append(o_t)
        for pair in range(NSA_REP // 2):
            lanes = slice((g * NSA_REP + 2 * pair) * NSA_DH, (g * NSA_REP + 2 * pair + 2) * NSA_DH)
            o_ref[tok_rows, lanes] = jnp.transpose(jnp.concatenate(o_heads[2 * pair:2 * pair + 2], axis=0)
                                                   ).astype(o_ref.dtype)


def _nsa(nq, nkv, kc, vc, small, overlap, bsz, seq):
    assert NSA_Q_SUB * Q_BLOCK == SEL_KEY_BLOCK
    rows = NSA_Q_SUB * Q_BLOCK
    nstep = seq // rows
    in_specs = [pl.BlockSpec((rows, NSA_HEADS * HEAD_PAD), lambda b, i: (b * nstep + i, 0)),
                pl.BlockSpec((seq, nkv.shape[1]), lambda b, i: (b, 0)),
                pl.BlockSpec((1,) + kc.shape[1:], lambda b, i: (b, 0, 0, 0)),
                pl.BlockSpec((1,) + vc.shape[1:], lambda b, i: (b, 0, 0, 0)),
                pl.BlockSpec((rows, LANE), lambda b, i: (b * nstep + i, 0)),
                _full_spec(overlap.shape)]
    return pl.pallas_call(
        functools.partial(_nsa_kernel, seq=seq), grid=(bsz, nstep), in_specs=in_specs,
        out_specs=pl.BlockSpec((rows, NSA_QW), lambda b, i: (b * nstep + i, 0)),
        out_shape=jax.ShapeDtypeStruct((bsz * seq, NSA_QW), BF16),
        scratch_shapes=[pltpu.VMEM((seq, NSA_KV_HEADS * HEAD_PAD), BF16),
                        pltpu.VMEM((seq // SEL_KEY_BLOCK, NSA_KV_HEADS, HEAD_PAD, SEL_KEY_BLOCK), BF16),
                        pltpu.VMEM((seq // Q_BLOCK, NSA_KV_HEADS, HEAD_PAD, Q_BLOCK), BF16)],
        compiler_params=_cparams(("parallel", "arbitrary")), name="nsa_attention",
    )(nq, nkv, kc, vc, small, overlap)


def _merge_kernel(oa_ref, ob_ref, mix_ref, x_ref, wa_ref, wb_ref, wo_ref, g_ref, b_ref, h_ref):
    ya = jnp.dot(oa_ref[...], wa_ref[...], preferred_element_type=F32)
    yb = jnp.dot(ob_ref[...], wb_ref[...], preferred_element_type=F32)
    mix = mix_ref[...].astype(F32)
    mixed = _sigmoid(mix[:, 0:D_MODEL]) * ya + _sigmoid(mix[:, D_MODEL:2 * D_MODEL]) * yb
    y = _dot(mixed, wo_ref[...])
    h_ref[...] = _layer_norm(DN_ALPHA * x_ref[...] + y, g_ref[...], b_ref[...])


def _merge(oa, ob, mix, x2d, wa, wb, wo, g, b):
    m = x2d.shape[0]
    tm = 2 * ROW_TILE
    row = lambda n: pl.BlockSpec((tm, n), lambda i: (i, 0))
    in_specs = [row(oa.shape[1]), row(ob.shape[1]), row(mix.shape[1]), row(D_MODEL),
                _full_spec(wa.shape), _full_spec(wb.shape), _full_spec(wo.shape),
                _full_spec(g.shape), _full_spec(b.shape)]
    return pl.pallas_call(
        _merge_kernel, grid=(m // tm,), in_specs=in_specs, out_specs=row(D_MODEL),
        out_shape=jax.ShapeDtypeStruct((m, D_MODEL), F32),
        compiler_params=_cparams(("parallel",)), name="merge_ln",
    )(oa, ob, mix, x2d, wa, wb, wo, g, b)


def _ffn_kernel(h_ref, halo_ref, wup_ref, cw_ref, wd_ref, g_ref, b_ref, o_ref,
                hb, ug, uv, act, *, seq, tm, halo):
    i = pl.program_id(0)
    first = lax.rem(i * tm, seq) == 0
    hb[0:halo, :] = jnp.where(first, 0.0, halo_ref[...]).astype(BF16)
    hb[halo:, :] = h_ref[...].astype(BF16)

    def conv(u_ref, slot, cols):
        out = None
        for tap in range(FFN_CONV):
            term = cw_ref[tap:tap + 1, cols] * u_ref[slot, pl.ds(halo - (FFN_CONV - 1) + tap, tm), :]
            out = term if out is None else out + term
        return out

    for j in range(D_FF // FFN_CHUNK):
        slot = j % FFN_SLOTS
        cols_g = slice(j * FFN_CHUNK, (j + 1) * FFN_CHUNK)
        cols_v = slice(D_FF + j * FFN_CHUNK, D_FF + (j + 1) * FFN_CHUNK)
        ug[slot] = jnp.dot(hb[...], wup_ref[:, cols_g], preferred_element_type=F32)
        uv[slot] = jnp.dot(hb[...], wup_ref[:, cols_v], preferred_element_type=F32)
        a = conv(ug, slot, cols_g)
        act[:, cols_g] = (a * _sigmoid(a) * conv(uv, slot, cols_v)).astype(act.dtype)

    f = jnp.dot(act[...], wd_ref[...], preferred_element_type=F32)
    o_ref[...] = _layer_norm(DN_ALPHA * h_ref[...] + f, g_ref[...], b_ref[...])


def _ffn(h, w_up, conv_w, w_down, g, b, seq):
    m = h.shape[0]
    tm = ROW_TILE
    halo = 16
    resident = lambda shape: pl.BlockSpec(shape, lambda i: (0, 0), pipeline_mode=pl.Buffered(1))
    in_specs = [pl.BlockSpec((tm, D_MODEL), lambda i: (i, 0)),
                pl.BlockSpec((halo, D_MODEL), lambda i: (jnp.maximum(i * (tm // halo) - 1, 0), 0)),
                resident(w_up.shape), resident(conv_w.shape), resident(w_down.shape),
                resident(g.shape), resident(b.shape)]
    return pl.pallas_call(
        functools.partial(_ffn_kernel, seq=seq, tm=tm, halo=halo), grid=(m // tm,),
        in_specs=in_specs, out_specs=pl.BlockSpec((tm, D_MODEL), lambda i: (i, 0)),
        out_shape=jax.ShapeDtypeStruct((m, D_MODEL), F32),
        scratch_shapes=[pltpu.VMEM((tm + halo, D_MODEL), BF16),
                        pltpu.VMEM((FFN_SLOTS, tm + halo, FFN_CHUNK), F32),
                        pltpu.VMEM((FFN_SLOTS, tm + halo, FFN_CHUNK), F32),
                        pltpu.VMEM((tm, D_FF), BF16)],
        compiler_params=_cparams(("parallel",)), name="conv_ffn_ln",
    )(h, h, w_up, conv_w, w_down, g, b)


def _compress_weights(w1):
    half = CMP_BLOCK // 2
    w1r = w1.reshape(2, half, NSA_DH, CMP_HIDDEN)
    eye = jnp.eye(NSA_KV_HEADS, dtype=w1.dtype)
    ex = jnp.einsum("aidh,gk->agikdh", w1r, eye)
    ex = ex.reshape(2 * NSA_KV_HEADS, half * NSA_KVW, CMP_HIDDEN)
    return jnp.concatenate([ex[n] for n in range(2 * NSA_KV_HEADS)], axis=1)


def _overlap_matrix(seq):
    n_cmp = (seq - CMP_BLOCK) // CMP_STRIDE + 1
    n_sel = seq // SEL_BLOCK
    starts = np.arange(n_cmp) * CMP_STRIDE
    jb = np.arange(n_sel) * SEL_BLOCK
    ov = ((starts[:, None] < jb[None] + SEL_BLOCK) & (starts[:, None] + CMP_BLOCK > jb[None])).astype(np.float32)
    out = np.zeros((n_sel, seq // CMP_STRIDE), np.float32)
    out[:, :n_cmp] = ov.T
    return jnp.asarray(out, BF16)


def kernel(x, w_in, gdn_conv_w, gdn_a_log, gdn_dt_bias, gdn_norm_w, cmp_pos_k, cmp_w1_k, cmp_w2_k,
           cmp_pos_v, cmp_w1_v, cmp_w2_v, w_branch_gdn, w_branch_nsa, w_out, ln1_g, ln1_b, w_up,
           ffn_conv_w, w_down, ln2_g, ln2_b):
    bsz, seq, _ = x.shape
    m = bsz * seq
    for i in range(DEPTH):
        x2d = x.reshape(m, D_MODEL)
        w = w_in[i]
        o_gdn = 2 * GDN_QK + 2 * GDN_VW
        o_nq = o_gdn + 2 * GDN_HEADS
        o_kv = o_nq + NSA_QW
        o_gate = o_kv + 6 * NSA_KVW
        o_mix = o_gate + 3 * NSA_HEADS
        kv = [w[:, o_kv + n * NSA_KVW:o_kv + (n + 1) * NSA_KVW] for n in range(6)]
        small_w = jnp.concatenate([w[:, o_gdn:o_nq], w[:, o_gate:o_mix]], axis=1)
        small_w = jnp.pad(small_w, ((0, 0), (0, LANE - small_w.shape[1])))
        weights = [w[:, :o_gdn],
                   w[:, o_nq:o_kv],
                   w[:, o_kv + 2 * NSA_KVW:o_gate],
                   kv[0], kv[1],
                   w[:, o_mix:],
                   small_w]
        weights = [t.astype(BF16) for t in weights]
        widths = [o_gdn, NSA_HEADS * HEAD_PAD, 4 * NSA_KV_HEADS * HEAD_PAD, NSA_KVW, NSA_KVW, 2 * D_MODEL, LANE]
        scales = [1.0, NSA_DH ** -0.5 * LOG2E, 1.0, 1.0, 1.0, 1.0, 1.0]
        gqkvz, nq, nkv, kcp, vcp, mix, small, grow = _inproj(x2d, weights, [BF16] * 6 + [F32], widths, scales,
                                                             2 * GDN_HEADS)

        alog = jnp.broadcast_to(gdn_a_log[i][:, None, None], (GDN_HEADS, 1, LANE))
        dtb = jnp.broadcast_to(gdn_dt_bias[i][:, None, None], (GDN_HEADS, 1, LANE))
        o_a = _gdn(gqkvz, gdn_conv_w[i], small, grow, alog, dtb, gdn_norm_w[i][None, :], bsz, seq)

        grp = seq // CMP_STRIDE
        k16 = kcp.reshape(bsz, grp, CMP_STRIDE * NSA_KVW)
        v16 = vcp.reshape(bsz, grp, CMP_STRIDE * NSA_KVW)
        pos_rows = lambda p: jnp.pad(p.reshape(1, CMP_BLOCK * NSA_DH), ((0, 7), (0, 0))).astype(BF16)
        w2_pad = lambda t: jnp.pad(t, ((0, 0), (0, HEAD_PAD - NSA_DH))).astype(BF16)
        kc, vc = _compress(k16, v16,
                           _compress_weights(cmp_w1_k[i]).astype(BF16), _compress_weights(cmp_w1_v[i]).astype(BF16),
                           cmp_w1_k[i].astype(BF16), cmp_w1_v[i].astype(BF16),
                           pos_rows(cmp_pos_k[i]), pos_rows(cmp_pos_v[i]),
                           w2_pad(cmp_w2_k[i]), w2_pad(cmp_w2_v[i]))
        o_b = _nsa(nq, nkv, kc, vc, small, _overlap_matrix(seq), bsz, seq)

        h = _merge(o_a, o_b, mix, x2d, w_branch_gdn[i].astype(BF16), w_branch_nsa[i].astype(BF16), w_out[i].astype(BF16),
                   ln1_g[i][None, :], ln1_b[i][None, :])

        x = _ffn(h, w_up[i].astype(BF16), ffn_conv_w[i], w_down[i].astype(BF16),
                 ln2_g[i][None, :], ln2_b[i][None, :], seq).reshape(bsz, seq, D_MODEL)
    return x
```

```python
import functools
import math

import numpy as np
import jax
import jax.numpy as jnp
from jax import lax
from jax.experimental import pallas as pl
from jax.experimental.pallas import tpu as pltpu

F32 = jnp.float32
BF16 = jnp.bfloat16

D_MODEL = 1024
GDN_HEADS = 4
GDN_DK = 128
GDN_DV = 128
GDN_CONV = 4
GDN_CHUNK = 64
NSA_HEADS = 8
NSA_KV_HEADS = 2
NSA_REP = NSA_HEADS // NSA_KV_HEADS
NSA_DH = 64
CMP_BLOCK = 32
CMP_STRIDE = 16
CMP_HIDDEN = 256
SEL_BLOCK = 64
SEL_SHIFT = 6
SEL_TOPK = 16
WINDOW = 512
FORCE_SCORE = 1e9
D_FF = 2816
FFN_CONV = 3
DEPTH = 1
DN_ALPHA = (2 * DEPTH) ** 0.25
LN_EPS = 1e-5
RMS_EPS = 1e-6
LOG2E = math.log2(math.e)

GDN_QK = GDN_HEADS * GDN_DK
GDN_VW = GDN_HEADS * GDN_DV
NSA_QW = NSA_HEADS * NSA_DH
NSA_KVW = NSA_KV_HEADS * NSA_DH

LANE = 128
HEAD_PAD = LANE
NEG_BIG = -1e30
VMEM_LIMIT = 56 * 1024 * 1024

Q_BLOCK = 128
NSA_Q_SUB = 2
SEL_KEY_BLOCK = 256
FFN_CHUNK = 256
FFN_SLOTS = 2
ROW_TILE = 512
GDN_CHUNK_SHIFT = 6
GDN_PAIR = 2 * GDN_CHUNK
GDN_UNROLL = 16


def _cparams(sem):
    return pltpu.CompilerParams(dimension_semantics=sem, vmem_limit_bytes=VMEM_LIMIT)


def _dot(a, b):
    return jnp.dot(a.astype(BF16), b.astype(BF16), preferred_element_type=F32)


def _dot_nt(a, b):
    return lax.dot_general(a.astype(BF16), b.astype(BF16), (((1,), (1,)), ((), ())),
                           preferred_element_type=F32)


def _dot_tn(a, b):
    return lax.dot_general(a.astype(BF16), b.astype(BF16), (((0,), (0,)), ((), ())),
                           preferred_element_type=F32)


def _sigmoid(x):
    return jax.nn.sigmoid(x)


def _softplus(x):
    return jnp.maximum(x, 0.0) + jnp.log1p(jnp.exp(-jnp.abs(x)))


def _layer_norm(x, g, b):
    mu = jnp.mean(x, axis=-1, keepdims=True)
    xc = x - mu
    var = jnp.mean(xc * xc, axis=-1, keepdims=True)
    return xc * lax.rsqrt(var + LN_EPS) * g + b


def _full_spec(shape):
    nd = len(shape)
    return pl.BlockSpec(shape, lambda *_: (0,) * nd, pipeline_mode=pl.Buffered(1))


def _inproj_kernel(x_ref, *refs, t_rows, scales):
    n = (len(refs) - 1) // 2
    xb = x_ref[...].astype(BF16)
    for w_ref, o_ref, scale in zip(refs[:n], refs[n:2 * n], scales):
        res = jnp.dot(xb, w_ref[...], preferred_element_type=F32)
        if scale != 1.0:
            res = res * scale
        if o_ref.shape[1] == w_ref.shape[1]:
            o_ref[...] = res.astype(o_ref.dtype)
        else:
            out = res.astype(o_ref.dtype)
            zeros = jnp.zeros((out.shape[0], HEAD_PAD - NSA_DH), o_ref.dtype)
            for hd in range(w_ref.shape[1] // NSA_DH):
                o_ref[:, hd * HEAD_PAD:hd * HEAD_PAD + NSA_DH] = out[:, hd * NSA_DH:(hd + 1) * NSA_DH]
                o_ref[:, hd * HEAD_PAD + NSA_DH:(hd + 1) * HEAD_PAD] = zeros
    t_ref = refs[2 * n]
    for j in range(t_ref.shape[0]):
        t_ref[j] = jnp.transpose(res[j * LANE:(j + 1) * LANE, :])[0:t_rows, :]


def _inproj(x2d, weights, out_dtypes, out_widths, scales, t_rows):
    assert weights[-1].shape[1] == LANE and out_dtypes[-1] == F32
    m = x2d.shape[0]
    tm = 2 * ROW_TILE
    in_specs = [pl.BlockSpec((tm, D_MODEL), lambda i: (i, 0))]
    in_specs += [_full_spec(w.shape) for w in weights]
    out_specs = [pl.BlockSpec((tm, n), lambda i: (i, 0)) for n in out_widths]
    out_specs.append(pl.BlockSpec((tm // LANE, t_rows, LANE), lambda i: (i, 0, 0)))
    out_shape = [jax.ShapeDtypeStruct((m, n), dt) for n, dt in zip(out_widths, out_dtypes)]
    out_shape.append(jax.ShapeDtypeStruct((m // LANE, t_rows, LANE), F32))
    return pl.pallas_call(
        functools.partial(_inproj_kernel, t_rows=t_rows, scales=tuple(scales)), grid=(m // tm,),
        in_specs=in_specs, out_specs=out_specs,
        out_shape=out_shape, compiler_params=_cparams(("parallel",)), name="inproj")(x2d, *weights)


def _gdn_kernel(q_ref, k_ref, v_ref, z_ref, cwq_ref, cwk_ref, cwv_ref, gcol_ref, grow_ref,
                alog_ref, dtb_ref, nw_ref, o_ref,
                xpad, qh, kh, vh, xp, qk_s, rhs, kd, qd, gl, sol, mq, ns, o0, st, *, seq):
    c = GDN_CHUNK
    pr = GDN_PAIR
    npair = seq // pr
    head = pl.program_id(1)
    pad = 8
    conv_rows = 256
    xpad[0:pad, :] = jnp.zeros((pad, 3 * LANE), F32)
    xpad[pad:, 0:LANE] = q_ref[...].astype(F32)
    xpad[pad:, LANE:2 * LANE] = k_ref[...].astype(F32)
    xpad[pad:, 2 * LANE:3 * LANE] = v_ref[...].astype(F32)
    cw = jnp.concatenate([cwq_ref[...], cwk_ref[...], cwv_ref[...]], axis=1)

    for t0 in range(0, seq, conv_rows):
        xe = xpad[t0:t0 + pad + conv_rows, :]
        acc = xe * cw[0:1, :]
        for j in range(1, GDN_CONV):
            acc = xe * cw[j:j + 1, :] + pltpu.roll(acc, 1, 0)
        acc = acc[pad:, :]
        y = acc * _sigmoid(acc)
        yq, yk, yv = y[:, 0:LANE], y[:, LANE:2 * LANE], y[:, 2 * LANE:3 * LANE]
        yq = yq * lax.rsqrt(jnp.sum(yq * yq, axis=-1, keepdims=True) + RMS_EPS) * (GDN_DK ** -0.5)
        yk = yk * lax.rsqrt(jnp.sum(yk * yk, axis=-1, keepdims=True) + RMS_EPS)
        qh[t0:t0 + conv_rows, :] = yq
        kh[t0:t0 + conv_rows, :] = yk
        vh[t0:t0 + conv_rows, :] = yv

    neg_a = -jnp.exp(alog_ref[0, :, 0:1])
    dtb = dtb_ref[0, :, 0:1]
    ii = lax.broadcasted_iota(jnp.int32, (pr, pr), 0)
    jj = lax.broadcasted_iota(jnp.int32, (pr, pr), 1)
    same = jnp.right_shift(ii, GDN_CHUNK_SHIFT) == jnp.right_shift(jj, GDN_CHUNK_SHIFT)
    incl = same & (jj <= ii)
    strict = same & (jj < ii)
    upper = same & (ii <= jj)
    eye = (ii == jj).astype(F32)
    sub8 = lax.broadcasted_iota(jnp.int32, (8, LANE), 0)
    lane_id = lax.broadcasted_iota(jnp.int32, (1, LANE), 1)

    def over_pairs(body):
        def run(gi, carry):
            for uu in range(GDN_UNROLL):
                body(gi * GDN_UNROLL + uu)
            return carry
        lax.fori_loop(0, npair // GDN_UNROLL, run, 0)

    def stage_gates(p):
        rows = pl.ds(pl.multiple_of(p * pr, pr), pr)
        q = qh[rows, :]
        k = kh[rows, :]
        v = vh[rows, :]
        gcol = gcol_ref[rows, :]
        b_logit = jnp.sum(jnp.where(lane_id == head, gcol, 0.0), axis=1, keepdims=True)
        a_logit = jnp.sum(jnp.where(lane_id == GDN_HEADS + head, gcol, 0.0), axis=1, keepdims=True)
        a_row = jnp.sum(jnp.where(sub8[:, 0:1] == GDN_HEADS + head, grow_ref[p], 0.0), axis=0, keepdims=True)
        beta = _sigmoid(b_logit)
        g_c = neg_a * _softplus(a_logit + dtb)
        g_r = neg_a * _softplus(a_row + dtb)
        gc_col = jnp.sum(jnp.where(incl, g_r, 0.0), axis=1, keepdims=True)
        gc_row = jnp.sum(jnp.where(upper, g_c, 0.0), axis=0, keepdims=True)
        gt_col = jnp.sum(jnp.where(same, g_r, 0.0), axis=1, keepdims=True)
        decay = jnp.where(incl, jnp.exp(jnp.where(incl, gc_col - gc_row, 0.0)), 0.0)
        kq_k = _dot_nt(jnp.concatenate([k, q], axis=0), k)
        a_mat = jnp.where(strict, kq_k[0:pr] * decay * beta, 0.0)
        xp[p, 0:pr, :] = eye - a_mat
        xp[p, pr:2 * pr, :] = _dot(a_mat, a_mat)
        qk_s[p] = (kq_k[pr:2 * pr] * decay).astype(qk_s.dtype)
        e_gc = jnp.exp(gc_col)
        rhs[rows, 0:GDN_DV] = (v * beta).astype(rhs.dtype)
        rhs[rows, GDN_DV:GDN_DV + GDN_DK] = (k * (beta * e_gc)).astype(rhs.dtype)
        kd[rows, :] = (k * jnp.exp(gt_col - gc_col)).astype(kd.dtype)
        qd[rows, :] = q * e_gc
        g_last = jnp.exp(gt_col)
        gl[p] = jnp.where(sub8 == 0, g_last[0:1, :], g_last[c:c + 1, :])

    def stage_double(p):
        y = _dot(xp[p], xp[p, pr:2 * pr, :])
        xp[p, 0:pr, :] = xp[p, 0:pr, :] + y[0:pr]
        xp[p, pr:2 * pr, :] = y[pr:2 * pr]

    def stage_solve(p):
        rows = pl.ds(pl.multiple_of(p * pr, pr), pr)
        x_inv = xp[p, 0:pr, :]
        x_inv = x_inv + _dot(x_inv, xp[p, pr:2 * pr, :])
        sol[rows, :] = _dot(x_inv, rhs[rows, :])

    def stage_affine(p):
        r0 = pl.multiple_of(p * pr, pr)
        rows = pl.ds(r0, pr)
        e = _dot(qk_s[p], sol[rows, :])
        qe = qd[rows, :] - e[:, GDN_DV:GDN_DV + GDN_DK]
        g_rows = gl[p]
        for cc in range(pr // c):
            rc = pl.ds(r0 + cc * c, c)
            mn = _dot_tn(kd[rc, :], sol[rc, :])
            idx = p * (pr // c) + cc
            mq[head, idx, 0:GDN_DK, :] = (eye * g_rows[cc:cc + 1, :] - mn[:, GDN_DV:GDN_DV + GDN_DK]).astype(mq.dtype)
            mq[head, idx, GDN_DK:GDN_DK + c, :] = qe[cc * c:(cc + 1) * c].astype(mq.dtype)
            ns[head, idx] = mn[:, 0:GDN_DV]
            o0[head, idx] = e[cc * c:(cc + 1) * c, 0:GDN_DV]

    over_pairs(stage_gates)
    for _ in range(int(math.log2(c)) - 2):
        over_pairs(stage_double)
    over_pairs(stage_solve)
    over_pairs(stage_affine)

    @pl.when(head == GDN_HEADS - 1)
    def _():
        st[...] = jnp.zeros_like(st)
        nw = nw_ref[...]

        def step(ci, carry):
            r0 = pl.multiple_of(ci * c, c)
            for h in range(GDN_HEADS):
                cols = slice(h * LANE, (h + 1) * LANE)
                ys = jnp.dot(mq[h, ci], st[h], preferred_element_type=F32)
                st[h] = (ys[0:GDN_DK] + ns[h, ci]).astype(st.dtype)
                o = ys[GDN_DK:GDN_DK + c] + o0[h, ci]
                o = o * lax.rsqrt(jnp.mean(o * o, axis=-1, keepdims=True) + RMS_EPS) * nw
                z = z_ref[pl.ds(r0, c), cols].astype(F32)
                o_ref[pl.ds(r0, c), cols] = (o * (z * _sigmoid(z))).astype(o_ref.dtype)
            return carry

        lax.fori_loop(0, seq // c, step, 0, unroll=8)


def _gdn(gqkvz, conv_w, gcol, grow, alog, dtb, nw, bsz, seq):
    assert GDN_DK == GDN_PAIR == GDN_DV == LANE
    h = GDN_HEADS
    npair = seq // GDN_PAIR
    nchunk = seq // GDN_CHUNK
    blk = lambda off: pl.BlockSpec((seq, LANE), lambda b, hh: (b, off + hh))
    cwb = lambda off: pl.BlockSpec((GDN_CONV, LANE), lambda b, hh: (0, off + hh))
    in_specs = [blk(0), blk(h), blk(2 * h),
                pl.BlockSpec((seq, GDN_VW), lambda b, hh: (b, 3)),
                cwb(0), cwb(h), cwb(2 * h),
                pl.BlockSpec((seq, LANE), lambda b, hh: (b, 0)),
                pl.BlockSpec((npair, 8, GDN_PAIR), lambda b, hh: (b, 0, 0)),
                pl.BlockSpec((1, 1, LANE), lambda b, hh: (hh, 0, 0)),
                pl.BlockSpec((1, 1, LANE), lambda b, hh: (hh, 0, 0)),
                pl.BlockSpec((1, LANE), lambda b, hh: (0, 0))]
    scratch = [pltpu.VMEM((seq + 8, 3 * LANE), F32),
               pltpu.VMEM((seq, LANE), F32), pltpu.VMEM((seq, LANE), F32), pltpu.VMEM((seq, LANE), F32),
               pltpu.VMEM((npair, 2 * GDN_PAIR, GDN_PAIR), F32),
               pltpu.VMEM((npair, GDN_PAIR, GDN_PAIR), BF16),
               pltpu.VMEM((seq, GDN_DV + GDN_DK), BF16),
               pltpu.VMEM((seq, GDN_DK), BF16),
               pltpu.VMEM((seq, GDN_DK), F32),
               pltpu.VMEM((npair, 8, LANE), F32),
               pltpu.VMEM((seq, GDN_DV + GDN_DK), F32),
               pltpu.VMEM((h, nchunk, GDN_DK + GDN_CHUNK, GDN_DK), BF16),
               pltpu.VMEM((h, nchunk, GDN_DK, GDN_DV), F32),
               pltpu.VMEM((h, nchunk, GDN_CHUNK, GDN_DV), F32),
               pltpu.VMEM((h, GDN_DK, GDN_DV), BF16)]
    return pl.pallas_call(
        functools.partial(_gdn_kernel, seq=seq), grid=(bsz, h), in_specs=in_specs,
        out_specs=pl.BlockSpec((seq, GDN_VW), lambda b, hh: (b, 0)),
        out_shape=jax.ShapeDtypeStruct((bsz * seq, GDN_VW), BF16),
        scratch_shapes=scratch,
        compiler_params=_cparams(("parallel", "arbitrary")), name="gdn",
    )(gqkvz, gqkvz, gqkvz, gqkvz, conv_w, conv_w, conv_w, gcol, grow, alog, dtb, nw)


def _gelu_tanh(x):
    return 0.5 * x * (1.0 + jnp.tanh(math.sqrt(2.0 / math.pi) * (x + 0.044715 * x * x * x)))


def _compress_kernel(k16_ref, v16_ref, wkc_ref, wvc_ref, w1k_ref, w1v_ref, pk_ref, pv_ref,
                     w2k_ref, w2v_ref, kc_ref, vc_ref):
    nblk = k16_ref.shape[1]
    for x_ref, wc_ref, w1_ref, p_ref, w2_ref, o_ref in (
            (k16_ref, wkc_ref, w1k_ref, pk_ref, w2k_ref, kc_ref),
            (v16_ref, wvc_ref, w1v_ref, pv_ref, w2v_ref, vc_ref)):
        y = jnp.dot(x_ref[0], wc_ref[...], preferred_element_type=F32)
        pos_term = jnp.dot(p_ref[...], w1_ref[...], preferred_element_type=F32)[0:1, :]
        for g in range(NSA_KV_HEADS):
            top = y[:, g * CMP_HIDDEN:(g + 1) * CMP_HIDDEN]
            bot = y[:, (NSA_KV_HEADS + g) * CMP_HIDDEN:(NSA_KV_HEADS + g + 1) * CMP_HIDDEN]
            hid = top + pltpu.roll(bot, nblk - 1, 0) + pos_term
            out = _dot(_gelu_tanh(hid), w2_ref[...])
            o_ref[0, g] = (jnp.transpose(out) if o_ref is vc_ref else out).astype(o_ref.dtype)


def _compress(k16, v16, wkc, wvc, w1k, w1v, pk, pv, w2k, w2v):
    bsz, nblk, width = k16.shape
    in_specs = [pl.BlockSpec((1, nblk, width), lambda b: (b, 0, 0))] * 2
    in_specs += [_full_spec(a.shape) for a in (wkc, wvc, w1k, w1v, pk, pv, w2k, w2v)]
    out_spec = pl.BlockSpec((1, NSA_KV_HEADS, nblk, HEAD_PAD), lambda b: (b, 0, 0, 0))
    out_shape = jax.ShapeDtypeStruct((bsz, NSA_KV_HEADS, nblk, HEAD_PAD), BF16)
    return pl.pallas_call(
        _compress_kernel, grid=(bsz,), in_specs=in_specs, out_specs=[out_spec, out_spec],
        out_shape=[out_shape, out_shape], compiler_params=_cparams(("parallel",)), name="nsa_compress",
    )(k16, v16, wkc, wvc, w1k, w1v, pk, pv, w2k, w2v)


def _nsa_kernel(q_ref, kv_ref, kc_ref, vct_ref, sm_ref, ovt_ref, o_ref, ksa, vst, vwt, *, seq):
    step = pl.program_id(1)
    n_cmp_pad = kc_ref.shape[2]
    n_sel_blk = seq // SEL_BLOCK
    kb_rows = SEL_KEY_BLOCK
    one_row = NSA_DH
    groups = range(NSA_KV_HEADS)
    units = [(sub, g) for sub in range(NSA_Q_SUB) for g in groups]

    @pl.when(step == 0)
    def _():
        lane = lax.broadcasted_iota(jnp.int32, (1, NSA_KV_HEADS * HEAD_PAD), 1) & (HEAD_PAD - 1)
        row_t = lax.broadcasted_iota(jnp.int32, (HEAD_PAD, 1), 0)
        tiles = kb_rows // Q_BLOCK

        def fill(bi, carry):
            r0 = pl.multiple_of(bi * kb_rows, kb_rows)
            rows = pl.ds(r0, kb_rows)
            blk = jnp.right_shift(r0 + lax.broadcasted_iota(jnp.int32, (kb_rows, 1), 0), SEL_SHIFT)
            ksa[rows, :] = kv_ref[rows, 0:2 * HEAD_PAD] + (lane == NSA_DH + blk).astype(ksa.dtype)
            for g in range(NSA_KV_HEADS):
                v_t = jnp.transpose(kv_ref[rows, (2 + g) * HEAD_PAD:(3 + g) * HEAD_PAD].astype(F32))
                vst[bi, g] = jnp.where(row_t == one_row, 1.0, v_t).astype(vst.dtype)
                w_t = jnp.transpose(kv_ref[rows, (6 + g) * HEAD_PAD:(7 + g) * HEAD_PAD].astype(F32))
                w_t = jnp.where(row_t == one_row, 1.0, w_t).astype(vwt.dtype)
                for part in range(tiles):
                    vwt[bi * tiles + part, g] = w_t[:, part * Q_BLOCK:(part + 1) * Q_BLOCK]
            return carry

        lax.fori_loop(0, seq // kb_rows, fill, 0)

    tok = lax.broadcasted_iota(jnp.int32, (1, Q_BLOCK), 1)
    lane_q = lax.broadcasted_iota(jnp.int32, (1, HEAD_PAD), 1)
    pen_lanes = (lane_q >= NSA_DH) & (lane_q < NSA_DH + n_sel_blk)
    place = (lax.broadcasted_iota(jnp.int32, (n_sel_blk, HEAD_PAD), 1)
             == NSA_DH + lax.broadcasted_iota(jnp.int32, (n_sel_blk, HEAD_PAD), 0)).astype(BF16)
    n_sub = lax.broadcasted_iota(jnp.int32, (n_cmp_pad, 1), 0)

    def add_bias(s, bias):
        return jnp.concatenate([s[:, r * Q_BLOCK:(r + 1) * Q_BLOCK] + bias for r in range(s.shape[1] // Q_BLOCK)],
                               axis=1)

    q0_s, t_lane_s, gates_s, qp, o_cmp, imp_s, qs_s = [], [], [], {}, {}, {}, {}
    for sub in range(NSA_Q_SUB):
        q0 = (step * NSA_Q_SUB + sub) * Q_BLOCK
        tok_rows = slice(sub * Q_BLOCK, (sub + 1) * Q_BLOCK)
        t_lane = q0 + tok
        q0_s.append(q0)
        t_lane_s.append(t_lane)
        gates_s.append(jnp.transpose(_sigmoid(sm_ref[tok_rows, :])))

        cmp_bias = jnp.where((n_sub * CMP_STRIDE + (CMP_BLOCK - 1)) <= t_lane, 0.0, NEG_BIG)
        cmp_any = jnp.concatenate([(t_lane >= CMP_BLOCK - 1).astype(F32)] * NSA_REP, axis=1)

        for g in groups:
            qs = jnp.concatenate(
                [q_ref[tok_rows, (g * NSA_REP + r) * HEAD_PAD:(g * NSA_REP + r + 1) * HEAD_PAD]
                 for r in range(NSA_REP)], axis=0)

            s_c = add_bias(_dot_nt(kc_ref[0, g], qs), cmp_bias)
            e_c = jnp.exp2(s_c - jnp.max(s_c, axis=0, keepdims=True))
            p_c = e_c * (cmp_any / jnp.sum(e_c, axis=0, keepdims=True))
            o_cmp[sub, g] = _dot(vct_ref[0, g], p_c)[0:NSA_DH]

            p_sum = p_c[:, 0:Q_BLOCK]
            for r in range(1, NSA_REP):
                p_sum = p_sum + p_c[:, r * Q_BLOCK:(r + 1) * Q_BLOCK]
            imp_s[sub, g] = _dot(ovt_ref[...], p_sum)
            qs_s[sub, g] = qs

    def rank_select(imp_t, t_lane):
        j_sub = lax.broadcasted_iota(jnp.int32, (n_sel_blk, 1), 0)
        cur = jnp.right_shift(t_lane, SEL_SHIFT)
        forced = (j_sub == 0) | (j_sub == cur) | (j_sub == cur - 1)
        causal_blk = (j_sub * SEL_BLOCK) <= t_lane
        imp = jnp.where(forced, FORCE_SCORE, jnp.where(causal_blk, imp_t, -jnp.inf))
        rank = jnp.zeros((n_sel_blk, Q_BLOCK), F32)
        for jp in range(n_sel_blk):
            row = imp[jp:jp + 1, :]
            ahead = (row > imp) | ((row == imp) & (j_sub > jp))
            rank = rank + ahead.astype(F32)
        return (rank < SEL_TOPK).astype(BF16)

    sel_all = lax.cond(
        (step + 1) * (NSA_Q_SUB * Q_BLOCK) <= SEL_TOPK * SEL_BLOCK,
        lambda: tuple(jnp.ones((n_sel_blk, Q_BLOCK), BF16) for _ in units),
        lambda: tuple(rank_select(imp_s[u], t_lane_s[u[0]]) for u in units))
    for i, unit in enumerate(units):
        sel_lanes = _dot_tn(sel_all[i], place)
        pen = jnp.where(pen_lanes, (sel_lanes - 1.0) * (-NEG_BIG), 0.0).astype(BF16)
        qp[unit] = qs_s[unit] + jnp.concatenate([pen] * NSA_REP, axis=0)

    n_full = step

    def scores(kb, unit):
        k0 = pl.multiple_of(kb * kb_rows, kb_rows)
        g = unit[1]
        return _dot_nt(ksa[pl.ds(k0, kb_rows), g * HEAD_PAD:(g + 1) * HEAD_PAD], qp[unit])

    def absorb(state, s, kb, unit):
        m_i, acc = state
        m_new = jnp.maximum(m_i, jnp.max(s, axis=0, keepdims=True))
        return m_new, jnp.exp2(m_i - m_new) * acc + _dot(vst[kb, unit[1]], jnp.exp2(s - m_new))

    kpos_last = n_full * kb_rows + lax.broadcasted_iota(jnp.int32, (kb_rows, 1), 0)
    first = []
    for unit in units:
        s = add_bias(scores(n_full, unit), jnp.where(kpos_last <= t_lane_s[unit[0]], 0.0, NEG_BIG))
        m = jnp.max(s, axis=0, keepdims=True)
        first.append((m, _dot(vst[n_full, unit[1]], jnp.exp2(s - m))))

    def full_blocks(state):
        def loop_body(kb, carry):
            st, s_cur = carry
            s_next = tuple(scores(kb + 1, u) for u in units)
            return tuple(absorb(st[i], s_cur[i], kb, u) for i, u in enumerate(units)), s_next

        st, s_cur = lax.fori_loop(0, n_full - 1, loop_body, (state, tuple(scores(0, u) for u in units)))
        return tuple(absorb(st[i], s_cur[i], n_full - 1, u) for i, u in enumerate(units))

    sel_out = lax.cond(n_full > 0, full_blocks, lambda st: st, tuple(first))

    span = WINDOW + Q_BLOCK
    for i, (sub, g) in enumerate(units):
        q0, t_lane = q0_s[sub], t_lane_s[sub]
        tok_rows = slice(sub * Q_BLOCK, (sub + 1) * Q_BLOCK)
        w0 = pl.multiple_of(jnp.maximum(q0 - WINDOW, 0), Q_BLOCK)
        kpos_w = w0 + lax.broadcasted_iota(jnp.int32, (span, 1), 0)
        win_bias = jnp.where((kpos_w <= t_lane) & (kpos_w > t_lane - WINDOW), 0.0, NEG_BIG)

        acc_s = sel_out[i][1]
        o_slc = acc_s[0:NSA_DH] * (1.0 / acc_s[one_row:one_row + 1, :])

        k_w = kv_ref[pl.ds(w0, span), (4 + g) * HEAD_PAD:(5 + g) * HEAD_PAD]
        s_w = add_bias(_dot_nt(k_w, qp[sub, g]), win_bias)
        tile0 = lax.div(w0, Q_BLOCK)
        v_wt = jnp.concatenate([vwt[tile0 + t, g] for t in range(span // Q_BLOCK)], axis=1)
        acc_w = _dot(v_wt, jnp.exp2(s_w - jnp.max(s_w, axis=0, keepdims=True)))
        o_win = acc_w[0:NSA_DH] * (1.0 / acc_w[one_row:one_row + 1, :])

        gates_t = gates_s[sub]
        o_heads = []
        for r in range(NSA_REP):
            c0 = 8 + (g * NSA_REP + r) * 3
            sl = slice(r * Q_BLOCK, (r + 1) * Q_BLOCK)
            o_t = (gates_t[c0:c0 + 1, :] * o_cmp[sub, g][:, sl] + gates_t[c0 + 1:c0 + 2, :] * o_slc[:, sl]
                   + gates_t[c0 + 2:c0 + 3, :] * o_win[:, sl])
            o_heads.append(o_t)
        for pair in range(NSA_REP // 2):
            lanes = slice((g * NSA_REP + 2 * pair) * NSA_DH, (g * NSA_REP + 2 * pair + 2) * NSA_DH)
            o_ref[tok_rows, lanes] = jnp.transpose(jnp.concatenate(o_heads[2 * pair:2 * pair + 2], axis=0)
                                                   ).astype(o_ref.dtype)


def _nsa(nq, nkv, kc, vc, small, overlap, bsz, seq):
    assert NSA_Q_SUB * Q_BLOCK == SEL_KEY_BLOCK
    rows = NSA_Q_SUB * Q_BLOCK
    nstep = seq // rows
    in_specs = [pl.BlockSpec((rows, NSA_HEADS * HEAD_PAD), lambda b, i: (b * nstep + i, 0)),
                pl.BlockSpec((seq, nkv.shape[1]), lambda b, i: (b, 0)),
                pl.BlockSpec((1,) + kc.shape[1:], lambda b, i: (b, 0, 0, 0)),
                pl.BlockSpec((1,) + vc.shape[1:], lambda b, i: (b, 0, 0, 0)),
                pl.BlockSpec((rows, LANE), lambda b, i: (b * nstep + i, 0)),
                _full_spec(overlap.shape)]
    return pl.pallas_call(
        functools.partial(_nsa_kernel, seq=seq), grid=(bsz, nstep), in_specs=in_specs,
        out_specs=pl.BlockSpec((rows, NSA_QW), lambda b, i: (b * nstep + i, 0)),
        out_shape=jax.ShapeDtypeStruct((bsz * seq, NSA_QW), BF16),
        scratch_shapes=[pltpu.VMEM((seq, NSA_KV_HEADS * HEAD_PAD), BF16),
                        pltpu.VMEM((seq // SEL_KEY_BLOCK, NSA_KV_HEADS, HEAD_PAD, SEL_KEY_BLOCK), BF16),
                        pltpu.VMEM((seq // Q_BLOCK, NSA_KV_HEADS, HEAD_PAD, Q_BLOCK), BF16)],
        compiler_params=_cparams(("parallel", "arbitrary")), name="nsa_attention",
    )(nq, nkv, kc, vc, small, overlap)


def _merge_kernel(oa_ref, ob_ref, mix_ref, x_ref, wa_ref, wb_ref, wo_ref, g_ref, b_ref, h_ref):
    ya = jnp.dot(oa_ref[...], wa_ref[...], preferred_element_type=F32)
    yb = jnp.dot(ob_ref[...], wb_ref[...], preferred_element_type=F32)
    mix = mix_ref[...].astype(F32)
    mixed = _sigmoid(mix[:, 0:D_MODEL]) * ya + _sigmoid(mix[:, D_MODEL:2 * D_MODEL]) * yb
    y = _dot(mixed, wo_ref[...])
    h_ref[...] = _layer_norm(DN_ALPHA * x_ref[...] + y, g_ref[...], b_ref[...])


def _merge(oa, ob, mix, x2d, wa, wb, wo, g, b):
    m = x2d.shape[0]
    tm = 2 * ROW_TILE
    row = lambda n: pl.BlockSpec((tm, n), lambda i: (i, 0))
    in_specs = [row(oa.shape[1]), row(ob.shape[1]), row(mix.shape[1]), row(D_MODEL),
                _full_spec(wa.shape), _full_spec(wb.shape), _full_spec(wo.shape),
                _full_spec(g.shape), _full_spec(b.shape)]
    return pl.pallas_call(
        _merge_kernel, grid=(m // tm,), in_specs=in_specs, out_specs=row(D_MODEL),
        out_shape=jax.ShapeDtypeStruct((m, D_MODEL), F32),
        compiler_params=_cparams(("parallel",)), name="merge_ln",
    )(oa, ob, mix, x2d, wa, wb, wo, g, b)


def _ffn_kernel(h_ref, halo_ref, wup_ref, cw_ref, wd_ref, g_ref, b_ref, o_ref,
                hb, ug, uv, act, *, seq, tm, halo):
    i = pl.program_id(0)
    first = lax.rem(i * tm, seq) == 0
    hb[0:halo, :] = jnp.where(first, 0.0, halo_ref[...]).astype(BF16)
    hb[halo:, :] = h_ref[...].astype(BF16)

    def conv(u_ref, slot, cols):
        out = None
        for tap in range(FFN_CONV):
            term = cw_ref[tap:tap + 1, cols] * u_ref[slot, pl.ds(halo - (FFN_CONV - 1) + tap, tm), :]
            out = term if out is None else out + term
        return out

    for j in range(D_FF // FFN_CHUNK):
        slot = j % FFN_SLOTS
        cols_g = slice(j * FFN_CHUNK, (j + 1) * FFN_CHUNK)
        cols_v = slice(D_FF + j * FFN_CHUNK, D_FF + (j + 1) * FFN_CHUNK)
        ug[slot] = jnp.dot(hb[...], wup_ref[:, cols_g], preferred_element_type=F32)
        uv[slot] = jnp.dot(hb[...], wup_ref[:, cols_v], preferred_element_type=F32)
        a = conv(ug, slot, cols_g)
        act[:, cols_g] = (a * _sigmoid(a) * conv(uv, slot, cols_v)).astype(act.dtype)

    f = jnp.dot(act[...], wd_ref[...], preferred_element_type=F32)
    o_ref[...] = _layer_norm(DN_ALPHA * h_ref[...] + f, g_ref[...], b_ref[...])


def _ffn(h, w_up, conv_w, w_down, g, b, seq):
    m = h.shape[0]
    tm = ROW_TILE
    halo = 16
    resident = lambda shape: pl.BlockSpec(shape, lambda i: (0, 0), pipeline_mode=pl.Buffered(1))
    in_specs = [pl.BlockSpec((tm, D_MODEL), lambda i: (i, 0)),
                pl.BlockSpec((halo, D_MODEL), lambda i: (jnp.maximum(i * (tm // halo) - 1, 0), 0)),
                resident(w_up.shape), resident(conv_w.shape), resident(w_down.shape),
                resident(g.shape), resident(b.shape)]
    return pl.pallas_call(
        functools.partial(_ffn_kernel, seq=seq, tm=tm, halo=halo), grid=(m // tm,),
        in_specs=in_specs, out_specs=pl.BlockSpec((tm, D_MODEL), lambda i: (i, 0)),
        out_shape=jax.ShapeDtypeStruct((m, D_MODEL), F32),
        scratch_shapes=[pltpu.VMEM((tm + halo, D_MODEL), BF16),
                        pltpu.VMEM((FFN_SLOTS, tm + halo, FFN_CHUNK), F32),
                        pltpu.VMEM((FFN_SLOTS, tm + halo, FFN_CHUNK), F32),
                        pltpu.VMEM((tm, D_FF), BF16)],
        compiler_params=_cparams(("parallel",)), name="conv_ffn_ln",
    )(h, h, w_up, conv_w, w_down, g, b)


def _compress_weights(w1):
    half = CMP_BLOCK // 2
    w1r = w1.reshape(2, half, NSA_DH, CMP_HIDDEN)
    eye = jnp.eye(NSA_KV_HEADS, dtype=w1.dtype)
    ex = jnp.einsum("aidh,gk->agikdh", w1r, eye)
    ex = ex.reshape(2 * NSA_KV_HEADS, half * NSA_KVW, CMP_HIDDEN)
    return jnp.concatenate([ex[n] for n in range(2 * NSA_KV_HEADS)], axis=1)


def _overlap_matrix(seq):
    n_cmp = (seq - CMP_BLOCK) // CMP_STRIDE + 1
    n_sel = seq // SEL_BLOCK
    starts = np.arange(n_cmp) * CMP_STRIDE
    jb = np.arange(n_sel) * SEL_BLOCK
    ov = ((starts[:, None] < jb[None] + SEL_BLOCK) & (starts[:, None] + CMP_BLOCK > jb[None])).astype(np.float32)
    out = np.zeros((n_sel, seq // CMP_STRIDE), np.float32)
    out[:, :n_cmp] = ov.T
    return jnp.asarray(out, BF16)


def kernel(x, w_in, gdn_conv_w, gdn_a_log, gdn_dt_bias, gdn_norm_w, cmp_pos_k, cmp_w1_k, cmp_w2_k,
           cmp_pos_v, cmp_w1_v, cmp_w2_v, w_branch_gdn, w_branch_nsa, w_out, ln1_g, ln1_b, w_up,
           ffn_conv_w, w_down, ln2_g, ln2_b):
    bsz, seq, _ = x.shape
    m = bsz * seq
    for i in range(DEPTH):
        x2d = x.reshape(m, D_MODEL)
        w = w_in[i]
        o_gdn = 2 * GDN_QK + 2 * GDN_VW
        o_nq = o_gdn + 2 * GDN_HEADS
        o_kv = o_nq + NSA_QW
        o_gate = o_kv + 6 * NSA_KVW
        o_mix = o_gate + 3 * NSA_HEADS
        kv = [w[:, o_kv + n * NSA_KVW:o_kv + (n + 1) * NSA_KVW] for n in range(6)]
        small_w = jnp.concatenate([w[:, o_gdn:o_nq], w[:, o_gate:o_mix]], axis=1)
        small_w = jnp.pad(small_w, ((0, 0), (0, LANE - small_w.shape[1])))
        weights = [w[:, :o_gdn],
                   w[:, o_nq:o_kv],
                   w[:, o_kv + 2 * NSA_KVW:o_gate],
                   kv[0], kv[1],
                   w[:, o_mix:],
                   small_w]
        weights = [t.astype(BF16) for t in weights]
        widths = [o_gdn, NSA_HEADS * HEAD_PAD, 4 * NSA_KV_HEADS * HEAD_PAD, NSA_KVW, NSA_KVW, 2 * D_MODEL, LANE]
        scales = [1.0, NSA_DH ** -0.5 * LOG2E, 1.0, 1.0, 1.0, 1.0, 1.0]
        gqkvz, nq, nkv, kcp, vcp, mix, small, grow = _inproj(x2d, weights, [BF16] * 6 + [F32], widths, scales,
                                                             2 * GDN_HEADS)

        alog = jnp.broadcast_to(gdn_a_log[i][:, None, None], (GDN_HEADS, 1, LANE))
        dtb = jnp.broadcast_to(gdn_dt_bias[i][:, None, None], (GDN_HEADS, 1, LANE))
        o_a = _gdn(gqkvz, gdn_conv_w[i], small, grow, alog, dtb, gdn_norm_w[i][None, :], bsz, seq)

        grp = seq // CMP_STRIDE
        k16 = kcp.reshape(bsz, grp, CMP_STRIDE * NSA_KVW)
        v16 = vcp.reshape(bsz, grp, CMP_STRIDE * NSA_KVW)
        pos_rows = lambda p: jnp.pad(p.reshape(1, CMP_BLOCK * NSA_DH), ((0, 7), (0, 0))).astype(BF16)
        w2_pad = lambda t: jnp.pad(t, ((0, 0), (0, HEAD_PAD - NSA_DH))).astype(BF16)
        kc, vc = _compress(k16, v16,
                           _compress_weights(cmp_w1_k[i]).astype(BF16), _compress_weights(cmp_w1_v[i]).astype(BF16),
                           cmp_w1_k[i].astype(BF16), cmp_w1_v[i].astype(BF16),
                           pos_rows(cmp_pos_k[i]), pos_rows(cmp_pos_v[i]),
                           w2_pad(cmp_w2_k[i]), w2_pad(cmp_w2_v[i]))
        o_b = _nsa(nq, nkv, kc, vc, small, _overlap_matrix(seq), bsz, seq)

        h = _merge(o_a, o_b, mix, x2d, w_branch_gdn[i].astype(BF16), w_branch_nsa[i].astype(BF16), w_out[i].astype(BF16),
                   ln1_g[i][None, :], ln1_b[i][None, :])

        x = _ffn(h, w_up[i].astype(BF16), ffn_conv_w[i], w_down[i].astype(BF16),
                 ln2_g[i][None, :], ln2_b[i][None, :], seq).reshape(bsz, seq, D_MODEL)
    return x
```

```python
import functools
import math

import numpy as np
import jax
import jax.numpy as jnp
from jax import lax
from jax.experimental import pallas as pl
from jax.experimental.pallas import tpu as pltpu

F32 = jnp.float32
BF16 = jnp.bfloat16

D_MODEL = 1024
GDN_HEADS = 4
GDN_DK = 128
GDN_DV = 128
GDN_CONV = 4
GDN_CHUNK = 64
NSA_HEADS = 8
NSA_KV_HEADS = 2
NSA_REP = NSA_HEADS // NSA_KV_HEADS
NSA_DH = 64
CMP_BLOCK = 32
CMP_STRIDE = 16
CMP_HIDDEN = 256
SEL_BLOCK = 64
SEL_SHIFT = 6
SEL_TOPK = 16
WINDOW = 512
FORCE_SCORE = 1e9
D_FF = 2816
FFN_CONV = 3
DEPTH = 1
DN_ALPHA = (2 * DEPTH) ** 0.25
LN_EPS = 1e-5
RMS_EPS = 1e-6
LOG2E = math.log2(math.e)

GDN_QK = GDN_HEADS * GDN_DK
GDN_VW = GDN_HEADS * GDN_DV
NSA_QW = NSA_HEADS * NSA_DH
NSA_KVW = NSA_KV_HEADS * NSA_DH

LANE = 128
HEAD_PAD = LANE
NEG_BIG = -1e30
VMEM_LIMIT = 56 * 1024 * 1024

Q_BLOCK = 128
NSA_Q_SUB = 2
SEL_KEY_BLOCK = 256
FFN_CHUNK = 256
FFN_SLOTS = 2
ROW_TILE = 512
GDN_CHUNK_SHIFT = 6
GDN_PAIR = 2 * GDN_CHUNK
GDN_UNROLL = 16


def _cparams(sem):
    return pltpu.CompilerParams(dimension_semantics=sem, vmem_limit_bytes=VMEM_LIMIT)


def _dot(a, b):
    return jnp.dot(a.astype(BF16), b.astype(BF16), preferred_element_type=F32)


def _dot_nt(a, b):
    return lax.dot_general(a.astype(BF16), b.astype(BF16), (((1,), (1,)), ((), ())),
                           preferred_element_type=F32)


def _dot_tn(a, b):
    return lax.dot_general(a.astype(BF16), b.astype(BF16), (((0,), (0,)), ((), ())),
                           preferred_element_type=F32)


def _sigmoid(x):
    return jax.nn.sigmoid(x)


def _softplus(x):
    return jnp.maximum(x, 0.0) + jnp.log1p(jnp.exp(-jnp.abs(x)))


def _layer_norm(x, g, b):
    mu = jnp.mean(x, axis=-1, keepdims=True)
    xc = x - mu
    var = jnp.mean(xc * xc, axis=-1, keepdims=True)
    return xc * lax.rsqrt(var + LN_EPS) * g + b


def _full_spec(shape):
    nd = len(shape)
    return pl.BlockSpec(shape, lambda *_: (0,) * nd, pipeline_mode=pl.Buffered(1))


def _inproj_kernel(x_ref, *refs, t_rows, scales):
    n = (len(refs) - 1) // 2
    xb = x_ref[...].astype(BF16)
    for w_ref, o_ref, scale in zip(refs[:n], refs[n:2 * n], scales):
        res = jnp.dot(xb, w_ref[...], preferred_element_type=F32)
        if scale != 1.0:
            res = res * scale
        if o_ref.shape[1] == w_ref.shape[1]:
            o_ref[...] = res.astype(o_ref.dtype)
        else:
            out = res.astype(o_ref.dtype)
            zeros = jnp.zeros((out.shape[0], HEAD_PAD - NSA_DH), o_ref.dtype)
            for hd in range(w_ref.shape[1] // NSA_DH):
                o_ref[:, hd * HEAD_PAD:hd * HEAD_PAD + NSA_DH] = out[:, hd * NSA_DH:(hd + 1) * NSA_DH]
                o_ref[:, hd * HEAD_PAD + NSA_DH:(hd + 1) * HEAD_PAD] = zeros
    t_ref = refs[2 * n]
    for j in range(t_ref.shape[0]):
        t_ref[j] = jnp.transpose(res[j * LANE:(j + 1) * LANE, :])[0:t_rows, :]


def _inproj(x2d, weights, out_dtypes, out_widths, scales, t_rows):
    assert weights[-1].shape[1] == LANE and out_dtypes[-1] == F32
    m = x2d.shape[0]
    tm = 2 * ROW_TILE
    in_specs = [pl.BlockSpec((tm, D_MODEL), lambda i: (i, 0))]
    in_specs += [_full_spec(w.shape) for w in weights]
    out_specs = [pl.BlockSpec((tm, n), lambda i: (i, 0)) for n in out_widths]
    out_specs.append(pl.BlockSpec((tm // LANE, t_rows, LANE), lambda i: (i, 0, 0)))
    out_shape = [jax.ShapeDtypeStruct((m, n), dt) for n, dt in zip(out_widths, out_dtypes)]
    out_shape.append(jax.ShapeDtypeStruct((m // LANE, t_rows, LANE), F32))
    return pl.pallas_call(
        functools.partial(_inproj_kernel, t_rows=t_rows, scales=tuple(scales)), grid=(m // tm,),
        in_specs=in_specs, out_specs=out_specs,
        out_shape=out_shape, compiler_params=_cparams(("parallel",)), name="inproj")(x2d, *weights)


def _gdn_kernel(q_ref, k_ref, v_ref, z_ref, cwq_ref, cwk_ref, cwv_ref, gcol_ref, grow_ref,
                alog_ref, dtb_ref, nw_ref, o_ref,
                xpad, qh, kh, vh, xp, qk_s, rhs, kd, qd, gl, sol, mq, ns, o0, st, *, seq):
    c = GDN_CHUNK
    pr = GDN_PAIR
    npair = seq // pr
    head = pl.program_id(1)
    pad = 8
    conv_rows = 256
    xpad[0:pad, :] = jnp.zeros((pad, 3 * LANE), F32)
    xpad[pad:, 0:LANE] = q_ref[...].astype(F32)
    xpad[pad:, LANE:2 * LANE] = k_ref[...].astype(F32)
    xpad[pad:, 2 * LANE:3 * LANE] = v_ref[...].astype(F32)
    cw = jnp.concatenate([cwq_ref[...], cwk_ref[...], cwv_ref[...]], axis=1)

    for t0 in range(0, seq, conv_rows):
        xe = xpad[t0:t0 + pad + conv_rows, :]
        acc = xe * cw[0:1, :]
        for j in range(1, GDN_CONV):
            acc = xe * cw[j:j + 1, :] + pltpu.roll(acc, 1, 0)
        acc = acc[pad:, :]
        y = acc * _sigmoid(acc)
        yq, yk, yv = y[:, 0:LANE], y[:, LANE:2 * LANE], y[:, 2 * LANE:3 * LANE]
        yq = yq * lax.rsqrt(jnp.sum(yq * yq, axis=-1, keepdims=True) + RMS_EPS) * (GDN_DK ** -0.5)
        yk = yk * lax.rsqrt(jnp.sum(yk * yk, axis=-1, keepdims=True) + RMS_EPS)
        qh[t0:t0 + conv_rows, :] = yq
        kh[t0:t0 + conv_rows, :] = yk
        vh[t0:t0 + conv_rows, :] = yv

    neg_a = -jnp.exp(alog_ref[0, :, 0:1])
    dtb = dtb_ref[0, :, 0:1]
    ii = lax.broadcasted_iota(jnp.int32, (pr, pr), 0)
    jj = lax.broadcasted_iota(jnp.int32, (pr, pr), 1)
    same = jnp.right_shift(ii, GDN_CHUNK_SHIFT) == jnp.right_shift(jj, GDN_CHUNK_SHIFT)
    incl = same & (jj <= ii)
    strict = same & (jj < ii)
    upper = same & (ii <= jj)
    eye = (ii == jj).astype(F32)
    sub8 = lax.broadcasted_iota(jnp.int32, (8, LANE), 0)
    lane_id = lax.broadcasted_iota(jnp.int32, (1, LANE), 1)

    def over_pairs(body):
        def run(gi, carry):
            for uu in range(GDN_UNROLL):
                body(gi * GDN_UNROLL + uu)
            return carry
        lax.fori_loop(0, npair // GDN_UNROLL, run, 0)

    def stage_gates(p):
        rows = pl.ds(pl.multiple_of(p * pr, pr), pr)
        q = qh[rows, :]
        k = kh[rows, :]
        v = vh[rows, :]
        gcol = gcol_ref[rows, :]
        b_logit = jnp.sum(jnp.where(lane_id == head, gcol, 0.0), axis=1, keepdims=True)
        a_logit = jnp.sum(jnp.where(lane_id == GDN_HEADS + head, gcol, 0.0), axis=1, keepdims=True)
        a_row = jnp.sum(jnp.where(sub8[:, 0:1] == GDN_HEADS + head, grow_ref[p], 0.0), axis=0, keepdims=True)
        beta = _sigmoid(b_logit)
        g_c = neg_a * _softplus(a_logit + dtb)
        g_r = neg_a * _softplus(a_row + dtb)
        gc_col = jnp.sum(jnp.where(incl, g_r, 0.0), axis=1, keepdims=True)
        gc_row = jnp.sum(jnp.where(upper, g_c, 0.0), axis=0, keepdims=True)
        gt_col = jnp.sum(jnp.where(same, g_r, 0.0), axis=1, keepdims=True)
        decay = jnp.where(incl, jnp.exp(jnp.where(incl, gc_col - gc_row, 0.0)), 0.0)
        kq_k = _dot_nt(jnp.concatenate([k, q], axis=0), k)
        a_mat = jnp.where(strict, kq_k[0:pr] * decay * beta, 0.0)
        xp[p, 0:pr, :] = eye - a_mat
        xp[p, pr:2 * pr, :] = _dot(a_mat, a_mat)
        qk_s[p] = (kq_k[pr:2 * pr] * decay).astype(qk_s.dtype)
        e_gc = jnp.exp(gc_col)
        rhs[rows, 0:GDN_DV] = (v * beta).astype(rhs.dtype)
        rhs[rows, GDN_DV:GDN_DV + GDN_DK] = (k * (beta * e_gc)).astype(rhs.dtype)
        kd[rows, :] = (k * jnp.exp(gt_col - gc_col)).astype(kd.dtype)
        qd[rows, :] = q * e_gc
        g_last = jnp.exp(gt_col)
        gl[p] = jnp.where(sub8 == 0, g_last[0:1, :], g_last[c:c + 1, :])

    def stage_double(p):
        y = _dot(xp[p], xp[p, pr:2 * pr, :])
        xp[p, 0:pr, :] = xp[p, 0:pr, :] + y[0:pr]
        xp[p, pr:2 * pr, :] = y[pr:2 * pr]

    def stage_solve(p):
        rows = pl.ds(pl.multiple_of(p * pr, pr), pr)
        x_inv = xp[p, 0:pr, :]
        x_inv = x_inv + _dot(x_inv, xp[p, pr:2 * pr, :])
        sol[rows, :] = _dot(x_inv, rhs[rows, :])

    def stage_affine(p):
        r0 = pl.multiple_of(p * pr, pr)
        rows = pl.ds(r0, pr)
        e = _dot(qk_s[p], sol[rows, :])
        qe = qd[rows, :] - e[:, GDN_DV:GDN_DV + GDN_DK]
        g_rows = gl[p]
        for cc in range(pr // c):
            rc = pl.ds(r0 + cc * c, c)
            mn = _dot_tn(kd[rc, :], sol[rc, :])
            idx = p * (pr // c) + cc
            mq[head, idx, 0:GDN_DK, :] = (eye * g_rows[cc:cc + 1, :] - mn[:, GDN_DV:GDN_DV + GDN_DK]).astype(mq.dtype)
            mq[head, idx, GDN_DK:GDN_DK + c, :] = qe[cc * c:(cc + 1) * c].astype(mq.dtype)
            ns[head, idx] = mn[:, 0:GDN_DV]
            o0[head, idx] = e[cc * c:(cc + 1) * c, 0:GDN_DV]

    over_pairs(stage_gates)
    for _ in range(int(math.log2(c)) - 2):
        over_pairs(stage_double)
    over_pairs(stage_solve)
    over_pairs(stage_affine)

    @pl.when(head == GDN_HEADS - 1)
    def _():
        st[...] = jnp.zeros_like(st)
        nw = nw_ref[...]

        def step(ci, carry):
            r0 = pl.multiple_of(ci * c, c)
            for h in range(GDN_HEADS):
                cols = slice(h * LANE, (h + 1) * LANE)
                ys = jnp.dot(mq[h, ci], st[h], preferred_element_type=F32)
                st[h] = (ys[0:GDN_DK] + ns[h, ci]).astype(st.dtype)
                o = ys[GDN_DK:GDN_DK + c] + o0[h, ci]
                o = o * lax.rsqrt(jnp.mean(o * o, axis=-1, keepdims=True) + RMS_EPS) * nw
                z = z_ref[pl.ds(r0, c), cols].astype(F32)
                o_ref[pl.ds(r0, c), cols] = (o * (z * _sigmoid(z))).astype(o_ref.dtype)
            return carry

        lax.fori_loop(0, seq // c, step, 0, unroll=8)


def _gdn(gqkvz, conv_w, gcol, grow, alog, dtb, nw, bsz, seq):
    assert GDN_DK == GDN_PAIR == GDN_DV == LANE
    h = GDN_HEADS
    npair = seq // GDN_PAIR
    nchunk = seq // GDN_CHUNK
    blk = lambda off: pl.BlockSpec((seq, LANE), lambda b, hh: (b, off + hh))
    cwb = lambda off: pl.BlockSpec((GDN_CONV, LANE), lambda b, hh: (0, off + hh))
    in_specs = [blk(0), blk(h), blk(2 * h),
                pl.BlockSpec((seq, GDN_VW), lambda b, hh: (b, 3)),
                cwb(0), cwb(h), cwb(2 * h),
                pl.BlockSpec((seq, LANE), lambda b, hh: (b, 0)),
                pl.BlockSpec((npair, 8, GDN_PAIR), lambda b, hh: (b, 0, 0)),
                pl.BlockSpec((1, 1, LANE), lambda b, hh: (hh, 0, 0)),
                pl.BlockSpec((1, 1, LANE), lambda b, hh: (hh, 0, 0)),
                pl.BlockSpec((1, LANE), lambda b, hh: (0, 0))]
    scratch = [pltpu.VMEM((seq + 8, 3 * LANE), F32),
               pltpu.VMEM((seq, LANE), F32), pltpu.VMEM((seq, LANE), F32), pltpu.VMEM((seq, LANE), F32),
               pltpu.VMEM((npair, 2 * GDN_PAIR, GDN_PAIR), F32),
               pltpu.VMEM((npair, GDN_PAIR, GDN_PAIR), BF16),
               pltpu.VMEM((seq, GDN_DV + GDN_DK), BF16),
               pltpu.VMEM((seq, GDN_DK), BF16),
               pltpu.VMEM((seq, GDN_DK), F32),
               pltpu.VMEM((npair, 8, LANE), F32),
               pltpu.VMEM((seq, GDN_DV + GDN_DK), F32),
               pltpu.VMEM((h, nchunk, GDN_DK + GDN_CHUNK, GDN_DK), BF16),
               pltpu.VMEM((h, nchunk, GDN_DK, GDN_DV), F32),
               pltpu.VMEM((h, nchunk, GDN_CHUNK, GDN_DV), F32),
               pltpu.VMEM((h, GDN_DK, GDN_DV), BF16)]
    return pl.pallas_call(
        functools.partial(_gdn_kernel, seq=seq), grid=(bsz, h), in_specs=in_specs,
        out_specs=pl.BlockSpec((seq, GDN_VW), lambda b, hh: (b, 0)),
        out_shape=jax.ShapeDtypeStruct((bsz * seq, GDN_VW), BF16),
        scratch_shapes=scratch,
        compiler_params=_cparams(("parallel", "arbitrary")), name="gdn",
    )(gqkvz, gqkvz, gqkvz, gqkvz, conv_w, conv_w, conv_w, gcol, grow, alog, dtb, nw)


def _gelu_tanh(x):
    return 0.5 * x * (1.0 + jnp.tanh(math.sqrt(2.0 / math.pi) * (x + 0.044715 * x * x * x)))


def _compress_kernel(k16_ref, v16_ref, wkc_ref, wvc_ref, w1k_ref, w1v_ref, pk_ref, pv_ref,
                     w2k_ref, w2v_ref, kc_ref, vc_ref):
    nblk = k16_ref.shape[1]
    for x_ref, wc_ref, w1_ref, p_ref, w2_ref, o_ref in (
            (k16_ref, wkc_ref, w1k_ref, pk_ref, w2k_ref, kc_ref),
            (v16_ref, wvc_ref, w1v_ref, pv_ref, w2v_ref, vc_ref)):
        y = jnp.dot(x_ref[0], wc_ref[...], preferred_element_type=F32)
        pos_term = jnp.dot(p_ref[...], w1_ref[...], preferred_element_type=F32)[0:1, :]
        for g in range(NSA_KV_HEADS):
            top = y[:, g * CMP_HIDDEN:(g + 1) * CMP_HIDDEN]
            bot = y[:, (NSA_KV_HEADS + g) * CMP_HIDDEN:(NSA_KV_HEADS + g + 1) * CMP_HIDDEN]
            hid = top + pltpu.roll(bot, nblk - 1, 0) + pos_term
            out = _dot(_gelu_tanh(hid), w2_ref[...])
            o_ref[0, g] = (jnp.transpose(out) if o_ref is vc_ref else out).astype(o_ref.dtype)


def _compress(k16, v16, wkc, wvc, w1k, w1v, pk, pv, w2k, w2v):
    bsz, nblk, width = k16.shape
    in_specs = [pl.BlockSpec((1, nblk, width), lambda b: (b, 0, 0))] * 2
    in_specs += [_full_spec(a.shape) for a in (wkc, wvc, w1k, w1v, pk, pv, w2k, w2v)]
    out_spec = pl.BlockSpec((1, NSA_KV_HEADS, nblk, HEAD_PAD), lambda b: (b, 0, 0, 0))
    out_shape = jax.ShapeDtypeStruct((bsz, NSA_KV_HEADS, nblk, HEAD_PAD), BF16)
    return pl.pallas_call(
        _compress_kernel, grid=(bsz,), in_specs=in_specs, out_specs=[out_spec, out_spec],
        out_shape=[out_shape, out_shape], compiler_params=_cparams(("parallel",)), name="nsa_compress",
    )(k16, v16, wkc, wvc, w1k, w1v, pk, pv, w2k, w2v)


def _nsa_kernel(q_ref, kv_ref, kc_ref, vct_ref, sm_ref, ovt_ref, o_ref, ksa, vst, vwt, *, seq):
    step = pl.program_id(1)
    n_cmp_pad = kc_ref.shape[2]
    n_sel_blk = seq // SEL_BLOCK
    kb_rows = SEL_KEY_BLOCK
    one_row = NSA_DH
    groups = range(NSA_KV_HEADS)
    units = [(sub, g) for sub in range(NSA_Q_SUB) for g in groups]

    @pl.when(step == 0)
    def _():
        lane = lax.broadcasted_iota(jnp.int32, (1, NSA_KV_HEADS * HEAD_PAD), 1) & (HEAD_PAD - 1)
        row_t = lax.broadcasted_iota(jnp.int32, (HEAD_PAD, 1), 0)
        tiles = kb_rows // Q_BLOCK

        def fill(bi, carry):
            r0 = pl.multiple_of(bi * kb_rows, kb_rows)
            rows = pl.ds(r0, kb_rows)
            blk = jnp.right_shift(r0 + lax.broadcasted_iota(jnp.int32, (kb_rows, 1), 0), SEL_SHIFT)
            ksa[rows, :] = kv_ref[rows, 0:2 * HEAD_PAD] + (lane == NSA_DH + blk).astype(ksa.dtype)
            for g in range(NSA_KV_HEADS):
                v_t = jnp.transpose(kv_ref[rows, (2 + g) * HEAD_PAD:(3 + g) * HEAD_PAD].astype(F32))
                vst[bi, g] = jnp.where(row_t == one_row, 1.0, v_t).astype(vst.dtype)
                w_t = jnp.transpose(kv_ref[rows, (6 + g) * HEAD_PAD:(7 + g) * HEAD_PAD].astype(F32))
                w_t = jnp.where(row_t == one_row, 1.0, w_t).astype(vwt.dtype)
                for part in range(tiles):
                    vwt[bi * tiles + part, g] = w_t[:, part * Q_BLOCK:(part + 1) * Q_BLOCK]
            return carry

        lax.fori_loop(0, seq // kb_rows, fill, 0)

    tok = lax.broadcasted_iota(jnp.int32, (1, Q_BLOCK), 1)
    lane_q = lax.broadcasted_iota(jnp.int32, (1, HEAD_PAD), 1)
    pen_lanes = (lane_q >= NSA_DH) & (lane_q < NSA_DH + n_sel_blk)
    place = (lax.broadcasted_iota(jnp.int32, (n_sel_blk, HEAD_PAD), 1)
             == NSA_DH + lax.broadcasted_iota(jnp.int32, (n_sel_blk, HEAD_PAD), 0)).astype(BF16)
    n_sub = lax.broadcasted_iota(jnp.int32, (n_cmp_pad, 1), 0)

    def add_bias(s, bias):
        return jnp.concatenate([s[:, r * Q_BLOCK:(r + 1) * Q_BLOCK] + bias for r in range(s.shape[1] // Q_BLOCK)],
                               axis=1)

    q0_s, t_lane_s, gates_s, qp, o_cmp, imp_s, qs_s = [], [], [], {}, {}, {}, {}
    for sub in range(NSA_Q_SUB):
        q0 = (step * NSA_Q_SUB + sub) * Q_BLOCK
        tok_rows = slice(sub * Q_BLOCK, (sub + 1) * Q_BLOCK)
        t_lane = q0 + tok
        q0_s.append(q0)
        t_lane_s.append(t_lane)
        gates_s.append(jnp.transpose(_sigmoid(sm_ref[tok_rows, :])))

        cmp_bias = jnp.where((n_sub * CMP_STRIDE + (CMP_BLOCK - 1)) <= t_lane, 0.0, NEG_BIG)
        cmp_any = jnp.concatenate([(t_lane >= CMP_BLOCK - 1).astype(F32)] * NSA_REP, axis=1)

        for g in groups:
            qs = jnp.concatenate(
                [q_ref[tok_rows, (g * NSA_REP + r) * HEAD_PAD:(g * NSA_REP + r + 1) * HEAD_PAD]
                 for r in range(NSA_REP)], axis=0)

            s_c = add_bias(_dot_nt(kc_ref[0, g], qs), cmp_bias)
            e_c = jnp.exp2(s_c - jnp.max(s_c, axis=0, keepdims=True))
            p_c = e_c * (cmp_any / jnp.sum(e_c, axis=0, keepdims=True))
            o_cmp[sub, g] = _dot(vct_ref[0, g], p_c)[0:NSA_DH]

            p_sum = p_c[:, 0:Q_BLOCK]
            for r in range(1, NSA_REP):
                p_sum = p_sum + p_c[:, r * Q_BLOCK:(r + 1) * Q_BLOCK]
            imp_s[sub, g] = _dot(ovt_ref[...], p_sum)
            qs_s[sub, g] = qs

    def rank_select(imp_t, t_lane):
        j_sub = lax.broadcasted_iota(jnp.int32, (n_sel_blk, 1), 0)
        cur = jnp.right_shift(t_lane, SEL_SHIFT)
        forced = (j_sub == 0) | (j_sub == cur) | (j_sub == cur - 1)
        causal_blk = (j_sub * SEL_BLOCK) <= t_lane
        imp = jnp.where(forced, FORCE_SCORE, jnp.where(causal_blk, imp_t, -jnp.inf))
        rank = jnp.zeros((n_sel_blk, Q_BLOCK), F32)
        for jp in range(n_sel_blk):
            row = imp[jp:jp + 1, :]
            ahead = (row > imp) | ((row == imp) & (j_sub > jp))
            rank = rank + ahead.astype(F32)
        return (rank < SEL_TOPK).astype(BF16)

    sel_all = lax.cond(
        (step + 1) * (NSA_Q_SUB * Q_BLOCK) <= SEL_TOPK * SEL_BLOCK,
        lambda: tuple(jnp.ones((n_sel_blk, Q_BLOCK), BF16) for _ in units),
        lambda: tuple(rank_select(imp_s[u], t_lane_s[u[0]]) for u in units))
    for i, unit in enumerate(units):
        sel_lanes = _dot_tn(sel_all[i], place)
        pen = jnp.where(pen_lanes, (sel_lanes - 1.0) * (-NEG_BIG), 0.0).astype(BF16)
        qp[unit] = qs_s[unit] + jnp.concatenate([pen] * NSA_REP, axis=0)

    n_full = step

    def scores(kb, unit):
        k0 = pl.multiple_of(kb * kb_rows, kb_rows)
        g = unit[1]
        return _dot_nt(ksa[pl.ds(k0, kb_rows), g * HEAD_PAD:(g + 1) * HEAD_PAD], qp[unit])

    def absorb(state, s, kb, unit):
        m_i, acc = state
        m_new = jnp.maximum(m_i, jnp.max(s, axis=0, keepdims=True))
        return m_new, jnp.exp2(m_i - m_new) * acc + _dot(vst[kb, unit[1]], jnp.exp2(s - m_new))

    kpos_last = n_full * kb_rows + lax.broadcasted_iota(jnp.int32, (kb_rows, 1), 0)
    first = []
    for unit in units:
        s = add_bias(scores(n_full, unit), jnp.where(kpos_last <= t_lane_s[unit[0]], 0.0, NEG_BIG))
        m = jnp.max(s, axis=0, keepdims=True)
        first.append((m, _dot(vst[n_full, unit[1]], jnp.exp2(s - m))))

    def loop_body(kb, carry):
        st, s_cur = carry
        s_next = tuple(scores(kb + 1, u) for u in units)
        return tuple(absorb(st[i], s_cur[i], kb, u) for i, u in enumerate(units)), s_next

    st, s_cur = lax.fori_loop(0, n_full - 1, loop_body, (tuple(first), tuple(scores(0, u) for u in units)))
    last = jnp.maximum(n_full - 1, 0)
    none_bias = jnp.where(n_full > 0, 0.0, NEG_BIG)
    sel_out = tuple(absorb(st[i], s_cur[i] + none_bias, last, u) for i, u in enumerate(units))

    span = WINDOW + Q_BLOCK
    for i, (sub, g) in enumerate(units):
        q0, t_lane = q0_s[sub], t_lane_s[sub]
        tok_rows = slice(sub * Q_BLOCK, (sub + 1) * Q_BLOCK)
        w0 = pl.multiple_of(jnp.maximum(q0 - WINDOW, 0), Q_BLOCK)
        kpos_w = w0 + lax.broadcasted_iota(jnp.int32, (span, 1), 0)
        win_bias = jnp.where((kpos_w <= t_lane) & (kpos_w > t_lane - WINDOW), 0.0, NEG_BIG)

        acc_s = sel_out[i][1]
        o_slc = acc_s[0:NSA_DH] * (1.0 / acc_s[one_row:one_row + 1, :])

        k_w = kv_ref[pl.ds(w0, span), (4 + g) * HEAD_PAD:(5 + g) * HEAD_PAD]
        s_w = add_bias(_dot_nt(k_w, qp[sub, g]), win_bias)
        tile0 = lax.div(w0, Q_BLOCK)
        v_wt = jnp.concatenate([vwt[tile0 + t, g] for t in range(span // Q_BLOCK)], axis=1)
        acc_w = _dot(v_wt, jnp.exp2(s_w - jnp.max(s_w, axis=0, keepdims=True)))
        o_win = acc_w[0:NSA_DH] * (1.0 / acc_w[one_row:one_row + 1, :])

        gates_t = gates_s[sub]
        o_heads = []
        for r in range(NSA_REP):
            c0 = 8 + (g * NSA_REP + r) * 3
            sl = slice(r * Q_BLOCK, (r + 1) * Q_BLOCK)
            o_t = (gates_t[c0:c0 + 1, :] * o_cmp[sub, g][:, sl] + gates_t[c0 + 1:c0 + 2, :] * o_slc[:, sl]
                   + gates_t[c0 + 2:c0 + 3, :] * o_win[:, sl])
            o_heads.append(o_t)
        for pair in range(NSA_REP // 2):
            lanes = slice((g * NSA_REP + 2 * pair) * NSA_DH, (g * NSA_REP + 2 * pair + 2) * NSA_DH)
            o_ref[tok_rows, lanes] = jnp.transpose(jnp.concatenate(o_heads[2 * pair:2 * pair + 2], axis=0)
                                                   ).astype(o_ref.dtype)


def _nsa(nq, nkv, kc, vc, small, overlap, bsz, seq):
    assert NSA_Q_SUB * Q_BLOCK == SEL_KEY_BLOCK
    rows = NSA_Q_SUB * Q_BLOCK
    nstep = seq // rows
    in_specs = [pl.BlockSpec((rows, NSA_HEADS * HEAD_PAD), lambda b, i: (b * nstep + i, 0)),
                pl.BlockSpec((seq, nkv.shape[1]), lambda b, i: (b, 0)),
                pl.BlockSpec((1,) + kc.shape[1:], lambda b, i: (b, 0, 0, 0)),
                pl.BlockSpec((1,) + vc.shape[1:], lambda b, i: (b, 0, 0, 0)),
                pl.BlockSpec((rows, LANE), lambda b, i: (b * nstep + i, 0)),
                _full_spec(overlap.shape)]
    return pl.pallas_call(
        functools.partial(_nsa_kernel, seq=seq), grid=(bsz, nstep), in_specs=in_specs,
        out_specs=pl.BlockSpec((rows, NSA_QW), lambda b, i: (b * nstep + i, 0)),
        out_shape=jax.ShapeDtypeStruct((bsz * seq, NSA_QW), BF16),
        scratch_shapes=[pltpu.VMEM((seq, NSA_KV_HEADS * HEAD_PAD), BF16),
                        pltpu.VMEM((seq // SEL_KEY_BLOCK, NSA_KV_HEADS, HEAD_PAD, SEL_KEY_BLOCK), BF16),
                        pltpu.VMEM((seq // Q_BLOCK, NSA_KV_HEADS, HEAD_PAD, Q_BLOCK), BF16)],
        compiler_params=_cparams(("parallel", "arbitrary")), name="nsa_attention",
    )(nq, nkv, kc, vc, small, overlap)


def _merge_kernel(oa_ref, ob_ref, mix_ref, x_ref, wa_ref, wb_ref, wo_ref, g_ref, b_ref, h_ref):
    ya = jnp.dot(oa_ref[...], wa_ref[...], preferred_element_type=F32)
    yb = jnp.dot(ob_ref[...], wb_ref[...], preferred_element_type=F32)
    mix = mix_ref[...].astype(F32)
    mixed = _sigmoid(mix[:, 0:D_MODEL]) * ya + _sigmoid(mix[:, D_MODEL:2 * D_MODEL]) * yb
    y = _dot(mixed, wo_ref[...])
    h_ref[...] = _layer_norm(DN_ALPHA * x_ref[...] + y, g_ref[...], b_ref[...])


def _merge(oa, ob, mix, x2d, wa, wb, wo, g, b):
    m = x2d.shape[0]
    tm = 2 * ROW_TILE
    row = lambda n: pl.BlockSpec((tm, n), lambda i: (i, 0))
    in_specs = [row(oa.shape[1]), row(ob.shape[1]), row(mix.shape[1]), row(D_MODEL),
                _full_spec(wa.shape), _full_spec(wb.shape), _full_spec(wo.shape),
                _full_spec(g.shape), _full_spec(b.shape)]
    return pl.pallas_call(
        _merge_kernel, grid=(m // tm,), in_specs=in_specs, out_specs=row(D_MODEL),
        out_shape=jax.ShapeDtypeStruct((m, D_MODEL), F32),
        compiler_params=_cparams(("parallel",)), name="merge_ln",
    )(oa, ob, mix, x2d, wa, wb, wo, g, b)


def _ffn_kernel(h_ref, halo_ref, wup_ref, cw_ref, wd_ref, g_ref, b_ref, o_ref,
                hb, ug, uv, act, *, seq, tm, halo):
    i = pl.program_id(0)
    first = lax.rem(i * tm, seq) == 0
    hb[0:halo, :] = jnp.where(first, 0.0, halo_ref[...]).astype(BF16)
    hb[halo:, :] = h_ref[...].astype(BF16)

    def conv(u_ref, slot, cols):
        out = None
        for tap in range(FFN_CONV):
            term = cw_ref[tap:tap + 1, cols] * u_ref[slot, pl.ds(halo - (FFN_CONV - 1) + tap, tm), :]
            out = term if out is None else out + term
        return out

    for j in range(D_FF // FFN_CHUNK):
        slot = j % FFN_SLOTS
        cols_g = slice(j * FFN_CHUNK, (j + 1) * FFN_CHUNK)
        cols_v = slice(D_FF + j * FFN_CHUNK, D_FF + (j + 1) * FFN_CHUNK)
        ug[slot] = jnp.dot(hb[...], wup_ref[:, cols_g], preferred_element_type=F32)
        uv[slot] = jnp.dot(hb[...], wup_ref[:, cols_v], preferred_element_type=F32)
        a = conv(ug, slot, cols_g)
        act[:, cols_g] = (a * _sigmoid(a) * conv(uv, slot, cols_v)).astype(act.dtype)

    f = jnp.dot(act[...], wd_ref[...], preferred_element_type=F32)
    o_ref[...] = _layer_norm(DN_ALPHA * h_ref[...] + f, g_ref[...], b_ref[...])


def _ffn(h, w_up, conv_w, w_down, g, b, seq):
    m = h.shape[0]
    tm = ROW_TILE
    halo = 16
    resident = lambda shape: pl.BlockSpec(shape, lambda i: (0, 0), pipeline_mode=pl.Buffered(1))
    in_specs = [pl.BlockSpec((tm, D_MODEL), lambda i: (i, 0)),
                pl.BlockSpec((halo, D_MODEL), lambda i: (jnp.maximum(i * (tm // halo) - 1, 0), 0)),
                resident(w_up.shape), resident(conv_w.shape), resident(w_down.shape),
                resident(g.shape), resident(b.shape)]
    return pl.pallas_call(
        functools.partial(_ffn_kernel, seq=seq, tm=tm, halo=halo), grid=(m // tm,),
        in_specs=in_specs, out_specs=pl.BlockSpec((tm, D_MODEL), lambda i: (i, 0)),
        out_shape=jax.ShapeDtypeStruct((m, D_MODEL), F32),
        scratch_shapes=[pltpu.VMEM((tm + halo, D_MODEL), BF16),
                        pltpu.VMEM((FFN_SLOTS, tm + halo, FFN_CHUNK), F32),
                        pltpu.VMEM((FFN_SLOTS, tm + halo, FFN_CHUNK), F32),
                        pltpu.VMEM((tm, D_FF), BF16)],
        compiler_params=_cparams(("parallel",)), name="conv_ffn_ln",
    )(h, h, w_up, conv_w, w_down, g, b)


def _compress_weights(w1):
    half = CMP_BLOCK // 2
    w1r = w1.reshape(2, half, NSA_DH, CMP_HIDDEN)
    eye = jnp.eye(NSA_KV_HEADS, dtype=w1.dtype)
    ex = jnp.einsum("aidh,gk->agikdh", w1r, eye)
    ex = ex.reshape(2 * NSA_KV_HEADS, half * NSA_KVW, CMP_HIDDEN)
    return jnp.concatenate([ex[n] for n in range(2 * NSA_KV_HEADS)], axis=1)


def _overlap_matrix(seq):
    n_cmp = (seq - CMP_BLOCK) // CMP_STRIDE + 1
    n_sel = seq // SEL_BLOCK
    starts = np.arange(n_cmp) * CMP_STRIDE
    jb = np.arange(n_sel) * SEL_BLOCK
    ov = ((starts[:, None] < jb[None] + SEL_BLOCK) & (starts[:, None] + CMP_BLOCK > jb[None])).astype(np.float32)
    out = np.zeros((n_sel, seq // CMP_STRIDE), np.float32)
    out[:, :n_cmp] = ov.T
    return jnp.asarray(out, BF16)


def kernel(x, w_in, gdn_conv_w, gdn_a_log, gdn_dt_bias, gdn_norm_w, cmp_pos_k, cmp_w1_k, cmp_w2_k,
           cmp_pos_v, cmp_w1_v, cmp_w2_v, w_branch_gdn, w_branch_nsa, w_out, ln1_g, ln1_b, w_up,
           ffn_conv_w, w_down, ln2_g, ln2_b):
    bsz, seq, _ = x.shape
    m = bsz * seq
    for i in range(DEPTH):
        x2d = x.reshape(m, D_MODEL)
        w = w_in[i]
        o_gdn = 2 * GDN_QK + 2 * GDN_VW
        o_nq = o_gdn + 2 * GDN_HEADS
        o_kv = o_nq + NSA_QW
        o_gate = o_kv + 6 * NSA_KVW
        o_mix = o_gate + 3 * NSA_HEADS
        kv = [w[:, o_kv + n * NSA_KVW:o_kv + (n + 1) * NSA_KVW] for n in range(6)]
        small_w = jnp.concatenate([w[:, o_gdn:o_nq], w[:, o_gate:o_mix]], axis=1)
        small_w = jnp.pad(small_w, ((0, 0), (0, LANE - small_w.shape[1])))
        weights = [w[:, :o_gdn],
                   w[:, o_nq:o_kv],
                   w[:, o_kv + 2 * NSA_KVW:o_gate],
                   kv[0], kv[1],
                   w[:, o_mix:],
                   small_w]
        weights = [t.astype(BF16) for t in weights]
        widths = [o_gdn, NSA_HEADS * HEAD_PAD, 4 * NSA_KV_HEADS * HEAD_PAD, NSA_KVW, NSA_KVW, 2 * D_MODEL, LANE]
        scales = [1.0, NSA_DH ** -0.5 * LOG2E, 1.0, 1.0, 1.0, 1.0, 1.0]
        gqkvz, nq, nkv, kcp, vcp, mix, small, grow = _inproj(x2d, weights, [BF16] * 6 + [F32], widths, scales,
                                                             2 * GDN_HEADS)

        alog = jnp.broadcast_to(gdn_a_log[i][:, None, None], (GDN_HEADS, 1, LANE))
        dtb = jnp.broadcast_to(gdn_dt_bias[i][:, None, None], (GDN_HEADS, 1, LANE))
        o_a = _gdn(gqkvz, gdn_conv_w[i], small, grow, alog, dtb, gdn_norm_w[i][None, :], bsz, seq)

        grp = seq // CMP_STRIDE
        k16 = kcp.reshape(bsz, grp, CMP_STRIDE * NSA_KVW)
        v16 = vcp.reshape(bsz, grp, CMP_STRIDE * NSA_KVW)
        pos_rows = lambda p: jnp.pad(p.reshape(1, CMP_BLOCK * NSA_DH), ((0, 7), (0, 0))).astype(BF16)
        w2_pad = lambda t: jnp.pad(t, ((0, 0), (0, HEAD_PAD - NSA_DH))).astype(BF16)
        kc, vc = _compress(k16, v16,
                           _compress_weights(cmp_w1_k[i]).astype(BF16), _compress_weights(cmp_w1_v[i]).astype(BF16),
                           cmp_w1_k[i].astype(BF16), cmp_w1_v[i].astype(BF16),
                           pos_rows(cmp_pos_k[i]), pos_rows(cmp_pos_v[i]),
                           w2_pad(cmp_w2_k[i]), w2_pad(cmp_w2_v[i]))
        o_b = _nsa(nq, nkv, kc, vc, small, _overlap_matrix(seq), bsz, seq)

        h = _merge(o_a, o_b, mix, x2d, w_branch_gdn[i].astype(BF16), w_branch_nsa[i].astype(BF16), w_out[i].astype(BF16),
                   ln1_g[i][None, :], ln1_b[i][None, :])

        x = _ffn(h, w_up[i].astype(BF16), ffn_conv_w[i], w_down[i].astype(BF16),
                 ln2_g[i][None, :], ln2_b[i][None, :], seq).reshape(bsz, seq, D_MODEL)
    return x
```

```python
import functools
import math

import numpy as np
import jax
import jax.numpy as jnp
from jax import lax
from jax.experimental import pallas as pl
from jax.experimental.pallas import tpu as pltpu

F32 = jnp.float32
BF16 = jnp.bfloat16

D_MODEL = 1024
GDN_HEADS = 4
GDN_DK = 128
GDN_DV = 128
GDN_CONV = 4
GDN_CHUNK = 64
NSA_HEADS = 8
NSA_KV_HEADS = 2
NSA_REP = NSA_HEADS // NSA_KV_HEADS
NSA_DH = 64
CMP_BLOCK = 32
CMP_STRIDE = 16
CMP_HIDDEN = 256
SEL_BLOCK = 64
SEL_SHIFT = 6
SEL_TOPK = 16
WINDOW = 512
FORCE_SCORE = 1e9
D_FF = 2816
FFN_CONV = 3
DEPTH = 1
DN_ALPHA = (2 * DEPTH) ** 0.25
LN_EPS = 1e-5
RMS_EPS = 1e-6
LOG2E = math.log2(math.e)

GDN_QK = GDN_HEADS * GDN_DK
GDN_VW = GDN_HEADS * GDN_DV
NSA_QW = NSA_HEADS * NSA_DH
NSA_KVW = NSA_KV_HEADS * NSA_DH

LANE = 128
HEAD_PAD = LANE
NEG_BIG = -1e30
VMEM_LIMIT = 56 * 1024 * 1024

Q_BLOCK = 128
NSA_Q_SUB = 2
SEL_KEY_BLOCK = 256
FFN_CHUNK = 256
FFN_SLOTS = 2
ROW_TILE = 512
GDN_CHUNK_SHIFT = 6
GDN_PAIR = 2 * GDN_CHUNK
GDN_UNROLL = 16


def _cparams(sem):
    return pltpu.CompilerParams(dimension_semantics=sem, vmem_limit_bytes=VMEM_LIMIT)


def _dot(a, b):
    return jnp.dot(a.astype(BF16), b.astype(BF16), preferred_element_type=F32)


def _dot_nt(a, b):
    return lax.dot_general(a.astype(BF16), b.astype(BF16), (((1,), (1,)), ((), ())),
                           preferred_element_type=F32)


def _dot_tn(a, b):
    return lax.dot_general(a.astype(BF16), b.astype(BF16), (((0,), (0,)), ((), ())),
                           preferred_element_type=F32)


def _sigmoid(x):
    return jax.nn.sigmoid(x)


def _softplus(x):
    return jnp.maximum(x, 0.0) + jnp.log1p(jnp.exp(-jnp.abs(x)))


def _layer_norm(x, g, b):
    mu = jnp.mean(x, axis=-1, keepdims=True)
    xc = x - mu
    var = jnp.mean(xc * xc, axis=-1, keepdims=True)
    return xc * lax.rsqrt(var + LN_EPS) * g + b


def _full_spec(shape):
    nd = len(shape)
    return pl.BlockSpec(shape, lambda *_: (0,) * nd, pipeline_mode=pl.Buffered(1))


def _inproj_kernel(x_ref, *refs, t_rows, scales):
    n = (len(refs) - 1) // 2
    xb = x_ref[...].astype(BF16)
    for w_ref, o_ref, scale in zip(refs[:n], refs[n:2 * n], scales):
        res = jnp.dot(xb, w_ref[...], preferred_element_type=F32)
        if scale != 1.0:
            res = res * scale
        if o_ref.shape[1] == w_ref.shape[1]:
            o_ref[...] = res.astype(o_ref.dtype)
        else:
            out = res.astype(o_ref.dtype)
            zeros = jnp.zeros((out.shape[0], HEAD_PAD - NSA_DH), o_ref.dtype)
            for hd in range(w_ref.shape[1] // NSA_DH):
                o_ref[:, hd * HEAD_PAD:hd * HEAD_PAD + NSA_DH] = out[:, hd * NSA_DH:(hd + 1) * NSA_DH]
                o_ref[:, hd * HEAD_PAD + NSA_DH:(hd + 1) * HEAD_PAD] = zeros
    t_ref = refs[2 * n]
    for j in range(t_ref.shape[0]):
        t_ref[j] = jnp.transpose(res[j * LANE:(j + 1) * LANE, :])[0:t_rows, :]


def _inproj(x2d, weights, out_dtypes, out_widths, scales, t_rows):
    assert weights[-1].shape[1] == LANE and out_dtypes[-1] == F32
    m = x2d.shape[0]
    tm = 2 * ROW_TILE
    in_specs = [pl.BlockSpec((tm, D_MODEL), lambda i: (i, 0))]
    in_specs += [_full_spec(w.shape) for w in weights]
    out_specs = [pl.BlockSpec((tm, n), lambda i: (i, 0)) for n in out_widths]
    out_specs.append(pl.BlockSpec((tm // LANE, t_rows, LANE), lambda i: (i, 0, 0)))
    out_shape = [jax.ShapeDtypeStruct((m, n), dt) for n, dt in zip(out_widths, out_dtypes)]
    out_shape.append(jax.ShapeDtypeStruct((m // LANE, t_rows, LANE), F32))
    return pl.pallas_call(
        functools.partial(_inproj_kernel, t_rows=t_rows, scales=tuple(scales)), grid=(m // tm,),
        in_specs=in_specs, out_specs=out_specs,
        out_shape=out_shape, compiler_params=_cparams(("parallel",)), name="inproj")(x2d, *weights)


def _gdn_kernel(q_ref, k_ref, v_ref, z_ref, cwq_ref, cwk_ref, cwv_ref, gcol_ref, grow_ref,
                alog_ref, dtb_ref, nw_ref, o_ref,
                xpad, qh, kh, vh, xp, qk_s, rhs, kd, qd, gl, sol, mq, ns, o0, st, *, seq):
    c = GDN_CHUNK
    pr = GDN_PAIR
    npair = seq // pr
    head = pl.program_id(1)
    pad = 8
    conv_rows = 256
    xpad[0:pad, :] = jnp.zeros((pad, 3 * LANE), F32)
    xpad[pad:, 0:LANE] = q_ref[...].astype(F32)
    xpad[pad:, LANE:2 * LANE] = k_ref[...].astype(F32)
    xpad[pad:, 2 * LANE:3 * LANE] = v_ref[...].astype(F32)
    cw = jnp.concatenate([cwq_ref[...], cwk_ref[...], cwv_ref[...]], axis=1)

    for t0 in range(0, seq, conv_rows):
        xe = xpad[t0:t0 + pad + conv_rows, :]
        acc = xe * cw[0:1, :]
        for j in range(1, GDN_CONV):
            acc = xe * cw[j:j + 1, :] + pltpu.roll(acc, 1, 0)
        acc = acc[pad:, :]
        y = acc * _sigmoid(acc)
        yq, yk, yv = y[:, 0:LANE], y[:, LANE:2 * LANE], y[:, 2 * LANE:3 * LANE]
        yq = yq * lax.rsqrt(jnp.sum(yq * yq, axis=-1, keepdims=True) + RMS_EPS) * (GDN_DK ** -0.5)
        yk = yk * lax.rsqrt(jnp.sum(yk * yk, axis=-1, keepdims=True) + RMS_EPS)
        qh[t0:t0 + conv_rows, :] = yq
        kh[t0:t0 + conv_rows, :] = yk
        vh[t0:t0 + conv_rows, :] = yv

    neg_a = -jnp.exp(alog_ref[0, :, 0:1])
    dtb = dtb_ref[0, :, 0:1]
    ii = lax.broadcasted_iota(jnp.int32, (pr, pr), 0)
    jj = lax.broadcasted_iota(jnp.int32, (pr, pr), 1)
    same = jnp.right_shift(ii, GDN_CHUNK_SHIFT) == jnp.right_shift(jj, GDN_CHUNK_SHIFT)
    incl = same & (jj <= ii)
    strict = same & (jj < ii)
    upper = same & (ii <= jj)
    eye = (ii == jj).astype(F32)
    sub8 = lax.broadcasted_iota(jnp.int32, (8, LANE), 0)
    lane_id = lax.broadcasted_iota(jnp.int32, (1, LANE), 1)

    def over_pairs(body):
        def run(gi, carry):
            for uu in range(GDN_UNROLL):
                body(gi * GDN_UNROLL + uu)
            return carry
        lax.fori_loop(0, npair // GDN_UNROLL, run, 0)

    def stage_gates(p):
        rows = pl.ds(pl.multiple_of(p * pr, pr), pr)
        q = qh[rows, :]
        k = kh[rows, :]
        v = vh[rows, :]
        gcol = gcol_ref[rows, :]
        b_logit = jnp.sum(jnp.where(lane_id == head, gcol, 0.0), axis=1, keepdims=True)
        a_logit = jnp.sum(jnp.where(lane_id == GDN_HEADS + head, gcol, 0.0), axis=1, keepdims=True)
        a_row = jnp.sum(jnp.where(sub8[:, 0:1] == GDN_HEADS + head, grow_ref[p], 0.0), axis=0, keepdims=True)
        beta = _sigmoid(b_logit)
        g_c = neg_a * _softplus(a_logit + dtb)
        g_r = neg_a * _softplus(a_row + dtb)
        gc_col = jnp.sum(jnp.where(incl, g_r, 0.0), axis=1, keepdims=True)
        gc_row = jnp.sum(jnp.where(upper, g_c, 0.0), axis=0, keepdims=True)
        gt_col = jnp.sum(jnp.where(same, g_r, 0.0), axis=1, keepdims=True)
        decay = jnp.where(incl, jnp.exp(jnp.where(incl, gc_col - gc_row, 0.0)), 0.0)
        kq_k = _dot_nt(jnp.concatenate([k, q], axis=0), k)
        a_mat = jnp.where(strict, kq_k[0:pr] * decay * beta, 0.0)
        xp[p, 0:pr, :] = eye - a_mat
        xp[p, pr:2 * pr, :] = _dot(a_mat, a_mat)
        qk_s[p] = (kq_k[pr:2 * pr] * decay).astype(qk_s.dtype)
        e_gc = jnp.exp(gc_col)
        rhs[rows, 0:GDN_DV] = (v * beta).astype(rhs.dtype)
        rhs[rows, GDN_DV:GDN_DV + GDN_DK] = (k * (beta * e_gc)).astype(rhs.dtype)
        kd[rows, :] = (k * jnp.exp(gt_col - gc_col)).astype(kd.dtype)
        qd[rows, :] = q * e_gc
        g_last = jnp.exp(gt_col)
        gl[p] = jnp.where(sub8 == 0, g_last[0:1, :], g_last[c:c + 1, :])

    def stage_double(p):
        y = _dot(xp[p], xp[p, pr:2 * pr, :])
        xp[p, 0:pr, :] = xp[p, 0:pr, :] + y[0:pr]
        xp[p, pr:2 * pr, :] = y[pr:2 * pr]

    def stage_solve(p):
        rows = pl.ds(pl.multiple_of(p * pr, pr), pr)
        x_inv = xp[p, 0:pr, :]
        x_inv = x_inv + _dot(x_inv, xp[p, pr:2 * pr, :])
        sol[rows, :] = _dot(x_inv, rhs[rows, :])

    def stage_affine(p):
        r0 = pl.multiple_of(p * pr, pr)
        rows = pl.ds(r0, pr)
        e = _dot(qk_s[p], sol[rows, :])
        qe = qd[rows, :] - e[:, GDN_DV:GDN_DV + GDN_DK]
        g_rows = gl[p]
        for cc in range(pr // c):
            rc = pl.ds(r0 + cc * c, c)
            mn = _dot_tn(kd[rc, :], sol[rc, :])
            idx = p * (pr // c) + cc
            mq[head, idx, 0:GDN_DK, :] = (eye * g_rows[cc:cc + 1, :] - mn[:, GDN_DV:GDN_DV + GDN_DK]).astype(mq.dtype)
            mq[head, idx, GDN_DK:GDN_DK + c, :] = qe[cc * c:(cc + 1) * c].astype(mq.dtype)
            ns[head, idx] = mn[:, 0:GDN_DV]
            o0[head, idx] = e[cc * c:(cc + 1) * c, 0:GDN_DV]

    over_pairs(stage_gates)
    for _ in range(int(math.log2(c)) - 2):
        over_pairs(stage_double)
    over_pairs(stage_solve)
    over_pairs(stage_affine)

    @pl.when(head == GDN_HEADS - 1)
    def _():
        st[...] = jnp.zeros_like(st)
        nw = nw_ref[...]

        def step(ci, carry):
            r0 = pl.multiple_of(ci * c, c)
            for h in range(GDN_HEADS):
                cols = slice(h * LANE, (h + 1) * LANE)
                ys = jnp.dot(mq[h, ci], st[h], preferred_element_type=F32)
                st[h] = (ys[0:GDN_DK] + ns[h, ci]).astype(st.dtype)
                o = ys[GDN_DK:GDN_DK + c] + o0[h, ci]
                o = o * lax.rsqrt(jnp.mean(o * o, axis=-1, keepdims=True) + RMS_EPS) * nw
                z = z_ref[pl.ds(r0, c), cols].astype(F32)
                o_ref[pl.ds(r0, c), cols] = (o * (z * _sigmoid(z))).astype(o_ref.dtype)
            return carry

        lax.fori_loop(0, seq // c, step, 0, unroll=8)


def _gdn(gqkvz, conv_w, gcol, grow, alog, dtb, nw, bsz, seq):
    assert GDN_DK == GDN_PAIR == GDN_DV == LANE
    h = GDN_HEADS
    npair = seq // GDN_PAIR
    nchunk = seq // GDN_CHUNK
    blk = lambda off: pl.BlockSpec((seq, LANE), lambda b, hh: (b, off + hh))
    cwb = lambda off: pl.BlockSpec((GDN_CONV, LANE), lambda b, hh: (0, off + hh))
    in_specs = [blk(0), blk(h), blk(2 * h),
                pl.BlockSpec((seq, GDN_VW), lambda b, hh: (b, 3)),
                cwb(0), cwb(h), cwb(2 * h),
                pl.BlockSpec((seq, LANE), lambda b, hh: (b, 0)),
                pl.BlockSpec((npair, 8, GDN_PAIR), lambda b, hh: (b, 0, 0)),
                pl.BlockSpec((1, 1, LANE), lambda b, hh: (hh, 0, 0)),
                pl.BlockSpec((1, 1, LANE), lambda b, hh: (hh, 0, 0)),
                pl.BlockSpec((1, LANE), lambda b, hh: (0, 0))]
    scratch = [pltpu.VMEM((seq + 8, 3 * LANE), F32),
               pltpu.VMEM((seq, LANE), F32), pltpu.VMEM((seq, LANE), F32), pltpu.VMEM((seq, LANE), F32),
               pltpu.VMEM((npair, 2 * GDN_PAIR, GDN_PAIR), F32),
               pltpu.VMEM((npair, GDN_PAIR, GDN_PAIR), BF16),
               pltpu.VMEM((seq, GDN_DV + GDN_DK), BF16),
               pltpu.VMEM((seq, GDN_DK), BF16),
               pltpu.VMEM((seq, GDN_DK), F32),
               pltpu.VMEM((npair, 8, LANE), F32),
               pltpu.VMEM((seq, GDN_DV + GDN_DK), F32),
               pltpu.VMEM((h, nchunk, GDN_DK + GDN_CHUNK, GDN_DK), BF16),
               pltpu.VMEM((h, nchunk, GDN_DK, GDN_DV), F32),
               pltpu.VMEM((h, nchunk, GDN_CHUNK, GDN_DV), F32),
               pltpu.VMEM((h, GDN_DK, GDN_DV), BF16)]
    return pl.pallas_call(
        functools.partial(_gdn_kernel, seq=seq), grid=(bsz, h), in_specs=in_specs,
        out_specs=pl.BlockSpec((seq, GDN_VW), lambda b, hh: (b, 0)),
        out_shape=jax.ShapeDtypeStruct((bsz * seq, GDN_VW), BF16),
        scratch_shapes=scratch,
        compiler_params=_cparams(("parallel", "arbitrary")), name="gdn",
    )(gqkvz, gqkvz, gqkvz, gqkvz, conv_w, conv_w, conv_w, gcol, grow, alog, dtb, nw)


def _gelu_tanh(x):
    return 0.5 * x * (1.0 + jnp.tanh(math.sqrt(2.0 / math.pi) * (x + 0.044715 * x * x * x)))


def _compress_kernel(k16_ref, v16_ref, wkc_ref, wvc_ref, w1k_ref, w1v_ref, pk_ref, pv_ref,
                     w2k_ref, w2v_ref, kc_ref, vc_ref):
    nblk = k16_ref.shape[1]
    for x_ref, wc_ref, w1_ref, p_ref, w2_ref, o_ref in (
            (k16_ref, wkc_ref, w1k_ref, pk_ref, w2k_ref, kc_ref),
            (v16_ref, wvc_ref, w1v_ref, pv_ref, w2v_ref, vc_ref)):
        y = jnp.dot(x_ref[0], wc_ref[...], preferred_element_type=F32)
        pos_term = jnp.dot(p_ref[...], w1_ref[...], preferred_element_type=F32)[0:1, :]
        for g in range(NSA_KV_HEADS):
            top = y[:, g * CMP_HIDDEN:(g + 1) * CMP_HIDDEN]
            bot = y[:, (NSA_KV_HEADS + g) * CMP_HIDDEN:(NSA_KV_HEADS + g + 1) * CMP_HIDDEN]
            hid = top + pltpu.roll(bot, nblk - 1, 0) + pos_term
            out = _dot(_gelu_tanh(hid), w2_ref[...])
            o_ref[0, g] = (jnp.transpose(out) if o_ref is vc_ref else out).astype(o_ref.dtype)


def _compress(k16, v16, wkc, wvc, w1k, w1v, pk, pv, w2k, w2v):
    bsz, nblk, width = k16.shape
    in_specs = [pl.BlockSpec((1, nblk, width), lambda b: (b, 0, 0))] * 2
    in_specs += [_full_spec(a.shape) for a in (wkc, wvc, w1k, w1v, pk, pv, w2k, w2v)]
    out_spec = pl.BlockSpec((1, NSA_KV_HEADS, nblk, HEAD_PAD), lambda b: (b, 0, 0, 0))
    out_shape = jax.ShapeDtypeStruct((bsz, NSA_KV_HEADS, nblk, HEAD_PAD), BF16)
    return pl.pallas_call(
        _compress_kernel, grid=(bsz,), in_specs=in_specs, out_specs=[out_spec, out_spec],
        out_shape=[out_shape, out_shape], compiler_params=_cparams(("parallel",)), name="nsa_compress",
    )(k16, v16, wkc, wvc, w1k, w1v, pk, pv, w2k, w2v)


def _nsa_kernel(q_ref, kv_ref, kc_ref, vct_ref, sm_ref, ovt_ref, o_ref, ksa, vst, vwt, *, seq):
    step = pl.program_id(1)
    n_cmp_pad = kc_ref.shape[2]
    n_sel_blk = seq // SEL_BLOCK
    kb_rows = SEL_KEY_BLOCK
    one_row = NSA_DH
    groups = range(NSA_KV_HEADS)
    units = [(sub, g) for sub in range(NSA_Q_SUB) for g in groups]

    @pl.when(step == 0)
    def _():
        lane = lax.broadcasted_iota(jnp.int32, (1, NSA_KV_HEADS * HEAD_PAD), 1) & (HEAD_PAD - 1)
        row_t = lax.broadcasted_iota(jnp.int32, (HEAD_PAD, 1), 0)
        tiles = kb_rows // Q_BLOCK

        def fill(bi, carry):
            r0 = pl.multiple_of(bi * kb_rows, kb_rows)
            rows = pl.ds(r0, kb_rows)
            blk = jnp.right_shift(r0 + lax.broadcasted_iota(jnp.int32, (kb_rows, 1), 0), SEL_SHIFT)
            ksa[rows, :] = kv_ref[rows, 0:2 * HEAD_PAD] + (lane == NSA_DH + blk).astype(ksa.dtype)
            for g in range(NSA_KV_HEADS):
                v_t = jnp.transpose(kv_ref[rows, (2 + g) * HEAD_PAD:(3 + g) * HEAD_PAD].astype(F32))
                vst[bi, g] = jnp.where(row_t == one_row, 1.0, v_t).astype(vst.dtype)
                w_t = jnp.transpose(kv_ref[rows, (6 + g) * HEAD_PAD:(7 + g) * HEAD_PAD].astype(F32))
                w_t = jnp.where(row_t == one_row, 1.0, w_t).astype(vwt.dtype)
                for part in range(tiles):
                    vwt[bi * tiles + part, g] = w_t[:, part * Q_BLOCK:(part + 1) * Q_BLOCK]
            return carry

        lax.fori_loop(0, seq // kb_rows, fill, 0)

    tok = lax.broadcasted_iota(jnp.int32, (1, Q_BLOCK), 1)
    lane_q = lax.broadcasted_iota(jnp.int32, (1, HEAD_PAD), 1)
    pen_lanes = (lane_q >= NSA_DH) & (lane_q < NSA_DH + n_sel_blk)
    place = (lax.broadcasted_iota(jnp.int32, (n_sel_blk, HEAD_PAD), 1)
             == NSA_DH + lax.broadcasted_iota(jnp.int32, (n_sel_blk, HEAD_PAD), 0)).astype(BF16)
    n_sub = lax.broadcasted_iota(jnp.int32, (n_cmp_pad, 1), 0)

    def add_bias(s, bias):
        return jnp.concatenate([s[:, r * Q_BLOCK:(r + 1) * Q_BLOCK] + bias for r in range(s.shape[1] // Q_BLOCK)],
                               axis=1)

    q0_s, t_lane_s, gates_s, qp, o_cmp, imp_s, qs_s = [], [], [], {}, {}, {}, {}
    for sub in range(NSA_Q_SUB):
        q0 = (step * NSA_Q_SUB + sub) * Q_BLOCK
        tok_rows = slice(sub * Q_BLOCK, (sub + 1) * Q_BLOCK)
        t_lane = q0 + tok
        q0_s.append(q0)
        t_lane_s.append(t_lane)
        gates_s.append(jnp.transpose(_sigmoid(sm_ref[tok_rows, :])))

        cmp_bias = jnp.where((n_sub * CMP_STRIDE + (CMP_BLOCK - 1)) <= t_lane, 0.0, NEG_BIG)
        cmp_any = jnp.concatenate([(t_lane >= CMP_BLOCK - 1).astype(F32)] * NSA_REP, axis=1)

        for g in groups:
            qs = jnp.concatenate(
                [q_ref[tok_rows, (g * NSA_REP + r) * HEAD_PAD:(g * NSA_REP + r + 1) * HEAD_PAD]
                 for r in range(NSA_REP)], axis=0)

            s_c = add_bias(_dot_nt(kc_ref[0, g], qs), cmp_bias)
            e_c = jnp.exp2(s_c - jnp.max(s_c, axis=0, keepdims=True))
            p_c = e_c * (cmp_any / jnp.sum(e_c, axis=0, keepdims=True))
            o_cmp[sub, g] = _dot(vct_ref[0, g], p_c)[0:NSA_DH]

            p_sum = p_c[:, 0:Q_BLOCK]
            for r in range(1, NSA_REP):
                p_sum = p_sum + p_c[:, r * Q_BLOCK:(r + 1) * Q_BLOCK]
            imp_s[sub, g] = _dot(ovt_ref[...], p_sum)
            qs_s[sub, g] = qs

    def rank_select(imp_t, t_lane):
        j_sub = lax.broadcasted_iota(jnp.int32, (n_sel_blk, 1), 0)
        cur = jnp.right_shift(t_lane, SEL_SHIFT)
        forced = (j_sub == 0) | (j_sub == cur) | (j_sub == cur - 1)
        causal_blk = (j_sub * SEL_BLOCK) <= t_lane
        imp = jnp.where(forced, FORCE_SCORE, jnp.where(causal_blk, imp_t, -jnp.inf))
        rank = jnp.zeros((n_sel_blk, Q_BLOCK), F32)
        for jp in range(n_sel_blk):
            row = imp[jp:jp + 1, :]
            ahead = (row > imp) | ((row == imp) & (j_sub > jp))
            rank = rank + ahead.astype(F32)
        return (rank < SEL_TOPK).astype(BF16)

    sel_all = tuple(rank_select(imp_s[u], t_lane_s[u[0]]) for u in units)
    for i, unit in enumerate(units):
        sel_lanes = _dot_tn(sel_all[i], place)
        pen = jnp.where(pen_lanes, (sel_lanes - 1.0) * (-NEG_BIG), 0.0).astype(BF16)
        qp[unit] = qs_s[unit] + jnp.concatenate([pen] * NSA_REP, axis=0)

    n_full = step

    def scores(kb, unit):
        k0 = pl.multiple_of(kb * kb_rows, kb_rows)
        g = unit[1]
        return _dot_nt(ksa[pl.ds(k0, kb_rows), g * HEAD_PAD:(g + 1) * HEAD_PAD], qp[unit])

    def absorb(state, s, kb, unit):
        m_i, acc = state
        m_new = jnp.maximum(m_i, jnp.max(s, axis=0, keepdims=True))
        return m_new, jnp.exp2(m_i - m_new) * acc + _dot(vst[kb, unit[1]], jnp.exp2(s - m_new))

    kpos_last = n_full * kb_rows + lax.broadcasted_iota(jnp.int32, (kb_rows, 1), 0)
    first = []
    for unit in units:
        s = add_bias(scores(n_full, unit), jnp.where(kpos_last <= t_lane_s[unit[0]], 0.0, NEG_BIG))
        m = jnp.max(s, axis=0, keepdims=True)
        first.append((m, _dot(vst[n_full, unit[1]], jnp.exp2(s - m))))

    def loop_body(kb, carry):
        st, s_cur = carry
        s_next = tuple(scores(kb + 1, u) for u in units)
        return tuple(absorb(st[i], s_cur[i], kb, u) for i, u in enumerate(units)), s_next

    st, s_cur = lax.fori_loop(0, n_full - 1, loop_body, (tuple(first), tuple(scores(0, u) for u in units)))
    last = jnp.maximum(n_full - 1, 0)
    none_bias = jnp.where(n_full > 0, 0.0, NEG_BIG)
    sel_out = tuple(absorb(st[i], s_cur[i] + none_bias, last, u) for i, u in enumerate(units))

    span = WINDOW + Q_BLOCK
    for i, (sub, g) in enumerate(units):
        q0, t_lane = q0_s[sub], t_lane_s[sub]
        tok_rows = slice(sub * Q_BLOCK, (sub + 1) * Q_BLOCK)
        w0 = pl.multiple_of(jnp.maximum(q0 - WINDOW, 0), Q_BLOCK)
        kpos_w = w0 + lax.broadcasted_iota(jnp.int32, (span, 1), 0)
        win_bias = jnp.where((kpos_w <= t_lane) & (kpos_w > t_lane - WINDOW), 0.0, NEG_BIG)

        acc_s = sel_out[i][1]
        o_slc = acc_s[0:NSA_DH] * (1.0 / acc_s[one_row:one_row + 1, :])

        k_w = kv_ref[pl.ds(w0, span), (4 + g) * HEAD_PAD:(5 + g) * HEAD_PAD]
        s_w = add_bias(_dot_nt(k_w, qp[sub, g]), win_bias)
        tile0 = lax.div(w0, Q_BLOCK)
        v_wt = jnp.concatenate([vwt[tile0 + t, g] for t in range(span // Q_BLOCK)], axis=1)
        acc_w = _dot(v_wt, jnp.exp2(s_w - jnp.max(s_w, axis=0, keepdims=True)))
        o_win = acc_w[0:NSA_DH] * (1.0 / acc_w[one_row:one_row + 1, :])

        gates_t = gates_s[sub]
        o_heads = []
        for r in range(NSA_REP):
            c0 = 8 + (g * NSA_REP + r) * 3
            sl = slice(r * Q_BLOCK, (r + 1) * Q_BLOCK)
            o_t = (gates_t[c0:c0 + 1, :] * o_cmp[sub, g][:, sl] + gates_t[c0 + 1:c0 + 2, :] * o_slc[:, sl]
                   + gates_t[c0 + 2:c0 + 3, :] * o_win[:, sl])
            o_heads.append(o_t)
        for pair in range(NSA_REP // 2):
            lanes = slice((g * NSA_REP + 2 * pair) * NSA_DH, (g * NSA_REP + 2 * pair + 2) * NSA_DH)
            o_ref[tok_rows, lanes] = jnp.transpose(jnp.concatenate(o_heads[2 * pair:2 * pair + 2], axis=0)
                                                   ).astype(o_ref.dtype)


def _nsa(nq, nkv, kc, vc, small, overlap, bsz, seq):
    assert NSA_Q_SUB * Q_BLOCK == SEL_KEY_BLOCK
    rows = NSA_Q_SUB * Q_BLOCK
    nstep = seq // rows
    in_specs = [pl.BlockSpec((rows, NSA_HEADS * HEAD_PAD), lambda b, i: (b * nstep + i, 0)),
                pl.BlockSpec((seq, nkv.shape[1]), lambda b, i: (b, 0)),
                pl.BlockSpec((1,) + kc.shape[1:], lambda b, i: (b, 0, 0, 0)),
                pl.BlockSpec((1,) + vc.shape[1:], lambda b, i: (b, 0, 0, 0)),
                pl.BlockSpec((rows, LANE), lambda b, i: (b * nstep + i, 0)),
                _full_spec(overlap.shape)]
    return pl.pallas_call(
        functools.partial(_nsa_kernel, seq=seq), grid=(bsz, nstep), in_specs=in_specs,
        out_specs=pl.BlockSpec((rows, NSA_QW), lambda b, i: (b * nstep + i, 0)),
        out_shape=jax.ShapeDtypeStruct((bsz * seq, NSA_QW), BF16),
        scratch_shapes=[pltpu.VMEM((seq, NSA_KV_HEADS * HEAD_PAD), BF16),
                        pltpu.VMEM((seq // SEL_KEY_BLOCK, NSA_KV_HEADS, HEAD_PAD, SEL_KEY_BLOCK), BF16),
                        pltpu.VMEM((seq // Q_BLOCK, NSA_KV_HEADS, HEAD_PAD, Q_BLOCK), BF16)],
        compiler_params=_cparams(("parallel", "arbitrary")), name="nsa_attention",
    )(nq, nkv, kc, vc, small, overlap)


def _merge_kernel(oa_ref, ob_ref, mix_ref, x_ref, wa_ref, wb_ref, wo_ref, g_ref, b_ref, h_ref):
    ya = jnp.dot(oa_ref[...], wa_ref[...], preferred_element_type=F32)
    yb = jnp.dot(ob_ref[...], wb_ref[...], preferred_element_type=F32)
    mix = mix_ref[...].astype(F32)
    mixed = _sigmoid(mix[:, 0:D_MODEL]) * ya + _sigmoid(mix[:, D_MODEL:2 * D_MODEL]) * yb
    y = _dot(mixed, wo_ref[...])
    h_ref[...] = _layer_norm(DN_ALPHA * x_ref[...] + y, g_ref[...], b_ref[...])


def _merge(oa, ob, mix, x2d, wa, wb, wo, g, b):
    m = x2d.shape[0]
    tm = 2 * ROW_TILE
    row = lambda n: pl.BlockSpec((tm, n), lambda i: (i, 0))
    in_specs = [row(oa.shape[1]), row(ob.shape[1]), row(mix.shape[1]), row(D_MODEL),
                _full_spec(wa.shape), _full_spec(wb.shape), _full_spec(wo.shape),
                _full_spec(g.shape), _full_spec(b.shape)]
    return pl.pallas_call(
        _merge_kernel, grid=(m // tm,), in_specs=in_specs, out_specs=row(D_MODEL),
        out_shape=jax.ShapeDtypeStruct((m, D_MODEL), F32),
        compiler_params=_cparams(("parallel",)), name="merge_ln",
    )(oa, ob, mix, x2d, wa, wb, wo, g, b)


def _ffn_kernel(h_ref, halo_ref, wup_ref, cw_ref, wd_ref, g_ref, b_ref, o_ref,
                hb, ug, uv, act, *, seq, tm, halo):
    i = pl.program_id(0)
    first = lax.rem(i * tm, seq) == 0
    hb[0:halo, :] = jnp.where(first, 0.0, halo_ref[...]).astype(BF16)
    hb[halo:, :] = h_ref[...].astype(BF16)

    def conv(u_ref, slot, cols):
        out = None
        for tap in range(FFN_CONV):
            term = cw_ref[tap:tap + 1, cols] * u_ref[slot, pl.ds(halo - (FFN_CONV - 1) + tap, tm), :]
            out = term if out is None else out + term
        return out

    for j in range(D_FF // FFN_CHUNK):
        slot = j % FFN_SLOTS
        cols_g = slice(j * FFN_CHUNK, (j + 1) * FFN_CHUNK)
        cols_v = slice(D_FF + j * FFN_CHUNK, D_FF + (j + 1) * FFN_CHUNK)
        ug[slot] = jnp.dot(hb[...], wup_ref[:, cols_g], preferred_element_type=F32)
        uv[slot] = jnp.dot(hb[...], wup_ref[:, cols_v], preferred_element_type=F32)
        a = conv(ug, slot, cols_g)
        act[:, cols_g] = (a * _sigmoid(a) * conv(uv, slot, cols_v)).astype(act.dtype)

    f = jnp.dot(act[...], wd_ref[...], preferred_element_type=F32)
    o_ref[...] = _layer_norm(DN_ALPHA * h_ref[...] + f, g_ref[...], b_ref[...])


def _ffn(h, w_up, conv_w, w_down, g, b, seq):
    m = h.shape[0]
    tm = ROW_TILE
    halo = 16
    resident = lambda shape: pl.BlockSpec(shape, lambda i: (0, 0), pipeline_mode=pl.Buffered(1))
    in_specs = [pl.BlockSpec((tm, D_MODEL), lambda i: (i, 0)),
                pl.BlockSpec((halo, D_MODEL), lambda i: (jnp.maximum(i * (tm // halo) - 1, 0), 0)),
                resident(w_up.shape), resident(conv_w.shape), resident(w_down.shape),
                resident(g.shape), resident(b.shape)]
    return pl.pallas_call(
        functools.partial(_ffn_kernel, seq=seq, tm=tm, halo=halo), grid=(m // tm,),
        in_specs=in_specs, out_specs=pl.BlockSpec((tm, D_MODEL), lambda i: (i, 0)),
        out_shape=jax.ShapeDtypeStruct((m, D_MODEL), F32),
        scratch_shapes=[pltpu.VMEM((tm + halo, D_MODEL), BF16),
                        pltpu.VMEM((FFN_SLOTS, tm + halo, FFN_CHUNK), F32),
                        pltpu.VMEM((FFN_SLOTS, tm + halo, FFN_CHUNK), F32),
                        pltpu.VMEM((tm, D_FF), BF16)],
        compiler_params=_cparams(("parallel",)), name="conv_ffn_ln",
    )(h, h, w_up, conv_w, w_down, g, b)


def _compress_weights(w1):
    half = CMP_BLOCK // 2
    w1r = w1.reshape(2, half, NSA_DH, CMP_HIDDEN)
    eye = jnp.eye(NSA_KV_HEADS, dtype=w1.dtype)
    ex = jnp.einsum("aidh,gk->agikdh", w1r, eye)
    ex = ex.reshape(2 * NSA_KV_HEADS, half * NSA_KVW, CMP_HIDDEN)
    return jnp.concatenate([ex[n] for n in range(2 * NSA_KV_HEADS)], axis=1)


def _overlap_matrix(seq):
    n_cmp = (seq - CMP_BLOCK) // CMP_STRIDE + 1
    n_sel = seq // SEL_BLOCK
    starts = np.arange(n_cmp) * CMP_STRIDE
    jb = np.arange(n_sel) * SEL_BLOCK
    ov = ((starts[:, None] < jb[None] + SEL_BLOCK) & (starts[:, None] + CMP_BLOCK > jb[None])).astype(np.float32)
    out = np.zeros((n_sel, seq // CMP_STRIDE), np.float32)
    out[:, :n_cmp] = ov.T
    return jnp.asarray(out, BF16)


def kernel(x, w_in, gdn_conv_w, gdn_a_log, gdn_dt_bias, gdn_norm_w, cmp_pos_k, cmp_w1_k, cmp_w2_k,
           cmp_pos_v, cmp_w1_v, cmp_w2_v, w_branch_gdn, w_branch_nsa, w_out, ln1_g, ln1_b, w_up,
           ffn_conv_w, w_down, ln2_g, ln2_b):
    bsz, seq, _ = x.shape
    m = bsz * seq
    for i in range(DEPTH):
        x2d = x.reshape(m, D_MODEL)
        w = w_in[i]
        o_gdn = 2 * GDN_QK + 2 * GDN_VW
        o_nq = o_gdn + 2 * GDN_HEADS
        o_kv = o_nq + NSA_QW
        o_gate = o_kv + 6 * NSA_KVW
        o_mix = o_gate + 3 * NSA_HEADS
        kv = [w[:, o_kv + n * NSA_KVW:o_kv + (n + 1) * NSA_KVW] for n in range(6)]
        small_w = jnp.concatenate([w[:, o_gdn:o_nq], w[:, o_gate:o_mix]], axis=1)
        small_w = jnp.pad(small_w, ((0, 0), (0, LANE - small_w.shape[1])))
        weights = [w[:, :o_gdn],
                   w[:, o_nq:o_kv],
                   w[:, o_kv + 2 * NSA_KVW:o_gate],
                   kv[0], kv[1],
                   w[:, o_mix:],
                   small_w]
        weights = [t.astype(BF16) for t in weights]
        widths = [o_gdn, NSA_HEADS * HEAD_PAD, 4 * NSA_KV_HEADS * HEAD_PAD, NSA_KVW, NSA_KVW, 2 * D_MODEL, LANE]
        scales = [1.0, NSA_DH ** -0.5 * LOG2E, 1.0, 1.0, 1.0, 1.0, 1.0]
        gqkvz, nq, nkv, kcp, vcp, mix, small, grow = _inproj(x2d, weights, [BF16] * 6 + [F32], widths, scales,
                                                             2 * GDN_HEADS)

        alog = jnp.broadcast_to(gdn_a_log[i][:, None, None], (GDN_HEADS, 1, LANE))
        dtb = jnp.broadcast_to(gdn_dt_bias[i][:, None, None], (GDN_HEADS, 1, LANE))
        o_a = _gdn(gqkvz, gdn_conv_w[i], small, grow, alog, dtb, gdn_norm_w[i][None, :], bsz, seq)

        grp = seq // CMP_STRIDE
        k16 = kcp.reshape(bsz, grp, CMP_STRIDE * NSA_KVW)
        v16 = vcp.reshape(bsz, grp, CMP_STRIDE * NSA_KVW)
        pos_rows = lambda p: jnp.pad(p.reshape(1, CMP_BLOCK * NSA_DH), ((0, 7), (0, 0))).astype(BF16)
        w2_pad = lambda t: jnp.pad(t, ((0, 0), (0, HEAD_PAD - NSA_DH))).astype(BF16)
        kc, vc = _compress(k16, v16,
                           _compress_weights(cmp_w1_k[i]).astype(BF16), _compress_weights(cmp_w1_v[i]).astype(BF16),
                           cmp_w1_k[i].astype(BF16), cmp_w1_v[i].astype(BF16),
                           pos_rows(cmp_pos_k[i]), pos_rows(cmp_pos_v[i]),
                           w2_pad(cmp_w2_k[i]), w2_pad(cmp_w2_v[i]))
        o_b = _nsa(nq, nkv, kc, vc, small, _overlap_matrix(seq), bsz, seq)

        h = _merge(o_a, o_b, mix, x2d, w_branch_gdn[i].astype(BF16), w_branch_nsa[i].astype(BF16), w_out[i].astype(BF16),
                   ln1_g[i][None, :], ln1_b[i][None, :])

        x = _ffn(h, w_up[i].astype(BF16), ffn_conv_w[i], w_down[i].astype(BF16),
                 ln2_g[i][None, :], ln2_b[i][None, :], seq).reshape(bsz, seq, D_MODEL)
    return x
```
